```python
import math
import jax, jax.numpy as jnp
from jax import lax
import numpy as np

D_MODEL = 1024
BATCH = 8
SEQ = 2048
DEPTH = 2
DEC_BATCH = 128
DEC_SEQ = 8
PAST_LEN = 2048
PAGE_SIZE = 128

N_A_LAYERS = DEPTH // 2
N_B_LAYERS = DEPTH - N_A_LAYERS
DN_ALPHA = (2.0 * DEPTH) ** 0.25
DN_BETA = (8.0 * DEPTH) ** -0.25
LN_EPS = 1e-5
RMS_EPS = 1e-5
D_FF = 4 * D_MODEL

M_D_INNER = 2 * D_MODEL
M_HEAD_DIM = 64
M_N_HEADS = M_D_INNER // M_HEAD_DIM
M_N_GROUPS = 4
M_HPG = M_N_HEADS // M_N_GROUPS
M_D_STATE = 128
M_CONV = 4
M_CHUNK = 128
M_CONV_DIM = M_D_INNER + 2 * M_N_GROUPS * M_D_STATE
M_IN_DIM = M_D_INNER + M_CONV_DIM + M_N_HEADS

A_HEAD_DIM = 64
A_N_HEADS = D_MODEL // A_HEAD_DIM
A_N_KV = 4
A_GROUP = A_N_HEADS // A_N_KV
A_Q_DIM = A_N_HEADS * A_HEAD_DIM
CMP_LEN = 32
CMP_STRIDE = 16
CMP_RATIO = CMP_LEN // CMP_STRIDE
CMP_HIDDEN = 2 * A_HEAD_DIM
SEL_BLOCK = 64
SEL_TOPN = 16
WINDOW = 512
Q_BLOCK = 128
N_BUCKETS = 32
MAX_DISTANCE = 128
MASK_VALUE = -1e30
FORCE_SCORE = 1e3

kernel_name = "yoco_mamba2_nsa_decoder_step"


def layer_norm(x, g, b):
    xf = x.astype(jnp.float32)
    mu = jnp.mean(xf, axis=-1, keepdims=True)
    var = jnp.mean(jnp.square(xf - mu), axis=-1, keepdims=True)
    return ((xf - mu) * lax.rsqrt(var + LN_EPS)).astype(x.dtype) * g + b


def sq_relu_mlp(x, w1, w2):
    return jnp.square(jax.nn.relu(x @ w1)) @ w2


def rel_bucket(dist):
    n = jnp.maximum(dist, 0)
    max_exact = N_BUCKETS // 2
    large = max_exact + (jnp.log(jnp.maximum(n, max_exact).astype(jnp.float32) / max_exact)
                         / math.log(MAX_DISTANCE / max_exact) * (N_BUCKETS - max_exact)).astype(jnp.int32)
    return jnp.where(n < max_exact, n, jnp.minimum(large, N_BUCKETS - 1))


def causal_conv(xbc, conv_state, w, b):
    L = xbc.shape[1]
    xp = jnp.concatenate([conv_state, xbc], axis=1)
    acc = b + xp[:, 0:L] * w[0]
    for k in range(1, M_CONV):
        acc = acc + xp[:, k:k + L] * w[k]
    return jax.nn.silu(acc), xp[:, L:]


def segsum(a):
    T = a.shape[-1]
    aa = jnp.broadcast_to(a[..., :, None], a.shape + (T,))
    aa = jnp.where(jnp.tril(jnp.ones((T, T), bool), -1), aa, 0.0)
    s = jnp.cumsum(aa, axis=-2)
    return jnp.where(jnp.tril(jnp.ones((T, T), bool)), s, -jnp.inf)


def ssd_scan(x, a, Bm, Cm, h0):
    Bn, L, G, R, P = x.shape
    Q = M_CHUNK if L % M_CHUNK == 0 else L
    c = L // Q
    x = x.reshape(Bn, c, Q, G, R, P)
    Bm = Bm.reshape(Bn, c, Q, G, -1)
    Cm = Cm.reshape(Bn, c, Q, G, -1)
    a = a.reshape(Bn, c, Q, G, R).transpose(0, 3, 4, 1, 2)
    a_cum = jnp.cumsum(a, axis=-1)
    lmat = jnp.exp(segsum(a))
    y_diag = jnp.einsum('bclgn,bcsgn,bgrcls,bcsgrp->bclgrp', Cm, Bm, lmat, x)
    decay_states = jnp.exp(a_cum[..., -1:] - a_cum)
    states = jnp.einsum('bclgn,bgrcl,bclgrp->bcgrpn', Bm, decay_states, x)
    states = jnp.concatenate([h0[:, None], states], axis=1)
    chunk_a = jnp.pad(a_cum[..., -1], ((0, 0), (0, 0), (0, 0), (1, 0)))
    decay_chunk = jnp.exp(segsum(chunk_a))
    states = jnp.einsum('bgrzc,bcgrpn->bzgrpn', decay_chunk, states)
    prev_states, h_last = states[:, :-1], states[:, -1]
    y_off = jnp.einsum('bclgn,bcgrpn,bgrcl->bclgrp', Cm, prev_states, jnp.exp(a_cum))
    return (y_diag + y_off).reshape(Bn, L, G, R, P), h_last


def mamba_mixer(h, ssm_state, conv_state, in_w, conv_w, conv_b, dt_bias, a_log, d_skip, norm_w, out_w):
    Bn, L, _ = h.shape
    f32 = jnp.float32
    zxbcdt = h @ in_w
    z, xbc, dt = jnp.split(zxbcdt, [M_D_INNER, M_D_INNER + M_CONV_DIM], axis=-1)
    xbc, new_conv = causal_conv(xbc, conv_state, conv_w, conv_b)
    xs, Bm, Cm = jnp.split(xbc, [M_D_INNER, M_D_INNER + M_N_GROUPS * M_D_STATE], axis=-1)
    xs = xs.reshape(Bn, L, M_N_GROUPS, M_HPG, M_HEAD_DIM).astype(f32)
    Bm = Bm.reshape(Bn, L, M_N_GROUPS, M_D_STATE).astype(f32)
    Cm = Cm.reshape(Bn, L, M_N_GROUPS, M_D_STATE).astype(f32)
    dt = jax.nn.softplus(dt.astype(f32) + dt_bias.astype(f32)).reshape(Bn, L, M_N_GROUPS, M_HPG)
    A = -jnp.exp(a_log.astype(f32)).reshape(M_N_GROUPS, M_HPG)
    h0 = ssm_state.astype(f32).reshape(Bn, M_N_GROUPS, M_HPG, M_HEAD_DIM, M_D_STATE)
    y, h_last = ssd_scan(xs * dt[..., None], dt * A, Bm, Cm, h0)
    y = y + xs * d_skip.astype(f32).reshape(M_N_GROUPS, M_HPG)[:, :, None]
    y = y.reshape(Bn, L, M_D_INNER) * jax.nn.silu(z.astype(f32))
    yg = y.reshape(Bn, L, M_N_GROUPS, M_D_INNER // M_N_GROUPS)
    yg = yg * lax.rsqrt(jnp.mean(yg * yg, axis=-1, keepdims=True) + RMS_EPS)
    y = (yg.reshape(Bn, L, M_D_INNER) * norm_w).astype(h.dtype)
    new_ssm = h_last.reshape(Bn, M_N_HEADS, M_HEAD_DIM, M_D_STATE).astype(ssm_state.dtype)
    return y @ out_w, new_ssm, new_conv


def compress_tokens(tok, w1, pe, w2):
    Bn, T = tok.shape[:2]
    n_chunks = T // CMP_STRIDE
    n_cmp = n_chunks - CMP_RATIO + 1
    chunks = tok[:, :n_chunks * CMP_STRIDE].reshape(Bn, n_chunks, CMP_STRIDE, A_N_KV, A_HEAD_DIM)
    hid = jnp.einsum('bcjgd,jde->bcge', chunks[:, 0:n_cmp] + pe[0:CMP_STRIDE][:, None, :], w1[0])
    for r in range(1, CMP_RATIO):
        part = chunks[:, r:r + n_cmp] + pe[r * CMP_STRIDE:(r + 1) * CMP_STRIDE][:, None, :]
        hid = hid + jnp.einsum('bcjgd,jde->bcge', part, w1[r])
    return jnp.einsum('bcge,ed->bcgd', jax.nn.silu(hid), w2)


def to_sel_blocks(tok):
    Bn, T = tok.shape[:2]
    n_sel = -(-T // SEL_BLOCK)
    tok = jnp.pad(tok, ((0, 0), (0, n_sel * SEL_BLOCK - T), (0, 0), (0, 0)))
    return tok.reshape(Bn, n_sel, SEL_BLOCK, A_N_KV, A_HEAD_DIM)


def shared_kv(h, start, past_cmp, past_sel, past_win, kv_w, cmp_w1, cmp_pe, cmp_w2):
    Bn, Sq, _ = h.shape
    kv = (h @ kv_w).reshape(Bn, Sq, 3, 2, A_N_KV, A_HEAD_DIM)
    new_cmp, new_sel, new_win = kv[:, :, 0], kv[:, :, 1], kv[:, :, 2]
    full_cmp = jnp.concatenate([past_cmp, new_cmp], axis=1)
    full_sel = jnp.concatenate([past_sel, new_sel], axis=1)
    win_rows = jnp.concatenate([past_win, new_win], axis=1)
    n_past_w = past_win.shape[1]
    n_pad = WINDOW - n_past_w
    win_full = jnp.pad(win_rows, ((0, 0), (n_pad, 0), (0, 0), (0, 0), (0, 0)))
    win_pos = jnp.concatenate([jnp.full((n_pad,), -1, jnp.int32),
                               jnp.arange(start - n_past_w, start + Sq, dtype=jnp.int32)])
    k_cmp = compress_tokens(full_cmp[:, :, 0], cmp_w1[0], cmp_pe[0], cmp_w2[0])
    v_cmp = compress_tokens(full_cmp[:, :, 1], cmp_w1[1], cmp_pe[1], cmp_w2[1])
    shared = (k_cmp, v_cmp, to_sel_blocks(full_sel[:, :, 0]), to_sel_blocks(full_sel[:, :, 1]),
              win_full[:, :, 0], win_full[:, :, 1])
    n_keep = min(WINDOW, win_rows.shape[1])
    return shared, win_pos, new_cmp, new_sel, win_rows[:, win_rows.shape[1] - n_keep:]


def softmax_masked(s, mask):
    p = jax.nn.softmax(jnp.where(mask, s, MASK_VALUE), axis=-1)
    return jnp.where(mask, p, 0.0)


def nsa_attend(q, gates, q_pos, shared, win_pos, rel_bias):
    k_cmp, v_cmp, k_sel, v_sel, k_win, v_win = shared
    f32 = jnp.float32
    Bn, Sq = q.shape[:2]
    n_cmp, n_sel = k_cmp.shape[1], k_sel.shape[1]
    qb = Q_BLOCK if Sq % Q_BLOCK == 0 else Sq
    nqb = Sq // qb
    top_n = min(SEL_TOPN, n_sel)
    win_len = WINDOW + qb
    scale = A_HEAD_DIM ** -0.5
    cmp_start = jnp.arange(n_cmp, dtype=jnp.int32) * CMP_STRIDE
    cmp_end = cmp_start + CMP_LEN - 1
    sel_start = jnp.arange(n_sel, dtype=jnp.int32) * SEL_BLOCK
    sel_idx = jnp.arange(n_sel, dtype=jnp.int32)
    overlap = ((cmp_start[:, None] < sel_start[None, :] + SEL_BLOCK)
               & (cmp_start[:, None] + CMP_LEN > sel_start[None, :])).astype(f32)
    bias_hgr = rel_bias.astype(f32).reshape(N_BUCKETS, A_N_KV, A_GROUP)
    bias_g = bias_hgr.transpose(1, 0, 2)
    g_idx = jnp.arange(A_N_KV)

    def per_block(args, kc, vc, ksg, vsg, kw, vw):
        qq, gg, pos, s0 = args
        s_c = jnp.einsum('qgrd,cgd->qgrc', qq, kc).astype(f32) * scale
        s_c = s_c + bias_hgr[rel_bucket(pos[:, None] - cmp_end[None, :])].transpose(0, 2, 3, 1)
        m_c = (cmp_end[None, :] <= pos[:, None])[:, None, None, :]
        p_c = softmax_masked(s_c, m_c)
        o_c = jnp.einsum('qgrc,cgd->qgrd', p_c.astype(vc.dtype), vc)
        imp = jnp.einsum('qgrc,cs->qgs', p_c, overlap)
        blk = pos // SEL_BLOCK
        valid = sel_start[None, :] <= pos[:, None]
        forced = (sel_idx[None, :] == 0) | (sel_idx[None, :] == blk[:, None]) | (sel_idx[None, :] == blk[:, None] - 1)
        score = jnp.where(valid[:, None, :], imp + jnp.where(forced, FORCE_SCORE, 0.0)[:, None, :], -1.0)
        _, idx = lax.top_k(score, top_n)
        kg = ksg[g_idx[None, :, None], idx]
        vg = vsg[g_idx[None, :, None], idx]
        k_pos = idx[..., None] * SEL_BLOCK + jnp.arange(SEL_BLOCK, dtype=jnp.int32)
        s_s = jnp.einsum('qgrd,qgkld->qgrkl', qq, kg).astype(f32) * scale
        b_s = bias_g[g_idx[None, :, None, None], rel_bucket(pos[:, None, None, None] - k_pos)]
        s_s = s_s + b_s.transpose(0, 1, 4, 2, 3)
        m_s = (k_pos <= pos[:, None, None, None])[:, :, None]
        p_s = softmax_masked(s_s.reshape(qb, A_N_KV, A_GROUP, -1), m_s.reshape(qb, A_N_KV, 1, -1))
        o_s = jnp.einsum('qgrm,qgmd->qgrd', p_s.astype(vg.dtype), vg.reshape(qb, A_N_KV, -1, A_HEAD_DIM))
        kwb = lax.dynamic_slice_in_dim(kw, s0, win_len, axis=0)
        vwb = lax.dynamic_slice_in_dim(vw, s0, win_len, axis=0)
        wpos = lax.dynamic_slice_in_dim(win_pos, s0, win_len, axis=0)
        dist = pos[:, None] - wpos[None, :]
        s_w = jnp.einsum('qgrd,kgd->qgrk', qq, kwb).astype(f32) * scale
        s_w = s_w + bias_hgr[rel_bucket(dist)].transpose(0, 2, 3, 1)
        m_w = ((wpos[None, :] >= 0) & (dist >= 0) & (dist < WINDOW))[:, None, None, :]
        p_w = softmax_masked(s_w, m_w)
        o_w = jnp.einsum('qgrk,kgd->qgrd', p_w.astype(vwb.dtype), vwb)
        gf = gg.astype(f32)
        o = gf[..., 0:1] * o_c + gf[..., 1:2] * o_s + gf[..., 2:3] * o_w
        return o.reshape(qb, A_Q_DIM).astype(qq.dtype)

    def per_seq(args):
        qs, gs, kc, vc, ks, vs, kw, vw = args
        ksg = ks.transpose(2, 0, 1, 3)
        vsg = vs.transpose(2, 0, 1, 3)
        blocks = (qs.reshape(nqb, qb, A_N_KV, A_GROUP, A_HEAD_DIM),
                  gs.reshape(nqb, qb, A_N_KV, A_GROUP, 3),
                  q_pos.reshape(nqb, qb),
                  jnp.arange(nqb, dtype=jnp.int32) * qb)
        out = lax.map(lambda a: per_block(a, kc, vc, ksg, vsg, kw, vw), blocks)
        return out.reshape(Sq, A_Q_DIM)

    return lax.map(per_seq, (q, gates, k_cmp, v_cmp, k_sel, v_sel, k_win, v_win))


def nsa_mixer(h, q_pos, shared, win_pos, qg_w, o_w, rel_bias):
    Bn, Sq, _ = h.shape
    qg = h @ qg_w
    q = qg[..., :A_Q_DIM].reshape(Bn, Sq, A_N_HEADS, A_HEAD_DIM)
    gates = jax.nn.sigmoid(qg[..., A_Q_DIM:].astype(jnp.float32)).reshape(Bn, Sq, A_N_HEADS, 3)
    o = nsa_attend(q, gates, q_pos, shared, win_pos, rel_bias)
    return o @ o_w


def run_group(x, start, ssm0, conv0, past_cmp, past_sel, past_win, weights):
    (ln_g, ln_b, m_in_w, m_conv_w, m_conv_b, m_dt_bias, m_a_log, m_d, m_norm_w, m_out_w,
     kv_w, cmp_w1, cmp_pe, cmp_w2, q_w, o_w, rel_bias, mlp_w1, mlp_w2) = weights
    Bn, Sq, _ = x.shape
    q_pos = start + jnp.arange(Sq, dtype=jnp.int32)
    h = x
    new_ssm, new_conv = [], []
    for i in range(DEPTH):
        if i < N_A_LAYERS:
            mix, s_i, c_i = mamba_mixer(h, ssm0[i], conv0[i], m_in_w[i], m_conv_w[i], m_conv_b[i],
                                        m_dt_bias[i], m_a_log[i], m_d[i], m_norm_w[i], m_out_w[i])
            new_ssm.append(s_i)
            new_conv.append(c_i)
        else:
            if i == N_A_LAYERS:
                shared, win_pos, new_cmp, new_sel, new_win = shared_kv(
                    h, start, past_cmp, past_sel, past_win, kv_w, cmp_w1, cmp_pe, cmp_w2)
            j = i - N_A_LAYERS
            mix = nsa_mixer(h, q_pos, shared, win_pos, q_w[j], o_w[j], rel_bias)
        h = layer_norm(DN_ALPHA * h + mix, ln_g[i, 0], ln_b[i, 0])
        h = layer_norm(DN_ALPHA * h + sq_relu_mlp(h, mlp_w1[i], mlp_w2[i]), ln_g[i, 1], ln_b[i, 1])
    return h, jnp.stack(new_ssm), jnp.stack(new_conv), new_cmp, new_sel, new_win


def setup_inputs(seed: int = 0) -> dict:
    key = jax.random.key(seed)
    ks = jax.random.split(key, 28)
    f32 = jnp.float32

    def nrm(k, shape, scale):
        return jax.random.normal(k, shape, f32) * scale

    n_pages = PAST_LEN // PAGE_SIZE
    n_used = DEC_BATCH * n_pages
    n_phys = n_used + n_used // 4
    win_keep = min(WINDOW, PAST_LEN)
    kv_row = (2, A_N_KV, A_HEAD_DIM)
    page_table = jax.random.permutation(ks[7], n_phys)[:n_used].reshape(DEC_BATCH, n_pages).astype(jnp.int32)
    dt0 = jnp.exp(jax.random.uniform(ks[13], (N_A_LAYERS, M_N_HEADS), f32, math.log(1e-3), math.log(1e-1)))
    return {
        "x_prompt": nrm(ks[0], (BATCH, SEQ, D_MODEL), 1.0),
        "x_sample": nrm(ks[1], (DEC_BATCH, DEC_SEQ, D_MODEL), 1.0),
        "state_ssm": nrm(ks[2], (N_A_LAYERS, DEC_BATCH, M_N_HEADS, M_HEAD_DIM, M_D_STATE), 0.1),
        "state_conv": nrm(ks[3], (N_A_LAYERS, DEC_BATCH, M_CONV - 1, M_CONV_DIM), 1.0),
        "cache_cmp_kv": nrm(ks[4], (n_phys, PAGE_SIZE) + kv_row, 1.0),
        "cache_sel_kv": nrm(ks[5], (n_phys, PAGE_SIZE) + kv_row, 1.0),
        "cache_win_kv": nrm(ks[6], (DEC_BATCH, win_keep) + kv_row, 1.0),
        "page_table": page_table,
        "ln_g": 1.0 + nrm(ks[8], (DEPTH, 2, D_MODEL), 0.05),
        "ln_b": nrm(ks[9], (DEPTH, 2, D_MODEL), 0.02),
        "m_in_w": nrm(ks[10], (N_A_LAYERS, D_MODEL, M_IN_DIM), D_MODEL ** -0.5),
        "m_conv_w": nrm(ks[11], (N_A_LAYERS, M_CONV, M_CONV_DIM), M_CONV ** -0.5),
        "m_conv_b": nrm(ks[12], (N_A_LAYERS, M_CONV_DIM), 0.02),
        "m_dt_bias": dt0 + jnp.log(-jnp.expm1(-dt0)),
        "m_a_log": jnp.log(jax.random.uniform(ks[14], (N_A_LAYERS, M_N_HEADS), f32, 1.0, 16.0)),
        "m_d": 1.0 + nrm(ks[15], (N_A_LAYERS, M_N_HEADS), 0.1),
        "m_norm_w": 1.0 + nrm(ks[16], (N_A_LAYERS, M_D_INNER), 0.05),
        "m_out_w": nrm(ks[17], (N_A_LAYERS, M_D_INNER, D_MODEL), M_D_INNER ** -0.5 * DN_BETA),
        "kv_w": nrm(ks[18], (D_MODEL, 3 * 2 * A_N_KV * A_HEAD_DIM), D_MODEL ** -0.5),
        "cmp_w1": nrm(ks[19], (2, CMP_RATIO, CMP_STRIDE, A_HEAD_DIM, CMP_HIDDEN), (CMP_LEN * A_HEAD_DIM) ** -0.5),
        "cmp_pe": nrm(ks[20], (2, CMP_LEN, A_HEAD_DIM), 0.5),
        "cmp_w2": nrm(ks[21], (2, CMP_HIDDEN, A_HEAD_DIM), CMP_HIDDEN ** -0.5),
        "q_w": nrm(ks[22], (N_B_LAYERS, D_MODEL, A_Q_DIM + 3 * A_N_HEADS), D_MODEL ** -0.5),
        "o_w": nrm(ks[23], (N_B_LAYERS, A_Q_DIM, D_MODEL), A_Q_DIM ** -0.5 * DN_BETA),
        "rel_bias": nrm(ks[24], (N_BUCKETS, A_N_HEADS), 0.5),
        "mlp_w1": nrm(ks[25], (DEPTH, D_MODEL, D_FF), D_MODEL ** -0.5),
        "mlp_w2": nrm(ks[26], (DEPTH, D_FF, D_MODEL), D_FF ** -0.5 * DN_BETA),
    }


def reference(x_prompt, x_sample, state_ssm, state_conv, cache_cmp_kv, cache_sel_kv, cache_win_kv, page_table,
              ln_g, ln_b, m_in_w, m_conv_w, m_conv_b, m_dt_bias, m_a_log, m_d, m_norm_w, m_out_w,
              kv_w, cmp_w1, cmp_pe, cmp_w2, q_w, o_w, rel_bias, mlp_w1, mlp_w2):
    weights = (ln_g, ln_b, m_in_w, m_conv_w, m_conv_b, m_dt_bias, m_a_log, m_d, m_norm_w, m_out_w,
               kv_w, cmp_w1, cmp_pe, cmp_w2, q_w, o_w, rel_bias, mlp_w1, mlp_w2)
    kv_row = (2, A_N_KV, A_HEAD_DIM)
    empty_kv = jnp.zeros((BATCH, 0) + kv_row, cache_cmp_kv.dtype)
    y_prompt, p_ssm, p_conv, p_cmp, p_sel, p_win = run_group(
        x_prompt, 0,
        jnp.zeros((N_A_LAYERS, BATCH, M_N_HEADS, M_HEAD_DIM, M_D_STATE), state_ssm.dtype),
        jnp.zeros((N_A_LAYERS, BATCH, M_CONV - 1, M_CONV_DIM), state_conv.dtype),
        empty_kv, empty_kv, jnp.zeros((BATCH, 0) + kv_row, cache_win_kv.dtype), weights)
    n_pages = PAST_LEN // PAGE_SIZE
    past_cmp = cache_cmp_kv[page_table].reshape((DEC_BATCH, n_pages * PAGE_SIZE) + kv_row)
    past_sel = cache_sel_kv[page_table].reshape((DEC_BATCH, n_pages * PAGE_SIZE) + kv_row)
    y_sample, s_ssm, s_conv, s_cmp, s_sel, s_win = run_group(
        x_sample, PAST_LEN, state_ssm, state_conv, past_cmp, past_sel, cache_win_kv, weights)
    return (y_prompt, y_sample, p_ssm, p_conv, p_cmp, p_sel, p_win, s_ssm, s_conv, s_cmp, s_sel, s_win)
```

```python
import functools
import math

import jax
import jax.numpy as jnp
import numpy as np
from jax import lax
from jax.experimental import pallas as pl
from jax.experimental.pallas import tpu as pltpu

f32 = jnp.float32
bf16 = jnp.bfloat16
i32 = jnp.int32

D_MODEL = 1024
DEPTH = 2
DN_ALPHA = (2.0 * DEPTH) ** 0.25
LN_EPS = 1e-5
RMS_EPS = 1e-5
D_FF = 4 * D_MODEL
M_D_INNER = 2 * D_MODEL
M_HEAD_DIM = 64
M_N_HEADS = M_D_INNER // M_HEAD_DIM
M_N_GROUPS = 4
M_HPG = M_N_HEADS // M_N_GROUPS
M_D_STATE = 128
M_CONV = 4
M_CHUNK = 128
M_CONV_DIM = M_D_INNER + 2 * M_N_GROUPS * M_D_STATE
A_HEAD_DIM = 64
A_N_HEADS = D_MODEL // A_HEAD_DIM
A_N_KV = 4
A_GROUP = A_N_HEADS // A_N_KV
A_Q_DIM = A_N_HEADS * A_HEAD_DIM
KV_ROW = 2 * A_N_KV * A_HEAD_DIM
CMP_LEN = 32
CMP_STRIDE = 16
CMP_RATIO = CMP_LEN // CMP_STRIDE
CMP_HIDDEN = 2 * A_HEAD_DIM
SEL_BLOCK = 64
SEL_TOPN = 16
WINDOW = 512
Q_BLOCK = 128
N_BUCKETS = 32
MAX_DISTANCE = 128
MASK_VALUE = -1e30
FORCE_SCORE = 1e3
PAGE_SIZE = 128

LANES = 128
SUBLANES = 8
VMEM_LIMIT = 56 * 1024 * 1024

_HI = lax.Precision.HIGHEST


def _cparams(sem):
    return pltpu.CompilerParams(dimension_semantics=sem, vmem_limit_bytes=VMEM_LIMIT)


def _dot(a, b):
    return jnp.dot(a, b, preferred_element_type=f32)


def _dot_nt(a, b):
    return lax.dot_general(a, b, (((1,), (1,)), ((), ())), preferred_element_type=f32)


def _dot_tn(a, b):
    return lax.dot_general(a, b, (((0,), (0,)), ((), ())), preferred_element_type=f32)


def _silu(x):
    return x * (1.0 / (1.0 + jnp.exp(-x)))


def _layer_norm(x, g, b):
    mu = jnp.mean(x, axis=-1, keepdims=True)
    xc = x - mu
    var = jnp.mean(xc * xc, axis=-1, keepdims=True)
    return xc * lax.rsqrt(var + LN_EPS) * g + b


def _mm_body(x_ref, w_ref, o_ref, *, act):
    y = _dot(x_ref[...], w_ref[...])
    if act == "sigmoid":
        y = 1.0 / (1.0 + jnp.exp(-y))
    o_ref[...] = y.astype(o_ref.dtype)


def matmul(x, w, out_dtype, act=None, tm=512, tn=512):
    M, K = x.shape
    N = w.shape[1]
    tn = min(tn, N)
    return pl.pallas_call(
        functools.partial(_mm_body, act=act),
        out_shape=jax.ShapeDtypeStruct((M, N), out_dtype),
        grid=(M // tm, N // tn),
        in_specs=[pl.BlockSpec((tm, K), lambda i, j: (i, 0)), pl.BlockSpec((K, tn), lambda i, j: (0, j))],
        out_specs=pl.BlockSpec((tm, tn), lambda i, j: (i, j)),
        compiler_params=_cparams(("parallel", "parallel")),
        name="matmul",
    )(x, w)


def _mm_res_ln_body(x_ref, w_ref, r_ref, g_ref, b_ref, of_ref, ob_ref):
    y = DN_ALPHA * r_ref[...] + _dot(x_ref[...], w_ref[...])
    h = _layer_norm(y, g_ref[...], b_ref[...])
    of_ref[...] = h
    ob_ref[...] = h.astype(bf16)


def matmul_residual_ln(x, w, resid, g, b, tm=512):
    M, K = x.shape
    N = w.shape[1]
    return pl.pallas_call(
        _mm_res_ln_body,
        out_shape=(jax.ShapeDtypeStruct((M, N), f32), jax.ShapeDtypeStruct((M, N), bf16)),
        grid=(M // tm,),
        in_specs=[
            pl.BlockSpec((tm, K), lambda i: (i, 0)),
            pl.BlockSpec((K, N), lambda i: (0, 0)),
            pl.BlockSpec((tm, N), lambda i: (i, 0)),
            pl.BlockSpec((1, N), lambda i: (0, 0)),
            pl.BlockSpec((1, N), lambda i: (0, 0)),
        ],
        out_specs=(pl.BlockSpec((tm, N), lambda i: (i, 0)), pl.BlockSpec((tm, N), lambda i: (i, 0))),
        compiler_params=_cparams(("parallel",)),
        name="matmul_residual_ln",
    )(x, w, resid, g, b)


def _mlp_body(hb_ref, hf_ref, w1_ref, w2_ref, g_ref, b_ref, of_ref, ob_ref, acc_ref):
    j = pl.program_id(1)

    @pl.when(j == 0)
    def _():
        acc_ref[...] = jnp.zeros_like(acc_ref)

    u = jnp.maximum(_dot(hb_ref[...], w1_ref[...]), 0.0)
    acc_ref[...] += _dot((u * u).astype(bf16), w2_ref[...])

    @pl.when(j == pl.num_programs(1) - 1)
    def _():
        h = _layer_norm(DN_ALPHA * hf_ref[...] + acc_ref[...], g_ref[...], b_ref[...])
        of_ref[...] = h
        ob_ref[...] = h.astype(bf16)


def mlp_residual_ln(hb, hf, w1, w2, g, b, tm=512, tf=512):
    M, D = hb.shape
    F = w1.shape[1]
    return pl.pallas_call(
        _mlp_body,
        out_shape=(jax.ShapeDtypeStruct((M, D), f32), jax.ShapeDtypeStruct((M, D), bf16)),
        grid=(M // tm, F // tf),
        in_specs=[
            pl.BlockSpec((tm, D), lambda i, j: (i, 0)),
            pl.BlockSpec((tm, D), lambda i, j: (i, 0)),
            pl.BlockSpec((D, tf), lambda i, j: (0, j)),
            pl.BlockSpec((tf, D), lambda i, j: (j, 0)),
            pl.BlockSpec((1, D), lambda i, j: (0, 0)),
            pl.BlockSpec((1, D), lambda i, j: (0, 0)),
        ],
        out_specs=(pl.BlockSpec((tm, D), lambda i, j: (i, 0)), pl.BlockSpec((tm, D), lambda i, j: (i, 0))),
        scratch_shapes=[pltpu.VMEM((tm, D), f32)],
        compiler_params=_cparams(("parallel", "arbitrary")),
        name="mlp_residual_ln",
    )(hb, hf, w1, w2, g, b)


def _ssd_body(*refs, Q, has_init):
    if has_init:
        (xbc_ref, z_ref, dt_ref, conv0_ref, h0_ref, cw_ref, cb_ref, dtb_ref, alog_ref, dsk_ref, nw_ref,
         y_ref, hout_ref, cout_ref, st_ref, xpad_ref, xc_ref, ybuf_ref) = refs
    else:
        (xbc_ref, z_ref, dt_ref, cw_ref, cb_ref, dtb_ref, alog_ref, dsk_ref, nw_ref,
         y_ref, hout_ref, cout_ref, st_ref, xpad_ref, xc_ref, ybuf_ref) = refs
    c = pl.program_id(1)
    P, N, R, G = M_HEAD_DIM, M_D_STATE, M_HPG, M_N_GROUPS
    PAD = SUBLANES

    @pl.when(c == 0)
    def _():
        xpad_ref[0:PAD, :] = jnp.zeros((PAD, M_CONV_DIM), f32)
        if has_init:
            xpad_ref[PAD - (M_CONV - 1):PAD, :] = conv0_ref[0]
            for g in range(G):
                for r in range(R):
                    st_ref[g, :, r * P:(r + 1) * P] = h0_ref[0, g * R + r].T
        else:
            st_ref[...] = jnp.zeros_like(st_ref)

    xpad_ref[PAD:PAD + Q, :] = xbc_ref[...]
    acc = cb_ref[...] + xpad_ref[pl.ds(PAD - 3, Q), :] * cw_ref[0:1, :]
    for k in range(1, M_CONV):
        acc = acc + xpad_ref[pl.ds(PAD - 3 + k, Q), :] * cw_ref[k:k + 1, :]
    xc_ref[...] = _silu(acc)
    cout_ref[0] = xpad_ref[pl.ds(Q + PAD - 3, 3), :]
    xpad_ref[0:PAD, :] = xpad_ref[pl.ds(Q, PAD), :]

    xdt = dt_ref[...] + dtb_ref[...]
    dt = jnp.maximum(xdt, 0.0) + jnp.log1p(jnp.exp(-jnp.abs(xdt)))
    a = dt * (-jnp.exp(alog_ref[...]))
    ri = lax.broadcasted_iota(i32, (Q, Q), 0)
    ci = lax.broadcasted_iota(i32, (Q, Q), 1)
    tril = ri >= ci
    a_cum = jnp.dot(tril.astype(f32), a, preferred_element_type=f32, precision=_HI)
    a_cum_t = a_cum.T
    a_last = a_cum[Q - 1:Q, :]
    e_cum = jnp.exp(a_cum)
    e_rest = jnp.exp(a_last - a_cum)
    e_last = jnp.exp(a_last)
    dsk = dsk_ref[...]

    for g in range(G):
        bg = xc_ref[:, M_D_INNER + g * N:M_D_INNER + (g + 1) * N].astype(bf16)
        cg = xc_ref[:, M_D_INNER + G * N + g * N:M_D_INNER + G * N + (g + 1) * N].astype(bf16)
        gmat = _dot_nt(cg, bg)
        yoff = _dot(cg, st_ref[g].astype(bf16))
        xds = []
        for r in range(R):
            h = g * R + r
            col = a_cum[:, h:h + 1]
            row = a_cum_t[h:h + 1, :]
            lmat = jnp.exp(jnp.where(tril, col - row, -jnp.inf))
            xh = xc_ref[:, h * P:(h + 1) * P]
            xdt_h = xh * dt[:, h:h + 1]
            ydiag = _dot((gmat * lmat).astype(bf16), xdt_h.astype(bf16))
            y_h = ydiag + yoff[:, r * P:(r + 1) * P] * e_cum[:, h:h + 1] + xh * dsk[:, h:h + 1]
            ybuf_ref[:, h * P:(h + 1) * P] = y_h
            xds.append((xdt_h * e_rest[:, h:h + 1]).astype(bf16))
        new = _dot_tn(bg, jnp.concatenate(xds, axis=1))
        dec = jnp.concatenate([jnp.broadcast_to(e_last[:, g * R + r:g * R + r + 1], (1, P)) for r in range(R)], axis=1)
        st_ref[g] = st_ref[g] * dec + new

    GW = M_D_INNER // G
    for g in range(G):
        yg = ybuf_ref[:, g * GW:(g + 1) * GW] * _silu(z_ref[:, g * GW:(g + 1) * GW])
        ms = jnp.mean(yg * yg, axis=-1, keepdims=True)
        y_ref[:, g * GW:(g + 1) * GW] = (yg * lax.rsqrt(ms + RMS_EPS) * nw_ref[:, g * GW:(g + 1) * GW]).astype(y_ref.dtype)

    @pl.when(c == pl.num_programs(1) - 1)
    def _():
        for g in range(G):
            for r in range(R):
                hout_ref[0, g * R + r] = st_ref[g, :, r * P:(r + 1) * P].T


def ssd_mixer_core(xbc, z, dt, row0, Bn, L, conv0, h0, conv_w, conv_b, dt_bias, a_log, d_skip, norm_w):
    Q = M_CHUNK if L % M_CHUNK == 0 else L
    nc = L // Q
    has_init = h0 is not None
    blk0 = row0 // Q
    row_map = lambda b, c: (blk0 + b * nc + c, 0)
    const2 = lambda b, c: (0, 0)
    in_specs = [
        pl.BlockSpec((Q, M_CONV_DIM), row_map),
        pl.BlockSpec((Q, M_D_INNER), row_map),
        pl.BlockSpec((Q, LANES), row_map),
    ]
    args = [xbc, z, dt]
    if has_init:
        in_specs += [
            pl.BlockSpec((1, M_CONV - 1, M_CONV_DIM), lambda b, c: (b, 0, 0)),
            pl.BlockSpec((1, M_N_HEADS, M_HEAD_DIM, M_D_STATE), lambda b, c: (b, 0, 0, 0)),
        ]
        args += [conv0, h0]
    in_specs += [
        pl.BlockSpec((M_CONV, M_CONV_DIM), const2),
        pl.BlockSpec((1, M_CONV_DIM), const2),
        pl.BlockSpec((1, LANES), const2),
        pl.BlockSpec((1, LANES), const2),
        pl.BlockSpec((1, LANES), const2),
        pl.BlockSpec((1, M_D_INNER), const2),
    ]
    args += [conv_w, conv_b, dt_bias, a_log, d_skip, norm_w]
    y_dtype = bf16 if Q % 16 == 0 else f32
    return pl.pallas_call(
        functools.partial(_ssd_body, Q=Q, has_init=has_init),
        out_shape=(
            jax.ShapeDtypeStruct((Bn * L, M_D_INNER), y_dtype),
            jax.ShapeDtypeStruct((Bn, M_N_HEADS, M_HEAD_DIM, M_D_STATE), f32),
            jax.ShapeDtypeStruct((Bn, M_CONV - 1, M_CONV_DIM), f32),
        ),
        grid=(Bn, nc),
        in_specs=in_specs,
        out_specs=(
            pl.BlockSpec((Q, M_D_INNER), lambda b, c: (b * nc + c, 0)),
            pl.BlockSpec((1, M_N_HEADS, M_HEAD_DIM, M_D_STATE), lambda b, c: (b, 0, 0, 0)),
            pl.BlockSpec((1, M_CONV - 1, M_CONV_DIM), lambda b, c: (b, 0, 0)),
        ),
        scratch_shapes=[
            pltpu.VMEM((M_N_GROUPS, M_D_STATE, M_HPG * M_HEAD_DIM), f32),
            pltpu.VMEM((Q + SUBLANES, M_CONV_DIM), f32),
            pltpu.VMEM((Q, M_CONV_DIM), f32),
            pltpu.VMEM((Q, M_D_INNER), f32),
        ],
        compiler_params=_cparams(("parallel", "arbitrary")),
        name="ssd_mixer_core",
    )(*args)


N_PAGES = 16
CHUNKS_PER_PAGE = PAGE_SIZE // CMP_STRIDE
PAGE_LANES = CMP_STRIDE * KV_ROW
N_CHUNKS = N_PAGES * CHUNKS_PER_PAGE


def _compress_body(pt_ref, *refs):
    del pt_ref
    pages = refs[:N_PAGES]
    wbd_ref, pe_ref, w1f_ref, w2_ref, kc_ref, vc_ref = refs[N_PAGES:]
    H = CMP_HIDDEN
    row = lax.broadcasted_iota(i32, (N_CHUNKS, H), 0)
    for kv, out_ref in enumerate((kc_ref, vc_ref)):
        pe_term = _dot(pe_ref[kv], w1f_ref[kv])[0:1, :]
        for gp in range(A_N_KV // 2):
            acc = jnp.zeros((N_CHUNKS, 2 * CMP_RATIO * H), f32)
            for j in range(CMP_STRIDE):
                t = (j * 2 + kv) * (A_N_KV // 2) + gp
                x = jnp.concatenate([p[0, :, t * LANES:(t + 1) * LANES] for p in pages], axis=0).astype(bf16)
                acc = acc + _dot(x, wbd_ref[kv, j])
            for gi in range(2):
                g = gp * 2 + gi
                p0 = acc[:, gi * CMP_RATIO * H:gi * CMP_RATIO * H + H]
                p1 = acc[:, gi * CMP_RATIO * H + H:(gi + 1) * CMP_RATIO * H]
                p1_next = jnp.where(row == N_CHUNKS - 1, 0.0, pltpu.roll(p1, N_CHUNKS - 1, 0))
                hid = _silu(p0 + p1_next + pe_term)
                out_ref[0, :, g * A_HEAD_DIM:(g + 1) * A_HEAD_DIM] = _dot(hid.astype(bf16), w2_ref[kv]).astype(out_ref.dtype)


def compress_kv(pages, page_table, wbd, pe_rows, w1_flat, w2):
    Bn = page_table.shape[0]
    page_specs = [
        pl.BlockSpec((1, CHUNKS_PER_PAGE, PAGE_LANES), functools.partial(lambda b, pt, j: (pt[b, j], 0, 0), j=j))
        for j in range(N_PAGES)
    ]
    const = lambda nd: (lambda b, pt: (0,) * nd)
    out_sds = jax.ShapeDtypeStruct((Bn, N_CHUNKS, A_N_KV * A_HEAD_DIM), bf16)
    out_spec = pl.BlockSpec((1, N_CHUNKS, A_N_KV * A_HEAD_DIM), lambda b, pt: (b, 0, 0))
    return pl.pallas_call(
        _compress_body,
        out_shape=(out_sds, out_sds),
        grid_spec=pltpu.PrefetchScalarGridSpec(
            num_scalar_prefetch=1,
            grid=(Bn,),
            in_specs=page_specs + [
                pl.BlockSpec(wbd.shape, const(4)),
                pl.BlockSpec(pe_rows.shape, const(3)),
                pl.BlockSpec(w1_flat.shape, const(3)),
                pl.BlockSpec(w2.shape, const(3)),
            ],
            out_specs=(out_spec, out_spec),
        ),
        compiler_params=_cparams(("parallel",)),
        name="compress_kv",
    )(page_table, *([pages] * N_PAGES), wbd, pe_rows, w1_flat, w2)


def _bias_table_body(rb_ref, tbl_ref):
    n = lax.broadcasted_iota(i32, (A_N_HEADS, LANES), 1)
    max_exact = N_BUCKETS // 2
    large = max_exact + (jnp.log(jnp.maximum(n, max_exact).astype(f32) / max_exact)
                         / math.log(MAX_DISTANCE / max_exact) * (N_BUCKETS - max_exact)).astype(i32)
    bucket = jnp.where(n < max_exact, n, jnp.minimum(large, N_BUCKETS - 1))
    tbl = jnp.zeros((A_N_HEADS, LANES), f32)
    for b in range(N_BUCKETS):
        tbl = jnp.where(bucket == b, rb_ref[:, b:b + 1], tbl)
    tbl_ref[...] = tbl


def bias_table(rel_bias_t):
    assert MAX_DISTANCE <= LANES
    return pl.pallas_call(
        _bias_table_body, out_shape=jax.ShapeDtypeStruct((A_N_HEADS, LANES), f32), name="bias_table",
    )(rel_bias_t)


TILE = 128
WIN_TILES = WINDOW // TILE


def _attn_body(*refs, qb, start, n_past_tiles, n_sel):
    paged = n_past_tiles > 0
    if paged:
        pt_ref = refs[0]
        del pt_ref
        refs = refs[1:]
        q_ref, gate_ref, kc_ref, vc_ref, tbl_ref, ovl_ref = refs[:6]
        sel_pages = refs[6:6 + n_past_tiles]
        selnew_ref, winpast_ref, winnew_ref, o_ref = refs[6 + n_past_tiles:]
    else:
        q_ref, gate_ref, kc_ref, vc_ref, tbl_ref, ovl_ref, selnew_ref, winnew_ref, o_ref = refs
    i = pl.program_id(1)
    R = A_GROUP * qb
    dh = A_HEAD_DIM
    GD = A_N_KV * dh
    pos0 = start + i * qb
    t0 = pos0 // TILE
    scale = dh ** -0.5

    lane = lax.broadcasted_iota(i32, (R, TILE), 1)
    q_in_blk = jnp.concatenate([lax.broadcasted_iota(i32, (qb, TILE), 0)] * A_GROUP, axis=0)
    qpos = pos0 + q_in_blk

    def new_tile(ref, t_new):
        if paged:
            rows = ref[...].astype(f32)
            return jnp.concatenate([rows, jnp.zeros((TILE - rows.shape[0], rows.shape[1]), f32)], axis=0).astype(bf16)
        return ref[pl.ds(pl.multiple_of(t_new * TILE, TILE), TILE), :]

    def update(st, qg, k, v, bias, mask):
        m, l, acc = st
        s = _dot_nt(qg, k) + bias
        s = jnp.where(mask, s, MASK_VALUE)
        m_new = jnp.maximum(m, jnp.max(s, axis=1, keepdims=True))
        alpha = jnp.exp(m - m_new)
        p = jnp.where(mask, jnp.exp(s - m_new), 0.0)
        l_new = alpha * l + jnp.sum(p, axis=1, keepdims=True)
        acc_new = alpha * acc + _dot(p.astype(bf16), v)
        return (m_new, l_new, acc_new), p

    def init_state():
        return (jnp.full((R, 1), MASK_VALUE, f32), jnp.zeros((R, 1), f32), jnp.zeros((R, dh), f32))

    def finish(st):
        _, l, acc = st
        return acc / jnp.where(l == 0.0, 1.0, l)

    sel_lane = lax.broadcasted_iota(i32, (qb, TILE), 1)
    q_pos_sel = pos0 + lax.broadcasted_iota(i32, (qb, TILE), 0)
    blk = q_pos_sel // SEL_BLOCK
    sel_valid = sel_lane * SEL_BLOCK <= q_pos_sel
    sel_forced = (sel_lane == 0) | (sel_lane == blk) | (sel_lane == blk - 1)
    e_row = lax.broadcasted_iota(i32, (TILE, TILE), 0)
    e_col = lax.broadcasted_iota(i32, (TILE, TILE), 1)
    blocks_per_tile = TILE // SEL_BLOCK

    for g in range(A_N_KV):
        heads = [g * A_GROUP + r for r in range(A_GROUP)]
        qg = jnp.concatenate([q_ref[:, h * dh:(h + 1) * dh] for h in heads], axis=0)
        qg = (qg.astype(f32) * scale).astype(bf16)
        tbl_g = jnp.concatenate([jnp.broadcast_to(tbl_ref[h:h + 1, :], (qb, TILE)) for h in heads], axis=0)
        far_bias = jnp.concatenate(
            [jnp.broadcast_to(tbl_ref[h:h + 1, MAX_DISTANCE - 1:MAX_DISTANCE], (qb, 1)) for h in heads], axis=0)

        def near_bias(dist):
            return jnp.take_along_axis(tbl_g, jnp.clip(dist, 0, MAX_DISTANCE - 1), axis=1)

        cmp_end = lane * CMP_STRIDE + (CMP_LEN - 1)
        dist_c = qpos - cmp_end
        st_c, p_c = update(init_state(), qg, kc_ref[0, :, g * dh:(g + 1) * dh], vc_ref[0, :, g * dh:(g + 1) * dh],
                           near_bias(dist_c), dist_c >= 0)
        o_c = finish(st_c)
        l_c = st_c[1]
        p_c = p_c / jnp.where(l_c == 0.0, 1.0, l_c)

        p_sum = p_c[0:qb]
        for r in range(1, A_GROUP):
            p_sum = p_sum + p_c[r * qb:(r + 1) * qb]
        imp = jnp.dot(p_sum, ovl_ref[...], preferred_element_type=f32, precision=_HI)
        score = jnp.where(sel_valid, imp + jnp.where(sel_forced, FORCE_SCORE, 0.0), -1.0)
        score = jnp.where(sel_lane < n_sel, score, -3.0)
        rank = jnp.zeros((qb, TILE), f32)
        for s2 in range(n_sel):
            col = score[:, s2:s2 + 1]
            beats = (col > score) | ((col == score) & (sel_lane > s2))
            rank = rank + jnp.where(beats, 1.0, 0.0)
        chosen = jnp.where(rank < float(min(SEL_TOPN, n_sel)), 1.0, 0.0).astype(bf16)

        def sel_mask(t, dist):
            expand = jnp.where(e_row == t * blocks_per_tile + e_col // SEL_BLOCK, 1.0, 0.0).astype(bf16)
            km = _dot(chosen, expand)
            km = jnp.concatenate([km] * A_GROUP, axis=0)
            return (km > 0.5) & (dist >= 0)

        def sel_kv(t):
            if paged and isinstance(t, int) and t < n_past_tiles:
                row = sel_pages[t][0]
                return row[:, g * dh:(g + 1) * dh].astype(bf16), row[:, GD + g * dh:GD + (g + 1) * dh].astype(bf16)
            tile = new_tile(selnew_ref, t - n_past_tiles)
            return tile[:, g * dh:(g + 1) * dh], tile[:, GD + g * dh:GD + (g + 1) * dh]

        def far_step(t, st):
            k, v = sel_kv(t)
            dist = qpos - (t * TILE + lane)
            return update(st, qg, k, v, far_bias, sel_mask(t, dist))[0]

        st_s = init_state()
        if paged:
            for t in range(n_past_tiles - 1):
                st_s = far_step(t, st_s)
            near_tiles = [(n_past_tiles - 1, True), (n_past_tiles, True)]
        else:
            st_s = lax.fori_loop(0, jnp.maximum(t0 - 1, 0), far_step, st_s)
            near_tiles = [(jnp.maximum(t0 - 1, 0), t0 >= 1), (t0, True)]
        for t, ok in near_tiles:
            k, v = sel_kv(t)
            dist = qpos - (t * TILE + lane)
            if ok is not True:
                dist = jnp.where(ok, dist, -1)
            st_s = update(st_s, qg, k, v, near_bias(dist), sel_mask(t, dist))[0]
        o_s = finish(st_s)

        st_w = init_state()
        for j in range(WIN_TILES + 1):
            t = t0 - WIN_TILES + j
            if paged and j < WIN_TILES:
                row = winpast_ref[0, j * TILE:(j + 1) * TILE, :]
                k, v = row[:, g * dh:(g + 1) * dh].astype(bf16), row[:, GD + g * dh:GD + (g + 1) * dh].astype(bf16)
                exists = True
            else:
                tile = new_tile(winnew_ref, jnp.maximum(t, 0) - n_past_tiles if not paged else 0)
                k, v = tile[:, g * dh:(g + 1) * dh], tile[:, GD + g * dh:GD + (g + 1) * dh]
                exists = True if paged else (t >= 0)
            dist = qpos - (t * TILE + lane)
            if exists is not True:
                dist = jnp.where(exists, dist, -1)
            mask = (dist >= 0) & (dist < WINDOW)
            bias = near_bias(dist) if j >= WIN_TILES - 1 else far_bias
            st_w = update(st_w, qg, k, v, bias, mask)[0]
        o_w = finish(st_w)

        for r, h in enumerate(heads):
            rows = slice(r * qb, (r + 1) * qb)
            gt = gate_ref[:, 3 * h:3 * h + 3]
            o_h = gt[:, 0:1] * o_c[rows] + gt[:, 1:2] * o_s[rows] + gt[:, 2:3] * o_w[rows]
            o_ref[:, h * dh:(h + 1) * dh] = o_h.astype(o_ref.dtype)


def nsa_attention(q, gates, row0, Bn, L, start, kc, vc, tbl, overlap, sel_new, win_new,
                  sel_pages=None, page_table=None, win_past=None):
    paged = sel_pages is not None
    qb = Q_BLOCK if L % Q_BLOCK == 0 else L
    nqb = L // qb
    n_past_tiles = N_PAGES if paged else 0
    n_sel = -(-(start + L) // SEL_BLOCK)
    assert start % TILE == 0 and (not paged or nqb == 1)
    blk0 = row0 // qb
    body = functools.partial(_attn_body, qb=qb, start=start, n_past_tiles=n_past_tiles, n_sel=n_sel)
    out_dtype = bf16 if qb % 16 == 0 else f32
    out_shape = jax.ShapeDtypeStruct((Bn * L, A_Q_DIM), out_dtype)
    if not paged:
        row_map = lambda b, i: (blk0 + b * nqb + i, 0)
        seq_map = lambda b, i: (b, 0)
        return pl.pallas_call(
            body,
            out_shape=out_shape,
            grid=(Bn, nqb),
            in_specs=[
                pl.BlockSpec((qb, A_Q_DIM), row_map),
                pl.BlockSpec((qb, LANES), row_map),
                pl.BlockSpec((1, N_CHUNKS, A_N_KV * A_HEAD_DIM), lambda b, i: (b, 0, 0)),
                pl.BlockSpec((1, N_CHUNKS, A_N_KV * A_HEAD_DIM), lambda b, i: (b, 0, 0)),
                pl.BlockSpec((A_N_HEADS, LANES), lambda b, i: (0, 0)),
                pl.BlockSpec((TILE, TILE), lambda b, i: (0, 0)),
                pl.BlockSpec((L, KV_ROW), seq_map),
                pl.BlockSpec((L, KV_ROW), seq_map),
            ],
            out_specs=pl.BlockSpec((qb, A_Q_DIM), lambda b, i: (b * nqb + i, 0)),
            compiler_params=_cparams(("parallel", "arbitrary")),
            name="nsa_attention_prompt",
        )(q, gates, kc, vc, tbl, overlap, sel_new, win_new)
    row_map = lambda b, i, pt: (blk0 + b, 0)
    page_specs = [
        pl.BlockSpec((1, PAGE_SIZE, KV_ROW), functools.partial(lambda b, i, pt, j: (pt[b, j], 0, 0), j=j))
        for j in range(N_PAGES)
    ]
    return pl.pallas_call(
        body,
        out_shape=out_shape,
        grid_spec=pltpu.PrefetchScalarGridSpec(
            num_scalar_prefetch=1,
            grid=(Bn, 1),
            in_specs=[
                pl.BlockSpec((qb, A_Q_DIM), row_map),
                pl.BlockSpec((qb, LANES), row_map),
                pl.BlockSpec((1, N_CHUNKS, A_N_KV * A_HEAD_DIM), lambda b, i, pt: (b, 0, 0)),
                pl.BlockSpec((1, N_CHUNKS, A_N_KV * A_HEAD_DIM), lambda b, i, pt: (b, 0, 0)),
                pl.BlockSpec((A_N_HEADS, LANES), lambda b, i, pt: (0, 0)),
                pl.BlockSpec((TILE, TILE), lambda b, i, pt: (0, 0)),
            ] + page_specs + [
                pl.BlockSpec((qb, KV_ROW), row_map),
                pl.BlockSpec((1, WINDOW, KV_ROW), lambda b, i, pt: (b, 0, 0)),
                pl.BlockSpec((qb, KV_ROW), row_map),
            ],
            out_specs=pl.BlockSpec((qb, A_Q_DIM), lambda b, i, pt: (b, 0)),
        ),
        compiler_params=_cparams(("parallel", "arbitrary")),
        name="nsa_attention_sample",
    )(page_table, q, gates, kc, vc, tbl, overlap, *([sel_pages] * N_PAGES), sel_new, win_past, win_new)


def _pad_lanes(v):
    return jnp.pad(v, (0, LANES - v.shape[0])).reshape(1, LANES)


def _overlap_matrix():
    n_cmp = N_CHUNKS - CMP_RATIO + 1
    c = np.arange(TILE)[:, None] * CMP_STRIDE
    s = np.arange(TILE)[None, :] * SEL_BLOCK
    ov = (c < s + SEL_BLOCK) & (c + CMP_LEN > s) & (np.arange(TILE)[:, None] < n_cmp)
    return jnp.asarray(ov.astype(np.float32))


def kernel(x_prompt, x_sample, state_ssm, state_conv, cache_cmp_kv, cache_sel_kv, cache_win_kv, page_table, ln_g, ln_b, m_in_w, m_conv_w, m_conv_b, m_dt_bias, m_a_log, m_d, m_norm_w, m_out_w, kv_w, cmp_w1, cmp_pe, cmp_w2, q_w, o_w, rel_bias, mlp_w1, mlp_w2):
    Bp, Lp, D = x_prompt.shape
    Bs, Ls, _ = x_sample.shape
    NP, NS = Bp * Lp, Bs * Ls
    past_len = page_table.shape[1] * PAGE_SIZE
    n_phys = cache_cmp_kv.shape[0]
    assert past_len == N_PAGES * PAGE_SIZE and Lp == N_PAGES * PAGE_SIZE and cache_win_kv.shape[1] == WINDOW

    in_w = m_in_w[0].astype(bf16)
    z_w = in_w[:, :M_D_INNER]
    xbc_w = in_w[:, M_D_INNER:M_D_INNER + M_CONV_DIM]
    dt_w = jnp.pad(in_w[:, M_D_INNER + M_CONV_DIM:], ((0, 0), (0, LANES - M_N_HEADS)))
    kvw = kv_w.astype(bf16)
    qw = q_w[0].astype(bf16)
    gate_w = jnp.pad(qw[:, A_Q_DIM:], ((0, 0), (0, LANES - 3 * A_N_HEADS)))
    w1b = cmp_w1.astype(bf16)
    w_j = jnp.transpose(w1b, (0, 2, 3, 1, 4)).reshape(2, CMP_STRIDE, A_HEAD_DIM, CMP_RATIO * CMP_HIDDEN)
    zeros = jnp.zeros_like(w_j)
    wbd = jnp.concatenate([jnp.concatenate([w_j, zeros], axis=3), jnp.concatenate([zeros, w_j], axis=3)], axis=2)
    pe_rows = jnp.broadcast_to(cmp_pe.astype(bf16).reshape(2, 1, CMP_LEN * A_HEAD_DIM), (2, SUBLANES, CMP_LEN * A_HEAD_DIM))
    w1_flat = w1b.reshape(2, CMP_LEN * A_HEAD_DIM, CMP_HIDDEN)
    w2b = cmp_w2.astype(bf16)

    x = jnp.concatenate([x_prompt.reshape(NP, D), x_sample.reshape(NS, D)], axis=0)
    xb = x.astype(bf16)
    z = matmul(xb, z_w, f32)
    xbc = matmul(xb, xbc_w, f32)
    dt = matmul(xb, dt_w, f32)
    ssd_w = (m_conv_w[0], m_conv_b[0].reshape(1, -1), _pad_lanes(m_dt_bias[0]), _pad_lanes(m_a_log[0]),
             _pad_lanes(m_d[0]), m_norm_w[0].reshape(1, -1))
    y_p, p_ssm, p_conv = ssd_mixer_core(xbc, z, dt, 0, Bp, Lp, None, None, *ssd_w)
    y_s, s_ssm, s_conv = ssd_mixer_core(xbc, z, dt, NP, Bs, Ls, state_conv[0], state_ssm[0], *ssd_w)
    y = jnp.concatenate([y_p, y_s.astype(bf16)], axis=0)
    h_f, h_b = matmul_residual_ln(y, m_out_w[0].astype(bf16), x, ln_g[0, 0].reshape(1, D), ln_b[0, 0].reshape(1, D))
    h_f, h_b = mlp_residual_ln(h_b, h_f, mlp_w1[0].astype(bf16), mlp_w2[0].astype(bf16),
                               ln_g[0, 1].reshape(1, D), ln_b[0, 1].reshape(1, D))

    new_cmp = matmul(h_b, kvw[:, 0:KV_ROW], f32)
    new_sel = matmul(h_b, kvw[:, KV_ROW:2 * KV_ROW], f32)
    new_win = matmul(h_b, kvw[:, 2 * KV_ROW:3 * KV_ROW], f32)
    kc_p, vc_p = compress_kv(new_cmp[:NP].reshape(NP // PAGE_SIZE, CHUNKS_PER_PAGE, PAGE_LANES),
                             jnp.arange(NP // PAGE_SIZE, dtype=i32).reshape(Bp, N_PAGES), wbd, pe_rows, w1_flat, w2b)
    kc_s, vc_s = compress_kv(cache_cmp_kv.reshape(n_phys, CHUNKS_PER_PAGE, PAGE_LANES), page_table,
                             wbd, pe_rows, w1_flat, w2b)

    q = matmul(h_b, qw[:, :A_Q_DIM], f32)
    gates = matmul(h_b, gate_w, f32, act="sigmoid")
    tbl = bias_table(rel_bias.T)
    overlap = _overlap_matrix()
    o_p = nsa_attention(q, gates, 0, Bp, Lp, 0, kc_p, vc_p, tbl, overlap, new_sel.astype(bf16), new_win.astype(bf16))
    o_s = nsa_attention(q, gates, NP, Bs, Ls, past_len, kc_s, vc_s, tbl, overlap, new_sel, new_win,
                        sel_pages=cache_sel_kv.reshape(n_phys, PAGE_SIZE, KV_ROW), page_table=page_table,
                        win_past=cache_win_kv.reshape(Bs, WINDOW, KV_ROW))
    o = jnp.concatenate([o_p, o_s.astype(bf16)], axis=0)
    h_f, h_b = matmul_residual_ln(o, o_w[0].astype(bf16), h_f, ln_g[1, 0].reshape(1, D), ln_b[1, 0].reshape(1, D))
    h_f, _ = mlp_residual_ln(h_b, h_f, mlp_w1[1].astype(bf16), mlp_w2[1].astype(bf16),
                             ln_g[1, 1].reshape(1, D), ln_b[1, 1].reshape(1, D))

    kv_shape = (2, A_N_KV, A_HEAD_DIM)
    n_keep = min(WINDOW, Lp)
    p_win = new_win[:NP].reshape(Bp, Lp, KV_ROW)[:, Lp - n_keep:].reshape((Bp, n_keep) + kv_shape)
    s_win_new = new_win[NP:].reshape((Bs, Ls) + kv_shape)
    s_win = jnp.concatenate([cache_win_kv[:, Ls:], s_win_new], axis=1)
    return (
        h_f[:NP].reshape(Bp, Lp, D), h_f[NP:].reshape(Bs, Ls, D),
        p_ssm[None], p_conv[None],
        new_cmp[:NP].reshape((Bp, Lp) + kv_shape), new_sel[:NP].reshape((Bp, Lp) + kv_shape), p_win,
        s_ssm[None], s_conv[None],
        new_cmp[NP:].reshape((Bs, Ls) + kv_shape), new_sel[NP:].reshape((Bs, Ls) + kv_shape), s_win,
    )
```

```python
import functools
import math

import jax
import jax.numpy as jnp
import numpy as np
from jax import lax
from jax.experimental import pallas as pl
from jax.experimental.pallas import tpu as pltpu

f32 = jnp.float32
bf16 = jnp.bfloat16
i32 = jnp.int32

D_MODEL = 1024
DEPTH = 2
DN_ALPHA = (2.0 * DEPTH) ** 0.25
LN_EPS = 1e-5
RMS_EPS = 1e-5
D_FF = 4 * D_MODEL
M_D_INNER = 2 * D_MODEL
M_HEAD_DIM = 64
M_N_HEADS = M_D_INNER // M_HEAD_DIM
M_N_GROUPS = 4
M_HPG = M_N_HEADS // M_N_GROUPS
M_D_STATE = 128
M_CONV = 4
M_CHUNK = 128
M_CONV_DIM = M_D_INNER + 2 * M_N_GROUPS * M_D_STATE
A_HEAD_DIM = 64
A_N_HEADS = D_MODEL // A_HEAD_DIM
A_N_KV = 4
A_GROUP = A_N_HEADS // A_N_KV
A_Q_DIM = A_N_HEADS * A_HEAD_DIM
KV_HALF = A_N_KV * A_HEAD_DIM
KV_ROW = 2 * KV_HALF
CMP_LEN = 32
CMP_STRIDE = 16
CMP_RATIO = CMP_LEN // CMP_STRIDE
CMP_HIDDEN = 2 * A_HEAD_DIM
SEL_BLOCK = 64
SEL_TOPN = 16
WINDOW = 512
Q_BLOCK = 128
N_BUCKETS = 32
MAX_DISTANCE = 128
MASK_VALUE = -1e30
FORCE_SCORE = 1e3
PAGE_SIZE = 128

LANES = 128
SUBLANES = 8
VMEM_LIMIT = 56 * 1024 * 1024

_HI = lax.Precision.HIGHEST


def _cparams(sem):
    return pltpu.CompilerParams(dimension_semantics=sem, vmem_limit_bytes=VMEM_LIMIT)


def _dot(a, b):
    return jnp.dot(a, b, preferred_element_type=f32)


def _dot_nt(a, b):
    return lax.dot_general(a, b, (((1,), (1,)), ((), ())), preferred_element_type=f32)


def _dot_tn(a, b):
    return lax.dot_general(a, b, (((0,), (0,)), ((), ())), preferred_element_type=f32)


def _silu(x):
    return x * (1.0 / (1.0 + jnp.exp(-x)))


def _layer_norm(x, g, b):
    mu = jnp.mean(x, axis=-1, keepdims=True)
    xc = x - mu
    var = jnp.mean(xc * xc, axis=-1, keepdims=True)
    return xc * lax.rsqrt(var + LN_EPS) * g + b


def _mm_body(x_ref, w_ref, o_ref, *, act):
    y = _dot(x_ref[...], w_ref[...])
    if act == "sigmoid":
        y = 1.0 / (1.0 + jnp.exp(-y))
    o_ref[...] = y.astype(o_ref.dtype)


def matmul(x, w, out_dtype, act=None, tm=512, tn=512):
    M, K = x.shape
    N = w.shape[1]
    tn = min(tn, N)
    return pl.pallas_call(
        functools.partial(_mm_body, act=act),
        out_shape=jax.ShapeDtypeStruct((M, N), out_dtype),
        grid=(M // tm, N // tn),
        in_specs=[pl.BlockSpec((tm, K), lambda i, j: (i, 0)), pl.BlockSpec((K, tn), lambda i, j: (0, j))],
        out_specs=pl.BlockSpec((tm, tn), lambda i, j: (i, j)),
        compiler_params=_cparams(("parallel", "parallel")),
        name="matmul",
    )(x, w)


def _mm_res_ln_body(x_ref, w_ref, r_ref, g_ref, b_ref, of_ref, ob_ref):
    y = DN_ALPHA * r_ref[...] + _dot(x_ref[...], w_ref[...])
    h = _layer_norm(y, g_ref[...], b_ref[...])
    of_ref[...] = h
    ob_ref[...] = h.astype(bf16)


def matmul_residual_ln(x, w, resid, g, b, tm=512):
    M, K = x.shape
    N = w.shape[1]
    return pl.pallas_call(
        _mm_res_ln_body,
        out_shape=(jax.ShapeDtypeStruct((M, N), f32), jax.ShapeDtypeStruct((M, N), bf16)),
        grid=(M // tm,),
        in_specs=[
            pl.BlockSpec((tm, K), lambda i: (i, 0)),
            pl.BlockSpec((K, N), lambda i: (0, 0)),
            pl.BlockSpec((tm, N), lambda i: (i, 0)),
            pl.BlockSpec((1, N), lambda i: (0, 0)),
            pl.BlockSpec((1, N), lambda i: (0, 0)),
        ],
        out_specs=(pl.BlockSpec((tm, N), lambda i: (i, 0)), pl.BlockSpec((tm, N), lambda i: (i, 0))),
        compiler_params=_cparams(("parallel",)),
        name="matmul_residual_ln",
    )(x, w, resid, g, b)


def _mlp_body(hb_ref, hf_ref, w1_ref, w2_ref, g_ref, b_ref, of_ref, ob_ref, acc_ref):
    j = pl.program_id(1)

    @pl.when(j == 0)
    def _():
        acc_ref[...] = jnp.zeros_like(acc_ref)

    u = jnp.maximum(_dot(hb_ref[...], w1_ref[...]), 0.0)
    acc_ref[...] += _dot((u * u).astype(bf16), w2_ref[...])

    @pl.when(j == pl.num_programs(1) - 1)
    def _():
        h = _layer_norm(DN_ALPHA * hf_ref[...] + acc_ref[...], g_ref[...], b_ref[...])
        of_ref[...] = h
        ob_ref[...] = h.astype(bf16)


def mlp_residual_ln(hb, hf, w1, w2, g, b, tm=512, tf=512):
    M, D = hb.shape
    F = w1.shape[1]
    return pl.pallas_call(
        _mlp_body,
        out_shape=(jax.ShapeDtypeStruct((M, D), f32), jax.ShapeDtypeStruct((M, D), bf16)),
        grid=(M // tm, F // tf),
        in_specs=[
            pl.BlockSpec((tm, D), lambda i, j: (i, 0)),
            pl.BlockSpec((tm, D), lambda i, j: (i, 0)),
            pl.BlockSpec((D, tf), lambda i, j: (0, j)),
            pl.BlockSpec((tf, D), lambda i, j: (j, 0)),
            pl.BlockSpec((1, D), lambda i, j: (0, 0)),
            pl.BlockSpec((1, D), lambda i, j: (0, 0)),
        ],
        out_specs=(pl.BlockSpec((tm, D), lambda i, j: (i, 0)), pl.BlockSpec((tm, D), lambda i, j: (i, 0))),
        scratch_shapes=[pltpu.VMEM((tm, D), f32)],
        compiler_params=_cparams(("parallel", "arbitrary")),
        name="mlp_residual_ln",
    )(hb, hf, w1, w2, g, b)


def _kv_project_body(wt_ref, h_ref, cmp_ref, sel_ref, win_ref, selp_ref, winp_ref):
    res = _dot_nt(wt_ref[...], h_ref[...])
    tm = h_ref.shape[0]
    cmp_ref[0] = res[0:KV_ROW]
    sel_ref[0] = res[KV_ROW:2 * KV_ROW]
    win_ref[0] = res[2 * KV_ROW:3 * KV_ROW]
    for k in range(tm // PAGE_SIZE):
        cols = slice(k * PAGE_SIZE, (k + 1) * PAGE_SIZE)
        selp_ref[0, k] = res[KV_ROW:2 * KV_ROW, cols].astype(bf16)
        winp_ref[0, k] = res[2 * KV_ROW:3 * KV_ROW, cols].astype(bf16)


def kv_project_feature_major(h_b, w_t, Bn, L, tm=512):
    nj = L // tm
    pages_per_step = tm // PAGE_SIZE
    fm = jax.ShapeDtypeStruct((Bn, KV_ROW, L), f32)
    pg = jax.ShapeDtypeStruct((Bn, L // PAGE_SIZE, KV_ROW, PAGE_SIZE), bf16)
    fm_spec = pl.BlockSpec((1, KV_ROW, tm), lambda b, j: (b, 0, j))
    pg_spec = pl.BlockSpec((1, pages_per_step, KV_ROW, PAGE_SIZE), lambda b, j: (b, j, 0, 0))
    return pl.pallas_call(
        _kv_project_body,
        out_shape=(fm, fm, fm, pg, pg),
        grid=(Bn, nj),
        in_specs=[
            pl.BlockSpec(w_t.shape, lambda b, j: (0, 0)),
            pl.BlockSpec((tm, h_b.shape[1]), lambda b, j: (b * nj + j, 0)),
        ],
        out_specs=(fm_spec, fm_spec, fm_spec, pg_spec, pg_spec),
        compiler_params=_cparams(("parallel", "parallel")),
        name="kv_project_feature_major",
    )(w_t, h_b)


def _ssd_body(*refs, Q, has_init):
    if has_init:
        (xbc_ref, z_ref, dt_ref, conv0_ref, h0_ref, cw_ref, cb_ref, dtb_ref, alog_ref, dsk_ref, nw_ref,
         y_ref, hout_ref, cout_ref, st_ref, xpad_ref, xc_ref, ybuf_ref) = refs
    else:
        (xbc_ref, z_ref, dt_ref, cw_ref, cb_ref, dtb_ref, alog_ref, dsk_ref, nw_ref,
         y_ref, hout_ref, cout_ref, st_ref, xpad_ref, xc_ref, ybuf_ref) = refs
    c = pl.program_id(1)
    P, N, R, G = M_HEAD_DIM, M_D_STATE, M_HPG, M_N_GROUPS
    PAD = SUBLANES

    @pl.when(c == 0)
    def _():
        xpad_ref[0:PAD, :] = jnp.zeros((PAD, M_CONV_DIM), f32)
        if has_init:
            xpad_ref[PAD - (M_CONV - 1):PAD, :] = conv0_ref[0]
            for g in range(G):
                for r in range(R):
                    st_ref[g, :, r * P:(r + 1) * P] = h0_ref[0, g * R + r].T
        else:
            st_ref[...] = jnp.zeros_like(st_ref)

    xpad_ref[PAD:PAD + Q, :] = xbc_ref[...]
    acc = cb_ref[...] + xpad_ref[pl.ds(PAD - 3, Q), :] * cw_ref[0:1, :]
    for k in range(1, M_CONV):
        acc = acc + xpad_ref[pl.ds(PAD - 3 + k, Q), :] * cw_ref[k:k + 1, :]
    xc_ref[...] = _silu(acc)
    cout_ref[0] = xpad_ref[pl.ds(Q + PAD - 3, 3), :]
    xpad_ref[0:PAD, :] = xpad_ref[pl.ds(Q, PAD), :]

    xdt = dt_ref[...] + dtb_ref[...]
    dt = jnp.maximum(xdt, 0.0) + jnp.log1p(jnp.exp(-jnp.abs(xdt)))
    a = dt * (-jnp.exp(alog_ref[...]))
    ri = lax.broadcasted_iota(i32, (Q, Q), 0)
    ci = lax.broadcasted_iota(i32, (Q, Q), 1)
    tril = ri >= ci
    a_cum = jnp.dot(tril.astype(f32), a, preferred_element_type=f32, precision=_HI)
    a_cum_t = a_cum.T
    a_last = a_cum[Q - 1:Q, :]
    e_cum = jnp.exp(a_cum)
    e_rest = jnp.exp(a_last - a_cum)
    e_last = jnp.exp(a_last)
    dsk = dsk_ref[...]

    for g in range(G):
        bg = xc_ref[:, M_D_INNER + g * N:M_D_INNER + (g + 1) * N].astype(bf16)
        cg = xc_ref[:, M_D_INNER + G * N + g * N:M_D_INNER + G * N + (g + 1) * N].astype(bf16)
        gmat = _dot_nt(cg, bg)
        yoff = _dot(cg, st_ref[g].astype(bf16))
        xds = []
        for r in range(R):
            h = g * R + r
            col = a_cum[:, h:h + 1]
            row = a_cum_t[h:h + 1, :]
            lmat = jnp.exp(jnp.where(tril, col - row, -jnp.inf))
            xh = xc_ref[:, h * P:(h + 1) * P]
            xdt_h = xh * dt[:, h:h + 1]
            ydiag = _dot((gmat * lmat).astype(bf16), xdt_h.astype(bf16))
            y_h = ydiag + yoff[:, r * P:(r + 1) * P] * e_cum[:, h:h + 1] + xh * dsk[:, h:h + 1]
            ybuf_ref[:, h * P:(h + 1) * P] = y_h
            xds.append((xdt_h * e_rest[:, h:h + 1]).astype(bf16))
        new = _dot_tn(bg, jnp.concatenate(xds, axis=1))
        dec = jnp.concatenate([jnp.broadcast_to(e_last[:, g * R + r:g * R + r + 1], (1, P)) for r in range(R)], axis=1)
        st_ref[g] = st_ref[g] * dec + new

    GW = M_D_INNER // G
    for g in range(G):
        yg = ybuf_ref[:, g * GW:(g + 1) * GW] * _silu(z_ref[:, g * GW:(g + 1) * GW])
        ms = jnp.mean(yg * yg, axis=-1, keepdims=True)
        y_ref[:, g * GW:(g + 1) * GW] = (yg * lax.rsqrt(ms + RMS_EPS) * nw_ref[:, g * GW:(g + 1) * GW]).astype(y_ref.dtype)

    @pl.when(c == pl.num_programs(1) - 1)
    def _():
        for g in range(G):
            for r in range(R):
                hout_ref[0, g * R + r] = st_ref[g, :, r * P:(r + 1) * P].T


def ssd_mixer_core(xbc, z, dt, row0, Bn, L, conv0, h0, conv_w, conv_b, dt_bias, a_log, d_skip, norm_w):
    Q = M_CHUNK if L % M_CHUNK == 0 else L
    nc = L // Q
    has_init = h0 is not None
    blk0 = row0 // Q
    row_map = lambda b, c: (blk0 + b * nc + c, 0)
    const2 = lambda b, c: (0, 0)
    in_specs = [
        pl.BlockSpec((Q, M_CONV_DIM), row_map),
        pl.BlockSpec((Q, M_D_INNER), row_map),
        pl.BlockSpec((Q, LANES), row_map),
    ]
    args = [xbc, z, dt]
    if has_init:
        in_specs += [
            pl.BlockSpec((1, M_CONV - 1, M_CONV_DIM), lambda b, c: (b, 0, 0)),
            pl.BlockSpec((1, M_N_HEADS, M_HEAD_DIM, M_D_STATE), lambda b, c: (b, 0, 0, 0)),
        ]
        args += [conv0, h0]
    in_specs += [
        pl.BlockSpec((M_CONV, M_CONV_DIM), const2),
        pl.BlockSpec((1, M_CONV_DIM), const2),
        pl.BlockSpec((1, LANES), const2),
        pl.BlockSpec((1, LANES), const2),
        pl.BlockSpec((1, LANES), const2),
        pl.BlockSpec((1, M_D_INNER), const2),
    ]
    args += [conv_w, conv_b, dt_bias, a_log, d_skip, norm_w]
    y_dtype = bf16 if Q % 16 == 0 else f32
    return pl.pallas_call(
        functools.partial(_ssd_body, Q=Q, has_init=has_init),
        out_shape=(
            jax.ShapeDtypeStruct((Bn * L, M_D_INNER), y_dtype),
            jax.ShapeDtypeStruct((Bn, M_N_HEADS, M_HEAD_DIM, M_D_STATE), f32),
            jax.ShapeDtypeStruct((Bn, M_CONV - 1, M_CONV_DIM), f32),
        ),
        grid=(Bn, nc),
        in_specs=in_specs,
        out_specs=(
            pl.BlockSpec((Q, M_D_INNER), lambda b, c: (b * nc + c, 0)),
            pl.BlockSpec((1, M_N_HEADS, M_HEAD_DIM, M_D_STATE), lambda b, c: (b, 0, 0, 0)),
            pl.BlockSpec((1, M_CONV - 1, M_CONV_DIM), lambda b, c: (b, 0, 0)),
        ),
        scratch_shapes=[
            pltpu.VMEM((M_N_GROUPS, M_D_STATE, M_HPG * M_HEAD_DIM), f32),
            pltpu.VMEM((Q + SUBLANES, M_CONV_DIM), f32),
            pltpu.VMEM((Q, M_CONV_DIM), f32),
            pltpu.VMEM((Q, M_D_INNER), f32),
        ],
        compiler_params=_cparams(("parallel", "arbitrary")),
        name="ssd_mixer_core",
    )(*args)


N_PAGES = 16
N_CHUNKS = N_PAGES * PAGE_SIZE // CMP_STRIDE
N_SLABS = KV_ROW // LANES


def _compress_body(*refs, paged):
    if paged:
        refs = refs[1:]
    pages = refs[:N_PAGES]
    wbd_ref, pe_ref, w1f_ref, w2_ref, w2t_ref, kct_ref, vc_ref, xs_ref = refs[N_PAGES:]
    H = CMP_HIDDEN
    for p in range(N_PAGES):
        for sl in range(N_SLABS):
            xs_ref[sl, p * PAGE_SIZE:(p + 1) * PAGE_SIZE, :] = pages[p][0, sl * LANES:(sl + 1) * LANES, :].T
    row = lax.broadcasted_iota(i32, (N_CHUNKS, H), 0)
    for kv in range(2):
        pe_term = _dot(pe_ref[kv], w1f_ref[kv])[0:1, :]
        for gp in range(A_N_KV // 2):
            acc = jnp.zeros((N_CHUNKS, 2 * CMP_RATIO * H), f32)
            for j in range(CMP_STRIDE):
                x = xs_ref[kv * (A_N_KV // 2) + gp, pl.ds(j, N_CHUNKS, stride=CMP_STRIDE), :].astype(bf16)
                acc = acc + _dot(x, wbd_ref[kv, j])
            for gi in range(2):
                g = gp * 2 + gi
                p0 = acc[:, gi * CMP_RATIO * H:gi * CMP_RATIO * H + H]
                p1 = acc[:, gi * CMP_RATIO * H + H:(gi + 1) * CMP_RATIO * H]
                p1_next = jnp.where(row == N_CHUNKS - 1, 0.0, pltpu.roll(p1, N_CHUNKS - 1, 0))
                hid = _silu(p0 + p1_next + pe_term).astype(bf16)
                if kv == 0:
                    kct_ref[0, g * A_HEAD_DIM:(g + 1) * A_HEAD_DIM, :] = _dot_nt(w2t_ref[...], hid).astype(kct_ref.dtype)
                else:
                    vc_ref[0, :, g * A_HEAD_DIM:(g + 1) * A_HEAD_DIM] = _dot(hid, w2_ref[...]).astype(vc_ref.dtype)


def compress_kv(pages, page_table, wbd, pe_rows, w1_flat, w2_v, w2t_k):
    paged = page_table is not None
    Bn = page_table.shape[0] if paged else pages.shape[0]
    if paged:
        page_specs = [
            pl.BlockSpec((1, KV_ROW, PAGE_SIZE), functools.partial(lambda b, pt, j: (pt[b, j], 0, 0), j=j))
            for j in range(N_PAGES)
        ]
        const = lambda nd: (lambda b, pt: (0,) * nd)
        out_map = lambda b, pt: (b, 0, 0)
    else:
        page_specs = [
            pl.BlockSpec((1, KV_ROW, PAGE_SIZE), functools.partial(lambda b, j: (b, 0, j), j=j)) for j in range(N_PAGES)
        ]
        const = lambda nd: (lambda b: (0,) * nd)
        out_map = lambda b: (b, 0, 0)
    in_specs = page_specs + [
        pl.BlockSpec(wbd.shape, const(4)),
        pl.BlockSpec(pe_rows.shape, const(3)),
        pl.BlockSpec(w1_flat.shape, const(3)),
        pl.BlockSpec(w2_v.shape, const(2)),
        pl.BlockSpec(w2t_k.shape, const(2)),
    ]
    out_shape = (jax.ShapeDtypeStruct((Bn, KV_HALF, N_CHUNKS), bf16), jax.ShapeDtypeStruct((Bn, N_CHUNKS, KV_HALF), bf16))
    out_specs = (pl.BlockSpec((1, KV_HALF, N_CHUNKS), out_map), pl.BlockSpec((1, N_CHUNKS, KV_HALF), out_map))
    scratch = [pltpu.VMEM((N_SLABS, N_PAGES * PAGE_SIZE, LANES), f32)]
    body = functools.partial(_compress_body, paged=paged)
    args = ([pages] * N_PAGES) + [wbd, pe_rows, w1_flat, w2_v, w2t_k]
    if paged:
        return pl.pallas_call(
            body, out_shape=out_shape,
            grid_spec=pltpu.PrefetchScalarGridSpec(
                num_scalar_prefetch=1, grid=(Bn,), in_specs=in_specs, out_specs=out_specs, scratch_shapes=scratch),
            compiler_params=_cparams(("parallel",)), name="compress_kv_paged",
        )(page_table, *args)
    return pl.pallas_call(
        body, out_shape=out_shape, grid=(Bn,), in_specs=in_specs, out_specs=out_specs, scratch_shapes=scratch,
        compiler_params=_cparams(("parallel",)), name="compress_kv",
    )(*args)


def _bias_table_body(rb_ref, tbl_ref):
    n = lax.broadcasted_iota(i32, (A_N_HEADS, LANES), 1)
    max_exact = N_BUCKETS // 2
    large = max_exact + (jnp.log(jnp.maximum(n, max_exact).astype(f32) / max_exact)
                         / math.log(MAX_DISTANCE / max_exact) * (N_BUCKETS - max_exact)).astype(i32)
    bucket = jnp.where(n < max_exact, n, jnp.minimum(large, N_BUCKETS - 1))
    tbl = jnp.zeros((A_N_HEADS, LANES), f32)
    for b in range(N_BUCKETS):
        tbl = jnp.where(bucket == b, rb_ref[:, b:b + 1], tbl)
    tbl_ref[...] = tbl


def bias_table(rel_bias_t):
    assert MAX_DISTANCE <= LANES
    return pl.pallas_call(
        _bias_table_body, out_shape=jax.ShapeDtypeStruct((A_N_HEADS, LANES), f32), name="bias_table",
    )(rel_bias_t)


TILE = 128
WIN_TILES = WINDOW // TILE


def _attn_body(q_ref, gate_ref, kct_ref, vc_ref, tbl_ref, ovl_ref, selp_ref, winp_ref, o_ref, *, qb, n_sel):
    i = pl.program_id(1)
    R = A_GROUP * qb
    dh = A_HEAD_DIM
    pos0 = i * qb
    t0 = pos0 // TILE
    scale = dh ** -0.5

    lane = lax.broadcasted_iota(i32, (R, TILE), 1)
    q_in_blk = jnp.concatenate([lax.broadcasted_iota(i32, (qb, TILE), 0)] * A_GROUP, axis=0)
    qpos = pos0 + q_in_blk

    def update(st, qg, kt, v, bias, mask, v_feature_major=True):
        m, l, acc = st
        s = _dot(qg, kt) + bias
        s = jnp.where(mask, s, MASK_VALUE)
        m_new = jnp.maximum(m, jnp.max(s, axis=1, keepdims=True))
        alpha = jnp.exp(m - m_new)
        p = jnp.where(mask, jnp.exp(s - m_new), 0.0)
        l_new = alpha * l + jnp.sum(p, axis=1, keepdims=True)
        pv = _dot_nt(p.astype(bf16), v) if v_feature_major else _dot(p.astype(bf16), v)
        return (m_new, l_new, alpha * acc + pv), p

    def init_state():
        return (jnp.full((R, 1), MASK_VALUE, f32), jnp.zeros((R, 1), f32), jnp.zeros((R, dh), f32))

    def finish(st):
        _, l, acc = st
        return acc / jnp.where(l == 0.0, 1.0, l)

    sel_lane = lax.broadcasted_iota(i32, (qb, TILE), 1)
    q_pos_sel = pos0 + lax.broadcasted_iota(i32, (qb, TILE), 0)
    blk = q_pos_sel // SEL_BLOCK
    sel_valid = sel_lane * SEL_BLOCK <= q_pos_sel
    sel_forced = (sel_lane == 0) | (sel_lane == blk) | (sel_lane == blk - 1)
    e_row = lax.broadcasted_iota(i32, (TILE, TILE), 0)
    e_col = lax.broadcasted_iota(i32, (TILE, TILE), 1)
    blocks_per_tile = TILE // SEL_BLOCK

    def page_kv(ref, t, g):
        return ref[0, t, g * dh:(g + 1) * dh, :], ref[0, t, KV_HALF + g * dh:KV_HALF + (g + 1) * dh, :]

    for g in range(A_N_KV):
        heads = [g * A_GROUP + r for r in range(A_GROUP)]
        qg = jnp.concatenate([q_ref[:, h * dh:(h + 1) * dh] for h in heads], axis=0)
        qg = (qg.astype(f32) * scale).astype(bf16)
        tbl_g = jnp.concatenate([jnp.broadcast_to(tbl_ref[h:h + 1, :], (qb, TILE)) for h in heads], axis=0)
        far_bias = jnp.concatenate(
            [jnp.broadcast_to(tbl_ref[h:h + 1, MAX_DISTANCE - 1:MAX_DISTANCE], (qb, 1)) for h in heads], axis=0)

        def near_bias(dist):
            return jnp.take_along_axis(tbl_g, jnp.clip(dist, 0, MAX_DISTANCE - 1), axis=1)

        cmp_end = lane * CMP_STRIDE + (CMP_LEN - 1)
        dist_c = qpos - cmp_end
        st_c, p_c = update(init_state(), qg, kct_ref[0, g * dh:(g + 1) * dh, :], vc_ref[0, :, g * dh:(g + 1) * dh],
                           near_bias(dist_c), dist_c >= 0, v_feature_major=False)
        o_c = finish(st_c)
        l_c = st_c[1]
        p_c = p_c / jnp.where(l_c == 0.0, 1.0, l_c)

        p_sum = p_c[0:qb]
        for r in range(1, A_GROUP):
            p_sum = p_sum + p_c[r * qb:(r + 1) * qb]
        imp = jnp.dot(p_sum, ovl_ref[...], preferred_element_type=f32, precision=_HI)
        score = jnp.where(sel_valid, imp + jnp.where(sel_forced, FORCE_SCORE, 0.0), -1.0)
        score = jnp.where(sel_lane < n_sel, score, -3.0)
        rank = jnp.zeros((qb, TILE), f32)
        for s2 in range(n_sel):
            col = score[:, s2:s2 + 1]
            beats = (col > score) | ((col == score) & (sel_lane > s2))
            rank = rank + jnp.where(beats, 1.0, 0.0)
        chosen = jnp.where(rank < float(min(SEL_TOPN, n_sel)), 1.0, 0.0).astype(bf16)

        def sel_mask(t, dist):
            expand = jnp.where(e_row == t * blocks_per_tile + e_col // SEL_BLOCK, 1.0, 0.0).astype(bf16)
            km = _dot(chosen, expand)
            km = jnp.concatenate([km] * A_GROUP, axis=0)
            return (km > 0.5) & (dist >= 0)

        def far_step(t, st):
            k, v = page_kv(selp_ref, t, g)
            dist = qpos - (t * TILE + lane)
            return update(st, qg, k, v, far_bias, sel_mask(t, dist))[0]

        st_s = lax.fori_loop(0, jnp.maximum(t0 - 1, 0), far_step, init_state())
        for t, ok in [(jnp.maximum(t0 - 1, 0), t0 >= 1), (t0, True)]:
            k, v = page_kv(selp_ref, t, g)
            dist = qpos - (t * TILE + lane)
            if ok is not True:
                dist = jnp.where(ok, dist, -1)
            st_s = update(st_s, qg, k, v, near_bias(dist), sel_mask(t, dist))[0]
        o_s = finish(st_s)

        st_w = init_state()
        for j in range(WIN_TILES + 1):
            t = t0 - WIN_TILES + j
            k, v = page_kv(winp_ref, jnp.maximum(t, 0), g)
            dist = jnp.where(t >= 0, qpos - (t * TILE + lane), -1)
            mask = (dist >= 0) & (dist < WINDOW)
            bias = near_bias(dist) if j >= WIN_TILES - 1 else far_bias
            st_w = update(st_w, qg, k, v, bias, mask)[0]
        o_w = finish(st_w)

        for r, h in enumerate(heads):
            rows = slice(r * qb, (r + 1) * qb)
            gt = gate_ref[:, 3 * h:3 * h + 3]
            o_h = gt[:, 0:1] * o_c[rows] + gt[:, 1:2] * o_s[rows] + gt[:, 2:3] * o_w[rows]
            o_ref[:, h * dh:(h + 1) * dh] = o_h.astype(o_ref.dtype)


def nsa_attention_prompt(q, gates, Bn, L, kct, vc, tbl, overlap, sel_pages, win_pages):
    qb = Q_BLOCK
    nqb = L // qb
    n_sel = -(-L // SEL_BLOCK)
    n_tiles = L // TILE
    row_map = lambda b, i: (b * nqb + i, 0)
    seq_map3 = lambda b, i: (b, 0, 0)
    seq_map4 = lambda b, i: (b, 0, 0, 0)
    return pl.pallas_call(
        functools.partial(_attn_body, qb=qb, n_sel=n_sel),
        out_shape=jax.ShapeDtypeStruct((Bn * L, A_Q_DIM), bf16),
        grid=(Bn, nqb),
        in_specs=[
            pl.BlockSpec((qb, A_Q_DIM), row_map),
            pl.BlockSpec((qb, LANES), row_map),
            pl.BlockSpec((1, KV_HALF, N_CHUNKS), seq_map3),
            pl.BlockSpec((1, N_CHUNKS, KV_HALF), seq_map3),
            pl.BlockSpec((A_N_HEADS, LANES), lambda b, i: (0, 0)),
            pl.BlockSpec((TILE, TILE), lambda b, i: (0, 0)),
            pl.BlockSpec((1, n_tiles, KV_ROW, TILE), seq_map4),
            pl.BlockSpec((1, n_tiles, KV_ROW, TILE), seq_map4),
        ],
        out_specs=pl.BlockSpec((qb, A_Q_DIM), row_map),
        compiler_params=_cparams(("parallel", "arbitrary")),
        name="nsa_attention_prompt",
    )(q, gates, kct, vc, tbl, overlap, sel_pages, win_pages)


def _attn_sample_body(pt_ref, q_ref, gate_ref, kct_ref, vc_ref, tbl_ref, ovl_ref, *refs, qb, start, n_sel):
    del pt_ref
    sel_pages = refs[:N_PAGES]
    selnew_ref, winpast_ref, winnew_ref, o_ref, s_scr = refs[N_PAGES:]
    dh = A_HEAD_DIM
    RG = A_GROUP * qb
    R = A_N_KV * RG
    scale = dh ** -0.5
    n_win_past = WINDOW // TILE
    assert R == TILE and n_sel <= SEL_BLOCK and start == N_PAGES * TILE

    qs = q_ref[...] * scale
    blocks = []
    for g in range(A_N_KV):
        qg = jnp.concatenate([qs[:, (g * A_GROUP + r) * dh:(g * A_GROUP + r + 1) * dh] for r in range(A_GROUP)], axis=0)
        parts = [qg if gg == g else jnp.zeros((RG, dh), f32) for gg in range(A_N_KV)]
        blocks.append(jnp.concatenate(parts, axis=1))
    qbd = jnp.concatenate(blocks, axis=0).astype(bf16)

    tbl_rows = jnp.concatenate([jnp.broadcast_to(tbl_ref[h:h + 1, :], (qb, LANES)) for h in range(A_N_HEADS)], axis=0)
    far_bias = tbl_rows[:, MAX_DISTANCE - 1:MAX_DISTANCE]
    lane = lax.broadcasted_iota(i32, (R, TILE), 1)
    row = lax.broadcasted_iota(i32, (R, TILE), 0)
    qpos = start + row % qb

    def near_bias(dist):
        return jnp.take_along_axis(tbl_rows, jnp.clip(dist, 0, MAX_DISTANCE - 1), axis=1)

    def softmax_rows(s):
        m = jnp.max(s, axis=1, keepdims=True)
        p = jnp.where(s > 0.5 * MASK_VALUE, jnp.exp(s - m), 0.0)
        l = jnp.sum(p, axis=1, keepdims=True)
        return p, jnp.where(l == 0.0, 1.0, l)

    def pad_rows(x):
        return jnp.concatenate([x, jnp.zeros((TILE - x.shape[0], x.shape[1]), x.dtype)], axis=0).astype(bf16)

    dist_c = qpos - (lane * CMP_STRIDE + (CMP_LEN - 1))
    s_c = _dot(qbd, kct_ref[0]) + near_bias(dist_c)
    p_c, l_c = softmax_rows(jnp.where(dist_c >= 0, s_c, MASK_VALUE))
    o_c = _dot(p_c.astype(bf16), vc_ref[0]) / l_c
    p_c = p_c / l_c

    p_sum = []
    for g in range(A_N_KV):
        acc = p_c[g * RG:g * RG + qb]
        for r in range(1, A_GROUP):
            acc = acc + p_c[g * RG + r * qb:g * RG + (r + 1) * qb]
        p_sum.append(acc)
    p_sum = jnp.concatenate(p_sum, axis=0)
    imp = jnp.dot(p_sum, ovl_ref[...], preferred_element_type=f32, precision=_HI)
    s_lane = lax.broadcasted_iota(i32, (A_N_KV * qb, TILE), 1)
    s_qpos = start + lax.broadcasted_iota(i32, (A_N_KV * qb, TILE), 0) % qb
    blk = s_qpos // SEL_BLOCK
    valid = s_lane * SEL_BLOCK <= s_qpos
    forced = (s_lane == 0) | (s_lane == blk) | (s_lane == blk - 1)
    score = jnp.where(valid, imp + jnp.where(forced, FORCE_SCORE, 0.0), -1.0)
    score = jnp.where(s_lane < n_sel, score, -3.0)
    rank = jnp.zeros(score.shape, f32)
    for s2 in range(n_sel):
        col = score[:, s2:s2 + 1]
        rank = rank + jnp.where((col > score) | ((col == score) & (s_lane > s2)), 1.0, 0.0)
    not_chosen = jnp.where(rank < float(min(SEL_TOPN, n_sel)), 0.0, -1.0)
    drop = jnp.concatenate(
        [not_chosen[g * qb:(g + 1) * qb] for g in range(A_N_KV) for _ in range(A_GROUP)], axis=0)
    drop = drop[:, 0:SEL_BLOCK].astype(bf16)
    b_row = lax.broadcasted_iota(i32, (SEL_BLOCK, TILE), 0)
    b_col = lax.broadcasted_iota(i32, (SEL_BLOCK, TILE), 1)

    def drop_unselected(t):
        expand = jnp.where(b_row == t * (TILE // SEL_BLOCK) + b_col // SEL_BLOCK, -MASK_VALUE, 0.0).astype(bf16)
        return _dot(drop, expand)

    def attend(tiles, new_ref, extra):
        n = len(tiles)
        for j, (k_t, _, kind) in enumerate(tiles):
            s = _dot(qbd, k_t.astype(bf16)) + extra(j)
            key0 = start - (n - j) * TILE
            dist = qpos - (key0 + lane)
            if kind == "far":
                s = s + far_bias
            else:
                s = s + near_bias(dist)
            if kind == "edge":
                s = jnp.where(dist < WINDOW, s, MASK_VALUE)
            s_scr[:, j * TILE:(j + 1) * TILE] = s
        new = new_ref[...]
        dist = qpos - (start + lane)
        s = _dot_nt(qbd, pad_rows(new[:, 0:KV_HALF])) + extra(n) + near_bias(dist)
        s_scr[:, n * TILE:(n + 1) * TILE] = jnp.where(dist >= 0, s, MASK_VALUE)
        p, l = softmax_rows(s_scr[:, 0:(n + 1) * TILE])
        p = p.astype(bf16)
        o = _dot(p[:, n * TILE:(n + 1) * TILE], pad_rows(new[:, KV_HALF:KV_ROW]))
        for j, (_, v_t, _) in enumerate(tiles):
            o = o + _dot_nt(p[:, j * TILE:(j + 1) * TILE], v_t.astype(bf16))
        return o / l

    sel_tiles = []
    for t in range(N_PAGES):
        page = sel_pages[t]
        sel_tiles.append((page[0, 0:KV_HALF, :], page[0, KV_HALF:KV_ROW, :], "near" if t == N_PAGES - 1 else "far"))
    o_s = attend(sel_tiles, selnew_ref, lambda j: drop_unselected(j))

    win_tiles = []
    for j in range(n_win_past):
        cols = slice(j * TILE, (j + 1) * TILE)
        kind = "edge" if j == 0 else ("near" if j == n_win_past - 1 else "far")
        win_tiles.append((winpast_ref[0, 0:KV_HALF, cols], winpast_ref[0, KV_HALF:KV_ROW, cols], kind))
    o_w = attend(win_tiles, winnew_ref, lambda j: 0.0)

    for g in range(A_N_KV):
        for r in range(A_GROUP):
            h = g * A_GROUP + r
            rows = slice(g * RG + r * qb, g * RG + (r + 1) * qb)
            cols = slice(g * dh, (g + 1) * dh)
            gt = gate_ref[:, 3 * h:3 * h + 3]
            o_h = gt[:, 0:1] * o_c[rows, cols] + gt[:, 1:2] * o_s[rows, cols] + gt[:, 2:3] * o_w[rows, cols]
            o_ref[:, h * dh:(h + 1) * dh] = o_h.astype(o_ref.dtype)


def nsa_attention_sample(q, gates, row0, Bn, L, start, kct, vc, tbl, overlap, sel_pages, page_table, sel_new,
                         win_past, win_new):
    qb = L
    n_sel = -(-(start + L) // SEL_BLOCK)
    blk0 = row0 // qb
    row_map = lambda b, pt: (blk0 + b, 0)
    seq_map = lambda b, pt: (b, 0, 0)
    page_specs = [
        pl.BlockSpec((1, KV_ROW, PAGE_SIZE), functools.partial(lambda b, pt, j: (pt[b, j], 0, 0), j=j))
        for j in range(N_PAGES)
    ]
    return pl.pallas_call(
        functools.partial(_attn_sample_body, qb=qb, start=start, n_sel=n_sel),
        out_shape=jax.ShapeDtypeStruct((Bn * L, A_Q_DIM), f32),
        grid_spec=pltpu.PrefetchScalarGridSpec(
            num_scalar_prefetch=1,
            grid=(Bn,),
            in_specs=[
                pl.BlockSpec((qb, A_Q_DIM), row_map),
                pl.BlockSpec((qb, LANES), row_map),
                pl.BlockSpec((1, KV_HALF, N_CHUNKS), seq_map),
                pl.BlockSpec((1, N_CHUNKS, KV_HALF), seq_map),
                pl.BlockSpec((A_N_HEADS, LANES), lambda b, pt: (0, 0)),
                pl.BlockSpec((TILE, TILE), lambda b, pt: (0, 0)),
            ] + page_specs + [
                pl.BlockSpec((qb, KV_ROW), row_map),
                pl.BlockSpec((1, KV_ROW, WINDOW), seq_map),
                pl.BlockSpec((qb, KV_ROW), row_map),
            ],
            out_specs=pl.BlockSpec((qb, A_Q_DIM), lambda b, pt: (b, 0)),
            scratch_shapes=[pltpu.VMEM((TILE, (N_PAGES + 1) * TILE), f32)],
        ),
        compiler_params=_cparams(("parallel",)),
        name="nsa_attention_sample",
    )(page_table, q, gates, kct, vc, tbl, overlap, *([sel_pages] * N_PAGES), sel_new, win_past, win_new)


def _pad_lanes(v):
    return jnp.pad(v, (0, LANES - v.shape[0])).reshape(1, LANES)


def _overlap_matrix():
    n_cmp = N_CHUNKS - CMP_RATIO + 1
    c = np.arange(TILE)[:, None] * CMP_STRIDE
    s = np.arange(TILE)[None, :] * SEL_BLOCK
    ov = (c < s + SEL_BLOCK) & (c + CMP_LEN > s) & (np.arange(TILE)[:, None] < n_cmp)
    return jnp.asarray(ov.astype(np.float32))


def _feature_major(x):
    lead = x.shape[:-4]
    n = len(lead)
    return jnp.transpose(x, tuple(range(n)) + (n + 1, n + 2, n + 3, n)).reshape(lead + (KV_ROW, x.shape[-4]))


def _token_major(x_t):
    B, _, T = x_t.shape
    return jnp.transpose(x_t.reshape(B, 2, A_N_KV, A_HEAD_DIM, T), (0, 4, 1, 2, 3))


def kernel(x_prompt, x_sample, state_ssm, state_conv, cache_cmp_kv, cache_sel_kv, cache_win_kv, page_table, ln_g, ln_b, m_in_w, m_conv_w, m_conv_b, m_dt_bias, m_a_log, m_d, m_norm_w, m_out_w, kv_w, cmp_w1, cmp_pe, cmp_w2, q_w, o_w, rel_bias, mlp_w1, mlp_w2):
    Bp, Lp, D = x_prompt.shape
    Bs, Ls, _ = x_sample.shape
    NP, NS = Bp * Lp, Bs * Ls
    past_len = page_table.shape[1] * PAGE_SIZE
    assert past_len == N_PAGES * PAGE_SIZE and Lp == N_PAGES * PAGE_SIZE and cache_win_kv.shape[1] == WINDOW

    in_w = m_in_w[0].astype(bf16)
    z_w = in_w[:, :M_D_INNER]
    xbc_w = in_w[:, M_D_INNER:M_D_INNER + M_CONV_DIM]
    dt_w = jnp.pad(in_w[:, M_D_INNER + M_CONV_DIM:], ((0, 0), (0, LANES - M_N_HEADS)))
    kvw = kv_w.astype(bf16)
    qw = q_w[0].astype(bf16)
    gate_w = jnp.pad(qw[:, A_Q_DIM:], ((0, 0), (0, LANES - 3 * A_N_HEADS)))
    w1b = cmp_w1.astype(bf16)
    w_j = jnp.transpose(w1b, (0, 2, 3, 1, 4)).reshape(2, CMP_STRIDE, A_HEAD_DIM, CMP_RATIO * CMP_HIDDEN)
    zeros = jnp.zeros_like(w_j)
    wbd = jnp.concatenate([jnp.concatenate([w_j, zeros], axis=3), jnp.concatenate([zeros, w_j], axis=3)], axis=2)
    pe_rows = jnp.broadcast_to(cmp_pe.astype(bf16).reshape(2, 1, CMP_LEN * A_HEAD_DIM), (2, SUBLANES, CMP_LEN * A_HEAD_DIM))
    w1_flat = w1b.reshape(2, CMP_LEN * A_HEAD_DIM, CMP_HIDDEN)
    cmp_w = (wbd, pe_rows, w1_flat, cmp_w2[1].astype(bf16), cmp_w2[0].T.astype(bf16))

    x = jnp.concatenate([x_prompt.reshape(NP, D), x_sample.reshape(NS, D)], axis=0)
    xb = x.astype(bf16)
    z = matmul(xb, z_w, f32)
    xbc = matmul(xb, xbc_w, f32)
    dt = matmul(xb, dt_w, f32)
    ssd_w = (m_conv_w[0], m_conv_b[0].reshape(1, -1), _pad_lanes(m_dt_bias[0]), _pad_lanes(m_a_log[0]),
             _pad_lanes(m_d[0]), m_norm_w[0].reshape(1, -1))
    y_p, p_ssm, p_conv = ssd_mixer_core(xbc, z, dt, 0, Bp, Lp, None, None, *ssd_w)
    y_s, s_ssm, s_conv = ssd_mixer_core(xbc, z, dt, NP, Bs, Ls, state_conv[0], state_ssm[0], *ssd_w)
    y = jnp.concatenate([y_p, y_s.astype(bf16)], axis=0)
    h_f, h_b = matmul_residual_ln(y, m_out_w[0].astype(bf16), x, ln_g[0, 0].reshape(1, D), ln_b[0, 0].reshape(1, D))
    h_f, h_b = mlp_residual_ln(h_b, h_f, mlp_w1[0].astype(bf16), mlp_w2[0].astype(bf16),
                               ln_g[0, 1].reshape(1, D), ln_b[0, 1].reshape(1, D))

    cmp_t, sel_t, win_t, sel_pg, win_pg = kv_project_feature_major(h_b, kvw.T, Bp, Lp)
    kv_s = matmul(h_b[NP:], kvw, f32)
    cmp_s, sel_s, win_s = kv_s[:, 0:KV_ROW], kv_s[:, KV_ROW:2 * KV_ROW], kv_s[:, 2 * KV_ROW:3 * KV_ROW]
    kct_p, vc_p = compress_kv(cmp_t, None, *cmp_w)
    kct_s, vc_s = compress_kv(_feature_major(cache_cmp_kv), page_table, *cmp_w)

    q = matmul(h_b, qw[:, :A_Q_DIM], f32)
    gates = matmul(h_b, gate_w, f32, act="sigmoid")
    tbl = bias_table(rel_bias.T)
    overlap = _overlap_matrix()
    o_p = nsa_attention_prompt(q, gates, Bp, Lp, kct_p, vc_p, tbl, overlap, sel_pg, win_pg)
    o_s = nsa_attention_sample(q[NP:], gates[NP:], 0, Bs, Ls, past_len, kct_s, vc_s, tbl, overlap,
                               _feature_major(cache_sel_kv), page_table, sel_s, _feature_major(cache_win_kv), win_s)
    o = jnp.concatenate([o_p, o_s.astype(bf16)], axis=0)
    h_f, h_b = matmul_residual_ln(o, o_w[0].astype(bf16), h_f, ln_g[1, 0].reshape(1, D), ln_b[1, 0].reshape(1, D))
    h_f, _ = mlp_residual_ln(h_b, h_f, mlp_w1[1].astype(bf16), mlp_w2[1].astype(bf16),
                             ln_g[1, 1].reshape(1, D), ln_b[1, 1].reshape(1, D))

    kv_shape = (2, A_N_KV, A_HEAD_DIM)
    n_keep = min(WINDOW, Lp)
    s_win = jnp.concatenate([cache_win_kv[:, Ls:], win_s.reshape((Bs, Ls) + kv_shape)], axis=1)
    return (
        h_f[:NP].reshape(Bp, Lp, D), h_f[NP:].reshape(Bs, Ls, D),
        p_ssm[None], p_conv[None],
        _token_major(cmp_t), _token_major(sel_t), _token_major(win_t[:, :, Lp - n_keep:]),
        s_ssm[None], s_conv[None],
        cmp_s.reshape((Bs, Ls) + kv_shape), sel_s.reshape((Bs, Ls) + kv_shape), s_win,
    )
```

```python
import functools
import math

import jax
import jax.numpy as jnp
import numpy as np
from jax import lax
from jax.experimental import pallas as pl
from jax.experimental.pallas import tpu as pltpu

f32 = jnp.float32
bf16 = jnp.bfloat16
i32 = jnp.int32

D_MODEL = 1024
DEPTH = 2
DN_ALPHA = (2.0 * DEPTH) ** 0.25
LN_EPS = 1e-5
RMS_EPS = 1e-5
D_FF = 4 * D_MODEL
M_D_INNER = 2 * D_MODEL
M_HEAD_DIM = 64
M_N_HEADS = M_D_INNER // M_HEAD_DIM
M_N_GROUPS = 4
M_HPG = M_N_HEADS // M_N_GROUPS
M_D_STATE = 128
M_CONV = 4
M_CHUNK = 128
M_CONV_DIM = M_D_INNER + 2 * M_N_GROUPS * M_D_STATE
A_HEAD_DIM = 64
A_N_HEADS = D_MODEL // A_HEAD_DIM
A_N_KV = 4
A_GROUP = A_N_HEADS // A_N_KV
A_Q_DIM = A_N_HEADS * A_HEAD_DIM
KV_HALF = A_N_KV * A_HEAD_DIM
KV_ROW = 2 * KV_HALF
CMP_LEN = 32
CMP_STRIDE = 16
CMP_RATIO = CMP_LEN // CMP_STRIDE
CMP_HIDDEN = 2 * A_HEAD_DIM
SEL_BLOCK = 64
SEL_TOPN = 16
WINDOW = 512
Q_BLOCK = 128
N_BUCKETS = 32
MAX_DISTANCE = 128
MASK_VALUE = -1e30
FORCE_SCORE = 1e3
PAGE_SIZE = 128

LANES = 128
SUBLANES = 8
VMEM_LIMIT = 56 * 1024 * 1024

_HI = lax.Precision.HIGHEST


def _cparams(sem):
    return pltpu.CompilerParams(dimension_semantics=sem, vmem_limit_bytes=VMEM_LIMIT)


def _dot(a, b):
    return jnp.dot(a, b, preferred_element_type=f32)


def _dot_nt(a, b):
    return lax.dot_general(a, b, (((1,), (1,)), ((), ())), preferred_element_type=f32)


def _dot_tn(a, b):
    return lax.dot_general(a, b, (((0,), (0,)), ((), ())), preferred_element_type=f32)


def _silu(x):
    return x * (1.0 / (1.0 + jnp.exp(-x)))


def _layer_norm(x, g, b):
    mu = jnp.mean(x, axis=-1, keepdims=True)
    xc = x - mu
    var = jnp.mean(xc * xc, axis=-1, keepdims=True)
    return xc * lax.rsqrt(var + LN_EPS) * g + b


def _mm_body(x_ref, w_ref, o_ref, *, act):
    y = _dot(x_ref[...], w_ref[...])
    if act == "sigmoid":
        y = 1.0 / (1.0 + jnp.exp(-y))
    o_ref[...] = y.astype(o_ref.dtype)


def matmul(x, w, out_dtype, act=None, tm=512, tn=512):
    M, K = x.shape
    N = w.shape[1]
    tn = min(tn, N)
    return pl.pallas_call(
        functools.partial(_mm_body, act=act),
        out_shape=jax.ShapeDtypeStruct((M, N), out_dtype),
        grid=(M // tm, N // tn),
        in_specs=[pl.BlockSpec((tm, K), lambda i, j: (i, 0)), pl.BlockSpec((K, tn), lambda i, j: (0, j))],
        out_specs=pl.BlockSpec((tm, tn), lambda i, j: (i, j)),
        compiler_params=_cparams(("parallel", "parallel")),
        name="matmul",
    )(x, w)


def _mm_res_ln_body(x_ref, w_ref, r_ref, g_ref, b_ref, of_ref, ob_ref):
    y = DN_ALPHA * r_ref[...] + _dot(x_ref[...], w_ref[...])
    h = _layer_norm(y, g_ref[...], b_ref[...])
    of_ref[...] = h
    ob_ref[...] = h.astype(bf16)


def matmul_residual_ln(x, w, resid, g, b, tm=512):
    M, K = x.shape
    N = w.shape[1]
    return pl.pallas_call(
        _mm_res_ln_body,
        out_shape=(jax.ShapeDtypeStruct((M, N), f32), jax.ShapeDtypeStruct((M, N), bf16)),
        grid=(M // tm,),
        in_specs=[
            pl.BlockSpec((tm, K), lambda i: (i, 0)),
            pl.BlockSpec((K, N), lambda i: (0, 0)),
            pl.BlockSpec((tm, N), lambda i: (i, 0)),
            pl.BlockSpec((1, N), lambda i: (0, 0)),
            pl.BlockSpec((1, N), lambda i: (0, 0)),
        ],
        out_specs=(pl.BlockSpec((tm, N), lambda i: (i, 0)), pl.BlockSpec((tm, N), lambda i: (i, 0))),
        compiler_params=_cparams(("parallel",)),
        name="matmul_residual_ln",
    )(x, w, resid, g, b)


def _mlp_body(hb_ref, hf_ref, w1_ref, w2_ref, g_ref, b_ref, of_ref, ob_ref, acc_ref):
    j = pl.program_id(1)

    @pl.when(j == 0)
    def _():
        acc_ref[...] = jnp.zeros_like(acc_ref)

    u = jnp.maximum(_dot(hb_ref[...], w1_ref[...]), 0.0)
    acc_ref[...] += _dot((u * u).astype(bf16), w2_ref[...])

    @pl.when(j == pl.num_programs(1) - 1)
    def _():
        h = _layer_norm(DN_ALPHA * hf_ref[...] + acc_ref[...], g_ref[...], b_ref[...])
        of_ref[...] = h
        ob_ref[...] = h.astype(bf16)


def mlp_residual_ln(hb, hf, w1, w2, g, b, tm=512, tf=512):
    M, D = hb.shape
    F = w1.shape[1]
    return pl.pallas_call(
        _mlp_body,
        out_shape=(jax.ShapeDtypeStruct((M, D), f32), jax.ShapeDtypeStruct((M, D), bf16)),
        grid=(M // tm, F // tf),
        in_specs=[
            pl.BlockSpec((tm, D), lambda i, j: (i, 0)),
            pl.BlockSpec((tm, D), lambda i, j: (i, 0)),
            pl.BlockSpec((D, tf), lambda i, j: (0, j)),
            pl.BlockSpec((tf, D), lambda i, j: (j, 0)),
            pl.BlockSpec((1, D), lambda i, j: (0, 0)),
            pl.BlockSpec((1, D), lambda i, j: (0, 0)),
        ],
        out_specs=(pl.BlockSpec((tm, D), lambda i, j: (i, 0)), pl.BlockSpec((tm, D), lambda i, j: (i, 0))),
        scratch_shapes=[pltpu.VMEM((tm, D), f32)],
        compiler_params=_cparams(("parallel", "arbitrary")),
        name="mlp_residual_ln",
    )(hb, hf, w1, w2, g, b)


def _kv_project_body(wt_ref, h_ref, cmp_ref, sel_ref, win_ref, selp_ref, winp_ref):
    res = _dot_nt(wt_ref[...], h_ref[...])
    tm = h_ref.shape[0]
    cmp_ref[0] = res[0:KV_ROW]
    sel_ref[0] = res[KV_ROW:2 * KV_ROW]
    win_ref[0] = res[2 * KV_ROW:3 * KV_ROW]
    for k in range(tm // PAGE_SIZE):
        cols = slice(k * PAGE_SIZE, (k + 1) * PAGE_SIZE)
        selp_ref[0, k] = res[KV_ROW:2 * KV_ROW, cols].astype(bf16)
        winp_ref[0, k] = res[2 * KV_ROW:3 * KV_ROW, cols].astype(bf16)


def kv_project_feature_major(h_b, w_t, Bn, L, tm=512):
    nj = L // tm
    pages_per_step = tm // PAGE_SIZE
    fm = jax.ShapeDtypeStruct((Bn, KV_ROW, L), f32)
    pg = jax.ShapeDtypeStruct((Bn, L // PAGE_SIZE, KV_ROW, PAGE_SIZE), bf16)
    fm_spec = pl.BlockSpec((1, KV_ROW, tm), lambda b, j: (b, 0, j))
    pg_spec = pl.BlockSpec((1, pages_per_step, KV_ROW, PAGE_SIZE), lambda b, j: (b, j, 0, 0))
    return pl.pallas_call(
        _kv_project_body,
        out_shape=(fm, fm, fm, pg, pg),
        grid=(Bn, nj),
        in_specs=[
            pl.BlockSpec(w_t.shape, lambda b, j: (0, 0)),
            pl.BlockSpec((tm, h_b.shape[1]), lambda b, j: (b * nj + j, 0)),
        ],
        out_specs=(fm_spec, fm_spec, fm_spec, pg_spec, pg_spec),
        compiler_params=_cparams(("parallel", "parallel")),
        name="kv_project_feature_major",
    )(w_t, h_b)


def _ssd_body(*refs, Q, has_init):
    if has_init:
        (xbc_ref, z_ref, dt_ref, conv0_ref, h0_ref, cw_ref, cb_ref, dtb_ref, alog_ref, dsk_ref, nw_ref,
         y_ref, hout_ref, cout_ref, st_ref, xpad_ref, xc_ref, ybuf_ref) = refs
    else:
        (xbc_ref, z_ref, dt_ref, cw_ref, cb_ref, dtb_ref, alog_ref, dsk_ref, nw_ref,
         y_ref, hout_ref, cout_ref, st_ref, xpad_ref, xc_ref, ybuf_ref) = refs
    c = pl.program_id(1)
    P, N, R, G = M_HEAD_DIM, M_D_STATE, M_HPG, M_N_GROUPS
    PAD = SUBLANES

    @pl.when(c == 0)
    def _():
        xpad_ref[0:PAD, :] = jnp.zeros((PAD, M_CONV_DIM), f32)
        if has_init:
            xpad_ref[PAD - (M_CONV - 1):PAD, :] = conv0_ref[0]
            for g in range(G):
                for r in range(R):
                    st_ref[g, :, r * P:(r + 1) * P] = h0_ref[0, g * R + r].T
        else:
            st_ref[...] = jnp.zeros_like(st_ref)

    xpad_ref[PAD:PAD + Q, :] = xbc_ref[...]
    acc = cb_ref[...] + xpad_ref[pl.ds(PAD - 3, Q), :] * cw_ref[0:1, :]
    for k in range(1, M_CONV):
        acc = acc + xpad_ref[pl.ds(PAD - 3 + k, Q), :] * cw_ref[k:k + 1, :]
    xc_ref[...] = _silu(acc)
    cout_ref[0] = xpad_ref[pl.ds(Q + PAD - 3, 3), :]
    xpad_ref[0:PAD, :] = xpad_ref[pl.ds(Q, PAD), :]

    xdt = dt_ref[...] + dtb_ref[...]
    dt = jnp.maximum(xdt, 0.0) + jnp.log1p(jnp.exp(-jnp.abs(xdt)))
    a = dt * (-jnp.exp(alog_ref[...]))
    ri = lax.broadcasted_iota(i32, (Q, Q), 0)
    ci = lax.broadcasted_iota(i32, (Q, Q), 1)
    tril = ri >= ci
    a_cum = jnp.dot(tril.astype(f32), a, preferred_element_type=f32, precision=_HI)
    a_cum_t = a_cum.T
    a_last = a_cum[Q - 1:Q, :]
    e_cum = jnp.exp(a_cum)
    e_rest = jnp.exp(a_last - a_cum)
    e_last = jnp.exp(a_last)
    dsk = dsk_ref[...]

    for g in range(G):
        bg = xc_ref[:, M_D_INNER + g * N:M_D_INNER + (g + 1) * N].astype(bf16)
        cg = xc_ref[:, M_D_INNER + G * N + g * N:M_D_INNER + G * N + (g + 1) * N].astype(bf16)
        gmat = _dot_nt(cg, bg)
        yoff = _dot(cg, st_ref[g].astype(bf16))
        xds = []
        for r in range(R):
            h = g * R + r
            col = a_cum[:, h:h + 1]
            row = a_cum_t[h:h + 1, :]
            lmat = jnp.exp(jnp.where(tril, col - row, -jnp.inf))
            xh = xc_ref[:, h * P:(h + 1) * P]
            xdt_h = xh * dt[:, h:h + 1]
            ydiag = _dot((gmat * lmat).astype(bf16), xdt_h.astype(bf16))
            y_h = ydiag + yoff[:, r * P:(r + 1) * P] * e_cum[:, h:h + 1] + xh * dsk[:, h:h + 1]
            ybuf_ref[:, h * P:(h + 1) * P] = y_h
            xds.append((xdt_h * e_rest[:, h:h + 1]).astype(bf16))
        new = _dot_tn(bg, jnp.concatenate(xds, axis=1))
        dec = jnp.concatenate([jnp.broadcast_to(e_last[:, g * R + r:g * R + r + 1], (1, P)) for r in range(R)], axis=1)
        st_ref[g] = st_ref[g] * dec + new

    GW = M_D_INNER // G
    for g in range(G):
        yg = ybuf_ref[:, g * GW:(g + 1) * GW] * _silu(z_ref[:, g * GW:(g + 1) * GW])
        ms = jnp.mean(yg * yg, axis=-1, keepdims=True)
        y_ref[:, g * GW:(g + 1) * GW] = (yg * lax.rsqrt(ms + RMS_EPS) * nw_ref[:, g * GW:(g + 1) * GW]).astype(y_ref.dtype)

    @pl.when(c == pl.num_programs(1) - 1)
    def _():
        for g in range(G):
            for r in range(R):
                hout_ref[0, g * R + r] = st_ref[g, :, r * P:(r + 1) * P].T


def ssd_mixer_core(xbc, z, dt, row0, Bn, L, conv0, h0, conv_w, conv_b, dt_bias, a_log, d_skip, norm_w):
    Q = M_CHUNK if L % M_CHUNK == 0 else L
    nc = L // Q
    has_init = h0 is not None
    blk0 = row0 // Q
    row_map = lambda b, c: (blk0 + b * nc + c, 0)
    const2 = lambda b, c: (0, 0)
    in_specs = [
        pl.BlockSpec((Q, M_CONV_DIM), row_map),
        pl.BlockSpec((Q, M_D_INNER), row_map),
        pl.BlockSpec((Q, LANES), row_map),
    ]
    args = [xbc, z, dt]
    if has_init:
        in_specs += [
            pl.BlockSpec((1, M_CONV - 1, M_CONV_DIM), lambda b, c: (b, 0, 0)),
            pl.BlockSpec((1, M_N_HEADS, M_HEAD_DIM, M_D_STATE), lambda b, c: (b, 0, 0, 0)),
        ]
        args += [conv0, h0]
    in_specs += [
        pl.BlockSpec((M_CONV, M_CONV_DIM), const2),
        pl.BlockSpec((1, M_CONV_DIM), const2),
        pl.BlockSpec((1, LANES), const2),
        pl.BlockSpec((1, LANES), const2),
        pl.BlockSpec((1, LANES), const2),
        pl.BlockSpec((1, M_D_INNER), const2),
    ]
    args += [conv_w, conv_b, dt_bias, a_log, d_skip, norm_w]
    y_dtype = bf16 if Q % 16 == 0 else f32
    return pl.pallas_call(
        functools.partial(_ssd_body, Q=Q, has_init=has_init),
        out_shape=(
            jax.ShapeDtypeStruct((Bn * L, M_D_INNER), y_dtype),
            jax.ShapeDtypeStruct((Bn, M_N_HEADS, M_HEAD_DIM, M_D_STATE), f32),
            jax.ShapeDtypeStruct((Bn, M_CONV - 1, M_CONV_DIM), f32),
        ),
        grid=(Bn, nc),
        in_specs=in_specs,
        out_specs=(
            pl.BlockSpec((Q, M_D_INNER), lambda b, c: (b * nc + c, 0)),
            pl.BlockSpec((1, M_N_HEADS, M_HEAD_DIM, M_D_STATE), lambda b, c: (b, 0, 0, 0)),
            pl.BlockSpec((1, M_CONV - 1, M_CONV_DIM), lambda b, c: (b, 0, 0)),
        ),
        scratch_shapes=[
            pltpu.VMEM((M_N_GROUPS, M_D_STATE, M_HPG * M_HEAD_DIM), f32),
            pltpu.VMEM((Q + SUBLANES, M_CONV_DIM), f32),
            pltpu.VMEM((Q, M_CONV_DIM), f32),
            pltpu.VMEM((Q, M_D_INNER), f32),
        ],
        compiler_params=_cparams(("parallel", "arbitrary")),
        name="ssd_mixer_core",
    )(*args)


N_PAGES = 16
N_CHUNKS = N_PAGES * PAGE_SIZE // CMP_STRIDE
N_SLABS = KV_ROW // LANES


def _compress_body(*refs, paged):
    if paged:
        refs = refs[1:]
    pages = refs[:N_PAGES]
    wbd_ref, pe_ref, w1f_ref, w2_ref, w2t_ref, kct_ref, vc_ref, xs_ref = refs[N_PAGES:]
    H = CMP_HIDDEN
    for p in range(N_PAGES):
        for sl in range(N_SLABS):
            xs_ref[sl, p * PAGE_SIZE:(p + 1) * PAGE_SIZE, :] = pages[p][0, sl * LANES:(sl + 1) * LANES, :].T
    row = lax.broadcasted_iota(i32, (N_CHUNKS, H), 0)
    for kv in range(2):
        pe_term = _dot(pe_ref[kv], w1f_ref[kv])[0:1, :]
        for gp in range(A_N_KV // 2):
            acc = jnp.zeros((N_CHUNKS, 2 * CMP_RATIO * H), f32)
            for j in range(CMP_STRIDE):
                x = xs_ref[kv * (A_N_KV // 2) + gp, pl.ds(j, N_CHUNKS, stride=CMP_STRIDE), :].astype(bf16)
                acc = acc + _dot(x, wbd_ref[kv, j])
            for gi in range(2):
                g = gp * 2 + gi
                p0 = acc[:, gi * CMP_RATIO * H:gi * CMP_RATIO * H + H]
                p1 = acc[:, gi * CMP_RATIO * H + H:(gi + 1) * CMP_RATIO * H]
                p1_next = jnp.where(row == N_CHUNKS - 1, 0.0, pltpu.roll(p1, N_CHUNKS - 1, 0))
                hid = _silu(p0 + p1_next + pe_term).astype(bf16)
                if kv == 0:
                    kct_ref[0, g * A_HEAD_DIM:(g + 1) * A_HEAD_DIM, :] = _dot_nt(w2t_ref[...], hid).astype(kct_ref.dtype)
                else:
                    vc_ref[0, :, g * A_HEAD_DIM:(g + 1) * A_HEAD_DIM] = _dot(hid, w2_ref[...]).astype(vc_ref.dtype)


def compress_kv(pages, page_table, wbd, pe_rows, w1_flat, w2_v, w2t_k):
    paged = page_table is not None
    Bn = page_table.shape[0] if paged else pages.shape[0]
    if paged:
        page_specs = [
            pl.BlockSpec((1, KV_ROW, PAGE_SIZE), functools.partial(lambda b, pt, j: (pt[b, j], 0, 0), j=j))
            for j in range(N_PAGES)
        ]
        const = lambda nd: (lambda b, pt: (0,) * nd)
        out_map = lambda b, pt: (b, 0, 0)
    else:
        page_specs = [
            pl.BlockSpec((1, KV_ROW, PAGE_SIZE), functools.partial(lambda b, j: (b, 0, j), j=j)) for j in range(N_PAGES)
        ]
        const = lambda nd: (lambda b: (0,) * nd)
        out_map = lambda b: (b, 0, 0)
    in_specs = page_specs + [
        pl.BlockSpec(wbd.shape, const(4)),
        pl.BlockSpec(pe_rows.shape, const(3)),
        pl.BlockSpec(w1_flat.shape, const(3)),
        pl.BlockSpec(w2_v.shape, const(2)),
        pl.BlockSpec(w2t_k.shape, const(2)),
    ]
    out_shape = (jax.ShapeDtypeStruct((Bn, KV_HALF, N_CHUNKS), bf16), jax.ShapeDtypeStruct((Bn, N_CHUNKS, KV_HALF), bf16))
    out_specs = (pl.BlockSpec((1, KV_HALF, N_CHUNKS), out_map), pl.BlockSpec((1, N_CHUNKS, KV_HALF), out_map))
    scratch = [pltpu.VMEM((N_SLABS, N_PAGES * PAGE_SIZE, LANES), f32)]
    body = functools.partial(_compress_body, paged=paged)
    args = ([pages] * N_PAGES) + [wbd, pe_rows, w1_flat, w2_v, w2t_k]
    if paged:
        return pl.pallas_call(
            body, out_shape=out_shape,
            grid_spec=pltpu.PrefetchScalarGridSpec(
                num_scalar_prefetch=1, grid=(Bn,), in_specs=in_specs, out_specs=out_specs, scratch_shapes=scratch),
            compiler_params=_cparams(("parallel",)), name="compress_kv_paged",
        )(page_table, *args)
    return pl.pallas_call(
        body, out_shape=out_shape, grid=(Bn,), in_specs=in_specs, out_specs=out_specs, scratch_shapes=scratch,
        compiler_params=_cparams(("parallel",)), name="compress_kv",
    )(*args)


def _bias_table_body(rb_ref, tbl_ref, tz0_ref, tz1_ref):
    n = lax.broadcasted_iota(i32, (A_N_HEADS, LANES), 1)
    max_exact = N_BUCKETS // 2
    large = max_exact + (jnp.log(jnp.maximum(n, max_exact).astype(f32) / max_exact)
                         / math.log(MAX_DISTANCE / max_exact) * (N_BUCKETS - max_exact)).astype(i32)
    bucket = jnp.where(n < max_exact, n, jnp.minimum(large, N_BUCKETS - 1))
    tbl = jnp.zeros((A_N_HEADS, LANES), f32)
    for b in range(N_BUCKETS):
        tbl = jnp.where(bucket == b, rb_ref[:, b:b + 1], tbl)
    tbl_ref[...] = tbl
    dist = lax.broadcasted_iota(i32, (LANES, LANES), 0) - lax.broadcasted_iota(i32, (LANES, LANES), 1)
    for h in range(A_N_HEADS):
        row = jnp.broadcast_to(tbl[h:h + 1, :], (LANES, LANES))
        own = jnp.take_along_axis(row, jnp.clip(dist, 0, MAX_DISTANCE - 1), axis=1)
        tz0_ref[h] = jnp.where(dist >= 0, own, MASK_VALUE)
        tz1_ref[h] = jnp.take_along_axis(row, jnp.minimum(dist + LANES, MAX_DISTANCE - 1), axis=1)


def bias_table(rel_bias_t):
    assert MAX_DISTANCE <= LANES
    tile = jax.ShapeDtypeStruct((A_N_HEADS, LANES, LANES), f32)
    return pl.pallas_call(
        _bias_table_body, out_shape=(jax.ShapeDtypeStruct((A_N_HEADS, LANES), f32), tile, tile), name="bias_table",
    )(rel_bias_t)


TILE = 128
WIN_TILES = WINDOW // TILE


WIDE = 2 * TILE
N_SLOTS = N_PAGES * TILE // WIDE
BIG = -MASK_VALUE


def _attn_prompt_body(q_ref, gate_ref, kct_ref, vc_ref, tbl_ref, tz0_ref, tz1_ref, ovlt_ref, selp_ref, winp_ref, o_ref,
                      s_scr, m_scr, l_scr, acc_scr, *, qb, n_sel):
    i = pl.program_id(1)
    R = A_GROUP * qb
    dh = A_HEAD_DIM
    pos0 = i * qb
    odd = (i % 2) == 1
    td = i // 2
    scale = dh ** -0.5
    n_rank = SUBLANES * (-(-n_sel // SUBLANES))

    lane = lax.broadcasted_iota(i32, (R, TILE), 1)
    q_in_blk = jnp.concatenate([lax.broadcasted_iota(i32, (qb, TILE), 0)] * A_GROUP, axis=0)
    qpos = pos0 + q_in_blk
    neg_tile = jnp.full((R, TILE), MASK_VALUE, f32)

    s_idx = lax.broadcasted_iota(i32, (n_rank, qb), 0)
    s_qpos = pos0 + lax.broadcasted_iota(i32, (n_rank, qb), 1)
    blk = s_qpos // SEL_BLOCK
    sel_valid = s_idx * SEL_BLOCK <= s_qpos
    sel_forced = (s_idx == 0) | (s_idx == blk) | (s_idx == blk - 1)

    f_row = lax.broadcasted_iota(i32, (TILE - dh, WIDE), 0)
    flag_rows = [jnp.where(f_row == k, BIG, 0.0).astype(bf16) for k in range(2)]
    zero_flag = jnp.zeros((TILE - dh, WIDE), bf16)
    zero_drop = jnp.zeros((TILE, WIDE), bf16)
    b_row = lax.broadcasted_iota(i32, (TILE, WIDE), 0)
    b_col = lax.broadcasted_iota(i32, (TILE, WIDE), 1)
    f_lane = lax.broadcasted_iota(i32, (R, TILE - dh), 1)
    flags = jnp.where(f_lane == 0, jnp.where(td < 1, -1.0, 0.0),
                      jnp.where(f_lane == 1, jnp.where(td < 2, -1.0, 0.0), 0.0)).astype(bf16)
    win_thr = q_in_blk + jnp.where(odd, TILE, 0)

    def pair(ref, T, g, half):
        rows = slice(half * KV_HALF + g * dh, half * KV_HALF + (g + 1) * dh)
        return jnp.concatenate([ref[0, 2 * T, rows, :], ref[0, 2 * T + 1, rows, :]], axis=1)

    def drop_rows(T):
        return jnp.where(b_row == T * (WIDE // SEL_BLOCK) + b_col // SEL_BLOCK, BIG, 0.0).astype(bf16)

    def store_scores(slot, s, left, right, which):
        s_l = s[:, 0:TILE] if left is None else s[:, 0:TILE] + left
        s_r = s[:, TILE:WIDE] if right is None else s[:, TILE:WIDE] + right
        s_scr[slot, :, 0:TILE] = s_l
        s_scr[slot, :, TILE:WIDE] = s_r
        m_scr[which] = jnp.maximum(m_scr[which], jnp.maximum(s_l, s_r))

    def accumulate(slot, shift, v_t):
        p_l = jnp.exp(s_scr[slot, :, 0:TILE] - shift)
        p_r = jnp.exp(s_scr[slot, :, TILE:WIDE] - shift)
        l_scr[...] += p_l + p_r
        acc_scr[...] += _dot_nt(jnp.concatenate([p_l, p_r], axis=1).astype(bf16), v_t)

    def reset():
        m_scr[...] = jnp.full(m_scr.shape, MASK_VALUE, f32)
        l_scr[...] = jnp.zeros(l_scr.shape, f32)
        acc_scr[...] = jnp.zeros(acc_scr.shape, f32)

    def row_max(far_bias):
        m = jnp.maximum(jnp.max(m_scr[0], axis=1, keepdims=True) + far_bias, jnp.max(m_scr[1], axis=1, keepdims=True))
        return jnp.broadcast_to(m - far_bias, (R, TILE)), jnp.broadcast_to(m, (R, TILE))

    def result():
        return acc_scr[...] / jnp.sum(l_scr[...], axis=1, keepdims=True)

    for g in range(A_N_KV):
        heads = [g * A_GROUP + r for r in range(A_GROUP)]
        qg = jnp.concatenate([q_ref[:, h * dh:(h + 1) * dh] for h in heads], axis=0)
        qg = (qg.astype(f32) * scale).astype(bf16)
        tbl_g = jnp.concatenate([jnp.broadcast_to(tbl_ref[h:h + 1, :], (qb, TILE)) for h in heads], axis=0)
        far_bias = jnp.concatenate(
            [jnp.broadcast_to(tbl_ref[h:h + 1, MAX_DISTANCE - 1:MAX_DISTANCE], (qb, 1)) for h in heads], axis=0)
        tz0 = tz0_ref[g * A_GROUP:(g + 1) * A_GROUP].reshape(R, TILE)
        tz1 = tz1_ref[g * A_GROUP:(g + 1) * A_GROUP].reshape(R, TILE)
        prev_right = jnp.where(odd, jnp.broadcast_to(far_bias, (R, TILE)), tz1)
        diag_left = jnp.where(odd, tz1, tz0)
        diag_right = jnp.where(odd, tz0, neg_tile)

        dist_c = qpos - (lane * CMP_STRIDE + (CMP_LEN - 1))
        s_c = _dot(qg, kct_ref[0, g * dh:(g + 1) * dh, :])
        s_c = s_c + jnp.take_along_axis(tbl_g, jnp.clip(dist_c, 0, MAX_DISTANCE - 1), axis=1)
        s_c = jnp.where(dist_c >= 0, s_c, MASK_VALUE)
        m_c = jnp.max(s_c, axis=1, keepdims=True)
        p_c = jnp.where(dist_c >= 0, jnp.exp(s_c - m_c), 0.0)
        l_c = jnp.sum(p_c, axis=1, keepdims=True)
        l_c = jnp.where(l_c == 0.0, 1.0, l_c)
        o_c = _dot(p_c.astype(bf16), vc_ref[0, :, g * dh:(g + 1) * dh]) / l_c
        p_c = p_c / l_c

        p_sum = p_c[0:qb]
        for r in range(1, A_GROUP):
            p_sum = p_sum + p_c[r * qb:(r + 1) * qb]
        imp_t = lax.dot_general(ovlt_ref[0:n_rank, :], p_sum, (((1,), (1,)), ((), ())),
                                preferred_element_type=f32, precision=_HI)
        score = jnp.where(sel_valid, imp_t + jnp.where(sel_forced, FORCE_SCORE, 0.0), -1.0)
        score = jnp.where(s_idx < n_sel, score, -3.0)
        rank = jnp.zeros((n_rank, qb), f32)
        for s2 in range(n_sel):
            other = score[s2:s2 + 1, :]
            rank = rank + jnp.where((other > score) | ((other == score) & (s_idx > s2)), 1.0, 0.0)
        dropped_t = jnp.where(rank < float(min(SEL_TOPN, n_sel)), 0.0, -1.0)
        dropped = jnp.concatenate([dropped_t, jnp.zeros((TILE - n_rank, qb), f32)], axis=0).T
        lhs_sel = jnp.concatenate([qg, flags, jnp.concatenate([dropped] * A_GROUP, axis=0).astype(bf16)], axis=1)
        lhs_win = jnp.concatenate([qg, flags, jnp.zeros((R, TILE), bf16)], axis=1)

        reset()

        def far_scores(T, carry):
            s = _dot(lhs_sel, jnp.concatenate([pair(selp_ref, T, g, 0), zero_flag, drop_rows(T)], axis=0))
            store_scores(T, s, None, None, 0)
            return carry

        lax.fori_loop(0, jnp.maximum(td - 1, 0), far_scores, 0)
        t_prev = jnp.maximum(td - 1, 0)
        s = _dot(lhs_sel, jnp.concatenate([pair(selp_ref, t_prev, g, 0), flag_rows[0], drop_rows(t_prev)], axis=0))
        store_scores(N_SLOTS - 2, s, far_bias, prev_right, 1)
        s = _dot(lhs_sel, jnp.concatenate([pair(selp_ref, td, g, 0), zero_flag, drop_rows(td)], axis=0))
        store_scores(N_SLOTS - 1, s, diag_left, diag_right, 1)
        shift_far, shift_near = row_max(far_bias)

        def far_accumulate(T, carry):
            accumulate(T, shift_far, pair(selp_ref, T, g, 1))
            return carry

        lax.fori_loop(0, jnp.maximum(td - 1, 0), far_accumulate, 0)
        accumulate(N_SLOTS - 2, shift_near, pair(selp_ref, t_prev, g, 1))
        accumulate(N_SLOTS - 1, shift_near, pair(selp_ref, td, g, 1))
        o_s = result()

        reset()
        t_first = jnp.maximum(td - 2, 0)
        s = _dot(lhs_win, jnp.concatenate([pair(winp_ref, t_first, g, 0), flag_rows[1], zero_drop], axis=0))
        too_old_l = jnp.where(lane > win_thr, 0.0, MASK_VALUE)
        too_old_r = jnp.where(lane + TILE > win_thr, 0.0, MASK_VALUE)
        store_scores(0, s, too_old_l, too_old_r, 0)
        s = _dot(lhs_win, jnp.concatenate([pair(winp_ref, t_prev, g, 0), flag_rows[0], zero_drop], axis=0))
        store_scores(1, s, far_bias, prev_right, 1)
        s = _dot(lhs_win, jnp.concatenate([pair(winp_ref, td, g, 0), zero_flag, zero_drop], axis=0))
        store_scores(2, s, diag_left, diag_right, 1)
        shift_far, shift_near = row_max(far_bias)
        accumulate(0, shift_far, pair(winp_ref, t_first, g, 1))
        accumulate(1, shift_near, pair(winp_ref, t_prev, g, 1))
        accumulate(2, shift_near, pair(winp_ref, td, g, 1))
        o_w = result()

        for r, h in enumerate(heads):
            rows = slice(r * qb, (r + 1) * qb)
            gt = gate_ref[:, 3 * h:3 * h + 3]
            o_h = gt[:, 0:1] * o_c[rows] + gt[:, 1:2] * o_s[rows] + gt[:, 2:3] * o_w[rows]
            o_ref[:, h * dh:(h + 1) * dh] = o_h.astype(o_ref.dtype)


def nsa_attention_prompt(q, gates, Bn, L, kct, vc, tbl, tz0, tz1, overlap_t, sel_pages, win_pages):
    qb = Q_BLOCK
    nqb = L // qb
    n_sel = -(-L // SEL_BLOCK)
    n_tiles = L // TILE
    assert n_tiles == N_PAGES and qb == TILE
    R = A_GROUP * qb
    row_map = lambda b, i: (b * nqb + i, 0)
    seq_map3 = lambda b, i: (b, 0, 0)
    seq_map4 = lambda b, i: (b, 0, 0, 0)
    const2 = lambda b, i: (0, 0)
    const3 = lambda b, i: (0, 0, 0)
    return pl.pallas_call(
        functools.partial(_attn_prompt_body, qb=qb, n_sel=n_sel),
        out_shape=jax.ShapeDtypeStruct((Bn * L, A_Q_DIM), bf16),
        grid=(Bn, nqb),
        in_specs=[
            pl.BlockSpec((qb, A_Q_DIM), row_map),
            pl.BlockSpec((qb, LANES), row_map),
            pl.BlockSpec((1, KV_HALF, N_CHUNKS), seq_map3),
            pl.BlockSpec((1, N_CHUNKS, KV_HALF), seq_map3),
            pl.BlockSpec((A_N_HEADS, LANES), const2),
            pl.BlockSpec((A_N_HEADS, TILE, TILE), const3),
            pl.BlockSpec((A_N_HEADS, TILE, TILE), const3),
            pl.BlockSpec((TILE, TILE), const2),
            pl.BlockSpec((1, n_tiles, KV_ROW, TILE), seq_map4),
            pl.BlockSpec((1, n_tiles, KV_ROW, TILE), seq_map4),
        ],
        out_specs=pl.BlockSpec((qb, A_Q_DIM), row_map),
        scratch_shapes=[
            pltpu.VMEM((N_SLOTS, R, WIDE), f32),
            pltpu.VMEM((2, R, TILE), f32),
            pltpu.VMEM((R, TILE), f32),
            pltpu.VMEM((R, A_HEAD_DIM), f32),
        ],
        compiler_params=_cparams(("parallel", "arbitrary")),
        name="nsa_attention_prompt",
    )(q, gates, kct, vc, tbl, tz0, tz1, overlap_t, sel_pages, win_pages)


def _attn_sample_body(pt_ref, q_ref, gate_ref, kct_ref, vc_ref, tbl_ref, ovl_ref, *refs, qb, start, n_sel):
    del pt_ref
    sel_pages = refs[:N_PAGES]
    selnew_ref, winpast_ref, winnew_ref, o_ref, s_scr = refs[N_PAGES:]
    dh = A_HEAD_DIM
    RG = A_GROUP * qb
    R = A_N_KV * RG
    scale = dh ** -0.5
    n_win_past = WINDOW // TILE
    assert R == TILE and n_sel <= SEL_BLOCK and start == N_PAGES * TILE

    qs = q_ref[...] * scale
    blocks = []
    for g in range(A_N_KV):
        qg = jnp.concatenate([qs[:, (g * A_GROUP + r) * dh:(g * A_GROUP + r + 1) * dh] for r in range(A_GROUP)], axis=0)
        parts = [qg if gg == g else jnp.zeros((RG, dh), f32) for gg in range(A_N_KV)]
        blocks.append(jnp.concatenate(parts, axis=1))
    qbd = jnp.concatenate(blocks, axis=0).astype(bf16)

    tbl_rows = jnp.concatenate([jnp.broadcast_to(tbl_ref[h:h + 1, :], (qb, LANES)) for h in range(A_N_HEADS)], axis=0)
    far_bias = tbl_rows[:, MAX_DISTANCE - 1:MAX_DISTANCE]
    lane = lax.broadcasted_iota(i32, (R, TILE), 1)
    row = lax.broadcasted_iota(i32, (R, TILE), 0)
    qpos = start + row % qb

    def near_bias(dist):
        return jnp.take_along_axis(tbl_rows, jnp.clip(dist, 0, MAX_DISTANCE - 1), axis=1)

    def softmax_rows(s):
        m = jnp.max(s, axis=1, keepdims=True)
        p = jnp.where(s > 0.5 * MASK_VALUE, jnp.exp(s - m), 0.0)
        l = jnp.sum(p, axis=1, keepdims=True)
        return p, jnp.where(l == 0.0, 1.0, l)

    def pad_rows(x):
        return jnp.concatenate([x, jnp.zeros((TILE - x.shape[0], x.shape[1]), x.dtype)], axis=0).astype(bf16)

    dist_c = qpos - (lane * CMP_STRIDE + (CMP_LEN - 1))
    s_c = _dot(qbd, kct_ref[0]) + near_bias(dist_c)
    p_c, l_c = softmax_rows(jnp.where(dist_c >= 0, s_c, MASK_VALUE))
    o_c = _dot(p_c.astype(bf16), vc_ref[0]) / l_c
    p_c = p_c / l_c

    p_sum = []
    for g in range(A_N_KV):
        acc = p_c[g * RG:g * RG + qb]
        for r in range(1, A_GROUP):
            acc = acc + p_c[g * RG + r * qb:g * RG + (r + 1) * qb]
        p_sum.append(acc)
    p_sum = jnp.concatenate(p_sum, axis=0)
    imp = jnp.dot(p_sum, ovl_ref[...], preferred_element_type=f32, precision=_HI)
    s_lane = lax.broadcasted_iota(i32, (A_N_KV * qb, TILE), 1)
    s_qpos = start + lax.broadcasted_iota(i32, (A_N_KV * qb, TILE), 0) % qb
    blk = s_qpos // SEL_BLOCK
    valid = s_lane * SEL_BLOCK <= s_qpos
    forced = (s_lane == 0) | (s_lane == blk) | (s_lane == blk - 1)
    score = jnp.where(valid, imp + jnp.where(forced, FORCE_SCORE, 0.0), -1.0)
    score = jnp.where(s_lane < n_sel, score, -3.0)
    rank = jnp.zeros(score.shape, f32)
    for s2 in range(n_sel):
        col = score[:, s2:s2 + 1]
        rank = rank + jnp.where((col > score) | ((col == score) & (s_lane > s2)), 1.0, 0.0)
    not_chosen = jnp.where(rank < float(min(SEL_TOPN, n_sel)), 0.0, -1.0)
    drop = jnp.concatenate(
        [not_chosen[g * qb:(g + 1) * qb] for g in range(A_N_KV) for _ in range(A_GROUP)], axis=0)
    drop = drop[:, 0:SEL_BLOCK].astype(bf16)
    b_row = lax.broadcasted_iota(i32, (SEL_BLOCK, TILE), 0)
    b_col = lax.broadcasted_iota(i32, (SEL_BLOCK, TILE), 1)

    def drop_unselected(t):
        expand = jnp.where(b_row == t * (TILE // SEL_BLOCK) + b_col // SEL_BLOCK, -MASK_VALUE, 0.0).astype(bf16)
        return _dot(drop, expand)

    def attend(tiles, new_ref, extra):
        n = len(tiles)
        for j, (k_t, _, kind) in enumerate(tiles):
            s = _dot(qbd, k_t.astype(bf16)) + extra(j)
            key0 = start - (n - j) * TILE
            dist = qpos - (key0 + lane)
            if kind == "far":
                s = s + far_bias
            else:
                s = s + near_bias(dist)
            if kind == "edge":
                s = jnp.where(dist < WINDOW, s, MASK_VALUE)
            s_scr[:, j * TILE:(j + 1) * TILE] = s
        new = new_ref[...]
        dist = qpos - (start + lane)
        s = _dot_nt(qbd, pad_rows(new[:, 0:KV_HALF])) + extra(n) + near_bias(dist)
        s_scr[:, n * TILE:(n + 1) * TILE] = jnp.where(dist >= 0, s, MASK_VALUE)
        p, l = softmax_rows(s_scr[:, 0:(n + 1) * TILE])
        p = p.astype(bf16)
        o = _dot(p[:, n * TILE:(n + 1) * TILE], pad_rows(new[:, KV_HALF:KV_ROW]))
        for j, (_, v_t, _) in enumerate(tiles):
            o = o + _dot_nt(p[:, j * TILE:(j + 1) * TILE], v_t.astype(bf16))
        return o / l

    sel_tiles = []
    for t in range(N_PAGES):
        page = sel_pages[t]
        sel_tiles.append((page[0, 0:KV_HALF, :], page[0, KV_HALF:KV_ROW, :], "near" if t == N_PAGES - 1 else "far"))
    o_s = attend(sel_tiles, selnew_ref, lambda j: drop_unselected(j))

    win_tiles = []
    for j in range(n_win_past):
        cols = slice(j * TILE, (j + 1) * TILE)
        kind = "edge" if j == 0 else ("near" if j == n_win_past - 1 else "far")
        win_tiles.append((winpast_ref[0, 0:KV_HALF, cols], winpast_ref[0, KV_HALF:KV_ROW, cols], kind))
    o_w = attend(win_tiles, winnew_ref, lambda j: 0.0)

    for g in range(A_N_KV):
        for r in range(A_GROUP):
            h = g * A_GROUP + r
            rows = slice(g * RG + r * qb, g * RG + (r + 1) * qb)
            cols = slice(g * dh, (g + 1) * dh)
            gt = gate_ref[:, 3 * h:3 * h + 3]
            o_h = gt[:, 0:1] * o_c[rows, cols] + gt[:, 1:2] * o_s[rows, cols] + gt[:, 2:3] * o_w[rows, cols]
            o_ref[:, h * dh:(h + 1) * dh] = o_h.astype(o_ref.dtype)


def nsa_attention_sample(q, gates, row0, Bn, L, start, kct, vc, tbl, overlap, sel_pages, page_table, sel_new,
                         win_past, win_new):
    qb = L
    n_sel = -(-(start + L) // SEL_BLOCK)
    blk0 = row0 // qb
    row_map = lambda b, pt: (blk0 + b, 0)
    seq_map = lambda b, pt: (b, 0, 0)
    page_specs = [
        pl.BlockSpec((1, KV_ROW, PAGE_SIZE), functools.partial(lambda b, pt, j: (pt[b, j], 0, 0), j=j))
        for j in range(N_PAGES)
    ]
    return pl.pallas_call(
        functools.partial(_attn_sample_body, qb=qb, start=start, n_sel=n_sel),
        out_shape=jax.ShapeDtypeStruct((Bn * L, A_Q_DIM), f32),
        grid_spec=pltpu.PrefetchScalarGridSpec(
            num_scalar_prefetch=1,
            grid=(Bn,),
            in_specs=[
                pl.BlockSpec((qb, A_Q_DIM), row_map),
                pl.BlockSpec((qb, LANES), row_map),
                pl.BlockSpec((1, KV_HALF, N_CHUNKS), seq_map),
                pl.BlockSpec((1, N_CHUNKS, KV_HALF), seq_map),
                pl.BlockSpec((A_N_HEADS, LANES), lambda b, pt: (0, 0)),
                pl.BlockSpec((TILE, TILE), lambda b, pt: (0, 0)),
            ] + page_specs + [
                pl.BlockSpec((qb, KV_ROW), row_map),
                pl.BlockSpec((1, KV_ROW, WINDOW), seq_map),
                pl.BlockSpec((qb, KV_ROW), row_map),
            ],
            out_specs=pl.BlockSpec((qb, A_Q_DIM), lambda b, pt: (b, 0)),
            scratch_shapes=[pltpu.VMEM((TILE, (N_PAGES + 1) * TILE), f32)],
        ),
        compiler_params=_cparams(("parallel",)),
        name="nsa_attention_sample",
    )(page_table, q, gates, kct, vc, tbl, overlap, *([sel_pages] * N_PAGES), sel_new, win_past, win_new)


def _pad_lanes(v):
    return jnp.pad(v, (0, LANES - v.shape[0])).reshape(1, LANES)


def _overlap_matrix():
    n_cmp = N_CHUNKS - CMP_RATIO + 1
    c = np.arange(TILE)[:, None] * CMP_STRIDE
    s = np.arange(TILE)[None, :] * SEL_BLOCK
    ov = (c < s + SEL_BLOCK) & (c + CMP_LEN > s) & (np.arange(TILE)[:, None] < n_cmp)
    return jnp.asarray(ov.astype(np.float32))


def _feature_major(x):
    lead = x.shape[:-4]
    n = len(lead)
    return jnp.transpose(x, tuple(range(n)) + (n + 1, n + 2, n + 3, n)).reshape(lead + (KV_ROW, x.shape[-4]))


def _token_major(x_t):
    B, _, T = x_t.shape
    return jnp.transpose(x_t.reshape(B, 2, A_N_KV, A_HEAD_DIM, T), (0, 4, 1, 2, 3))


def kernel(x_prompt, x_sample, state_ssm, state_conv, cache_cmp_kv, cache_sel_kv, cache_win_kv, page_table, ln_g, ln_b, m_in_w, m_conv_w, m_conv_b, m_dt_bias, m_a_log, m_d, m_norm_w, m_out_w, kv_w, cmp_w1, cmp_pe, cmp_w2, q_w, o_w, rel_bias, mlp_w1, mlp_w2):
    Bp, Lp, D = x_prompt.shape
    Bs, Ls, _ = x_sample.shape
    NP, NS = Bp * Lp, Bs * Ls
    past_len = page_table.shape[1] * PAGE_SIZE
    assert past_len == N_PAGES * PAGE_SIZE and Lp == N_PAGES * PAGE_SIZE and cache_win_kv.shape[1] == WINDOW

    in_w = m_in_w[0].astype(bf16)
    z_w = in_w[:, :M_D_INNER]
    xbc_w = in_w[:, M_D_INNER:M_D_INNER + M_CONV_DIM]
    dt_w = jnp.pad(in_w[:, M_D_INNER + M_CONV_DIM:], ((0, 0), (0, LANES - M_N_HEADS)))
    kvw = kv_w.astype(bf16)
    qw = q_w[0].astype(bf16)
    gate_w = jnp.pad(qw[:, A_Q_DIM:], ((0, 0), (0, LANES - 3 * A_N_HEADS)))
    w1b = cmp_w1.astype(bf16)
    w_j = jnp.transpose(w1b, (0, 2, 3, 1, 4)).reshape(2, CMP_STRIDE, A_HEAD_DIM, CMP_RATIO * CMP_HIDDEN)
    zeros = jnp.zeros_like(w_j)
    wbd = jnp.concatenate([jnp.concatenate([w_j, zeros], axis=3), jnp.concatenate([zeros, w_j], axis=3)], axis=2)
    pe_rows = jnp.broadcast_to(cmp_pe.astype(bf16).reshape(2, 1, CMP_LEN * A_HEAD_DIM), (2, SUBLANES, CMP_LEN * A_HEAD_DIM))
    w1_flat = w1b.reshape(2, CMP_LEN * A_HEAD_DIM, CMP_HIDDEN)
    cmp_w = (wbd, pe_rows, w1_flat, cmp_w2[1].astype(bf16), cmp_w2[0].T.astype(bf16))

    x = jnp.concatenate([x_prompt.reshape(NP, D), x_sample.reshape(NS, D)], axis=0)
    xb = x.astype(bf16)
    z = matmul(xb, z_w, f32)
    xbc = matmul(xb, xbc_w, f32)
    dt = matmul(xb, dt_w, f32)
    ssd_w = (m_conv_w[0], m_conv_b[0].reshape(1, -1), _pad_lanes(m_dt_bias[0]), _pad_lanes(m_a_log[0]),
             _pad_lanes(m_d[0]), m_norm_w[0].reshape(1, -1))
    y_p, p_ssm, p_conv = ssd_mixer_core(xbc, z, dt, 0, Bp, Lp, None, None, *ssd_w)
    y_s, s_ssm, s_conv = ssd_mixer_core(xbc, z, dt, NP, Bs, Ls, state_conv[0], state_ssm[0], *ssd_w)
    y = jnp.concatenate([y_p, y_s.astype(bf16)], axis=0)
    h_f, h_b = matmul_residual_ln(y, m_out_w[0].astype(bf16), x, ln_g[0, 0].reshape(1, D), ln_b[0, 0].reshape(1, D))
    h_f, h_b = mlp_residual_ln(h_b, h_f, mlp_w1[0].astype(bf16), mlp_w2[0].astype(bf16),
                               ln_g[0, 1].reshape(1, D), ln_b[0, 1].reshape(1, D))

    cmp_t, sel_t, win_t, sel_pg, win_pg = kv_project_feature_major(h_b, kvw.T, Bp, Lp)
    kv_s = matmul(h_b[NP:], kvw, f32)
    cmp_s, sel_s, win_s = kv_s[:, 0:KV_ROW], kv_s[:, KV_ROW:2 * KV_ROW], kv_s[:, 2 * KV_ROW:3 * KV_ROW]
    kct_p, vc_p = compress_kv(cmp_t, None, *cmp_w)
    kct_s, vc_s = compress_kv(_feature_major(cache_cmp_kv), page_table, *cmp_w)

    q = matmul(h_b, qw[:, :A_Q_DIM], f32)
    gates = matmul(h_b, gate_w, f32, act="sigmoid")
    tbl, tz0, tz1 = bias_table(rel_bias.T)
    overlap = _overlap_matrix()
    o_p = nsa_attention_prompt(q, gates, Bp, Lp, kct_p, vc_p, tbl, tz0, tz1, overlap.T, sel_pg, win_pg)
    o_s = nsa_attention_sample(q[NP:], gates[NP:], 0, Bs, Ls, past_len, kct_s, vc_s, tbl, overlap,
                               _feature_major(cache_sel_kv), page_table, sel_s, _feature_major(cache_win_kv), win_s)
    o = jnp.concatenate([o_p, o_s.astype(bf16)], axis=0)
    h_f, h_b = matmul_residual_ln(o, o_w[0].astype(bf16), h_f, ln_g[1, 0].reshape(1, D), ln_b[1, 0].reshape(1, D))
    h_f, _ = mlp_residual_ln(h_b, h_f, mlp_w1[1].astype(bf16), mlp_w2[1].astype(bf16),
                             ln_g[1, 1].reshape(1, D), ln_b[1, 1].reshape(1, D))

    kv_shape = (2, A_N_KV, A_HEAD_DIM)
    n_keep = min(WINDOW, Lp)
    s_win = jnp.concatenate([cache_win_kv[:, Ls:], win_s.reshape((Bs, Ls) + kv_shape)], axis=1)
    return (
        h_f[:NP].reshape(Bp, Lp, D), h_f[NP:].reshape(Bs, Ls, D),
        p_ssm[None], p_conv[None],
        _token_major(cmp_t), _token_major(sel_t), _token_major(win_t[:, :, Lp - n_keep:]),
        s_ssm[None], s_conv[None],
        cmp_s.reshape((Bs, Ls) + kv_shape), sel_s.reshape((Bs, Ls) + kv_shape), s_win,
    )
```

```python
import functools
import math

import jax
import jax.numpy as jnp
import numpy as np
from jax import lax
from jax.experimental import pallas as pl
from jax.experimental.pallas import tpu as pltpu

f32 = jnp.float32
bf16 = jnp.bfloat16
i32 = jnp.int32

D_MODEL = 1024
DEPTH = 2
DN_ALPHA = (2.0 * DEPTH) ** 0.25
LN_EPS = 1e-5
RMS_EPS = 1e-5
D_FF = 4 * D_MODEL
M_D_INNER = 2 * D_MODEL
M_HEAD_DIM = 64
M_N_HEADS = M_D_INNER // M_HEAD_DIM
M_N_GROUPS = 4
M_HPG = M_N_HEADS // M_N_GROUPS
M_D_STATE = 128
M_CONV = 4
M_CHUNK = 128
M_CONV_DIM = M_D_INNER + 2 * M_N_GROUPS * M_D_STATE
A_HEAD_DIM = 64
A_N_HEADS = D_MODEL // A_HEAD_DIM
A_N_KV = 4
A_GROUP = A_N_HEADS // A_N_KV
A_Q_DIM = A_N_HEADS * A_HEAD_DIM
KV_HALF = A_N_KV * A_HEAD_DIM
KV_ROW = 2 * KV_HALF
CMP_LEN = 32
CMP_STRIDE = 16
CMP_RATIO = CMP_LEN // CMP_STRIDE
CMP_HIDDEN = 2 * A_HEAD_DIM
SEL_BLOCK = 64
SEL_TOPN = 16
WINDOW = 512
Q_BLOCK = 128
N_BUCKETS = 32
MAX_DISTANCE = 128
MASK_VALUE = -1e30
FORCE_SCORE = 1e3
PAGE_SIZE = 128

LANES = 128
SUBLANES = 8
VMEM_LIMIT = 56 * 1024 * 1024

_HI = lax.Precision.HIGHEST


def _cparams(sem):
    return pltpu.CompilerParams(dimension_semantics=sem, vmem_limit_bytes=VMEM_LIMIT)


def _dot(a, b):
    return jnp.dot(a, b, preferred_element_type=f32)


def _dot_nt(a, b):
    return lax.dot_general(a, b, (((1,), (1,)), ((), ())), preferred_element_type=f32)


def _dot_tn(a, b):
    return lax.dot_general(a, b, (((0,), (0,)), ((), ())), preferred_element_type=f32)


def _silu(x):
    return x * (1.0 / (1.0 + jnp.exp(-x)))


def _layer_norm(x, g, b):
    mu = jnp.mean(x, axis=-1, keepdims=True)
    xc = x - mu
    var = jnp.mean(xc * xc, axis=-1, keepdims=True)
    return xc * lax.rsqrt(var + LN_EPS) * g + b


def _mm_body(x_ref, w_ref, o_ref, *, act):
    y = _dot(x_ref[...], w_ref[...])
    if act == "sigmoid":
        y = 1.0 / (1.0 + jnp.exp(-y))
    o_ref[...] = y.astype(o_ref.dtype)


def matmul(x, w, out_dtype, act=None, tm=512, tn=512):
    M, K = x.shape
    N = w.shape[1]
    tn = min(tn, N)
    return pl.pallas_call(
        functools.partial(_mm_body, act=act),
        out_shape=jax.ShapeDtypeStruct((M, N), out_dtype),
        grid=(M // tm, N // tn),
        in_specs=[pl.BlockSpec((tm, K), lambda i, j: (i, 0)), pl.BlockSpec((K, tn), lambda i, j: (0, j))],
        out_specs=pl.BlockSpec((tm, tn), lambda i, j: (i, j)),
        compiler_params=_cparams(("parallel", "parallel")),
        name="matmul",
    )(x, w)


def _mm_res_ln_body(x_ref, w_ref, r_ref, g_ref, b_ref, of_ref, ob_ref):
    y = DN_ALPHA * r_ref[...] + _dot(x_ref[...], w_ref[...])
    h = _layer_norm(y, g_ref[...], b_ref[...])
    of_ref[...] = h
    ob_ref[...] = h.astype(bf16)


def matmul_residual_ln(x, w, resid, g, b, tm=512):
    M, K = x.shape
    N = w.shape[1]
    return pl.pallas_call(
        _mm_res_ln_body,
        out_shape=(jax.ShapeDtypeStruct((M, N), f32), jax.ShapeDtypeStruct((M, N), bf16)),
        grid=(M // tm,),
        in_specs=[
            pl.BlockSpec((tm, K), lambda i: (i, 0)),
            pl.BlockSpec((K, N), lambda i: (0, 0)),
            pl.BlockSpec((tm, N), lambda i: (i, 0)),
            pl.BlockSpec((1, N), lambda i: (0, 0)),
            pl.BlockSpec((1, N), lambda i: (0, 0)),
        ],
        out_specs=(pl.BlockSpec((tm, N), lambda i: (i, 0)), pl.BlockSpec((tm, N), lambda i: (i, 0))),
        compiler_params=_cparams(("parallel",)),
        name="matmul_residual_ln",
    )(x, w, resid, g, b)


def _mlp_body(hb_ref, hf_ref, w1_ref, w2_ref, g_ref, b_ref, of_ref, ob_ref, acc_ref):
    j = pl.program_id(1)

    @pl.when(j == 0)
    def _():
        acc_ref[...] = jnp.zeros_like(acc_ref)

    u = jnp.maximum(_dot(hb_ref[...], w1_ref[...]), 0.0)
    acc_ref[...] += _dot((u * u).astype(bf16), w2_ref[...])

    @pl.when(j == pl.num_programs(1) - 1)
    def _():
        h = _layer_norm(DN_ALPHA * hf_ref[...] + acc_ref[...], g_ref[...], b_ref[...])
        of_ref[...] = h
        ob_ref[...] = h.astype(bf16)


def mlp_residual_ln(hb, hf, w1, w2, g, b, tm=512, tf=512):
    M, D = hb.shape
    F = w1.shape[1]
    return pl.pallas_call(
        _mlp_body,
        out_shape=(jax.ShapeDtypeStruct((M, D), f32), jax.ShapeDtypeStruct((M, D), bf16)),
        grid=(M // tm, F // tf),
        in_specs=[
            pl.BlockSpec((tm, D), lambda i, j: (i, 0)),
            pl.BlockSpec((tm, D), lambda i, j: (i, 0)),
            pl.BlockSpec((D, tf), lambda i, j: (0, j)),
            pl.BlockSpec((tf, D), lambda i, j: (j, 0)),
            pl.BlockSpec((1, D), lambda i, j: (0, 0)),
            pl.BlockSpec((1, D), lambda i, j: (0, 0)),
        ],
        out_specs=(pl.BlockSpec((tm, D), lambda i, j: (i, 0)), pl.BlockSpec((tm, D), lambda i, j: (i, 0))),
        scratch_shapes=[pltpu.VMEM((tm, D), f32)],
        compiler_params=_cparams(("parallel", "arbitrary")),
        name="mlp_residual_ln",
    )(hb, hf, w1, w2, g, b)


def _kv_project_body(wt_ref, h_ref, cmp_ref, sel_ref, win_ref, selp_ref, winp_ref):
    res = _dot_nt(wt_ref[...], h_ref[...])
    tm = h_ref.shape[0]
    cmp_ref[0] = res[0:KV_ROW]
    sel_ref[0] = res[KV_ROW:2 * KV_ROW]
    win_ref[0] = res[2 * KV_ROW:3 * KV_ROW]
    for k in range(tm // PAGE_SIZE):
        cols = slice(k * PAGE_SIZE, (k + 1) * PAGE_SIZE)
        selp_ref[0, k] = res[KV_ROW:2 * KV_ROW, cols].astype(bf16)
        winp_ref[0, k] = res[2 * KV_ROW:3 * KV_ROW, cols].astype(bf16)


def kv_project_feature_major(h_b, w_t, Bn, L, tm=512):
    nj = L // tm
    pages_per_step = tm // PAGE_SIZE
    fm = jax.ShapeDtypeStruct((Bn, KV_ROW, L), f32)
    pg = jax.ShapeDtypeStruct((Bn, L // PAGE_SIZE, KV_ROW, PAGE_SIZE), bf16)
    fm_spec = pl.BlockSpec((1, KV_ROW, tm), lambda b, j: (b, 0, j))
    pg_spec = pl.BlockSpec((1, pages_per_step, KV_ROW, PAGE_SIZE), lambda b, j: (b, j, 0, 0))
    return pl.pallas_call(
        _kv_project_body,
        out_shape=(fm, fm, fm, pg, pg),
        grid=(Bn, nj),
        in_specs=[
            pl.BlockSpec(w_t.shape, lambda b, j: (0, 0)),
            pl.BlockSpec((tm, h_b.shape[1]), lambda b, j: (b * nj + j, 0)),
        ],
        out_specs=(fm_spec, fm_spec, fm_spec, pg_spec, pg_spec),
        compiler_params=_cparams(("parallel", "parallel")),
        name="kv_project_feature_major",
    )(w_t, h_b)


def _ssd_body(*refs, Q, has_init):
    if has_init:
        (xbc_ref, z_ref, dt_ref, conv0_ref, h0_ref, cw_ref, cb_ref, dtb_ref, alog_ref, dsk_ref, nw_ref,
         y_ref, hout_ref, cout_ref, st_ref, xpad_ref, xc_ref, ybuf_ref) = refs
    else:
        (xbc_ref, z_ref, dt_ref, cw_ref, cb_ref, dtb_ref, alog_ref, dsk_ref, nw_ref,
         y_ref, hout_ref, cout_ref, st_ref, xpad_ref, xc_ref, ybuf_ref) = refs
    c = pl.program_id(1)
    P, N, R, G = M_HEAD_DIM, M_D_STATE, M_HPG, M_N_GROUPS
    PAD = SUBLANES

    @pl.when(c == 0)
    def _():
        xpad_ref[0:PAD, :] = jnp.zeros((PAD, M_CONV_DIM), f32)
        if has_init:
            xpad_ref[PAD - (M_CONV - 1):PAD, :] = conv0_ref[0]
            for g in range(G):
                for r in range(R):
                    st_ref[g, :, r * P:(r + 1) * P] = h0_ref[0, g * R + r].T
        else:
            st_ref[...] = jnp.zeros_like(st_ref)

    xpad_ref[PAD:PAD + Q, :] = xbc_ref[...]
    acc = cb_ref[...] + xpad_ref[pl.ds(PAD - 3, Q), :] * cw_ref[0:1, :]
    for k in range(1, M_CONV):
        acc = acc + xpad_ref[pl.ds(PAD - 3 + k, Q), :] * cw_ref[k:k + 1, :]
    xc_ref[...] = _silu(acc)
    cout_ref[0] = xpad_ref[pl.ds(Q + PAD - 3, 3), :]
    xpad_ref[0:PAD, :] = xpad_ref[pl.ds(Q, PAD), :]

    xdt = dt_ref[...] + dtb_ref[...]
    dt = jnp.maximum(xdt, 0.0) + jnp.log1p(jnp.exp(-jnp.abs(xdt)))
    a = dt * (-jnp.exp(alog_ref[...]))
    ri = lax.broadcasted_iota(i32, (Q, Q), 0)
    ci = lax.broadcasted_iota(i32, (Q, Q), 1)
    tril = ri >= ci
    a_cum = jnp.dot(tril.astype(f32), a, preferred_element_type=f32, precision=_HI)
    a_cum_t = a_cum.T
    a_last = a_cum[Q - 1:Q, :]
    e_cum = jnp.exp(a_cum)
    e_rest = jnp.exp(a_last - a_cum)
    e_last = jnp.exp(a_last)
    dsk = dsk_ref[...]

    for g in range(G):
        bg = xc_ref[:, M_D_INNER + g * N:M_D_INNER + (g + 1) * N].astype(bf16)
        cg = xc_ref[:, M_D_INNER + G * N + g * N:M_D_INNER + G * N + (g + 1) * N].astype(bf16)
        gmat = _dot_nt(cg, bg)
        yoff = _dot(cg, st_ref[g].astype(bf16))
        xds = []
        for r in range(R):
            h = g * R + r
            col = a_cum[:, h:h + 1]
            row = a_cum_t[h:h + 1, :]
            lmat = jnp.exp(jnp.where(tril, col - row, -jnp.inf))
            xh = xc_ref[:, h * P:(h + 1) * P]
            xdt_h = xh * dt[:, h:h + 1]
            ydiag = _dot((gmat * lmat).astype(bf16), xdt_h.astype(bf16))
            y_h = ydiag + yoff[:, r * P:(r + 1) * P] * e_cum[:, h:h + 1] + xh * dsk[:, h:h + 1]
            ybuf_ref[:, h * P:(h + 1) * P] = y_h
            xds.append((xdt_h * e_rest[:, h:h + 1]).astype(bf16))
        new = _dot_tn(bg, jnp.concatenate(xds, axis=1))
        dec = jnp.concatenate([jnp.broadcast_to(e_last[:, g * R + r:g * R + r + 1], (1, P)) for r in range(R)], axis=1)
        st_ref[g] = st_ref[g] * dec + new

    GW = M_D_INNER // G
    for g in range(G):
        yg = ybuf_ref[:, g * GW:(g + 1) * GW] * _silu(z_ref[:, g * GW:(g + 1) * GW])
        ms = jnp.mean(yg * yg, axis=-1, keepdims=True)
        y_ref[:, g * GW:(g + 1) * GW] = (yg * lax.rsqrt(ms + RMS_EPS) * nw_ref[:, g * GW:(g + 1) * GW]).astype(y_ref.dtype)

    @pl.when(c == pl.num_programs(1) - 1)
    def _():
        for g in range(G):
            for r in range(R):
                hout_ref[0, g * R + r] = st_ref[g, :, r * P:(r + 1) * P].T


def ssd_mixer_core(xbc, z, dt, row0, Bn, L, conv0, h0, conv_w, conv_b, dt_bias, a_log, d_skip, norm_w):
    Q = M_CHUNK if L % M_CHUNK == 0 else L
    nc = L // Q
    has_init = h0 is not None
    blk0 = row0 // Q
    row_map = lambda b, c: (blk0 + b * nc + c, 0)
    const2 = lambda b, c: (0, 0)
    in_specs = [
        pl.BlockSpec((Q, M_CONV_DIM), row_map),
        pl.BlockSpec((Q, M_D_INNER), row_map),
        pl.BlockSpec((Q, LANES), row_map),
    ]
    args = [xbc, z, dt]
    if has_init:
        in_specs += [
            pl.BlockSpec((1, M_CONV - 1, M_CONV_DIM), lambda b, c: (b, 0, 0)),
            pl.BlockSpec((1, M_N_HEADS, M_HEAD_DIM, M_D_STATE), lambda b, c: (b, 0, 0, 0)),
        ]
        args += [conv0, h0]
    in_specs += [
        pl.BlockSpec((M_CONV, M_CONV_DIM), const2),
        pl.BlockSpec((1, M_CONV_DIM), const2),
        pl.BlockSpec((1, LANES), const2),
        pl.BlockSpec((1, LANES), const2),
        pl.BlockSpec((1, LANES), const2),
        pl.BlockSpec((1, M_D_INNER), const2),
    ]
    args += [conv_w, conv_b, dt_bias, a_log, d_skip, norm_w]
    y_dtype = bf16 if Q % 16 == 0 else f32
    return pl.pallas_call(
        functools.partial(_ssd_body, Q=Q, has_init=has_init),
        out_shape=(
            jax.ShapeDtypeStruct((Bn * L, M_D_INNER), y_dtype),
            jax.ShapeDtypeStruct((Bn, M_N_HEADS, M_HEAD_DIM, M_D_STATE), f32),
            jax.ShapeDtypeStruct((Bn, M_CONV - 1, M_CONV_DIM), f32),
        ),
        grid=(Bn, nc),
        in_specs=in_specs,
        out_specs=(
            pl.BlockSpec((Q, M_D_INNER), lambda b, c: (b * nc + c, 0)),
            pl.BlockSpec((1, M_N_HEADS, M_HEAD_DIM, M_D_STATE), lambda b, c: (b, 0, 0, 0)),
            pl.BlockSpec((1, M_CONV - 1, M_CONV_DIM), lambda b, c: (b, 0, 0)),
        ),
        scratch_shapes=[
            pltpu.VMEM((M_N_GROUPS, M_D_STATE, M_HPG * M_HEAD_DIM), f32),
            pltpu.VMEM((Q + SUBLANES, M_CONV_DIM), f32),
            pltpu.VMEM((Q, M_CONV_DIM), f32),
            pltpu.VMEM((Q, M_D_INNER), f32),
        ],
        compiler_params=_cparams(("parallel", "arbitrary")),
        name="ssd_mixer_core",
    )(*args)


N_PAGES = 16
N_CHUNKS = N_PAGES * PAGE_SIZE // CMP_STRIDE
N_SLABS = KV_ROW // LANES


def _compress_body(*refs, paged):
    if paged:
        refs = refs[1:]
    pages = refs[:N_PAGES]
    wbd_ref, pe_ref, w1f_ref, w2_ref, w2t_ref, kct_ref, vc_ref, xs_ref = refs[N_PAGES:]
    H = CMP_HIDDEN
    for p in range(N_PAGES):
        for sl in range(N_SLABS):
            xs_ref[sl, p * PAGE_SIZE:(p + 1) * PAGE_SIZE, :] = pages[p][0, sl * LANES:(sl + 1) * LANES, :].T
    row = lax.broadcasted_iota(i32, (N_CHUNKS, H), 0)
    for kv in range(2):
        pe_term = _dot(pe_ref[kv], w1f_ref[kv])[0:1, :]
        for gp in range(A_N_KV // 2):
            acc = jnp.zeros((N_CHUNKS, 2 * CMP_RATIO * H), f32)
            for j in range(CMP_STRIDE):
                x = xs_ref[kv * (A_N_KV // 2) + gp, pl.ds(j, N_CHUNKS, stride=CMP_STRIDE), :].astype(bf16)
                acc = acc + _dot(x, wbd_ref[kv, j])
            for gi in range(2):
                g = gp * 2 + gi
                p0 = acc[:, gi * CMP_RATIO * H:gi * CMP_RATIO * H + H]
                p1 = acc[:, gi * CMP_RATIO * H + H:(gi + 1) * CMP_RATIO * H]
                p1_next = jnp.where(row == N_CHUNKS - 1, 0.0, pltpu.roll(p1, N_CHUNKS - 1, 0))
                hid = _silu(p0 + p1_next + pe_term).astype(bf16)
                if kv == 0:
                    kct_ref[0, g * A_HEAD_DIM:(g + 1) * A_HEAD_DIM, :] = _dot_nt(w2t_ref[...], hid).astype(kct_ref.dtype)
                else:
                    vc_ref[0, :, g * A_HEAD_DIM:(g + 1) * A_HEAD_DIM] = _dot(hid, w2_ref[...]).astype(vc_ref.dtype)


def compress_kv(pages, page_table, wbd, pe_rows, w1_flat, w2_v, w2t_k):
    paged = page_table is not None
    Bn = page_table.shape[0] if paged else pages.shape[0]
    if paged:
        page_specs = [
            pl.BlockSpec((1, KV_ROW, PAGE_SIZE), functools.partial(lambda b, pt, j: (pt[b, j], 0, 0), j=j))
            for j in range(N_PAGES)
        ]
        const = lambda nd: (lambda b, pt: (0,) * nd)
        out_map = lambda b, pt: (b, 0, 0)
    else:
        page_specs = [
            pl.BlockSpec((1, KV_ROW, PAGE_SIZE), functools.partial(lambda b, j: (b, 0, j), j=j)) for j in range(N_PAGES)
        ]
        const = lambda nd: (lambda b: (0,) * nd)
        out_map = lambda b: (b, 0, 0)
    in_specs = page_specs + [
        pl.BlockSpec(wbd.shape, const(4)),
        pl.BlockSpec(pe_rows.shape, const(3)),
        pl.BlockSpec(w1_flat.shape, const(3)),
        pl.BlockSpec(w2_v.shape, const(2)),
        pl.BlockSpec(w2t_k.shape, const(2)),
    ]
    out_shape = (jax.ShapeDtypeStruct((Bn, KV_HALF, N_CHUNKS), bf16), jax.ShapeDtypeStruct((Bn, N_CHUNKS, KV_HALF), bf16))
    out_specs = (pl.BlockSpec((1, KV_HALF, N_CHUNKS), out_map), pl.BlockSpec((1, N_CHUNKS, KV_HALF), out_map))
    scratch = [pltpu.VMEM((N_SLABS, N_PAGES * PAGE_SIZE, LANES), f32)]
    body = functools.partial(_compress_body, paged=paged)
    args = ([pages] * N_PAGES) + [wbd, pe_rows, w1_flat, w2_v, w2t_k]
    if paged:
        return pl.pallas_call(
            body, out_shape=out_shape,
            grid_spec=pltpu.PrefetchScalarGridSpec(
                num_scalar_prefetch=1, grid=(Bn,), in_specs=in_specs, out_specs=out_specs, scratch_shapes=scratch),
            compiler_params=_cparams(("parallel",)), name="compress_kv_paged",
        )(page_table, *args)
    return pl.pallas_call(
        body, out_shape=out_shape, grid=(Bn,), in_specs=in_specs, out_specs=out_specs, scratch_shapes=scratch,
        compiler_params=_cparams(("parallel",)), name="compress_kv",
    )(*args)


def _bias_table_body(rb_ref, tbl_ref, tz0_ref, tz1_ref):
    n = lax.broadcasted_iota(i32, (A_N_HEADS, LANES), 1)
    max_exact = N_BUCKETS // 2
    large = max_exact + (jnp.log(jnp.maximum(n, max_exact).astype(f32) / max_exact)
                         / math.log(MAX_DISTANCE / max_exact) * (N_BUCKETS - max_exact)).astype(i32)
    bucket = jnp.where(n < max_exact, n, jnp.minimum(large, N_BUCKETS - 1))
    tbl = jnp.zeros((A_N_HEADS, LANES), f32)
    for b in range(N_BUCKETS):
        tbl = jnp.where(bucket == b, rb_ref[:, b:b + 1], tbl)
    tbl_ref[...] = tbl
    dist = lax.broadcasted_iota(i32, (LANES, LANES), 0) - lax.broadcasted_iota(i32, (LANES, LANES), 1)
    for h in range(A_N_HEADS):
        row = jnp.broadcast_to(tbl[h:h + 1, :], (LANES, LANES))
        own = jnp.take_along_axis(row, jnp.clip(dist, 0, MAX_DISTANCE - 1), axis=1)
        tz0_ref[h] = jnp.where(dist >= 0, own, MASK_VALUE)
        tz1_ref[h] = jnp.take_along_axis(row, jnp.minimum(dist + LANES, MAX_DISTANCE - 1), axis=1)


def bias_table(rel_bias_t):
    assert MAX_DISTANCE <= LANES
    tile = jax.ShapeDtypeStruct((A_N_HEADS, LANES, LANES), f32)
    return pl.pallas_call(
        _bias_table_body, out_shape=(jax.ShapeDtypeStruct((A_N_HEADS, LANES), f32), tile, tile), name="bias_table",
    )(rel_bias_t)


TILE = 128
WIN_TILES = WINDOW // TILE


WIDE = 2 * TILE
N_SLOTS = N_PAGES * TILE // WIDE
BIG = -MASK_VALUE


SEL, WIN = 0, 1
FAR, NEAR = 0, 1
WIN_SLOT0 = N_SLOTS


def _attn_prompt_body(q_ref, gate_ref, kct_ref, vc_ref, tbl_ref, tz0_ref, tz1_ref, ovlt_ref, selp_ref, winp_ref, o_ref,
                      s_scr, m_scr, l_scr, acc_scr, lhs_scr, oc_scr, fb_scr, *, qb, n_sel):
    i = pl.program_id(1)
    R = A_GROUP * qb
    dh = A_HEAD_DIM
    pos0 = i * qb
    odd = (i % 2) == 1
    td = i // 2
    scale = dh ** -0.5
    n_rank = SUBLANES * (-(-n_sel // SUBLANES))

    lane = lax.broadcasted_iota(i32, (R, TILE), 1)
    q_in_blk = jnp.concatenate([lax.broadcasted_iota(i32, (qb, TILE), 0)] * A_GROUP, axis=0)
    qpos = pos0 + q_in_blk
    neg_tile = jnp.full((R, TILE), MASK_VALUE, f32)

    s_idx = lax.broadcasted_iota(i32, (n_rank, qb), 0)
    s_qpos = pos0 + lax.broadcasted_iota(i32, (n_rank, qb), 1)
    blk = s_qpos // SEL_BLOCK
    sel_valid = s_idx * SEL_BLOCK <= s_qpos
    sel_forced = (s_idx == 0) | (s_idx == blk) | (s_idx == blk - 1)

    f_row = lax.broadcasted_iota(i32, (TILE - dh, WIDE), 0)
    flag_rows = [jnp.where(f_row == k, BIG, 0.0).astype(bf16) for k in range(2)]
    zero_flag = jnp.zeros((TILE - dh, WIDE), bf16)
    zero_drop = jnp.zeros((TILE, WIDE), bf16)
    b_row = lax.broadcasted_iota(i32, (TILE, WIDE), 0)
    b_col = lax.broadcasted_iota(i32, (TILE, WIDE), 1)
    f_lane = lax.broadcasted_iota(i32, (R, TILE - dh), 1)
    flags = jnp.where(f_lane == 0, jnp.where(td < 1, -1.0, 0.0),
                      jnp.where(f_lane == 1, jnp.where(td < 2, -1.0, 0.0), 0.0)).astype(bf16)
    win_thr = q_in_blk + jnp.where(odd, TILE, 0)

    def pair(ref, T, g, half):
        rows = slice(half * KV_HALF + g * dh, half * KV_HALF + (g + 1) * dh)
        return jnp.concatenate([ref[0, 2 * T, rows, :], ref[0, 2 * T + 1, rows, :]], axis=1)

    def drop_rows(T):
        return jnp.where(b_row == T * (WIDE // SEL_BLOCK) + b_col // SEL_BLOCK, BIG, 0.0).astype(bf16)

    def store_scores(g, br, slot, s, left, right, kind, first=False):
        s_l = s[:, 0:TILE] if left is None else s[:, 0:TILE] + left
        s_r = s[:, TILE:WIDE] if right is None else s[:, TILE:WIDE] + right
        s_scr[g, slot, :, 0:TILE] = s_l
        s_scr[g, slot, :, TILE:WIDE] = s_r
        mx = jnp.maximum(s_l, s_r)
        m_scr[g, br, kind] = mx if first else jnp.maximum(m_scr[g, br, kind], mx)

    def accumulate(g, br, slot, kind, v_t, first=False):
        shift = m_scr[g, br, kind]
        p_l = jnp.exp(s_scr[g, slot, :, 0:TILE] - shift)
        p_r = jnp.exp(s_scr[g, slot, :, TILE:WIDE] - shift)
        pv = _dot_nt(jnp.concatenate([p_l, p_r], axis=1).astype(bf16), v_t)
        if first:
            l_scr[g, br] = p_l + p_r
            acc_scr[g, br] = pv
        else:
            l_scr[g, br] += p_l + p_r
            acc_scr[g, br] += pv

    def row_max(g, br):
        fb = fb_scr[g][:, 0:1]
        m = jnp.maximum(jnp.max(m_scr[g, br, FAR], axis=1, keepdims=True) + fb,
                        jnp.max(m_scr[g, br, NEAR], axis=1, keepdims=True))
        m_scr[g, br, FAR] = jnp.broadcast_to(m - fb, (R, TILE))
        m_scr[g, br, NEAR] = jnp.broadcast_to(m, (R, TILE))

    def result(g, br):
        return acc_scr[g, br] / jnp.sum(l_scr[g, br], axis=1, keepdims=True)

    def near_tiles(g):
        tz0 = tz0_ref[g * A_GROUP:(g + 1) * A_GROUP].reshape(R, TILE)
        tz1 = tz1_ref[g * A_GROUP:(g + 1) * A_GROUP].reshape(R, TILE)
        return jnp.where(odd, fb_scr[g], tz1), jnp.where(odd, tz1, tz0), jnp.where(odd, tz0, neg_tile)

    t_prev = jnp.maximum(td - 1, 0)
    t_first = jnp.maximum(td - 2, 0)
    n_far = jnp.maximum(td - 1, 0)

    for g in range(A_N_KV):
        heads = [g * A_GROUP + r for r in range(A_GROUP)]
        qg = jnp.concatenate([q_ref[:, h * dh:(h + 1) * dh] for h in heads], axis=0)
        qg = (qg.astype(f32) * scale).astype(bf16)
        tbl_g = jnp.concatenate([jnp.broadcast_to(tbl_ref[h:h + 1, :], (qb, TILE)) for h in heads], axis=0)
        fb_scr[g] = jnp.concatenate(
            [jnp.broadcast_to(tbl_ref[h:h + 1, MAX_DISTANCE - 1:MAX_DISTANCE], (qb, TILE)) for h in heads], axis=0)

        dist_c = qpos - (lane * CMP_STRIDE + (CMP_LEN - 1))
        s_c = _dot(qg, kct_ref[0, g * dh:(g + 1) * dh, :])
        s_c = s_c + jnp.take_along_axis(tbl_g, jnp.clip(dist_c, 0, MAX_DISTANCE - 1), axis=1)
        s_c = jnp.where(dist_c >= 0, s_c, MASK_VALUE)
        m_c = jnp.max(s_c, axis=1, keepdims=True)
        p_c = jnp.where(dist_c >= 0, jnp.exp(s_c - m_c), 0.0)
        l_c = jnp.sum(p_c, axis=1, keepdims=True)
        l_c = jnp.where(l_c == 0.0, 1.0, l_c)
        oc_scr[g] = _dot(p_c.astype(bf16), vc_ref[0, :, g * dh:(g + 1) * dh]) / l_c
        p_c = p_c / l_c

        p_sum = p_c[0:qb]
        for r in range(1, A_GROUP):
            p_sum = p_sum + p_c[r * qb:(r + 1) * qb]
        imp_t = lax.dot_general(ovlt_ref[0:n_rank, :], p_sum, (((1,), (1,)), ((), ())),
                                preferred_element_type=f32, precision=_HI)
        score = jnp.where(sel_valid, imp_t + jnp.where(sel_forced, FORCE_SCORE, 0.0), -1.0)
        score = jnp.where(s_idx < n_sel, score, -3.0)
        rank = jnp.zeros((n_rank, qb), f32)
        for s2 in range(n_sel):
            other = score[s2:s2 + 1, :]
            rank = rank + jnp.where((other > score) | ((other == score) & (s_idx > s2)), 1.0, 0.0)
        dropped_t = jnp.where(rank < float(min(SEL_TOPN, n_sel)), 0.0, -1.0)
        dropped = jnp.concatenate([dropped_t, jnp.zeros((TILE - n_rank, qb), f32)], axis=0).T
        lhs_scr[g, SEL] = jnp.concatenate([qg, flags, jnp.concatenate([dropped] * A_GROUP, axis=0).astype(bf16)], axis=1)
        lhs_scr[g, WIN] = jnp.concatenate([qg, flags, jnp.zeros((R, TILE), bf16)], axis=1)
        m_scr[g, SEL, FAR] = neg_tile

    too_old_l = jnp.where(lane > win_thr, 0.0, MASK_VALUE)
    too_old_r = jnp.where(lane + TILE > win_thr, 0.0, MASK_VALUE)
    for g in range(A_N_KV):
        prev_right, diag_left, diag_right = near_tiles(g)
        lhs_win = lhs_scr[g, WIN]
        s = _dot(lhs_win, jnp.concatenate([pair(winp_ref, t_first, g, 0), flag_rows[1], zero_drop], axis=0))
        store_scores(g, WIN, WIN_SLOT0, s, too_old_l, too_old_r, FAR, first=True)
        s = _dot(lhs_win, jnp.concatenate([pair(winp_ref, t_prev, g, 0), flag_rows[0], zero_drop], axis=0))
        store_scores(g, WIN, WIN_SLOT0 + 1, s, fb_scr[g], prev_right, NEAR, first=True)
        s = _dot(lhs_win, jnp.concatenate([pair(winp_ref, td, g, 0), zero_flag, zero_drop], axis=0))
        store_scores(g, WIN, WIN_SLOT0 + 2, s, diag_left, diag_right, NEAR)
        lhs_sel = lhs_scr[g, SEL]
        s = _dot(lhs_sel, jnp.concatenate([pair(selp_ref, t_prev, g, 0), flag_rows[0], drop_rows(t_prev)], axis=0))
        store_scores(g, SEL, N_SLOTS - 2, s, fb_scr[g], prev_right, NEAR, first=True)
        s = _dot(lhs_sel, jnp.concatenate([pair(selp_ref, td, g, 0), zero_flag, drop_rows(td)], axis=0))
        store_scores(g, SEL, N_SLOTS - 1, s, diag_left, diag_right, NEAR)

    def far_scores(T, carry):
        drop = drop_rows(T)
        for g in range(A_N_KV):
            s = _dot(lhs_scr[g, SEL], jnp.concatenate([pair(selp_ref, T, g, 0), zero_flag, drop], axis=0))
            store_scores(g, SEL, T, s, None, None, FAR)
        return carry

    lax.fori_loop(0, n_far, far_scores, 0)

    for g in range(A_N_KV):
        row_max(g, WIN)
        row_max(g, SEL)
    for g in range(A_N_KV):
        accumulate(g, WIN, WIN_SLOT0, FAR, pair(winp_ref, t_first, g, 1), first=True)
        accumulate(g, WIN, WIN_SLOT0 + 1, NEAR, pair(winp_ref, t_prev, g, 1))
        accumulate(g, WIN, WIN_SLOT0 + 2, NEAR, pair(winp_ref, td, g, 1))
        accumulate(g, SEL, N_SLOTS - 2, NEAR, pair(selp_ref, t_prev, g, 1), first=True)
        accumulate(g, SEL, N_SLOTS - 1, NEAR, pair(selp_ref, td, g, 1))

    def far_accumulate(T, carry):
        for g in range(A_N_KV):
            accumulate(g, SEL, T, FAR, pair(selp_ref, T, g, 1))
        return carry

    lax.fori_loop(0, n_far, far_accumulate, 0)

    for g in range(A_N_KV):
        o_c, o_s, o_w = oc_scr[g], result(g, SEL), result(g, WIN)
        for r in range(A_GROUP):
            h = g * A_GROUP + r
            rows = slice(r * qb, (r + 1) * qb)
            gt = gate_ref[:, 3 * h:3 * h + 3]
            o_h = gt[:, 0:1] * o_c[rows] + gt[:, 1:2] * o_s[rows] + gt[:, 2:3] * o_w[rows]
            o_ref[:, h * dh:(h + 1) * dh] = o_h.astype(o_ref.dtype)


def nsa_attention_prompt(q, gates, Bn, L, kct, vc, tbl, tz0, tz1, overlap_t, sel_pages, win_pages):
    qb = Q_BLOCK
    nqb = L // qb
    n_sel = -(-L // SEL_BLOCK)
    n_tiles = L // TILE
    assert n_tiles == N_PAGES and qb == TILE
    R = A_GROUP * qb
    row_map = lambda b, i: (b * nqb + i, 0)
    seq_map3 = lambda b, i: (b, 0, 0)
    seq_map4 = lambda b, i: (b, 0, 0, 0)
    const2 = lambda b, i: (0, 0)
    const3 = lambda b, i: (0, 0, 0)
    return pl.pallas_call(
        functools.partial(_attn_prompt_body, qb=qb, n_sel=n_sel),
        out_shape=jax.ShapeDtypeStruct((Bn * L, A_Q_DIM), bf16),
        grid=(Bn, nqb),
        in_specs=[
            pl.BlockSpec((qb, A_Q_DIM), row_map),
            pl.BlockSpec((qb, LANES), row_map),
            pl.BlockSpec((1, KV_HALF, N_CHUNKS), seq_map3),
            pl.BlockSpec((1, N_CHUNKS, KV_HALF), seq_map3),
            pl.BlockSpec((A_N_HEADS, LANES), const2),
            pl.BlockSpec((A_N_HEADS, TILE, TILE), const3),
            pl.BlockSpec((A_N_HEADS, TILE, TILE), const3),
            pl.BlockSpec((TILE, TILE), const2),
            pl.BlockSpec((1, n_tiles, KV_ROW, TILE), seq_map4),
            pl.BlockSpec((1, n_tiles, KV_ROW, TILE), seq_map4),
        ],
        out_specs=pl.BlockSpec((qb, A_Q_DIM), row_map),
        scratch_shapes=[
            pltpu.VMEM((A_N_KV, N_SLOTS + 3, R, WIDE), f32),
            pltpu.VMEM((A_N_KV, 2, 2, R, TILE), f32),
            pltpu.VMEM((A_N_KV, 2, R, TILE), f32),
            pltpu.VMEM((A_N_KV, 2, R, A_HEAD_DIM), f32),
            pltpu.VMEM((A_N_KV, 2, R, WIDE), bf16),
            pltpu.VMEM((A_N_KV, R, A_HEAD_DIM), f32),
            pltpu.VMEM((A_N_KV, R, TILE), f32),
        ],
        compiler_params=_cparams(("parallel", "arbitrary")),
        name="nsa_attention_prompt",
    )(q, gates, kct, vc, tbl, tz0, tz1, overlap_t, sel_pages, win_pages)


def _attn_sample_body(pt_ref, q_ref, gate_ref, kct_ref, vc_ref, tbl_ref, ovl_ref, *refs, qb, start, n_sel):
    del pt_ref
    sel_pages = refs[:N_PAGES]
    selnew_ref, winpast_ref, winnew_ref, o_ref, s_scr = refs[N_PAGES:]
    dh = A_HEAD_DIM
    RG = A_GROUP * qb
    R = A_N_KV * RG
    scale = dh ** -0.5
    n_win_past = WINDOW // TILE
    assert R == TILE and n_sel <= SEL_BLOCK and start == N_PAGES * TILE

    qs = q_ref[...] * scale
    blocks = []
    for g in range(A_N_KV):
        qg = jnp.concatenate([qs[:, (g * A_GROUP + r) * dh:(g * A_GROUP + r + 1) * dh] for r in range(A_GROUP)], axis=0)
        parts = [qg if gg == g else jnp.zeros((RG, dh), f32) for gg in range(A_N_KV)]
        blocks.append(jnp.concatenate(parts, axis=1))
    qbd = jnp.concatenate(blocks, axis=0).astype(bf16)

    tbl_rows = jnp.concatenate([jnp.broadcast_to(tbl_ref[h:h + 1, :], (qb, LANES)) for h in range(A_N_HEADS)], axis=0)
    far_bias = tbl_rows[:, MAX_DISTANCE - 1:MAX_DISTANCE]
    lane = lax.broadcasted_iota(i32, (R, TILE), 1)
    row = lax.broadcasted_iota(i32, (R, TILE), 0)
    qpos = start + row % qb

    def near_bias(dist):
        return jnp.take_along_axis(tbl_rows, jnp.clip(dist, 0, MAX_DISTANCE - 1), axis=1)

    def softmax_rows(s):
        m = jnp.max(s, axis=1, keepdims=True)
        p = jnp.where(s > 0.5 * MASK_VALUE, jnp.exp(s - m), 0.0)
        l = jnp.sum(p, axis=1, keepdims=True)
        return p, jnp.where(l == 0.0, 1.0, l)

    def pad_rows(x):
        return jnp.concatenate([x, jnp.zeros((TILE - x.shape[0], x.shape[1]), x.dtype)], axis=0).astype(bf16)

    dist_c = qpos - (lane * CMP_STRIDE + (CMP_LEN - 1))
    s_c = _dot(qbd, kct_ref[0]) + near_bias(dist_c)
    p_c, l_c = softmax_rows(jnp.where(dist_c >= 0, s_c, MASK_VALUE))
    o_c = _dot(p_c.astype(bf16), vc_ref[0]) / l_c
    p_c = p_c / l_c

    p_sum = []
    for g in range(A_N_KV):
        acc = p_c[g * RG:g * RG + qb]
        for r in range(1, A_GROUP):
            acc = acc + p_c[g * RG + r * qb:g * RG + (r + 1) * qb]
        p_sum.append(acc)
    p_sum = jnp.concatenate(p_sum, axis=0)
    imp = jnp.dot(p_sum, ovl_ref[...], preferred_element_type=f32, precision=_HI)
    s_lane = lax.broadcasted_iota(i32, (A_N_KV * qb, TILE), 1)
    s_qpos = start + lax.broadcasted_iota(i32, (A_N_KV * qb, TILE), 0) % qb
    blk = s_qpos // SEL_BLOCK
    valid = s_lane * SEL_BLOCK <= s_qpos
    forced = (s_lane == 0) | (s_lane == blk) | (s_lane == blk - 1)
    score = jnp.where(valid, imp + jnp.where(forced, FORCE_SCORE, 0.0), -1.0)
    score = jnp.where(s_lane < n_sel, score, -3.0)
    rank = jnp.zeros(score.shape, f32)
    for s2 in range(n_sel):
        col = score[:, s2:s2 + 1]
        rank = rank + jnp.where((col > score) | ((col == score) & (s_lane > s2)), 1.0, 0.0)
    not_chosen = jnp.where(rank < float(min(SEL_TOPN, n_sel)), 0.0, -1.0)
    drop = jnp.concatenate(
        [not_chosen[g * qb:(g + 1) * qb] for g in range(A_N_KV) for _ in range(A_GROUP)], axis=0)
    drop = drop[:, 0:SEL_BLOCK].astype(bf16)
    b_row = lax.broadcasted_iota(i32, (SEL_BLOCK, TILE), 0)
    b_col = lax.broadcasted_iota(i32, (SEL_BLOCK, TILE), 1)

    def drop_unselected(t):
        expand = jnp.where(b_row == t * (TILE // SEL_BLOCK) + b_col // SEL_BLOCK, -MASK_VALUE, 0.0).astype(bf16)
        return _dot(drop, expand)

    def attend(tiles, new_ref, extra):
        n = len(tiles)
        for j, (k_t, _, kind) in enumerate(tiles):
            s = _dot(qbd, k_t.astype(bf16)) + extra(j)
            key0 = start - (n - j) * TILE
            dist = qpos - (key0 + lane)
            if kind == "far":
                s = s + far_bias
            else:
                s = s + near_bias(dist)
            if kind == "edge":
                s = jnp.where(dist < WINDOW, s, MASK_VALUE)
            s_scr[:, j * TILE:(j + 1) * TILE] = s
        new = new_ref[...]
        dist = qpos - (start + lane)
        s = _dot_nt(qbd, pad_rows(new[:, 0:KV_HALF])) + extra(n) + near_bias(dist)
        s_scr[:, n * TILE:(n + 1) * TILE] = jnp.where(dist >= 0, s, MASK_VALUE)
        p, l = softmax_rows(s_scr[:, 0:(n + 1) * TILE])
        p = p.astype(bf16)
        o = _dot(p[:, n * TILE:(n + 1) * TILE], pad_rows(new[:, KV_HALF:KV_ROW]))
        for j, (_, v_t, _) in enumerate(tiles):
            o = o + _dot_nt(p[:, j * TILE:(j + 1) * TILE], v_t.astype(bf16))
        return o / l

    sel_tiles = []
    for t in range(N_PAGES):
        page = sel_pages[t]
        sel_tiles.append((page[0, 0:KV_HALF, :], page[0, KV_HALF:KV_ROW, :], "near" if t == N_PAGES - 1 else "far"))
    o_s = attend(sel_tiles, selnew_ref, lambda j: drop_unselected(j))

    win_tiles = []
    for j in range(n_win_past):
        cols = slice(j * TILE, (j + 1) * TILE)
        kind = "edge" if j == 0 else ("near" if j == n_win_past - 1 else "far")
        win_tiles.append((winpast_ref[0, 0:KV_HALF, cols], winpast_ref[0, KV_HALF:KV_ROW, cols], kind))
    o_w = attend(win_tiles, winnew_ref, lambda j: 0.0)

    for g in range(A_N_KV):
        for r in range(A_GROUP):
            h = g * A_GROUP + r
            rows = slice(g * RG + r * qb, g * RG + (r + 1) * qb)
            cols = slice(g * dh, (g + 1) * dh)
            gt = gate_ref[:, 3 * h:3 * h + 3]
            o_h = gt[:, 0:1] * o_c[rows, cols] + gt[:, 1:2] * o_s[rows, cols] + gt[:, 2:3] * o_w[rows, cols]
            o_ref[:, h * dh:(h + 1) * dh] = o_h.astype(o_ref.dtype)


def nsa_attention_sample(q, gates, row0, Bn, L, start, kct, vc, tbl, overlap, sel_pages, page_table, sel_new,
                         win_past, win_new):
    qb = L
    n_sel = -(-(start + L) // SEL_BLOCK)
    blk0 = row0 // qb
    row_map = lambda b, pt: (blk0 + b, 0)
    seq_map = lambda b, pt: (b, 0, 0)
    page_specs = [
        pl.BlockSpec((1, KV_ROW, PAGE_SIZE), functools.partial(lambda b, pt, j: (pt[b, j], 0, 0), j=j))
        for j in range(N_PAGES)
    ]
    return pl.pallas_call(
        functools.partial(_attn_sample_body, qb=qb, start=start, n_sel=n_sel),
        out_shape=jax.ShapeDtypeStruct((Bn * L, A_Q_DIM), f32),
        grid_spec=pltpu.PrefetchScalarGridSpec(
            num_scalar_prefetch=1,
            grid=(Bn,),
            in_specs=[
                pl.BlockSpec((qb, A_Q_DIM), row_map),
                pl.BlockSpec((qb, LANES), row_map),
                pl.BlockSpec((1, KV_HALF, N_CHUNKS), seq_map),
                pl.BlockSpec((1, N_CHUNKS, KV_HALF), seq_map),
                pl.BlockSpec((A_N_HEADS, LANES), lambda b, pt: (0, 0)),
                pl.BlockSpec((TILE, TILE), lambda b, pt: (0, 0)),
            ] + page_specs + [
                pl.BlockSpec((qb, KV_ROW), row_map),
                pl.BlockSpec((1, KV_ROW, WINDOW), seq_map),
                pl.BlockSpec((qb, KV_ROW), row_map),
            ],
            out_specs=pl.BlockSpec((qb, A_Q_DIM), lambda b, pt: (b, 0)),
            scratch_shapes=[pltpu.VMEM((TILE, (N_PAGES + 1) * TILE), f32)],
        ),
        compiler_params=_cparams(("parallel",)),
        name="nsa_attention_sample",
    )(page_table, q, gates, kct, vc, tbl, overlap, *([sel_pages] * N_PAGES), sel_new, win_past, win_new)


def _pad_lanes(v):
    return jnp.pad(v, (0, LANES - v.shape[0])).reshape(1, LANES)


def _overlap_matrix():
    n_cmp = N_CHUNKS - CMP_RATIO + 1
    c = np.arange(TILE)[:, None] * CMP_STRIDE
    s = np.arange(TILE)[None, :] * SEL_BLOCK
    ov = (c < s + SEL_BLOCK) & (c + CMP_LEN > s) & (np.arange(TILE)[:, None] < n_cmp)
    return jnp.asarray(ov.astype(np.float32))


def _feature_major(x):
    lead = x.shape[:-4]
    n = len(lead)
    return jnp.transpose(x, tuple(range(n)) + (n + 1, n + 2, n + 3, n)).reshape(lead + (KV_ROW, x.shape[-4]))


def _token_major(x_t):
    B, _, T = x_t.shape
    return jnp.transpose(x_t.reshape(B, 2, A_N_KV, A_HEAD_DIM, T), (0, 4, 1, 2, 3))


def kernel(x_prompt, x_sample, state_ssm, state_conv, cache_cmp_kv, cache_sel_kv, cache_win_kv, page_table, ln_g, ln_b, m_in_w, m_conv_w, m_conv_b, m_dt_bias, m_a_log, m_d, m_norm_w, m_out_w, kv_w, cmp_w1, cmp_pe, cmp_w2, q_w, o_w, rel_bias, mlp_w1, mlp_w2):
    Bp, Lp, D = x_prompt.shape
    Bs, Ls, _ = x_sample.shape
    NP, NS = Bp * Lp, Bs * Ls
    past_len = page_table.shape[1] * PAGE_SIZE
    assert past_len == N_PAGES * PAGE_SIZE and Lp == N_PAGES * PAGE_SIZE and cache_win_kv.shape[1] == WINDOW

    in_w = m_in_w[0].astype(bf16)
    z_w = in_w[:, :M_D_INNER]
    xbc_w = in_w[:, M_D_INNER:M_D_INNER + M_CONV_DIM]
    dt_w = jnp.pad(in_w[:, M_D_INNER + M_CONV_DIM:], ((0, 0), (0, LANES - M_N_HEADS)))
    kvw = kv_w.astype(bf16)
    qw = q_w[0].astype(bf16)
    gate_w = jnp.pad(qw[:, A_Q_DIM:], ((0, 0), (0, LANES - 3 * A_N_HEADS)))
    w1b = cmp_w1.astype(bf16)
    w_j = jnp.transpose(w1b, (0, 2, 3, 1, 4)).reshape(2, CMP_STRIDE, A_HEAD_DIM, CMP_RATIO * CMP_HIDDEN)
    zeros = jnp.zeros_like(w_j)
    wbd = jnp.concatenate([jnp.concatenate([w_j, zeros], axis=3), jnp.concatenate([zeros, w_j], axis=3)], axis=2)
    pe_rows = jnp.broadcast_to(cmp_pe.astype(bf16).reshape(2, 1, CMP_LEN * A_HEAD_DIM), (2, SUBLANES, CMP_LEN * A_HEAD_DIM))
    w1_flat = w1b.reshape(2, CMP_LEN * A_HEAD_DIM, CMP_HIDDEN)
    cmp_w = (wbd, pe_rows, w1_flat, cmp_w2[1].astype(bf16), cmp_w2[0].T.astype(bf16))

    x = jnp.concatenate([x_prompt.reshape(NP, D), x_sample.reshape(NS, D)], axis=0)
    xb = x.astype(bf16)
    z = matmul(xb, z_w, f32)
    xbc = matmul(xb, xbc_w, f32)
    dt = matmul(xb, dt_w, f32)
    ssd_w = (m_conv_w[0], m_conv_b[0].reshape(1, -1), _pad_lanes(m_dt_bias[0]), _pad_lanes(m_a_log[0]),
             _pad_lanes(m_d[0]), m_norm_w[0].reshape(1, -1))
    y_p, p_ssm, p_conv = ssd_mixer_core(xbc, z, dt, 0, Bp, Lp, None, None, *ssd_w)
    y_s, s_ssm, s_conv = ssd_mixer_core(xbc, z, dt, NP, Bs, Ls, state_conv[0], state_ssm[0], *ssd_w)
    y = jnp.concatenate([y_p, y_s.astype(bf16)], axis=0)
    h_f, h_b = matmul_residual_ln(y, m_out_w[0].astype(bf16), x, ln_g[0, 0].reshape(1, D), ln_b[0, 0].reshape(1, D))
    h_f, h_b = mlp_residual_ln(h_b, h_f, mlp_w1[0].astype(bf16), mlp_w2[0].astype(bf16),
                               ln_g[0, 1].reshape(1, D), ln_b[0, 1].reshape(1, D))

    cmp_t, sel_t, win_t, sel_pg, win_pg = kv_project_feature_major(h_b, kvw.T, Bp, Lp)
    kv_s = matmul(h_b[NP:], kvw, f32)
    cmp_s, sel_s, win_s = kv_s[:, 0:KV_ROW], kv_s[:, KV_ROW:2 * KV_ROW], kv_s[:, 2 * KV_ROW:3 * KV_ROW]
    kct_p, vc_p = compress_kv(cmp_t, None, *cmp_w)
    kct_s, vc_s = compress_kv(_feature_major(cache_cmp_kv), page_table, *cmp_w)

    q = matmul(h_b, qw[:, :A_Q_DIM], f32)
    gates = matmul(h_b, gate_w, f32, act="sigmoid")
    tbl, tz0, tz1 = bias_table(rel_bias.T)
    overlap = _overlap_matrix()
    o_p = nsa_attention_prompt(q, gates, Bp, Lp, kct_p, vc_p, tbl, tz0, tz1, overlap.T, sel_pg, win_pg)
    o_s = nsa_attention_sample(q[NP:], gates[NP:], 0, Bs, Ls, past_len, kct_s, vc_s, tbl, overlap,
                               _feature_major(cache_sel_kv), page_table, sel_s, _feature_major(cache_win_kv), win_s)
    o = jnp.concatenate([o_p, o_s.astype(bf16)], axis=0)
    h_f, h_b = matmul_residual_ln(o, o_w[0].astype(bf16), h_f, ln_g[1, 0].reshape(1, D), ln_b[1, 0].reshape(1, D))
    h_f, _ = mlp_residual_ln(h_b, h_f, mlp_w1[1].astype(bf16), mlp_w2[1].astype(bf16),
                             ln_g[1, 1].reshape(1, D), ln_b[1, 1].reshape(1, D))

    kv_shape = (2, A_N_KV, A_HEAD_DIM)
    n_keep = min(WINDOW, Lp)
    s_win = jnp.concatenate([cache_win_kv[:, Ls:], win_s.reshape((Bs, Ls) + kv_shape)], axis=1)
    return (
        h_f[:NP].reshape(Bp, Lp, D), h_f[NP:].reshape(Bs, Ls, D),
        p_ssm[None], p_conv[None],
        _token_major(cmp_t), _token_major(sel_t), _token_major(win_t[:, :, Lp - n_keep:]),
        s_ssm[None], s_conv[None],
        cmp_s.reshape((Bs, Ls) + kv_shape), sel_s.reshape((Bs, Ls) + kv_shape), s_win,
    )
```

```python
import functools
import math

import jax
import jax.numpy as jnp
import numpy as np
from jax import lax
from jax.experimental import pallas as pl
from jax.experimental.pallas import tpu as pltpu

f32 = jnp.float32
bf16 = jnp.bfloat16
i32 = jnp.int32

D_MODEL = 1024
DEPTH = 2
DN_ALPHA = (2.0 * DEPTH) ** 0.25
LN_EPS = 1e-5
RMS_EPS = 1e-5
D_FF = 4 * D_MODEL
M_D_INNER = 2 * D_MODEL
M_HEAD_DIM = 64
M_N_HEADS = M_D_INNER // M_HEAD_DIM
M_N_GROUPS = 4
M_HPG = M_N_HEADS // M_N_GROUPS
M_D_STATE = 128
M_CONV = 4
M_CHUNK = 128
M_CONV_DIM = M_D_INNER + 2 * M_N_GROUPS * M_D_STATE
A_HEAD_DIM = 64
A_N_HEADS = D_MODEL // A_HEAD_DIM
A_N_KV = 4
A_GROUP = A_N_HEADS // A_N_KV
A_Q_DIM = A_N_HEADS * A_HEAD_DIM
KV_HALF = A_N_KV * A_HEAD_DIM
KV_ROW = 2 * KV_HALF
CMP_LEN = 32
CMP_STRIDE = 16
CMP_RATIO = CMP_LEN // CMP_STRIDE
CMP_HIDDEN = 2 * A_HEAD_DIM
SEL_BLOCK = 64
SEL_TOPN = 16
WINDOW = 512
Q_BLOCK = 128
N_BUCKETS = 32
MAX_DISTANCE = 128
MASK_VALUE = -1e30
FORCE_SCORE = 1e3
PAGE_SIZE = 128

LANES = 128
SUBLANES = 8
VMEM_LIMIT = 56 * 1024 * 1024

_HI = lax.Precision.HIGHEST


def _cparams(sem):
    return pltpu.CompilerParams(dimension_semantics=sem, vmem_limit_bytes=VMEM_LIMIT)


def _dot(a, b):
    return jnp.dot(a, b, preferred_element_type=f32)


def _dot_nt(a, b):
    return lax.dot_general(a, b, (((1,), (1,)), ((), ())), preferred_element_type=f32)


def _dot_tn(a, b):
    return lax.dot_general(a, b, (((0,), (0,)), ((), ())), preferred_element_type=f32)


def _silu(x):
    return x * (1.0 / (1.0 + jnp.exp(-x)))


def _layer_norm(x, g, b):
    mu = jnp.mean(x, axis=-1, keepdims=True)
    xc = x - mu
    var = jnp.mean(xc * xc, axis=-1, keepdims=True)
    return xc * lax.rsqrt(var + LN_EPS) * g + b


def _mm_body(x_ref, w_ref, o_ref, *, act):
    y = _dot(x_ref[...], w_ref[...])
    if act == "sigmoid":
        y = 1.0 / (1.0 + jnp.exp(-y))
    o_ref[...] = y.astype(o_ref.dtype)


def matmul(x, w, out_dtype, act=None, tm=512, tn=512):
    M, K = x.shape
    N = w.shape[1]
    tn = min(tn, N)
    return pl.pallas_call(
        functools.partial(_mm_body, act=act),
        out_shape=jax.ShapeDtypeStruct((M, N), out_dtype),
        grid=(M // tm, N // tn),
        in_specs=[pl.BlockSpec((tm, K), lambda i, j: (i, 0)), pl.BlockSpec((K, tn), lambda i, j: (0, j))],
        out_specs=pl.BlockSpec((tm, tn), lambda i, j: (i, j)),
        compiler_params=_cparams(("parallel", "parallel")),
        name="matmul",
    )(x, w)


def _mm_res_ln_body(x_ref, w_ref, r_ref, g_ref, b_ref, of_ref, ob_ref):
    y = DN_ALPHA * r_ref[...] + _dot(x_ref[...], w_ref[...])
    h = _layer_norm(y, g_ref[...], b_ref[...])
    of_ref[...] = h
    ob_ref[...] = h.astype(bf16)


def matmul_residual_ln(x, w, resid, g, b, tm=512):
    M, K = x.shape
    N = w.shape[1]
    return pl.pallas_call(
        _mm_res_ln_body,
        out_shape=(jax.ShapeDtypeStruct((M, N), f32), jax.ShapeDtypeStruct((M, N), bf16)),
        grid=(M // tm,),
        in_specs=[
            pl.BlockSpec((tm, K), lambda i: (i, 0)),
            pl.BlockSpec((K, N), lambda i: (0, 0)),
            pl.BlockSpec((tm, N), lambda i: (i, 0)),
            pl.BlockSpec((1, N), lambda i: (0, 0)),
            pl.BlockSpec((1, N), lambda i: (0, 0)),
        ],
        out_specs=(pl.BlockSpec((tm, N), lambda i: (i, 0)), pl.BlockSpec((tm, N), lambda i: (i, 0))),
        compiler_params=_cparams(("parallel",)),
        name="matmul_residual_ln",
    )(x, w, resid, g, b)


def _mlp_body(hb_ref, hf_ref, w1_ref, w2_ref, g_ref, b_ref, of_ref, ob_ref, acc_ref):
    j = pl.program_id(1)

    @pl.when(j == 0)
    def _():
        acc_ref[...] = jnp.zeros_like(acc_ref)

    u = jnp.maximum(_dot(hb_ref[...], w1_ref[...]), 0.0)
    acc_ref[...] += _dot((u * u).astype(bf16), w2_ref[...])

    @pl.when(j == pl.num_programs(1) - 1)
    def _():
        h = _layer_norm(DN_ALPHA * hf_ref[...] + acc_ref[...], g_ref[...], b_ref[...])
        of_ref[...] = h
        ob_ref[...] = h.astype(bf16)


def mlp_residual_ln(hb, hf, w1, w2, g, b, tm=1024, tf=1024):
    M, D = hb.shape
    F = w1.shape[1]
    return pl.pallas_call(
        _mlp_body,
        out_shape=(jax.ShapeDtypeStruct((M, D), f32), jax.ShapeDtypeStruct((M, D), bf16)),
        grid=(M // tm, F // tf),
        in_specs=[
            pl.BlockSpec((tm, D), lambda i, j: (i, 0)),
            pl.BlockSpec((tm, D), lambda i, j: (i, 0)),
            pl.BlockSpec((D, tf), lambda i, j: (0, j)),
            pl.BlockSpec((tf, D), lambda i, j: (j, 0)),
            pl.BlockSpec((1, D), lambda i, j: (0, 0)),
            pl.BlockSpec((1, D), lambda i, j: (0, 0)),
        ],
        out_specs=(pl.BlockSpec((tm, D), lambda i, j: (i, 0)), pl.BlockSpec((tm, D), lambda i, j: (i, 0))),
        scratch_shapes=[pltpu.VMEM((tm, D), f32)],
        compiler_params=_cparams(("parallel", "arbitrary")),
        name="mlp_residual_ln",
    )(hb, hf, w1, w2, g, b)


def _kv_project_body(wt_ref, h_ref, cmp_ref, sel_ref, win_ref, selp_ref, winp_ref):
    res = _dot_nt(wt_ref[...], h_ref[...])
    tm = h_ref.shape[0]
    cmp_ref[0] = res[0:KV_ROW]
    sel_ref[0] = res[KV_ROW:2 * KV_ROW]
    win_ref[0] = res[2 * KV_ROW:3 * KV_ROW]
    for k in range(tm // PAGE_SIZE):
        cols = slice(k * PAGE_SIZE, (k + 1) * PAGE_SIZE)
        selp_ref[0, k] = res[KV_ROW:2 * KV_ROW, cols].astype(bf16)
        winp_ref[0, k] = res[2 * KV_ROW:3 * KV_ROW, cols].astype(bf16)


def kv_project_feature_major(h_b, w_t, Bn, L, tm=512):
    nj = L // tm
    pages_per_step = tm // PAGE_SIZE
    fm = jax.ShapeDtypeStruct((Bn, KV_ROW, L), f32)
    pg = jax.ShapeDtypeStruct((Bn, L // PAGE_SIZE, KV_ROW, PAGE_SIZE), bf16)
    fm_spec = pl.BlockSpec((1, KV_ROW, tm), lambda b, j: (b, 0, j))
    pg_spec = pl.BlockSpec((1, pages_per_step, KV_ROW, PAGE_SIZE), lambda b, j: (b, j, 0, 0))
    return pl.pallas_call(
        _kv_project_body,
        out_shape=(fm, fm, fm, pg, pg),
        grid=(Bn, nj),
        in_specs=[
            pl.BlockSpec(w_t.shape, lambda b, j: (0, 0)),
            pl.BlockSpec((tm, h_b.shape[1]), lambda b, j: (b * nj + j, 0)),
        ],
        out_specs=(fm_spec, fm_spec, fm_spec, pg_spec, pg_spec),
        compiler_params=_cparams(("parallel", "parallel")),
        name="kv_project_feature_major",
    )(w_t, h_b)


SSD_SEQS_PER_STEP = 4


def _ssd_body(*refs, Q, BB, nc, has_init):
    if has_init:
        (xbc_ref, z_ref, dt_ref, conv0_ref, h0_ref, cw_ref, cb_ref, dtb_ref, alog_ref, dsk_ref, nw_ref,
         y_ref, hout_ref, cout_ref, st_ref, xpad_ref, xc_ref, ybuf_ref) = refs
    else:
        (xbc_ref, z_ref, dt_ref, cw_ref, cb_ref, dtb_ref, alog_ref, dsk_ref, nw_ref,
         y_ref, hout_ref, cout_ref, st_ref, xpad_ref, xc_ref, ybuf_ref) = refs
    c = pl.program_id(1)
    single_chunk = nc == 1
    P, N, R, G = M_HEAD_DIM, M_D_STATE, M_HPG, M_N_GROUPS
    PAD = SUBLANES

    per_head_state = single_chunk and has_init

    def first_chunk():
        for bb in range(BB):
            xpad_ref[bb, 0:PAD, :] = jnp.zeros((PAD, M_CONV_DIM), f32)
            if has_init:
                xpad_ref[bb, PAD - (M_CONV - 1):PAD, :] = conv0_ref[bb]
            if per_head_state:
                continue
            if has_init:
                for g in range(G):
                    for r in range(R):
                        st_ref[bb, g, :, r * P:(r + 1) * P] = h0_ref[bb, g * R + r].T
            else:
                st_ref[bb] = jnp.zeros(st_ref.shape[1:], f32)

    def last_chunk():
        if per_head_state:
            return
        for bb in range(BB):
            for g in range(G):
                for r in range(R):
                    hout_ref[bb, g * R + r] = st_ref[bb, g, :, r * P:(r + 1) * P].T

    if single_chunk:
        first_chunk()
    else:
        pl.when(c == 0)(first_chunk)

    ri = lax.broadcasted_iota(i32, (Q, Q), 0)
    ci = lax.broadcasted_iota(i32, (Q, Q), 1)
    tril = ri >= ci
    dsk = dsk_ref[...]
    GW = M_D_INNER // G

    seqs = range(BB)
    rows = [slice(bb * Q, (bb + 1) * Q) for bb in seqs]
    dt, a_cum, a_cum_t, e_cum, e_rest, e_last = [], [], [], [], [], []
    for bb in seqs:
        xpad_ref[bb, PAD:PAD + Q, :] = xbc_ref[rows[bb], :]
        acc = cb_ref[...] + xpad_ref[bb, pl.ds(PAD - 3, Q), :] * cw_ref[0:1, :]
        for k in range(1, M_CONV):
            acc = acc + xpad_ref[bb, pl.ds(PAD - 3 + k, Q), :] * cw_ref[k:k + 1, :]
        xc_ref[bb] = _silu(acc)
        cout_ref[bb] = xpad_ref[bb, pl.ds(Q + PAD - 3, 3), :]
        xpad_ref[bb, 0:PAD, :] = xpad_ref[bb, pl.ds(Q, PAD), :]

        xdt = dt_ref[rows[bb], :] + dtb_ref[...]
        dt.append(jnp.maximum(xdt, 0.0) + jnp.log1p(jnp.exp(-jnp.abs(xdt))))
        a = dt[bb] * (-jnp.exp(alog_ref[...]))
        a_cum.append(jnp.dot(tril.astype(f32), a, preferred_element_type=f32, precision=_HI))
        a_cum_t.append(a_cum[bb].T)
        a_last = a_cum[bb][Q - 1:Q, :]
        e_cum.append(jnp.exp(a_cum[bb]))
        e_rest.append(jnp.exp(a_last - a_cum[bb]))
        e_last.append(jnp.exp(a_last))

    for g in range(G):
        bg = [xc_ref[bb, :, M_D_INNER + g * N:M_D_INNER + (g + 1) * N].astype(bf16) for bb in seqs]
        cg = [xc_ref[bb, :, M_D_INNER + G * N + g * N:M_D_INNER + G * N + (g + 1) * N].astype(bf16) for bb in seqs]
        gmat = [_dot_nt(cg[bb], bg[bb]) for bb in seqs]
        if not per_head_state:
            yoff = [_dot(cg[bb], st_ref[bb, g].astype(bf16)) for bb in seqs]
        xds = [[] for _ in seqs]
        for r in range(R):
            h = g * R + r
            for bb in seqs:
                col = a_cum[bb][:, h:h + 1]
                row = a_cum_t[bb][h:h + 1, :]
                lmat = jnp.exp(jnp.where(tril, col - row, -jnp.inf))
                xh = xc_ref[bb, :, h * P:(h + 1) * P]
                xdt_h = xh * dt[bb][:, h:h + 1]
                ydiag = _dot((gmat[bb] * lmat).astype(bf16), xdt_h.astype(bf16))
                xds_h = (xdt_h * e_rest[bb][:, h:h + 1]).astype(bf16)
                if per_head_state:
                    h_in = h0_ref[bb, h]
                    yoff_h = _dot_nt(cg[bb], h_in.astype(bf16))
                    hout_ref[bb, h] = h_in * e_last[bb][:, h:h + 1] + _dot_tn(xds_h, bg[bb])
                else:
                    yoff_h = yoff[bb][:, r * P:(r + 1) * P]
                    xds[bb].append(xds_h)
                y_h = ydiag + yoff_h * e_cum[bb][:, h:h + 1] + xh * dsk[:, h:h + 1]
                ybuf_ref[bb, :, h * P:(h + 1) * P] = y_h
        if per_head_state:
            continue
        for bb in seqs:
            new = _dot_tn(bg[bb], jnp.concatenate(xds[bb], axis=1))
            dec = jnp.concatenate(
                [jnp.broadcast_to(e_last[bb][:, g * R + r:g * R + r + 1], (1, P)) for r in range(R)], axis=1)
            st_ref[bb, g] = st_ref[bb, g] * dec + new

    for g in range(G):
        cols = slice(g * GW, (g + 1) * GW)
        for bb in seqs:
            yg = ybuf_ref[bb, :, cols] * _silu(z_ref[rows[bb], cols])
            ms = jnp.mean(yg * yg, axis=-1, keepdims=True)
            y_ref[rows[bb], cols] = (yg * lax.rsqrt(ms + RMS_EPS) * nw_ref[:, cols]).astype(y_ref.dtype)

    if single_chunk:
        last_chunk()
    else:
        pl.when(c == pl.num_programs(1) - 1)(last_chunk)


def ssd_mixer_core(xbc, z, dt, row0, Bn, L, conv0, h0, conv_w, conv_b, dt_bias, a_log, d_skip, norm_w):
    Q = M_CHUNK if L % M_CHUNK == 0 else L
    nc = L // Q
    has_init = h0 is not None
    BB = SSD_SEQS_PER_STEP if nc == 1 and Bn % SSD_SEQS_PER_STEP == 0 else 1
    blk0 = row0 // (BB * Q)
    row_map = lambda b, c: (blk0 + b * nc + c, 0)
    const2 = lambda b, c: (0, 0)
    in_specs = [
        pl.BlockSpec((BB * Q, M_CONV_DIM), row_map),
        pl.BlockSpec((BB * Q, M_D_INNER), row_map),
        pl.BlockSpec((BB * Q, LANES), row_map),
    ]
    args = [xbc, z, dt]
    if has_init:
        in_specs += [
            pl.BlockSpec((BB, M_CONV - 1, M_CONV_DIM), lambda b, c: (b, 0, 0)),
            pl.BlockSpec((BB, M_N_HEADS, M_HEAD_DIM, M_D_STATE), lambda b, c: (b, 0, 0, 0)),
        ]
        args += [conv0, h0]
    in_specs += [
        pl.BlockSpec((M_CONV, M_CONV_DIM), const2),
        pl.BlockSpec((1, M_CONV_DIM), const2),
        pl.BlockSpec((1, LANES), const2),
        pl.BlockSpec((1, LANES), const2),
        pl.BlockSpec((1, LANES), const2),
        pl.BlockSpec((1, M_D_INNER), const2),
    ]
    args += [conv_w, conv_b, dt_bias, a_log, d_skip, norm_w]
    y_dtype = bf16 if (BB * Q) % 16 == 0 else f32
    return pl.pallas_call(
        functools.partial(_ssd_body, Q=Q, BB=BB, nc=nc, has_init=has_init),
        out_shape=(
            jax.ShapeDtypeStruct((Bn * L, M_D_INNER), y_dtype),
            jax.ShapeDtypeStruct((Bn, M_N_HEADS, M_HEAD_DIM, M_D_STATE), f32),
            jax.ShapeDtypeStruct((Bn, M_CONV - 1, M_CONV_DIM), f32),
        ),
        grid=(Bn // BB, nc),
        in_specs=in_specs,
        out_specs=(
            pl.BlockSpec((BB * Q, M_D_INNER), lambda b, c: (b * nc + c, 0)),
            pl.BlockSpec((BB, M_N_HEADS, M_HEAD_DIM, M_D_STATE), lambda b, c: (b, 0, 0, 0)),
            pl.BlockSpec((BB, M_CONV - 1, M_CONV_DIM), lambda b, c: (b, 0, 0)),
        ),
        scratch_shapes=[
            pltpu.VMEM((1, 1, SUBLANES, LANES) if nc == 1 and has_init
                       else (BB, M_N_GROUPS, M_D_STATE, M_HPG * M_HEAD_DIM), f32),
            pltpu.VMEM((BB, Q + SUBLANES, M_CONV_DIM), f32),
            pltpu.VMEM((BB, Q, M_CONV_DIM), f32),
            pltpu.VMEM((BB, Q, M_D_INNER), f32),
        ],
        compiler_params=_cparams(("parallel", "arbitrary")),
        name="ssd_mixer_core",
    )(*args)


N_PAGES = 16
N_CHUNKS = N_PAGES * PAGE_SIZE // CMP_STRIDE
N_SLABS = KV_ROW // LANES


def _compress_body(*refs, paged):
    if paged:
        refs = refs[1:]
    pages = refs[:N_PAGES]
    wbd_ref, pe_ref, w1f_ref, w2_ref, w2t_ref, kct_ref, vc_ref, xs_ref = refs[N_PAGES:]
    H = CMP_HIDDEN
    for p in range(N_PAGES):
        for sl in range(N_SLABS):
            xs_ref[sl, p * PAGE_SIZE:(p + 1) * PAGE_SIZE, :] = pages[p][0, sl * LANES:(sl + 1) * LANES, :].T
    row = lax.broadcasted_iota(i32, (N_CHUNKS, H), 0)
    for kv in range(2):
        pe_term = _dot(pe_ref[kv], w1f_ref[kv])[0:1, :]
        for gp in range(A_N_KV // 2):
            acc = jnp.zeros((N_CHUNKS, 2 * CMP_RATIO * H), f32)
            for jp in range(CMP_STRIDE // 2):
                x = jnp.concatenate(
                    [xs_ref[kv * (A_N_KV // 2) + gp, pl.ds(2 * jp + k, N_CHUNKS, stride=CMP_STRIDE), :] for k in range(2)],
                    axis=1).astype(bf16)
                acc = acc + _dot(x, wbd_ref[kv, jp])
            for gi in range(2):
                g = gp * 2 + gi
                p0 = acc[:, gi * CMP_RATIO * H:gi * CMP_RATIO * H + H]
                p1 = acc[:, gi * CMP_RATIO * H + H:(gi + 1) * CMP_RATIO * H]
                p1_next = jnp.where(row == N_CHUNKS - 1, 0.0, pltpu.roll(p1, N_CHUNKS - 1, 0))
                hid = _silu(p0 + p1_next + pe_term).astype(bf16)
                if kv == 0:
                    kct_ref[0, g * A_HEAD_DIM:(g + 1) * A_HEAD_DIM, :] = _dot_nt(w2t_ref[...], hid).astype(kct_ref.dtype)
                else:
                    vc_ref[0, :, g * A_HEAD_DIM:(g + 1) * A_HEAD_DIM] = _dot(hid, w2_ref[...]).astype(vc_ref.dtype)


def compress_kv(pages, page_table, wbd, pe_rows, w1_flat, w2_v, w2t_k):
    paged = page_table is not None
    Bn = page_table.shape[0] if paged else pages.shape[0]
    if paged:
        page_specs = [
            pl.BlockSpec((1, KV_ROW, PAGE_SIZE), functools.partial(lambda b, pt, j: (pt[b, j], 0, 0), j=j))
            for j in range(N_PAGES)
        ]
        const = lambda nd: (lambda b, pt: (0,) * nd)
        out_map = lambda b, pt: (b, 0, 0)
    else:
        page_specs = [
            pl.BlockSpec((1, KV_ROW, PAGE_SIZE), functools.partial(lambda b, j: (b, 0, j), j=j)) for j in range(N_PAGES)
        ]
        const = lambda nd: (lambda b: (0,) * nd)
        out_map = lambda b: (b, 0, 0)
    in_specs = page_specs + [
        pl.BlockSpec(wbd.shape, const(4)),
        pl.BlockSpec(pe_rows.shape, const(3)),
        pl.BlockSpec(w1_flat.shape, const(3)),
        pl.BlockSpec(w2_v.shape, const(2)),
        pl.BlockSpec(w2t_k.shape, const(2)),
    ]
    out_shape = (jax.ShapeDtypeStruct((Bn, KV_HALF, N_CHUNKS), bf16), jax.ShapeDtypeStruct((Bn, N_CHUNKS, KV_HALF), bf16))
    out_specs = (pl.BlockSpec((1, KV_HALF, N_CHUNKS), out_map), pl.BlockSpec((1, N_CHUNKS, KV_HALF), out_map))
    scratch = [pltpu.VMEM((N_SLABS, N_PAGES * PAGE_SIZE, LANES), f32)]
    body = functools.partial(_compress_body, paged=paged)
    args = ([pages] * N_PAGES) + [wbd, pe_rows, w1_flat, w2_v, w2t_k]
    if paged:
        return pl.pallas_call(
            body, out_shape=out_shape,
            grid_spec=pltpu.PrefetchScalarGridSpec(
                num_scalar_prefetch=1, grid=(Bn,), in_specs=in_specs, out_specs=out_specs, scratch_shapes=scratch),
            compiler_params=_cparams(("parallel",)), name="compress_kv_paged",
        )(page_table, *args)
    return pl.pallas_call(
        body, out_shape=out_shape, grid=(Bn,), in_specs=in_specs, out_specs=out_specs, scratch_shapes=scratch,
        compiler_params=_cparams(("parallel",)), name="compress_kv",
    )(*args)


def _bias_table_body(rb_ref, tbl_ref, tz0_ref, tz1_ref):
    n = lax.broadcasted_iota(i32, (A_N_HEADS, LANES), 1)
    max_exact = N_BUCKETS // 2
    large = max_exact + (jnp.log(jnp.maximum(n, max_exact).astype(f32) / max_exact)
                         / math.log(MAX_DISTANCE / max_exact) * (N_BUCKETS - max_exact)).astype(i32)
    bucket = jnp.where(n < max_exact, n, jnp.minimum(large, N_BUCKETS - 1))
    tbl = jnp.zeros((A_N_HEADS, LANES), f32)
    for b in range(N_BUCKETS):
        tbl = jnp.where(bucket == b, rb_ref[:, b:b + 1], tbl)
    tbl_ref[...] = tbl
    dist = lax.broadcasted_iota(i32, (LANES, LANES), 0) - lax.broadcasted_iota(i32, (LANES, LANES), 1)
    for h in range(A_N_HEADS):
        row = jnp.broadcast_to(tbl[h:h + 1, :], (LANES, LANES))
        own = jnp.take_along_axis(row, jnp.clip(dist, 0, MAX_DISTANCE - 1), axis=1)
        tz0_ref[h] = jnp.where(dist >= 0, own, MASK_VALUE)
        tz1_ref[h] = jnp.take_along_axis(row, jnp.minimum(dist + LANES, MAX_DISTANCE - 1), axis=1)


def bias_table(rel_bias_t):
    assert MAX_DISTANCE <= LANES
    tile = jax.ShapeDtypeStruct((A_N_HEADS, LANES, LANES), f32)
    return pl.pallas_call(
        _bias_table_body, out_shape=(jax.ShapeDtypeStruct((A_N_HEADS, LANES), f32), tile, tile), name="bias_table",
    )(rel_bias_t)


TILE = 128
WIN_TILES = WINDOW // TILE


WIDE = 2 * TILE
N_SLOTS = N_PAGES * TILE // WIDE
BIG = -MASK_VALUE


SEL, WIN = 0, 1
FAR, NEAR = 0, 1
WIN_SLOT0 = N_SLOTS


def _attn_prompt_body(q_ref, gate_ref, kct_ref, vc_ref, tbl_ref, tz0_ref, tz1_ref, ovlt_ref, selp_ref, winp_ref, o_ref,
                      s_scr, m_scr, l_scr, acc_scr, lhs_scr, oc_scr, fb_scr, *, qb, n_sel):
    i = pl.program_id(1)
    R = A_GROUP * qb
    dh = A_HEAD_DIM
    pos0 = i * qb
    odd = (i % 2) == 1
    td = i // 2
    scale = dh ** -0.5
    n_rank = SUBLANES * (-(-n_sel // SUBLANES))

    lane = lax.broadcasted_iota(i32, (R, TILE), 1)
    q_in_blk = jnp.concatenate([lax.broadcasted_iota(i32, (qb, TILE), 0)] * A_GROUP, axis=0)
    qpos = pos0 + q_in_blk
    neg_tile = jnp.full((R, TILE), MASK_VALUE, f32)

    s_idx = lax.broadcasted_iota(i32, (n_rank, qb), 0)
    s_qpos = pos0 + lax.broadcasted_iota(i32, (n_rank, qb), 1)
    blk = s_qpos // SEL_BLOCK
    sel_valid = s_idx * SEL_BLOCK <= s_qpos
    sel_forced = (s_idx == 0) | (s_idx == blk) | (s_idx == blk - 1)

    f_row = lax.broadcasted_iota(i32, (TILE - dh, WIDE), 0)
    flag_rows = [jnp.where(f_row == k, BIG, 0.0).astype(bf16) for k in range(2)]
    zero_flag = jnp.zeros((TILE - dh, WIDE), bf16)
    zero_drop = jnp.zeros((TILE, WIDE), bf16)
    b_row = lax.broadcasted_iota(i32, (TILE, WIDE), 0)
    b_col = lax.broadcasted_iota(i32, (TILE, WIDE), 1)
    f_lane = lax.broadcasted_iota(i32, (R, TILE - dh), 1)
    flags = jnp.where(f_lane == 0, jnp.where(td < 1, -1.0, 0.0),
                      jnp.where(f_lane == 1, jnp.where(td < 2, -1.0, 0.0), 0.0)).astype(bf16)
    win_thr = q_in_blk + jnp.where(odd, TILE, 0)

    def pair(ref, T, g, half):
        rows = slice(half * KV_HALF + g * dh, half * KV_HALF + (g + 1) * dh)
        return jnp.concatenate([ref[0, 2 * T, rows, :], ref[0, 2 * T + 1, rows, :]], axis=1)

    def drop_rows(T):
        return jnp.where(b_row == T * (WIDE // SEL_BLOCK) + b_col // SEL_BLOCK, BIG, 0.0).astype(bf16)

    def store_scores(g, br, slot, s, left, right, kind, first=False):
        s_l = s[:, 0:TILE] if left is None else s[:, 0:TILE] + left
        s_r = s[:, TILE:WIDE] if right is None else s[:, TILE:WIDE] + right
        s_scr[g, slot, :, 0:TILE] = s_l
        s_scr[g, slot, :, TILE:WIDE] = s_r
        mx = jnp.maximum(s_l, s_r)
        m_scr[g, br, kind] = mx if first else jnp.maximum(m_scr[g, br, kind], mx)

    def accumulate(g, br, slot, kind, v_t, first=False):
        shift = m_scr[g, br, kind]
        p_l = jnp.exp(s_scr[g, slot, :, 0:TILE] - shift)
        p_r = jnp.exp(s_scr[g, slot, :, TILE:WIDE] - shift)
        pv = _dot_nt(jnp.concatenate([p_l, p_r], axis=1).astype(bf16), v_t)
        if first:
            l_scr[g, br] = p_l + p_r
            acc_scr[g, br] = pv
        else:
            l_scr[g, br] += p_l + p_r
            acc_scr[g, br] += pv

    def row_max(g, br):
        fb = fb_scr[g][:, 0:1]
        m = jnp.maximum(jnp.max(m_scr[g, br, FAR], axis=1, keepdims=True) + fb,
                        jnp.max(m_scr[g, br, NEAR], axis=1, keepdims=True))
        m_scr[g, br, FAR] = jnp.broadcast_to(m - fb, (R, TILE))
        m_scr[g, br, NEAR] = jnp.broadcast_to(m, (R, TILE))

    def result(g, br):
        return acc_scr[g, br] / jnp.sum(l_scr[g, br], axis=1, keepdims=True)

    def near_tiles(g):
        tz0 = tz0_ref[g * A_GROUP:(g + 1) * A_GROUP].reshape(R, TILE)
        tz1 = tz1_ref[g * A_GROUP:(g + 1) * A_GROUP].reshape(R, TILE)
        return jnp.where(odd, fb_scr[g], tz1), jnp.where(odd, tz1, tz0), jnp.where(odd, tz0, neg_tile)

    t_prev = jnp.maximum(td - 1, 0)
    t_first = jnp.maximum(td - 2, 0)
    n_far = jnp.maximum(td - 1, 0)

    for g in range(A_N_KV):
        heads = [g * A_GROUP + r for r in range(A_GROUP)]
        qg = jnp.concatenate([q_ref[:, h * dh:(h + 1) * dh] for h in heads], axis=0)
        qg = (qg.astype(f32) * scale).astype(bf16)
        tbl_g = jnp.concatenate([jnp.broadcast_to(tbl_ref[h:h + 1, :], (qb, TILE)) for h in heads], axis=0)
        fb_scr[g] = jnp.concatenate(
            [jnp.broadcast_to(tbl_ref[h:h + 1, MAX_DISTANCE - 1:MAX_DISTANCE], (qb, TILE)) for h in heads], axis=0)

        dist_c = qpos - (lane * CMP_STRIDE + (CMP_LEN - 1))
        s_c = _dot(qg, kct_ref[0, g * dh:(g + 1) * dh, :])
        s_c = s_c + jnp.take_along_axis(tbl_g, jnp.clip(dist_c, 0, MAX_DISTANCE - 1), axis=1)
        s_c = jnp.where(dist_c >= 0, s_c, MASK_VALUE)
        m_c = jnp.max(s_c, axis=1, keepdims=True)
        p_c = jnp.where(dist_c >= 0, jnp.exp(s_c - m_c), 0.0)
        l_c = jnp.sum(p_c, axis=1, keepdims=True)
        l_c = jnp.where(l_c == 0.0, 1.0, l_c)
        oc_scr[g] = _dot(p_c.astype(bf16), vc_ref[0, :, g * dh:(g + 1) * dh]) / l_c
        p_c = p_c / l_c

        p_sum = p_c[0:qb]
        for r in range(1, A_GROUP):
            p_sum = p_sum + p_c[r * qb:(r + 1) * qb]
        imp_t = lax.dot_general(ovlt_ref[0:n_rank, :], p_sum, (((1,), (1,)), ((), ())),
                                preferred_element_type=f32, precision=_HI)
        score = jnp.where(sel_valid, imp_t + jnp.where(sel_forced, FORCE_SCORE, 0.0), -1.0)
        score = jnp.where(s_idx < n_sel, score, -3.0)
        rank = jnp.zeros((n_rank, qb), f32)
        for s2 in range(n_sel):
            other = score[s2:s2 + 1, :]
            rank = rank + jnp.where((other > score) | ((other == score) & (s_idx > s2)), 1.0, 0.0)
        dropped_t = jnp.where(rank < float(min(SEL_TOPN, n_sel)), 0.0, -1.0)
        dropped = jnp.concatenate([dropped_t, jnp.zeros((TILE - n_rank, qb), f32)], axis=0).T
        lhs_scr[g, SEL] = jnp.concatenate([qg, flags, jnp.concatenate([dropped] * A_GROUP, axis=0).astype(bf16)], axis=1)
        lhs_scr[g, WIN] = jnp.concatenate([qg, flags, jnp.zeros((R, TILE), bf16)], axis=1)
        m_scr[g, SEL, FAR] = neg_tile

    too_old_l = jnp.where(lane > win_thr, 0.0, MASK_VALUE)
    too_old_r = jnp.where(lane + TILE > win_thr, 0.0, MASK_VALUE)
    for g in range(A_N_KV):
        prev_right, diag_left, diag_right = near_tiles(g)
        lhs_win = lhs_scr[g, WIN]
        s = _dot(lhs_win, jnp.concatenate([pair(winp_ref, t_first, g, 0), flag_rows[1], zero_drop], axis=0))
        store_scores(g, WIN, WIN_SLOT0, s, too_old_l, too_old_r, FAR, first=True)
        s = _dot(lhs_win, jnp.concatenate([pair(winp_ref, t_prev, g, 0), flag_rows[0], zero_drop], axis=0))
        store_scores(g, WIN, WIN_SLOT0 + 1, s, fb_scr[g], prev_right, NEAR, first=True)
        s = _dot(lhs_win, jnp.concatenate([pair(winp_ref, td, g, 0), zero_flag, zero_drop], axis=0))
        store_scores(g, WIN, WIN_SLOT0 + 2, s, diag_left, diag_right, NEAR)
        lhs_sel = lhs_scr[g, SEL]
        s = _dot(lhs_sel, jnp.concatenate([pair(selp_ref, t_prev, g, 0), flag_rows[0], drop_rows(t_prev)], axis=0))
        store_scores(g, SEL, N_SLOTS - 2, s, fb_scr[g], prev_right, NEAR, first=True)
        s = _dot(lhs_sel, jnp.concatenate([pair(selp_ref, td, g, 0), zero_flag, drop_rows(td)], axis=0))
        store_scores(g, SEL, N_SLOTS - 1, s, diag_left, diag_right, NEAR)

    def far_scores(T, carry):
        drop = drop_rows(T)
        for g in range(A_N_KV):
            s = _dot(lhs_scr[g, SEL], jnp.concatenate([pair(selp_ref, T, g, 0), zero_flag, drop], axis=0))
            store_scores(g, SEL, T, s, None, None, FAR)
        return carry

    lax.fori_loop(0, n_far, far_scores, 0)

    for g in range(A_N_KV):
        row_max(g, WIN)
        row_max(g, SEL)
    for g in range(A_N_KV):
        accumulate(g, WIN, WIN_SLOT0, FAR, pair(winp_ref, t_first, g, 1), first=True)
        accumulate(g, WIN, WIN_SLOT0 + 1, NEAR, pair(winp_ref, t_prev, g, 1))
        accumulate(g, WIN, WIN_SLOT0 + 2, NEAR, pair(winp_ref, td, g, 1))
        accumulate(g, SEL, N_SLOTS - 2, NEAR, pair(selp_ref, t_prev, g, 1), first=True)
        accumulate(g, SEL, N_SLOTS - 1, NEAR, pair(selp_ref, td, g, 1))

    def far_accumulate(T, carry):
        for g in range(A_N_KV):
            accumulate(g, SEL, T, FAR, pair(selp_ref, T, g, 1))
        return carry

    lax.fori_loop(0, n_far, far_accumulate, 0)

    for g in range(A_N_KV):
        o_c, o_s, o_w = oc_scr[g], result(g, SEL), result(g, WIN)
        for r in range(A_GROUP):
            h = g * A_GROUP + r
            rows = slice(r * qb, (r + 1) * qb)
            gt = gate_ref[:, 3 * h:3 * h + 3]
            o_h = gt[:, 0:1] * o_c[rows] + gt[:, 1:2] * o_s[rows] + gt[:, 2:3] * o_w[rows]
            o_ref[:, h * dh:(h + 1) * dh] = o_h.astype(o_ref.dtype)


def nsa_attention_prompt(q, gates, Bn, L, kct, vc, tbl, tz0, tz1, overlap_t, sel_pages, win_pages):
    qb = Q_BLOCK
    nqb = L // qb
    n_sel = -(-L // SEL_BLOCK)
    n_tiles = L // TILE
    assert n_tiles == N_PAGES and qb == TILE
    R = A_GROUP * qb
    row_map = lambda b, i: (b * nqb + i, 0)
    seq_map3 = lambda b, i: (b, 0, 0)
    seq_map4 = lambda b, i: (b, 0, 0, 0)
    const2 = lambda b, i: (0, 0)
    const3 = lambda b, i: (0, 0, 0)
    return pl.pallas_call(
        functools.partial(_attn_prompt_body, qb=qb, n_sel=n_sel),
        out_shape=jax.ShapeDtypeStruct((Bn * L, A_Q_DIM), bf16),
        grid=(Bn, nqb),
        in_specs=[
            pl.BlockSpec((qb, A_Q_DIM), row_map),
            pl.BlockSpec((qb, LANES), row_map),
            pl.BlockSpec((1, KV_HALF, N_CHUNKS), seq_map3),
            pl.BlockSpec((1, N_CHUNKS, KV_HALF), seq_map3),
            pl.BlockSpec((A_N_HEADS, LANES), const2),
            pl.BlockSpec((A_N_HEADS, TILE, TILE), const3),
            pl.BlockSpec((A_N_HEADS, TILE, TILE), const3),
            pl.BlockSpec((TILE, TILE), const2),
            pl.BlockSpec((1, n_tiles, KV_ROW, TILE), seq_map4),
            pl.BlockSpec((1, n_tiles, KV_ROW, TILE), seq_map4),
        ],
        out_specs=pl.BlockSpec((qb, A_Q_DIM), row_map),
        scratch_shapes=[
            pltpu.VMEM((A_N_KV, N_SLOTS + 3, R, WIDE), f32),
            pltpu.VMEM((A_N_KV, 2, 2, R, TILE), f32),
            pltpu.VMEM((A_N_KV, 2, R, TILE), f32),
            pltpu.VMEM((A_N_KV, 2, R, A_HEAD_DIM), f32),
            pltpu.VMEM((A_N_KV, 2, R, WIDE), bf16),
            pltpu.VMEM((A_N_KV, R, A_HEAD_DIM), f32),
            pltpu.VMEM((A_N_KV, R, TILE), f32),
        ],
        compiler_params=_cparams(("parallel", "arbitrary")),
        name="nsa_attention_prompt",
    )(q, gates, kct, vc, tbl, tz0, tz1, overlap_t, sel_pages, win_pages)


def _attn_sample_body(pt_ref, q_ref, gate_ref, kct_ref, vc_ref, tbl_ref, ovl_ref, *refs, qb, start, n_sel):
    del pt_ref
    sel_pages = refs[:N_PAGES]
    selnew_ref, winpast_ref, winnew_ref, o_ref, s_scr = refs[N_PAGES:]
    dh = A_HEAD_DIM
    RG = A_GROUP * qb
    R = A_N_KV * RG
    scale = dh ** -0.5
    n_win_past = WINDOW // TILE
    assert R == TILE and n_sel <= SEL_BLOCK and start == N_PAGES * TILE

    qs = q_ref[...] * scale
    blocks = []
    for g in range(A_N_KV):
        qg = jnp.concatenate([qs[:, (g * A_GROUP + r) * dh:(g * A_GROUP + r + 1) * dh] for r in range(A_GROUP)], axis=0)
        parts = [qg if gg == g else jnp.zeros((RG, dh), f32) for gg in range(A_N_KV)]
        blocks.append(jnp.concatenate(parts, axis=1))
    qbd = jnp.concatenate(blocks, axis=0).astype(bf16)

    tbl_rows = jnp.concatenate([jnp.broadcast_to(tbl_ref[h:h + 1, :], (qb, LANES)) for h in range(A_N_HEADS)], axis=0)
    far_bias = tbl_rows[:, MAX_DISTANCE - 1:MAX_DISTANCE]
    lane = lax.broadcasted_iota(i32, (R, TILE), 1)
    row = lax.broadcasted_iota(i32, (R, TILE), 0)
    qpos = start + row % qb

    def near_bias(dist):
        return jnp.take_along_axis(tbl_rows, jnp.clip(dist, 0, MAX_DISTANCE - 1), axis=1)

    def softmax_rows(s):
        m = jnp.max(s, axis=1, keepdims=True)
        p = jnp.where(s > 0.5 * MASK_VALUE, jnp.exp(s - m), 0.0)
        l = jnp.sum(p, axis=1, keepdims=True)
        return p, jnp.where(l == 0.0, 1.0, l)

    def pad_rows(x):
        return jnp.concatenate([x, jnp.zeros((TILE - x.shape[0], x.shape[1]), x.dtype)], axis=0).astype(bf16)

    dist_c = qpos - (lane * CMP_STRIDE + (CMP_LEN - 1))
    s_c = _dot(qbd, kct_ref[0]) + near_bias(dist_c)
    p_c, l_c = softmax_rows(jnp.where(dist_c >= 0, s_c, MASK_VALUE))
    o_c = _dot(p_c.astype(bf16), vc_ref[0]) / l_c
    p_c = p_c / l_c

    p_sum = []
    for g in range(A_N_KV):
        acc = p_c[g * RG:g * RG + qb]
        for r in range(1, A_GROUP):
            acc = acc + p_c[g * RG + r * qb:g * RG + (r + 1) * qb]
        p_sum.append(acc)
    p_sum = jnp.concatenate(p_sum, axis=0)
    imp = jnp.dot(p_sum, ovl_ref[...], preferred_element_type=f32, precision=_HI)
    s_lane = lax.broadcasted_iota(i32, (A_N_KV * qb, TILE), 1)
    s_qpos = start + lax.broadcasted_iota(i32, (A_N_KV * qb, TILE), 0) % qb
    blk = s_qpos // SEL_BLOCK
    valid = s_lane * SEL_BLOCK <= s_qpos
    forced = (s_lane == 0) | (s_lane == blk) | (s_lane == blk - 1)
    score = jnp.where(valid, imp + jnp.where(forced, FORCE_SCORE, 0.0), -1.0)
    score = jnp.where(s_lane < n_sel, score, -3.0)
    rank = jnp.zeros(score.shape, f32)
    for s2 in range(n_sel):
        col = score[:, s2:s2 + 1]
        rank = rank + jnp.where((col > score) | ((col == score) & (s_lane > s2)), 1.0, 0.0)
    not_chosen = jnp.where(rank < float(min(SEL_TOPN, n_sel)), 0.0, -1.0)
    drop = jnp.concatenate(
        [not_chosen[g * qb:(g + 1) * qb] for g in range(A_N_KV) for _ in range(A_GROUP)], axis=0)
    drop = drop[:, 0:SEL_BLOCK].astype(bf16)
    b_row = lax.broadcasted_iota(i32, (SEL_BLOCK, TILE), 0)
    b_col = lax.broadcasted_iota(i32, (SEL_BLOCK, TILE), 1)

    def drop_unselected(t):
        expand = jnp.where(b_row == t * (TILE // SEL_BLOCK) + b_col // SEL_BLOCK, -MASK_VALUE, 0.0).astype(bf16)
        return _dot(drop, expand)

    def attend(tiles, new_ref, extra):
        n = len(tiles)
        for j, (k_t, _, kind) in enumerate(tiles):
            s = _dot(qbd, k_t.astype(bf16)) + extra(j)
            key0 = start - (n - j) * TILE
            dist = qpos - (key0 + lane)
            if kind == "far":
                s = s + far_bias
            else:
                s = s + near_bias(dist)
            if kind == "edge":
                s = jnp.where(dist < WINDOW, s, MASK_VALUE)
            s_scr[:, j * TILE:(j + 1) * TILE] = s
        new = new_ref[...]
        dist = qpos - (start + lane)
        s = _dot_nt(qbd, pad_rows(new[:, 0:KV_HALF])) + extra(n) + near_bias(dist)
        s_scr[:, n * TILE:(n + 1) * TILE] = jnp.where(dist >= 0, s, MASK_VALUE)
        p, l = softmax_rows(s_scr[:, 0:(n + 1) * TILE])
        p = p.astype(bf16)
        o = _dot(p[:, n * TILE:(n + 1) * TILE], pad_rows(new[:, KV_HALF:KV_ROW]))
        for j, (_, v_t, _) in enumerate(tiles):
            o = o + _dot_nt(p[:, j * TILE:(j + 1) * TILE], v_t.astype(bf16))
        return o / l

    sel_tiles = []
    for t in range(N_PAGES):
        page = sel_pages[t]
        sel_tiles.append((page[0, 0:KV_HALF, :], page[0, KV_HALF:KV_ROW, :], "near" if t == N_PAGES - 1 else "far"))
    o_s = attend(sel_tiles, selnew_ref, lambda j: drop_unselected(j))

    win_tiles = []
    for j in range(n_win_past):
        cols = slice(j * TILE, (j + 1) * TILE)
        kind = "edge" if j == 0 else ("near" if j == n_win_past - 1 else "far")
        win_tiles.append((winpast_ref[0, 0:KV_HALF, cols], winpast_ref[0, KV_HALF:KV_ROW, cols], kind))
    o_w = attend(win_tiles, winnew_ref, lambda j: 0.0)

    for g in range(A_N_KV):
        for r in range(A_GROUP):
            h = g * A_GROUP + r
            rows = slice(g * RG + r * qb, g * RG + (r + 1) * qb)
            cols = slice(g * dh, (g + 1) * dh)
            gt = gate_ref[:, 3 * h:3 * h + 3]
            o_h = gt[:, 0:1] * o_c[rows, cols] + gt[:, 1:2] * o_s[rows, cols] + gt[:, 2:3] * o_w[rows, cols]
            o_ref[:, h * dh:(h + 1) * dh] = o_h.astype(o_ref.dtype)


def nsa_attention_sample(q, gates, row0, Bn, L, start, kct, vc, tbl, overlap, sel_pages, page_table, sel_new,
                         win_past, win_new):
    qb = L
    n_sel = -(-(start + L) // SEL_BLOCK)
    blk0 = row0 // qb
    row_map = lambda b, pt: (blk0 + b, 0)
    seq_map = lambda b, pt: (b, 0, 0)
    page_specs = [
        pl.BlockSpec((1, KV_ROW, PAGE_SIZE), functools.partial(lambda b, pt, j: (pt[b, j], 0, 0), j=j))
        for j in range(N_PAGES)
    ]
    return pl.pallas_call(
        functools.partial(_attn_sample_body, qb=qb, start=start, n_sel=n_sel),
        out_shape=jax.ShapeDtypeStruct((Bn * L, A_Q_DIM), f32),
        grid_spec=pltpu.PrefetchScalarGridSpec(
            num_scalar_prefetch=1,
            grid=(Bn,),
            in_specs=[
                pl.BlockSpec((qb, A_Q_DIM), row_map),
                pl.BlockSpec((qb, LANES), row_map),
                pl.BlockSpec((1, KV_HALF, N_CHUNKS), seq_map),
                pl.BlockSpec((1, N_CHUNKS, KV_HALF), seq_map),
                pl.BlockSpec((A_N_HEADS, LANES), lambda b, pt: (0, 0)),
                pl.BlockSpec((TILE, TILE), lambda b, pt: (0, 0)),
            ] + page_specs + [
                pl.BlockSpec((qb, KV_ROW), row_map),
                pl.BlockSpec((1, KV_ROW, WINDOW), seq_map),
                pl.BlockSpec((qb, KV_ROW), row_map),
            ],
            out_specs=pl.BlockSpec((qb, A_Q_DIM), lambda b, pt: (b, 0)),
            scratch_shapes=[pltpu.VMEM((TILE, (N_PAGES + 1) * TILE), f32)],
        ),
        compiler_params=_cparams(("parallel",)),
        name="nsa_attention_sample",
    )(page_table, q, gates, kct, vc, tbl, overlap, *([sel_pages] * N_PAGES), sel_new, win_past, win_new)


def _pad_lanes(v):
    return jnp.pad(v, (0, LANES - v.shape[0])).reshape(1, LANES)


def _overlap_matrix():
    n_cmp = N_CHUNKS - CMP_RATIO + 1
    c = np.arange(TILE)[:, None] * CMP_STRIDE
    s = np.arange(TILE)[None, :] * SEL_BLOCK
    ov = (c < s + SEL_BLOCK) & (c + CMP_LEN > s) & (np.arange(TILE)[:, None] < n_cmp)
    return jnp.asarray(ov.astype(np.float32))


def _feature_major(x):
    lead = x.shape[:-4]
    n = len(lead)
    return jnp.transpose(x, tuple(range(n)) + (n + 1, n + 2, n + 3, n)).reshape(lead + (KV_ROW, x.shape[-4]))


def _token_major(x_t):
    B, _, T = x_t.shape
    return jnp.transpose(x_t.reshape(B, 2, A_N_KV, A_HEAD_DIM, T), (0, 4, 1, 2, 3))


def kernel(x_prompt, x_sample, state_ssm, state_conv, cache_cmp_kv, cache_sel_kv, cache_win_kv, page_table, ln_g, ln_b, m_in_w, m_conv_w, m_conv_b, m_dt_bias, m_a_log, m_d, m_norm_w, m_out_w, kv_w, cmp_w1, cmp_pe, cmp_w2, q_w, o_w, rel_bias, mlp_w1, mlp_w2):
    Bp, Lp, D = x_prompt.shape
    Bs, Ls, _ = x_sample.shape
    NP, NS = Bp * Lp, Bs * Ls
    past_len = page_table.shape[1] * PAGE_SIZE
    assert past_len == N_PAGES * PAGE_SIZE and Lp == N_PAGES * PAGE_SIZE and cache_win_kv.shape[1] == WINDOW

    in_w = m_in_w[0].astype(bf16)
    z_w = in_w[:, :M_D_INNER]
    xbc_w = in_w[:, M_D_INNER:M_D_INNER + M_CONV_DIM]
    dt_w = jnp.pad(in_w[:, M_D_INNER + M_CONV_DIM:], ((0, 0), (0, LANES - M_N_HEADS)))
    kvw = kv_w.astype(bf16)
    qw = q_w[0].astype(bf16)
    gate_w = jnp.pad(qw[:, A_Q_DIM:], ((0, 0), (0, LANES - 3 * A_N_HEADS)))
    w1b = cmp_w1.astype(bf16)
    w_j = jnp.transpose(w1b, (0, 2, 3, 1, 4)).reshape(2, CMP_STRIDE, A_HEAD_DIM, CMP_RATIO * CMP_HIDDEN)
    zeros = jnp.zeros_like(w_j)
    wbd = jnp.concatenate([jnp.concatenate([w_j, zeros], axis=3), jnp.concatenate([zeros, w_j], axis=3)], axis=2)
    wbd = wbd.reshape(2, CMP_STRIDE // 2, 2 * LANES, 2 * CMP_RATIO * CMP_HIDDEN)
    pe_rows = jnp.broadcast_to(cmp_pe.astype(bf16).reshape(2, 1, CMP_LEN * A_HEAD_DIM), (2, SUBLANES, CMP_LEN * A_HEAD_DIM))
    w1_flat = w1b.reshape(2, CMP_LEN * A_HEAD_DIM, CMP_HIDDEN)
    cmp_w = (wbd, pe_rows, w1_flat, cmp_w2[1].astype(bf16), cmp_w2[0].T.astype(bf16))

    x = jnp.concatenate([x_prompt.reshape(NP, D), x_sample.reshape(NS, D)], axis=0)
    xb = x.astype(bf16)
    z = matmul(xb, z_w, f32)
    xbc = matmul(xb, xbc_w, f32)
    dt = matmul(xb, dt_w, f32)
    ssd_w = (m_conv_w[0], m_conv_b[0].reshape(1, -1), _pad_lanes(m_dt_bias[0]), _pad_lanes(m_a_log[0]),
             _pad_lanes(m_d[0]), m_norm_w[0].reshape(1, -1))
    y_p, p_ssm, p_conv = ssd_mixer_core(xbc, z, dt, 0, Bp, Lp, None, None, *ssd_w)
    y_s, s_ssm, s_conv = ssd_mixer_core(xbc, z, dt, NP, Bs, Ls, state_conv[0], state_ssm[0], *ssd_w)
    y = jnp.concatenate([y_p, y_s.astype(bf16)], axis=0)
    h_f, h_b = matmul_residual_ln(y, m_out_w[0].astype(bf16), x, ln_g[0, 0].reshape(1, D), ln_b[0, 0].reshape(1, D))
    h_f, h_b = mlp_residual_ln(h_b, h_f, mlp_w1[0].astype(bf16), mlp_w2[0].astype(bf16),
                               ln_g[0, 1].reshape(1, D), ln_b[0, 1].reshape(1, D))

    cmp_t, sel_t, win_t, sel_pg, win_pg = kv_project_feature_major(h_b, kvw.T, Bp, Lp)
    kv_s = matmul(h_b[NP:], kvw, f32)
    cmp_s, sel_s, win_s = kv_s[:, 0:KV_ROW], kv_s[:, KV_ROW:2 * KV_ROW], kv_s[:, 2 * KV_ROW:3 * KV_ROW]
    kct_p, vc_p = compress_kv(cmp_t, None, *cmp_w)
    kct_s, vc_s = compress_kv(_feature_major(cache_cmp_kv), page_table, *cmp_w)

    q = matmul(h_b, qw[:, :A_Q_DIM], f32)
    gates = matmul(h_b, gate_w, f32, act="sigmoid")
    tbl, tz0, tz1 = bias_table(rel_bias.T)
    overlap = _overlap_matrix()
    o_p = nsa_attention_prompt(q, gates, Bp, Lp, kct_p, vc_p, tbl, tz0, tz1, overlap.T, sel_pg, win_pg)
    o_s = nsa_attention_sample(q[NP:], gates[NP:], 0, Bs, Ls, past_len, kct_s, vc_s, tbl, overlap,
                               _feature_major(cache_sel_kv), page_table, sel_s, _feature_major(cache_win_kv), win_s)
    o = jnp.concatenate([o_p, o_s.astype(bf16)], axis=0)
    h_f, h_b = matmul_residual_ln(o, o_w[0].astype(bf16), h_f, ln_g[1, 0].reshape(1, D), ln_b[1, 0].reshape(1, D))
    h_f, _ = mlp_residual_ln(h_b, h_f, mlp_w1[1].astype(bf16), mlp_w2[1].astype(bf16),
                             ln_g[1, 1].reshape(1, D), ln_b[1, 1].reshape(1, D))

    kv_shape = (2, A_N_KV, A_HEAD_DIM)
    n_keep = min(WINDOW, Lp)
    s_win = jnp.concatenate([cache_win_kv[:, Ls:], win_s.reshape((Bs, Ls) + kv_shape)], axis=1)
    return (
        h_f[:NP].reshape(Bp, Lp, D), h_f[NP:].reshape(Bs, Ls, D),
        p_ssm[None], p_conv[None],
        _token_major(cmp_t), _token_major(sel_t), _token_major(win_t[:, :, Lp - n_keep:]),
        s_ssm[None], s_conv[None],
        cmp_s.reshape((Bs, Ls) + kv_shape), sel_s.reshape((Bs, Ls) + kv_shape), s_win,
    )
```

```python
import functools
import math

import jax
import jax.numpy as jnp
import numpy as np
from jax import lax
from jax.experimental import pallas as pl
from jax.experimental.pallas import tpu as pltpu

f32 = jnp.float32
bf16 = jnp.bfloat16
i32 = jnp.int32

D_MODEL = 1024
DEPTH = 2
DN_ALPHA = (2.0 * DEPTH) ** 0.25
LN_EPS = 1e-5
RMS_EPS = 1e-5
D_FF = 4 * D_MODEL
M_D_INNER = 2 * D_MODEL
M_HEAD_DIM = 64
M_N_HEADS = M_D_INNER // M_HEAD_DIM
M_N_GROUPS = 4
M_HPG = M_N_HEADS // M_N_GROUPS
M_D_STATE = 128
M_CONV = 4
M_CHUNK = 128
M_CONV_DIM = M_D_INNER + 2 * M_N_GROUPS * M_D_STATE
A_HEAD_DIM = 64
A_N_HEADS = D_MODEL // A_HEAD_DIM
A_N_KV = 4
A_GROUP = A_N_HEADS // A_N_KV
A_Q_DIM = A_N_HEADS * A_HEAD_DIM
KV_HALF = A_N_KV * A_HEAD_DIM
KV_ROW = 2 * KV_HALF
CMP_LEN = 32
CMP_STRIDE = 16
CMP_RATIO = CMP_LEN // CMP_STRIDE
CMP_HIDDEN = 2 * A_HEAD_DIM
SEL_BLOCK = 64
SEL_TOPN = 16
WINDOW = 512
Q_BLOCK = 128
N_BUCKETS = 32
MAX_DISTANCE = 128
MASK_VALUE = -1e30
FORCE_SCORE = 1e3
PAGE_SIZE = 128

LANES = 128
SUBLANES = 8
VMEM_LIMIT = 56 * 1024 * 1024

_HI = lax.Precision.HIGHEST


def _cparams(sem):
    return pltpu.CompilerParams(dimension_semantics=sem, vmem_limit_bytes=VMEM_LIMIT)


def _dot(a, b):
    return jnp.dot(a, b, preferred_element_type=f32)


def _dot_nt(a, b):
    return lax.dot_general(a, b, (((1,), (1,)), ((), ())), preferred_element_type=f32)


def _dot_tn(a, b):
    return lax.dot_general(a, b, (((0,), (0,)), ((), ())), preferred_element_type=f32)


def _silu(x):
    return x * (1.0 / (1.0 + jnp.exp(-x)))


def _layer_norm(x, g, b):
    mu = jnp.mean(x, axis=-1, keepdims=True)
    xc = x - mu
    var = jnp.mean(xc * xc, axis=-1, keepdims=True)
    return xc * lax.rsqrt(var + LN_EPS) * g + b


def _mm_body(x_ref, w_ref, o_ref, *, act):
    y = _dot(x_ref[...], w_ref[...])
    if act == "sigmoid":
        y = 1.0 / (1.0 + jnp.exp(-y))
    o_ref[...] = y.astype(o_ref.dtype)


def matmul(x, w, out_dtype, act=None, tm=1024, tn=1024):
    M, K = x.shape
    N = w.shape[1]
    tn = next(t for t in (tn, 512, 256, LANES) if N % t == 0)
    return pl.pallas_call(
        functools.partial(_mm_body, act=act),
        out_shape=jax.ShapeDtypeStruct((M, N), out_dtype),
        grid=(M // tm, N // tn),
        in_specs=[pl.BlockSpec((tm, K), lambda i, j: (i, 0)), pl.BlockSpec((K, tn), lambda i, j: (0, j))],
        out_specs=pl.BlockSpec((tm, tn), lambda i, j: (i, j)),
        compiler_params=_cparams(("parallel", "parallel")),
        name="matmul",
    )(x, w)


def _mm_res_ln_body(x_ref, w_ref, r_ref, g_ref, b_ref, of_ref, ob_ref):
    y = DN_ALPHA * r_ref[...] + _dot(x_ref[...], w_ref[...])
    h = _layer_norm(y, g_ref[...], b_ref[...])
    of_ref[...] = h
    ob_ref[...] = h.astype(bf16)


def matmul_residual_ln(x, w, resid, g, b, tm=512):
    M, K = x.shape
    N = w.shape[1]
    return pl.pallas_call(
        _mm_res_ln_body,
        out_shape=(jax.ShapeDtypeStruct((M, N), f32), jax.ShapeDtypeStruct((M, N), bf16)),
        grid=(M // tm,),
        in_specs=[
            pl.BlockSpec((tm, K), lambda i: (i, 0)),
            pl.BlockSpec((K, N), lambda i: (0, 0)),
            pl.BlockSpec((tm, N), lambda i: (i, 0)),
            pl.BlockSpec((1, N), lambda i: (0, 0)),
            pl.BlockSpec((1, N), lambda i: (0, 0)),
        ],
        out_specs=(pl.BlockSpec((tm, N), lambda i: (i, 0)), pl.BlockSpec((tm, N), lambda i: (i, 0))),
        compiler_params=_cparams(("parallel",)),
        name="matmul_residual_ln",
    )(x, w, resid, g, b)


def _mlp_body(hb_ref, hf_ref, w1_ref, w2_ref, g_ref, b_ref, o0_ref, o1_ref, acc_ref, *, n_head_blocks):
    i = pl.program_id(0)
    j = pl.program_id(1)

    @pl.when(j == 0)
    def _():
        acc_ref[...] = jnp.zeros_like(acc_ref)

    u = jnp.maximum(_dot(hb_ref[...], w1_ref[...]), 0.0)
    acc_ref[...] += _dot((u * u).astype(bf16), w2_ref[...])
    last = j == pl.num_programs(1) - 1

    def result():
        return _layer_norm(DN_ALPHA * hf_ref[...] + acc_ref[...], g_ref[...], b_ref[...])

    if n_head_blocks is None:
        @pl.when(last)
        def _():
            h = result()
            o0_ref[...] = h
            o1_ref[...] = h.astype(bf16)
    else:
        @pl.when(last & (i < n_head_blocks))
        def _():
            o0_ref[...] = result()

        @pl.when(last & (i >= n_head_blocks))
        def _():
            o1_ref[...] = result()


def mlp_residual_ln(hb, hf, w1, w2, g, b, split_rows=None, tm=1024, tf=1024):
    M, D = hb.shape
    F = w1.shape[1]
    row_map = lambda i, j: (i, 0)
    if split_rows is None:
        n_head = None
        out_shape = (jax.ShapeDtypeStruct((M, D), f32), jax.ShapeDtypeStruct((M, D), bf16))
        out_specs = (pl.BlockSpec((tm, D), row_map), pl.BlockSpec((tm, D), row_map))
    else:
        assert split_rows % tm == 0 and (M - split_rows) % tm == 0
        n_head = split_rows // tm
        out_shape = (jax.ShapeDtypeStruct((split_rows, D), f32), jax.ShapeDtypeStruct((M - split_rows, D), f32))
        out_specs = (pl.BlockSpec((tm, D), lambda i, j: (jnp.minimum(i, n_head - 1), 0)),
                     pl.BlockSpec((tm, D), lambda i, j: (jnp.maximum(i - n_head, 0), 0)))
    return pl.pallas_call(
        functools.partial(_mlp_body, n_head_blocks=n_head),
        out_shape=out_shape,
        grid=(M // tm, F // tf),
        in_specs=[
            pl.BlockSpec((tm, D), row_map),
            pl.BlockSpec((tm, D), row_map),
            pl.BlockSpec((D, tf), lambda i, j: (0, j)),
            pl.BlockSpec((tf, D), lambda i, j: (j, 0)),
            pl.BlockSpec((1, D), lambda i, j: (0, 0)),
            pl.BlockSpec((1, D), lambda i, j: (0, 0)),
        ],
        out_specs=out_specs,
        scratch_shapes=[pltpu.VMEM((tm, D), f32)],
        compiler_params=_cparams(("arbitrary", "arbitrary")),
        name="mlp_residual_ln",
    )(hb, hf, w1, w2, g, b)


def _kv_project_body(wt_ref, h_ref, cmp_ref, sel_ref, win_ref, selp_ref, winp_ref):
    res = _dot_nt(wt_ref[...], h_ref[...])
    tm = h_ref.shape[0]
    cmp_ref[0] = res[0:KV_ROW]
    sel_ref[0] = res[KV_ROW:2 * KV_ROW]
    win_ref[0] = res[2 * KV_ROW:3 * KV_ROW]
    for k in range(tm // PAGE_SIZE):
        cols = slice(k * PAGE_SIZE, (k + 1) * PAGE_SIZE)
        selp_ref[0, k] = res[KV_ROW:2 * KV_ROW, cols].astype(bf16)
        winp_ref[0, k] = res[2 * KV_ROW:3 * KV_ROW, cols].astype(bf16)


def kv_project_feature_major(h_b, w_t, Bn, L, tm=512):
    nj = L // tm
    pages_per_step = tm // PAGE_SIZE
    fm = jax.ShapeDtypeStruct((Bn, KV_ROW, L), f32)
    pg = jax.ShapeDtypeStruct((Bn, L // PAGE_SIZE, KV_ROW, PAGE_SIZE), bf16)
    fm_spec = pl.BlockSpec((1, KV_ROW, tm), lambda b, j: (b, 0, j))
    pg_spec = pl.BlockSpec((1, pages_per_step, KV_ROW, PAGE_SIZE), lambda b, j: (b, j, 0, 0))
    return pl.pallas_call(
        _kv_project_body,
        out_shape=(fm, fm, fm, pg, pg),
        grid=(Bn, nj),
        in_specs=[
            pl.BlockSpec(w_t.shape, lambda b, j: (0, 0)),
            pl.BlockSpec((tm, h_b.shape[1]), lambda b, j: (b * nj + j, 0)),
        ],
        out_specs=(fm_spec, fm_spec, fm_spec, pg_spec, pg_spec),
        compiler_params=_cparams(("parallel", "parallel")),
        name="kv_project_feature_major",
    )(w_t, h_b)


SSD_SEQS_PER_STEP = 4
SSD_SEQS_PER_STEP_CHUNKED = 2


def _ssd_body(*refs, Q, BB, nc, has_init):
    n_in = 1 if nc == 1 else BB
    xbc_refs, z_refs, dt_refs = refs[0:n_in], refs[n_in:2 * n_in], refs[2 * n_in:3 * n_in]
    refs = refs[3 * n_in:]
    if has_init:
        conv0_ref, h0_ref = refs[0:2]
        refs = refs[2:]
    (cw_ref, cb_ref, dtb_ref, alog_ref, dskx_ref, nw_ref, e64_ref, eq_ref,
     y_ref, hout_ref, cout_ref, st_ref, xpad_ref, xc_ref, ybuf_ref, xdt_ref, xds_ref, ecx_ref, acx_ref) = refs

    def rows_of(row_refs, bb, cols=slice(None)):
        return row_refs[0][bb * Q:(bb + 1) * Q, cols] if nc == 1 else row_refs[bb][:, cols]

    def store_y(bb, cols, value):
        if nc == 1:
            y_ref[bb * Q:(bb + 1) * Q, cols] = value
        else:
            y_ref[bb, :, cols] = value

    c = pl.program_id(1)
    single_chunk = nc == 1
    P, N, R, G = M_HEAD_DIM, M_D_STATE, M_HPG, M_N_GROUPS
    PAD = SUBLANES

    per_head_state = single_chunk and has_init

    def first_chunk():
        for bb in range(BB):
            xpad_ref[bb, 0:PAD, :] = jnp.zeros((PAD, M_CONV_DIM), f32)
            if has_init:
                xpad_ref[bb, PAD - (M_CONV - 1):PAD, :] = conv0_ref[bb]
            if per_head_state:
                continue
            if has_init:
                for g in range(G):
                    for r in range(R):
                        st_ref[bb, g, :, r * P:(r + 1) * P] = h0_ref[bb, g * R + r].T
            else:
                st_ref[bb] = jnp.zeros(st_ref.shape[1:], f32)

    def last_chunk():
        if per_head_state:
            return
        for bb in range(BB):
            for g in range(G):
                for r in range(R):
                    hout_ref[bb, g * R + r] = st_ref[bb, g, :, r * P:(r + 1) * P].T

    if single_chunk:
        first_chunk()
    else:
        pl.when(c == 0)(first_chunk)

    ri = lax.broadcasted_iota(i32, (Q, Q), 0)
    ci = lax.broadcasted_iota(i32, (Q, Q), 1)
    tril = ri >= ci
    GW = M_D_INNER // G
    lane = lax.broadcasted_iota(i32, (Q, LANES), 1)
    third = M_N_HEADS
    assert 3 * third <= LANES

    def spread(v, e_ref):
        hi = v.astype(bf16).astype(f32)
        r1 = v - hi
        mid = r1.astype(bf16).astype(f32)
        lo = r1 - mid
        packed = jnp.where(lane < third, hi, jnp.where(lane < 2 * third, pltpu.roll(mid, third, 1),
                                                       jnp.where(lane < 3 * third, pltpu.roll(lo, 2 * third, 1), 0.0)))
        return _dot(packed.astype(bf16), e_ref[...])

    seqs = range(BB)
    a_cum, a_cum_t, e_last = [], [], []
    for bb in seqs:
        xpad_ref[bb, PAD:PAD + Q, :] = rows_of(xbc_refs, bb)
        acc = cb_ref[...] + xpad_ref[bb, pl.ds(PAD - 3, Q), :] * cw_ref[0:1, :]
        for k in range(1, M_CONV):
            acc = acc + xpad_ref[bb, pl.ds(PAD - 3 + k, Q), :] * cw_ref[k:k + 1, :]
        xc_ref[bb] = _silu(acc)
        cout_ref[bb] = xpad_ref[bb, pl.ds(Q + PAD - 3, 3), :]
        xpad_ref[bb, 0:PAD, :] = xpad_ref[bb, pl.ds(Q, PAD), :]

        xdt = rows_of(dt_refs, bb) + dtb_ref[...]
        dt = jnp.maximum(xdt, 0.0) + jnp.log1p(jnp.exp(-jnp.abs(xdt)))
        a = dt * (-jnp.exp(alog_ref[...]))
        a_cum.append(jnp.dot(tril.astype(f32), a, preferred_element_type=f32, precision=_HI))
        a_cum_t.append(a_cum[bb].T)
        a_last = a_cum[bb][Q - 1:Q, :]
        e_last.append(jnp.exp(a_last))
        xdt_all = xc_ref[bb, :, 0:M_D_INNER] * spread(dt, e64_ref)
        xdt_ref[bb] = xdt_all
        xds_ref[bb] = xdt_all * spread(jnp.exp(a_last - a_cum[bb]), e64_ref)
        ecx_ref[bb] = spread(jnp.exp(a_cum[bb]), e64_ref)
        if Q == LANES:
            acx_ref[bb] = spread(a_cum[bb], eq_ref)

    for g in range(G):
        gcols = slice(g * GW, (g + 1) * GW)
        bg = [xc_ref[bb, :, M_D_INNER + g * N:M_D_INNER + (g + 1) * N].astype(bf16) for bb in seqs]
        cg = [xc_ref[bb, :, M_D_INNER + G * N + g * N:M_D_INNER + G * N + (g + 1) * N].astype(bf16) for bb in seqs]
        gmat = [_dot_nt(cg[bb], bg[bb]) for bb in seqs]
        for r in range(R):
            h = g * R + r
            hcols = slice(h * P, (h + 1) * P)
            for bb in seqs:
                col = acx_ref[bb, :, h * Q:(h + 1) * Q] if Q == LANES else a_cum[bb][:, h:h + 1]
                row = a_cum_t[bb][h:h + 1, :]
                lmat = jnp.exp(jnp.where(tril, col - row, -jnp.inf))
                ydiag = _dot((gmat[bb] * lmat).astype(bf16), xdt_ref[bb, :, hcols].astype(bf16))
                if per_head_state:
                    h_in = h0_ref[bb, h]
                    hout_ref[bb, h] = (h_in * e_last[bb][:, h:h + 1]
                                       + _dot_tn(xds_ref[bb, :, hcols].astype(bf16), bg[bb]))
                    ydiag = ydiag + _dot_nt(cg[bb], h_in.astype(bf16)) * ecx_ref[bb, :, hcols]
                ybuf_ref[bb, :, hcols] = ydiag
        for bb in seqs:
            extra = xc_ref[bb, :, gcols] * dskx_ref[:, gcols]
            if not per_head_state:
                extra = extra + _dot(cg[bb], st_ref[bb, g].astype(bf16)) * ecx_ref[bb, :, gcols]
                new = _dot_tn(bg[bb], xds_ref[bb, :, gcols].astype(bf16))
                st_ref[bb, g] = st_ref[bb, g] * ecx_ref[bb, Q - 1:Q, gcols] + new
            ybuf_ref[bb, :, gcols] += extra

    for g in range(G):
        cols = slice(g * GW, (g + 1) * GW)
        for bb in seqs:
            yg = ybuf_ref[bb, :, cols] * _silu(rows_of(z_refs, bb, cols))
            ms = jnp.mean(yg * yg, axis=-1, keepdims=True)
            store_y(bb, cols, (yg * lax.rsqrt(ms + RMS_EPS) * nw_ref[:, cols]).astype(y_ref.dtype))

    if single_chunk:
        last_chunk()
    else:
        pl.when(c == pl.num_programs(1) - 1)(last_chunk)


def _spread_matrix(width):
    rows = np.arange(LANES)[:, None]
    cols = np.arange(M_N_HEADS * width)[None, :]
    return jnp.asarray((rows < 3 * M_N_HEADS) & (rows % M_N_HEADS == cols // width), bf16)


def ssd_mixer_core(xbc, z, dt, row0, Bn, L, conv0, h0, conv_w, conv_b, dt_bias, a_log, d_skip_x, norm_w):
    Q = M_CHUNK if L % M_CHUNK == 0 else L
    nc = L // Q
    has_init = h0 is not None
    BB = next(n for n in ((SSD_SEQS_PER_STEP if nc == 1 else SSD_SEQS_PER_STEP_CHUNKED), 1) if Bn % n == 0)
    const2 = lambda b, c: (0, 0)
    if nc == 1:
        blk0 = row0 // (BB * Q)
        row_maps = [lambda b, c: (blk0 + b, 0)]
        y_shape, y_block, y_map = (Bn * L, M_D_INNER), (BB * Q, M_D_INNER), (lambda b, c: (b, 0))
    else:
        blk0 = row0 // Q
        row_maps = [functools.partial(lambda b, c, bb: (blk0 + (b * BB + bb) * nc + c, 0), bb=bb) for bb in range(BB)]
        y_shape, y_block, y_map = (Bn, L, M_D_INNER), (BB, Q, M_D_INNER), (lambda b, c: (b, c, 0))
    rows_per_block = BB * Q if nc == 1 else Q
    in_specs, args = [], []
    for arr, width in ((xbc, M_CONV_DIM), (z, M_D_INNER), (dt, LANES)):
        in_specs += [pl.BlockSpec((rows_per_block, width), m) for m in row_maps]
        args += [arr] * len(row_maps)
    if has_init:
        in_specs += [
            pl.BlockSpec((BB, M_CONV - 1, M_CONV_DIM), lambda b, c: (b, 0, 0)),
            pl.BlockSpec((BB, M_N_HEADS, M_HEAD_DIM, M_D_STATE), lambda b, c: (b, 0, 0, 0)),
        ]
        args += [conv0, h0]
    e64 = _spread_matrix(M_HEAD_DIM)
    eq = _spread_matrix(Q if Q == LANES else SUBLANES)
    in_specs += [
        pl.BlockSpec((M_CONV, M_CONV_DIM), const2),
        pl.BlockSpec((1, M_CONV_DIM), const2),
        pl.BlockSpec((1, LANES), const2),
        pl.BlockSpec((1, LANES), const2),
        pl.BlockSpec((1, M_D_INNER), const2),
        pl.BlockSpec((1, M_D_INNER), const2),
        pl.BlockSpec(e64.shape, const2),
        pl.BlockSpec(eq.shape, const2),
    ]
    args += [conv_w, conv_b, dt_bias, a_log, d_skip_x, norm_w, e64, eq]
    y_dtype = bf16 if y_block[-2] % 16 == 0 else f32
    y, new_ssm, new_conv = pl.pallas_call(
        functools.partial(_ssd_body, Q=Q, BB=BB, nc=nc, has_init=has_init),
        out_shape=(
            jax.ShapeDtypeStruct(y_shape, y_dtype),
            jax.ShapeDtypeStruct((Bn, M_N_HEADS, M_HEAD_DIM, M_D_STATE), f32),
            jax.ShapeDtypeStruct((Bn, M_CONV - 1, M_CONV_DIM), f32),
        ),
        grid=(Bn // BB, nc),
        in_specs=in_specs,
        out_specs=(
            pl.BlockSpec(y_block, y_map),
            pl.BlockSpec((BB, M_N_HEADS, M_HEAD_DIM, M_D_STATE), lambda b, c: (b, 0, 0, 0)),
            pl.BlockSpec((BB, M_CONV - 1, M_CONV_DIM), lambda b, c: (b, 0, 0)),
        ),
        scratch_shapes=[
            pltpu.VMEM((1, 1, SUBLANES, LANES) if nc == 1 and has_init
                       else (BB, M_N_GROUPS, M_D_STATE, M_HPG * M_HEAD_DIM), f32),
            pltpu.VMEM((BB, Q + SUBLANES, M_CONV_DIM), f32),
            pltpu.VMEM((BB, Q, M_CONV_DIM), f32),
            pltpu.VMEM((BB, Q, M_D_INNER), f32),
            pltpu.VMEM((BB, Q, M_D_INNER), f32),
            pltpu.VMEM((BB, Q, M_D_INNER), f32),
            pltpu.VMEM((BB, Q, M_D_INNER), f32),
            pltpu.VMEM((BB, Q, M_N_HEADS * Q) if Q == LANES else (1, SUBLANES, LANES), f32),
        ],
        compiler_params=_cparams(("parallel", "arbitrary")),
        name="ssd_mixer_core",
    )(*args)
    return y.reshape(Bn * L, M_D_INNER), new_ssm, new_conv


N_PAGES = 16
N_CHUNKS = N_PAGES * PAGE_SIZE // CMP_STRIDE
N_SLABS = KV_ROW // LANES


def _compress_body(*refs, paged):
    if paged:
        refs = refs[1:]
    pages = refs[:N_PAGES]
    wbd_ref, pe_ref, w1f_ref, w2_ref, w2t_ref, kct_ref, vc_ref, xs_ref = refs[N_PAGES:]
    H = CMP_HIDDEN
    for p in range(N_PAGES):
        for sl in range(N_SLABS):
            xs_ref[sl, p * PAGE_SIZE:(p + 1) * PAGE_SIZE, :] = pages[p][0, sl * LANES:(sl + 1) * LANES, :].T
    row = lax.broadcasted_iota(i32, (N_CHUNKS, H), 0)
    for kv in range(2):
        pe_term = _dot(pe_ref[kv], w1f_ref[kv])[0:1, :]
        for gp in range(A_N_KV // 2):
            acc = jnp.zeros((N_CHUNKS, 2 * CMP_RATIO * H), f32)
            for jp in range(CMP_STRIDE // 2):
                x = jnp.concatenate(
                    [xs_ref[kv * (A_N_KV // 2) + gp, pl.ds(2 * jp + k, N_CHUNKS, stride=CMP_STRIDE), :] for k in range(2)],
                    axis=1).astype(bf16)
                acc = acc + _dot(x, wbd_ref[kv, jp])
            for gi in range(2):
                g = gp * 2 + gi
                p0 = acc[:, gi * CMP_RATIO * H:gi * CMP_RATIO * H + H]
                p1 = acc[:, gi * CMP_RATIO * H + H:(gi + 1) * CMP_RATIO * H]
                p1_next = jnp.where(row == N_CHUNKS - 1, 0.0, pltpu.roll(p1, N_CHUNKS - 1, 0))
                hid = _silu(p0 + p1_next + pe_term).astype(bf16)
                if kv == 0:
                    kct_ref[0, g * A_HEAD_DIM:(g + 1) * A_HEAD_DIM, :] = _dot_nt(w2t_ref[...], hid).astype(kct_ref.dtype)
                else:
                    vc_ref[0, :, g * A_HEAD_DIM:(g + 1) * A_HEAD_DIM] = _dot(hid, w2_ref[...]).astype(vc_ref.dtype)


def compress_kv(pages, page_table, wbd, pe_rows, w1_flat, w2_v, w2t_k):
    paged = page_table is not None
    Bn = page_table.shape[0] if paged else pages.shape[0]
    if paged:
        page_specs = [
            pl.BlockSpec((1, KV_ROW, PAGE_SIZE), functools.partial(lambda b, pt, j: (pt[b, j], 0, 0), j=j))
            for j in range(N_PAGES)
        ]
        const = lambda nd: (lambda b, pt: (0,) * nd)
        out_map = lambda b, pt: (b, 0, 0)
    else:
        page_specs = [
            pl.BlockSpec((1, KV_ROW, PAGE_SIZE), functools.partial(lambda b, j: (b, 0, j), j=j)) for j in range(N_PAGES)
        ]
        const = lambda nd: (lambda b: (0,) * nd)
        out_map = lambda b: (b, 0, 0)
    in_specs = page_specs + [
        pl.BlockSpec(wbd.shape, const(4)),
        pl.BlockSpec(pe_rows.shape, const(3)),
        pl.BlockSpec(w1_flat.shape, const(3)),
        pl.BlockSpec(w2_v.shape, const(2)),
        pl.BlockSpec(w2t_k.shape, const(2)),
    ]
    out_shape = (jax.ShapeDtypeStruct((Bn, KV_HALF, N_CHUNKS), bf16), jax.ShapeDtypeStruct((Bn, N_CHUNKS, KV_HALF), bf16))
    out_specs = (pl.BlockSpec((1, KV_HALF, N_CHUNKS), out_map), pl.BlockSpec((1, N_CHUNKS, KV_HALF), out_map))
    scratch = [pltpu.VMEM((N_SLABS, N_PAGES * PAGE_SIZE, LANES), f32)]
    body = functools.partial(_compress_body, paged=paged)
    args = ([pages] * N_PAGES) + [wbd, pe_rows, w1_flat, w2_v, w2t_k]
    if paged:
        return pl.pallas_call(
            body, out_shape=out_shape,
            grid_spec=pltpu.PrefetchScalarGridSpec(
                num_scalar_prefetch=1, grid=(Bn,), in_specs=in_specs, out_specs=out_specs, scratch_shapes=scratch),
            compiler_params=_cparams(("parallel",)), name="compress_kv_paged",
        )(page_table, *args)
    return pl.pallas_call(
        body, out_shape=out_shape, grid=(Bn,), in_specs=in_specs, out_specs=out_specs, scratch_shapes=scratch,
        compiler_params=_cparams(("parallel",)), name="compress_kv",
    )(*args)


def _bias_table_body(rb_ref, tbl_ref, tz0_ref, tz1_ref):
    n = lax.broadcasted_iota(i32, (A_N_HEADS, LANES), 1)
    max_exact = N_BUCKETS // 2
    large = max_exact + (jnp.log(jnp.maximum(n, max_exact).astype(f32) / max_exact)
                         / math.log(MAX_DISTANCE / max_exact) * (N_BUCKETS - max_exact)).astype(i32)
    bucket = jnp.where(n < max_exact, n, jnp.minimum(large, N_BUCKETS - 1))
    tbl = jnp.zeros((A_N_HEADS, LANES), f32)
    for b in range(N_BUCKETS):
        tbl = jnp.where(bucket == b, rb_ref[:, b:b + 1], tbl)
    tbl_ref[...] = tbl
    dist = lax.broadcasted_iota(i32, (LANES, LANES), 0) - lax.broadcasted_iota(i32, (LANES, LANES), 1)
    for h in range(A_N_HEADS):
        row = jnp.broadcast_to(tbl[h:h + 1, :], (LANES, LANES))
        own = jnp.take_along_axis(row, jnp.clip(dist, 0, MAX_DISTANCE - 1), axis=1)
        tz0_ref[h] = jnp.where(dist >= 0, own, MASK_VALUE)
        tz1_ref[h] = jnp.take_along_axis(row, jnp.minimum(dist + LANES, MAX_DISTANCE - 1), axis=1)


def bias_table(rel_bias_t):
    assert MAX_DISTANCE <= LANES
    tile = jax.ShapeDtypeStruct((A_N_HEADS, LANES, LANES), f32)
    return pl.pallas_call(
        _bias_table_body, out_shape=(jax.ShapeDtypeStruct((A_N_HEADS, LANES), f32), tile, tile), name="bias_table",
    )(rel_bias_t)


TILE = 128
WIN_TILES = WINDOW // TILE


WIDE = 2 * TILE
N_SLOTS = N_PAGES * TILE // WIDE
BIG = -MASK_VALUE


SEL, WIN = 0, 1
FAR, NEAR = 0, 1
WIN_SLOT0 = N_SLOTS


def _attn_prompt_body(q_ref, gate_ref, kct_ref, vc_ref, tbl_ref, tz0_ref, tz1_ref, ovlt_ref, selp_ref, winp_ref, o_ref,
                      s_scr, m_scr, l_scr, acc_scr, lhs_scr, oc_scr, fb_scr, *, qb, n_sel):
    i = pl.program_id(1)
    R = A_GROUP * qb
    dh = A_HEAD_DIM
    pos0 = i * qb
    odd = (i % 2) == 1
    td = i // 2
    scale = dh ** -0.5
    n_rank = SUBLANES * (-(-n_sel // SUBLANES))

    lane = lax.broadcasted_iota(i32, (R, TILE), 1)
    q_in_blk = jnp.concatenate([lax.broadcasted_iota(i32, (qb, TILE), 0)] * A_GROUP, axis=0)
    qpos = pos0 + q_in_blk
    neg_tile = jnp.full((R, TILE), MASK_VALUE, f32)

    s_idx = lax.broadcasted_iota(i32, (n_rank, qb), 0)
    s_qpos = pos0 + lax.broadcasted_iota(i32, (n_rank, qb), 1)
    blk = s_qpos // SEL_BLOCK
    sel_valid = s_idx * SEL_BLOCK <= s_qpos
    sel_forced = (s_idx == 0) | (s_idx == blk) | (s_idx == blk - 1)

    f_row = lax.broadcasted_iota(i32, (TILE - dh, WIDE), 0)
    flag_rows = [jnp.where(f_row == k, BIG, 0.0).astype(bf16) for k in range(2)]
    zero_flag = jnp.zeros((TILE - dh, WIDE), bf16)
    zero_drop = jnp.zeros((TILE, WIDE), bf16)
    b_row = lax.broadcasted_iota(i32, (TILE, WIDE), 0)
    b_col = lax.broadcasted_iota(i32, (TILE, WIDE), 1)
    f_lane = lax.broadcasted_iota(i32, (R, TILE - dh), 1)
    flags = jnp.where(f_lane == 0, jnp.where(td < 1, -1.0, 0.0),
                      jnp.where(f_lane == 1, jnp.where(td < 2, -1.0, 0.0), 0.0)).astype(bf16)
    win_thr = q_in_blk + jnp.where(odd, TILE, 0)

    def pair(ref, T, g, half):
        rows = slice(half * KV_HALF + g * dh, half * KV_HALF + (g + 1) * dh)
        return jnp.concatenate([ref[0, 2 * T, rows, :], ref[0, 2 * T + 1, rows, :]], axis=1)

    def drop_rows(T):
        return jnp.where(b_row == T * (WIDE // SEL_BLOCK) + b_col // SEL_BLOCK, BIG, 0.0).astype(bf16)

    def store_scores(g, br, slot, s, left, right, kind, first=False):
        s_l = s[:, 0:TILE] if left is None else s[:, 0:TILE] + left
        s_r = s[:, TILE:WIDE] if right is None else s[:, TILE:WIDE] + right
        s_scr[g, slot, :, 0:TILE] = s_l
        s_scr[g, slot, :, TILE:WIDE] = s_r
        mx = jnp.maximum(s_l, s_r)
        m_scr[g, br, kind] = mx if first else jnp.maximum(m_scr[g, br, kind], mx)

    def accumulate(g, br, slot, kind, v_t, first=False):
        shift = m_scr[g, br, kind]
        p_l = jnp.exp(s_scr[g, slot, :, 0:TILE] - shift)
        p_r = jnp.exp(s_scr[g, slot, :, TILE:WIDE] - shift)
        pv = _dot_nt(jnp.concatenate([p_l, p_r], axis=1).astype(bf16), v_t)
        if first:
            l_scr[g, br] = p_l + p_r
            acc_scr[g, br] = pv
        else:
            l_scr[g, br] += p_l + p_r
            acc_scr[g, br] += pv

    def row_max(g, br):
        fb = fb_scr[g][:, 0:1]
        m = jnp.maximum(jnp.max(m_scr[g, br, FAR], axis=1, keepdims=True) + fb,
                        jnp.max(m_scr[g, br, NEAR], axis=1, keepdims=True))
        m_scr[g, br, FAR] = jnp.broadcast_to(m - fb, (R, TILE))
        m_scr[g, br, NEAR] = jnp.broadcast_to(m, (R, TILE))

    def result(g, br):
        return acc_scr[g, br] / jnp.sum(l_scr[g, br], axis=1, keepdims=True)

    def near_tiles(g):
        tz0 = tz0_ref[g * A_GROUP:(g + 1) * A_GROUP].reshape(R, TILE)
        tz1 = tz1_ref[g * A_GROUP:(g + 1) * A_GROUP].reshape(R, TILE)
        return jnp.where(odd, fb_scr[g], tz1), jnp.where(odd, tz1, tz0), jnp.where(odd, tz0, neg_tile)

    t_prev = jnp.maximum(td - 1, 0)
    t_first = jnp.maximum(td - 2, 0)
    n_far = jnp.maximum(td - 1, 0)

    for g in range(A_N_KV):
        heads = [g * A_GROUP + r for r in range(A_GROUP)]
        qg = jnp.concatenate([q_ref[:, h * dh:(h + 1) * dh] for h in heads], axis=0)
        qg = (qg.astype(f32) * scale).astype(bf16)
        tbl_g = jnp.concatenate([jnp.broadcast_to(tbl_ref[h:h + 1, :], (qb, TILE)) for h in heads], axis=0)
        fb_scr[g] = jnp.concatenate(
            [jnp.broadcast_to(tbl_ref[h:h + 1, MAX_DISTANCE - 1:MAX_DISTANCE], (qb, TILE)) for h in heads], axis=0)

        dist_c = qpos - (lane * CMP_STRIDE + (CMP_LEN - 1))
        s_c = _dot(qg, kct_ref[0, g * dh:(g + 1) * dh, :])
        s_c = s_c + jnp.take_along_axis(tbl_g, jnp.clip(dist_c, 0, MAX_DISTANCE - 1), axis=1)
        s_c = jnp.where(dist_c >= 0, s_c, MASK_VALUE)
        m_c = jnp.max(s_c, axis=1, keepdims=True)
        p_c = jnp.where(dist_c >= 0, jnp.exp(s_c - m_c), 0.0)
        l_c = jnp.sum(p_c, axis=1, keepdims=True)
        l_c = jnp.where(l_c == 0.0, 1.0, l_c)
        oc_scr[g] = _dot(p_c.astype(bf16), vc_ref[0, :, g * dh:(g + 1) * dh]) / l_c
        p_c = p_c / l_c

        p_sum = p_c[0:qb]
        for r in range(1, A_GROUP):
            p_sum = p_sum + p_c[r * qb:(r + 1) * qb]
        imp_t = lax.dot_general(ovlt_ref[0:n_rank, :], p_sum, (((1,), (1,)), ((), ())),
                                preferred_element_type=f32, precision=_HI)
        score = jnp.where(sel_valid, imp_t + jnp.where(sel_forced, FORCE_SCORE, 0.0), -1.0)
        score = jnp.where(s_idx < n_sel, score, -3.0)
        rank = jnp.zeros((n_rank, qb), f32)
        for s2 in range(n_sel):
            other = score[s2:s2 + 1, :]
            rank = rank + jnp.where((other > score) | ((other == score) & (s_idx > s2)), 1.0, 0.0)
        dropped_t = jnp.where(rank < float(min(SEL_TOPN, n_sel)), 0.0, -1.0)
        dropped = jnp.concatenate([dropped_t, jnp.zeros((TILE - n_rank, qb), f32)], axis=0).T
        lhs_scr[g, SEL] = jnp.concatenate([qg, flags, jnp.concatenate([dropped] * A_GROUP, axis=0).astype(bf16)], axis=1)
        lhs_scr[g, WIN] = jnp.concatenate([qg, flags, jnp.zeros((R, TILE), bf16)], axis=1)
        m_scr[g, SEL, FAR] = neg_tile

    too_old_l = jnp.where(lane > win_thr, 0.0, MASK_VALUE)
    too_old_r = jnp.where(lane + TILE > win_thr, 0.0, MASK_VALUE)
    for g in range(A_N_KV):
        prev_right, diag_left, diag_right = near_tiles(g)
        lhs_win = lhs_scr[g, WIN]
        s = _dot(lhs_win, jnp.concatenate([pair(winp_ref, t_first, g, 0), flag_rows[1], zero_drop], axis=0))
        store_scores(g, WIN, WIN_SLOT0, s, too_old_l, too_old_r, FAR, first=True)
        s = _dot(lhs_win, jnp.concatenate([pair(winp_ref, t_prev, g, 0), flag_rows[0], zero_drop], axis=0))
        store_scores(g, WIN, WIN_SLOT0 + 1, s, fb_scr[g], prev_right, NEAR, first=True)
        s = _dot(lhs_win, jnp.concatenate([pair(winp_ref, td, g, 0), zero_flag, zero_drop], axis=0))
        store_scores(g, WIN, WIN_SLOT0 + 2, s, diag_left, diag_right, NEAR)
        lhs_sel = lhs_scr[g, SEL]
        s = _dot(lhs_sel, jnp.concatenate([pair(selp_ref, t_prev, g, 0), flag_rows[0], drop_rows(t_prev)], axis=0))
        store_scores(g, SEL, N_SLOTS - 2, s, fb_scr[g], prev_right, NEAR, first=True)
        s = _dot(lhs_sel, jnp.concatenate([pair(selp_ref, td, g, 0), zero_flag, drop_rows(td)], axis=0))
        store_scores(g, SEL, N_SLOTS - 1, s, diag_left, diag_right, NEAR)

    def far_scores(T, carry):
        drop = drop_rows(T)
        for g in range(A_N_KV):
            s = _dot(lhs_scr[g, SEL], jnp.concatenate([pair(selp_ref, T, g, 0), zero_flag, drop], axis=0))
            store_scores(g, SEL, T, s, None, None, FAR)
        return carry

    lax.fori_loop(0, n_far, far_scores, 0)

    for g in range(A_N_KV):
        row_max(g, WIN)
        row_max(g, SEL)
    for g in range(A_N_KV):
        accumulate(g, WIN, WIN_SLOT0, FAR, pair(winp_ref, t_first, g, 1), first=True)
        accumulate(g, WIN, WIN_SLOT0 + 1, NEAR, pair(winp_ref, t_prev, g, 1))
        accumulate(g, WIN, WIN_SLOT0 + 2, NEAR, pair(winp_ref, td, g, 1))
        accumulate(g, SEL, N_SLOTS - 2, NEAR, pair(selp_ref, t_prev, g, 1), first=True)
        accumulate(g, SEL, N_SLOTS - 1, NEAR, pair(selp_ref, td, g, 1))

    def far_accumulate(T, carry):
        for g in range(A_N_KV):
            accumulate(g, SEL, T, FAR, pair(selp_ref, T, g, 1))
        return carry

    lax.fori_loop(0, n_far, far_accumulate, 0)

    for g in range(A_N_KV):
        o_c, o_s, o_w = oc_scr[g], result(g, SEL), result(g, WIN)
        for r in range(A_GROUP):
            h = g * A_GROUP + r
            rows = slice(r * qb, (r + 1) * qb)
            gt = gate_ref[:, 3 * h:3 * h + 3]
            o_h = gt[:, 0:1] * o_c[rows] + gt[:, 1:2] * o_s[rows] + gt[:, 2:3] * o_w[rows]
            o_ref[:, h * dh:(h + 1) * dh] = o_h.astype(o_ref.dtype)


def nsa_attention_prompt(q, gates, Bn, L, kct, vc, tbl, tz0, tz1, overlap_t, sel_pages, win_pages):
    qb = Q_BLOCK
    nqb = L // qb
    n_sel = -(-L // SEL_BLOCK)
    n_tiles = L // TILE
    assert n_tiles == N_PAGES and qb == TILE
    R = A_GROUP * qb
    row_map = lambda b, i: (b * nqb + i, 0)
    seq_map3 = lambda b, i: (b, 0, 0)
    seq_map4 = lambda b, i: (b, 0, 0, 0)
    const2 = lambda b, i: (0, 0)
    const3 = lambda b, i: (0, 0, 0)
    return pl.pallas_call(
        functools.partial(_attn_prompt_body, qb=qb, n_sel=n_sel),
        out_shape=jax.ShapeDtypeStruct((Bn * L, A_Q_DIM), bf16),
        grid=(Bn, nqb),
        in_specs=[
            pl.BlockSpec((qb, A_Q_DIM), row_map),
            pl.BlockSpec((qb, LANES), row_map),
            pl.BlockSpec((1, KV_HALF, N_CHUNKS), seq_map3),
            pl.BlockSpec((1, N_CHUNKS, KV_HALF), seq_map3),
            pl.BlockSpec((A_N_HEADS, LANES), const2),
            pl.BlockSpec((A_N_HEADS, TILE, TILE), const3),
            pl.BlockSpec((A_N_HEADS, TILE, TILE), const3),
            pl.BlockSpec((TILE, TILE), const2),
            pl.BlockSpec((1, n_tiles, KV_ROW, TILE), seq_map4),
            pl.BlockSpec((1, n_tiles, KV_ROW, TILE), seq_map4),
        ],
        out_specs=pl.BlockSpec((qb, A_Q_DIM), row_map),
        scratch_shapes=[
            pltpu.VMEM((A_N_KV, N_SLOTS + 3, R, WIDE), f32),
            pltpu.VMEM((A_N_KV, 2, 2, R, TILE), f32),
            pltpu.VMEM((A_N_KV, 2, R, TILE), f32),
            pltpu.VMEM((A_N_KV, 2, R, A_HEAD_DIM), f32),
            pltpu.VMEM((A_N_KV, 2, R, WIDE), bf16),
            pltpu.VMEM((A_N_KV, R, A_HEAD_DIM), f32),
            pltpu.VMEM((A_N_KV, R, TILE), f32),
        ],
        compiler_params=_cparams(("parallel", "arbitrary")),
        name="nsa_attention_prompt",
    )(q, gates, kct, vc, tbl, tz0, tz1, overlap_t, sel_pages, win_pages)


def _attn_sample_body(pt_ref, q_ref, gate_ref, kct_ref, vc_ref, tbl_ref, ovl_ref, *refs, qb, start, n_sel):
    del pt_ref
    sel_pages = refs[:N_PAGES]
    selnew_ref, winpast_ref, winnew_ref, o_ref, s_scr = refs[N_PAGES:]
    dh = A_HEAD_DIM
    RG = A_GROUP * qb
    R = A_N_KV * RG
    scale = dh ** -0.5
    n_win_past = WINDOW // TILE
    assert R == TILE and n_sel <= SEL_BLOCK and start == N_PAGES * TILE

    qs = q_ref[...] * scale
    blocks = []
    for g in range(A_N_KV):
        qg = jnp.concatenate([qs[:, (g * A_GROUP + r) * dh:(g * A_GROUP + r + 1) * dh] for r in range(A_GROUP)], axis=0)
        parts = [qg if gg == g else jnp.zeros((RG, dh), f32) for gg in range(A_N_KV)]
        blocks.append(jnp.concatenate(parts, axis=1))
    qbd = jnp.concatenate(blocks, axis=0).astype(bf16)

    tbl_rows = jnp.concatenate([jnp.broadcast_to(tbl_ref[h:h + 1, :], (qb, LANES)) for h in range(A_N_HEADS)], axis=0)
    far_bias = tbl_rows[:, MAX_DISTANCE - 1:MAX_DISTANCE]
    lane = lax.broadcasted_iota(i32, (R, TILE), 1)
    row = lax.broadcasted_iota(i32, (R, TILE), 0)
    qpos = start + row % qb

    def near_bias(dist):
        return jnp.take_along_axis(tbl_rows, jnp.clip(dist, 0, MAX_DISTANCE - 1), axis=1)

    def softmax_rows(s):
        m = jnp.max(s, axis=1, keepdims=True)
        p = jnp.where(s > 0.5 * MASK_VALUE, jnp.exp(s - m), 0.0)
        l = jnp.sum(p, axis=1, keepdims=True)
        return p, jnp.where(l == 0.0, 1.0, l)

    def pad_rows(x):
        return jnp.concatenate([x, jnp.zeros((TILE - x.shape[0], x.shape[1]), x.dtype)], axis=0).astype(bf16)

    dist_c = qpos - (lane * CMP_STRIDE + (CMP_LEN - 1))
    s_c = _dot(qbd, kct_ref[0]) + near_bias(dist_c)
    p_c, l_c = softmax_rows(jnp.where(dist_c >= 0, s_c, MASK_VALUE))
    o_c = _dot(p_c.astype(bf16), vc_ref[0]) / l_c
    p_c = p_c / l_c

    p_sum = []
    for g in range(A_N_KV):
        acc = p_c[g * RG:g * RG + qb]
        for r in range(1, A_GROUP):
            acc = acc + p_c[g * RG + r * qb:g * RG + (r + 1) * qb]
        p_sum.append(acc)
    p_sum = jnp.concatenate(p_sum, axis=0)
    imp = jnp.dot(p_sum, ovl_ref[...], preferred_element_type=f32, precision=_HI)
    s_lane = lax.broadcasted_iota(i32, (A_N_KV * qb, TILE), 1)
    s_qpos = start + lax.broadcasted_iota(i32, (A_N_KV * qb, TILE), 0) % qb
    blk = s_qpos // SEL_BLOCK
    valid = s_lane * SEL_BLOCK <= s_qpos
    forced = (s_lane == 0) | (s_lane == blk) | (s_lane == blk - 1)
    score = jnp.where(valid, imp + jnp.where(forced, FORCE_SCORE, 0.0), -1.0)
    score = jnp.where(s_lane < n_sel, score, -3.0)
    rank = jnp.zeros(score.shape, f32)
    for s2 in range(n_sel):
        col = score[:, s2:s2 + 1]
        rank = rank + jnp.where((col > score) | ((col == score) & (s_lane > s2)), 1.0, 0.0)
    not_chosen = jnp.where(rank < float(min(SEL_TOPN, n_sel)), 0.0, -1.0)
    drop = jnp.concatenate(
        [not_chosen[g * qb:(g + 1) * qb] for g in range(A_N_KV) for _ in range(A_GROUP)], axis=0)
    drop = drop[:, 0:SEL_BLOCK].astype(bf16)
    b_row = lax.broadcasted_iota(i32, (SEL_BLOCK, TILE), 0)
    b_col = lax.broadcasted_iota(i32, (SEL_BLOCK, TILE), 1)

    def drop_unselected(t):
        expand = jnp.where(b_row == t * (TILE // SEL_BLOCK) + b_col // SEL_BLOCK, -MASK_VALUE, 0.0).astype(bf16)
        return _dot(drop, expand)

    def attend(tiles, new_ref, extra):
        n = len(tiles)
        for j, (k_t, _, kind) in enumerate(tiles):
            s = _dot(qbd, k_t.astype(bf16)) + extra(j)
            key0 = start - (n - j) * TILE
            dist = qpos - (key0 + lane)
            if kind == "far":
                s = s + far_bias
            else:
                s = s + near_bias(dist)
            if kind == "edge":
                s = jnp.where(dist < WINDOW, s, MASK_VALUE)
            s_scr[:, j * TILE:(j + 1) * TILE] = s
        new = new_ref[...]
        dist = qpos - (start + lane)
        s = _dot_nt(qbd, pad_rows(new[:, 0:KV_HALF])) + extra(n) + near_bias(dist)
        s_scr[:, n * TILE:(n + 1) * TILE] = jnp.where(dist >= 0, s, MASK_VALUE)
        p, l = softmax_rows(s_scr[:, 0:(n + 1) * TILE])
        p = p.astype(bf16)
        o = _dot(p[:, n * TILE:(n + 1) * TILE], pad_rows(new[:, KV_HALF:KV_ROW]))
        for j, (_, v_t, _) in enumerate(tiles):
            o = o + _dot_nt(p[:, j * TILE:(j + 1) * TILE], v_t.astype(bf16))
        return o / l

    sel_tiles = []
    for t in range(N_PAGES):
        page = sel_pages[t]
        sel_tiles.append((page[0, 0:KV_HALF, :], page[0, KV_HALF:KV_ROW, :], "near" if t == N_PAGES - 1 else "far"))
    o_s = attend(sel_tiles, selnew_ref, lambda j: drop_unselected(j))

    win_tiles = []
    for j in range(n_win_past):
        cols = slice(j * TILE, (j + 1) * TILE)
        kind = "edge" if j == 0 else ("near" if j == n_win_past - 1 else "far")
        win_tiles.append((winpast_ref[0, 0:KV_HALF, cols], winpast_ref[0, KV_HALF:KV_ROW, cols], kind))
    o_w = attend(win_tiles, winnew_ref, lambda j: 0.0)

    for g in range(A_N_KV):
        for r in range(A_GROUP):
            h = g * A_GROUP + r
            rows = slice(g * RG + r * qb, g * RG + (r + 1) * qb)
            cols = slice(g * dh, (g + 1) * dh)
            gt = gate_ref[:, 3 * h:3 * h + 3]
            o_h = gt[:, 0:1] * o_c[rows, cols] + gt[:, 1:2] * o_s[rows, cols] + gt[:, 2:3] * o_w[rows, cols]
            o_ref[:, h * dh:(h + 1) * dh] = o_h.astype(o_ref.dtype)


def nsa_attention_sample(q, gates, row0, Bn, L, start, kct, vc, tbl, overlap, sel_pages, page_table, sel_new,
                         win_past, win_new):
    qb = L
    n_sel = -(-(start + L) // SEL_BLOCK)
    blk0 = row0 // qb
    row_map = lambda b, pt: (blk0 + b, 0)
    seq_map = lambda b, pt: (b, 0, 0)
    page_specs = [
        pl.BlockSpec((1, KV_ROW, PAGE_SIZE), functools.partial(lambda b, pt, j: (pt[b, j], 0, 0), j=j))
        for j in range(N_PAGES)
    ]
    return pl.pallas_call(
        functools.partial(_attn_sample_body, qb=qb, start=start, n_sel=n_sel),
        out_shape=jax.ShapeDtypeStruct((Bn * L, A_Q_DIM), f32),
        grid_spec=pltpu.PrefetchScalarGridSpec(
            num_scalar_prefetch=1,
            grid=(Bn,),
            in_specs=[
                pl.BlockSpec((qb, A_Q_DIM), row_map),
                pl.BlockSpec((qb, LANES), row_map),
                pl.BlockSpec((1, KV_HALF, N_CHUNKS), seq_map),
                pl.BlockSpec((1, N_CHUNKS, KV_HALF), seq_map),
                pl.BlockSpec((A_N_HEADS, LANES), lambda b, pt: (0, 0)),
                pl.BlockSpec((TILE, TILE), lambda b, pt: (0, 0)),
            ] + page_specs + [
                pl.BlockSpec((qb, KV_ROW), row_map),
                pl.BlockSpec((1, KV_ROW, WINDOW), seq_map),
                pl.BlockSpec((qb, KV_ROW), row_map),
            ],
            out_specs=pl.BlockSpec((qb, A_Q_DIM), lambda b, pt: (b, 0)),
            scratch_shapes=[pltpu.VMEM((TILE, (N_PAGES + 1) * TILE), f32)],
        ),
        compiler_params=_cparams(("parallel",)),
        name="nsa_attention_sample",
    )(page_table, q, gates, kct, vc, tbl, overlap, *([sel_pages] * N_PAGES), sel_new, win_past, win_new)


def _pad_lanes(v):
    return jnp.pad(v, (0, LANES - v.shape[0])).reshape(1, LANES)


def _overlap_matrix():
    n_cmp = N_CHUNKS - CMP_RATIO + 1
    c = np.arange(TILE)[:, None] * CMP_STRIDE
    s = np.arange(TILE)[None, :] * SEL_BLOCK
    ov = (c < s + SEL_BLOCK) & (c + CMP_LEN > s) & (np.arange(TILE)[:, None] < n_cmp)
    return jnp.asarray(ov.astype(np.float32))


def _feature_major(x):
    lead = x.shape[:-4]
    n = len(lead)
    return jnp.transpose(x, tuple(range(n)) + (n + 1, n + 2, n + 3, n)).reshape(lead + (KV_ROW, x.shape[-4]))


def _token_major(x_t):
    B, _, T = x_t.shape
    return jnp.transpose(x_t.reshape(B, 2, A_N_KV, A_HEAD_DIM, T), (0, 4, 1, 2, 3))


def kernel(x_prompt, x_sample, state_ssm, state_conv, cache_cmp_kv, cache_sel_kv, cache_win_kv, page_table, ln_g, ln_b, m_in_w, m_conv_w, m_conv_b, m_dt_bias, m_a_log, m_d, m_norm_w, m_out_w, kv_w, cmp_w1, cmp_pe, cmp_w2, q_w, o_w, rel_bias, mlp_w1, mlp_w2):
    Bp, Lp, D = x_prompt.shape
    Bs, Ls, _ = x_sample.shape
    NP, NS = Bp * Lp, Bs * Ls
    past_len = page_table.shape[1] * PAGE_SIZE
    assert past_len == N_PAGES * PAGE_SIZE and Lp == N_PAGES * PAGE_SIZE and cache_win_kv.shape[1] == WINDOW

    in_w = m_in_w[0].astype(bf16)
    z_w = in_w[:, :M_D_INNER]
    xbc_w = in_w[:, M_D_INNER:M_D_INNER + M_CONV_DIM]
    dt_w = jnp.pad(in_w[:, M_D_INNER + M_CONV_DIM:], ((0, 0), (0, LANES - M_N_HEADS)))
    kvw = kv_w.astype(bf16)
    qw = q_w[0].astype(bf16)
    gate_w = jnp.pad(qw[:, A_Q_DIM:], ((0, 0), (0, LANES - 3 * A_N_HEADS)))
    w1b = cmp_w1.astype(bf16)
    w_j = jnp.transpose(w1b, (0, 2, 3, 1, 4)).reshape(2, CMP_STRIDE, A_HEAD_DIM, CMP_RATIO * CMP_HIDDEN)
    zeros = jnp.zeros_like(w_j)
    wbd = jnp.concatenate([jnp.concatenate([w_j, zeros], axis=3), jnp.concatenate([zeros, w_j], axis=3)], axis=2)
    wbd = wbd.reshape(2, CMP_STRIDE // 2, 2 * LANES, 2 * CMP_RATIO * CMP_HIDDEN)
    pe_rows = jnp.broadcast_to(cmp_pe.astype(bf16).reshape(2, 1, CMP_LEN * A_HEAD_DIM), (2, SUBLANES, CMP_LEN * A_HEAD_DIM))
    w1_flat = w1b.reshape(2, CMP_LEN * A_HEAD_DIM, CMP_HIDDEN)
    cmp_w = (wbd, pe_rows, w1_flat, cmp_w2[1].astype(bf16), cmp_w2[0].T.astype(bf16))

    x = jnp.concatenate([x_prompt.reshape(NP, D), x_sample.reshape(NS, D)], axis=0)
    xb = x.astype(bf16)
    z = matmul(xb, z_w, f32)
    xbc = matmul(xb, xbc_w, f32)
    dt = matmul(xb, dt_w, f32)
    ssd_w = (m_conv_w[0], m_conv_b[0].reshape(1, -1), _pad_lanes(m_dt_bias[0]), _pad_lanes(m_a_log[0]),
             jnp.repeat(m_d[0], M_HEAD_DIM).reshape(1, -1), m_norm_w[0].reshape(1, -1))
    y_p, p_ssm, p_conv = ssd_mixer_core(xbc, z, dt, 0, Bp, Lp, None, None, *ssd_w)
    y_s, s_ssm, s_conv = ssd_mixer_core(xbc, z, dt, NP, Bs, Ls, state_conv[0], state_ssm[0], *ssd_w)
    y = jnp.concatenate([y_p, y_s.astype(bf16)], axis=0)
    h_f, h_b = matmul_residual_ln(y, m_out_w[0].astype(bf16), x, ln_g[0, 0].reshape(1, D), ln_b[0, 0].reshape(1, D))
    h_f, h_b = mlp_residual_ln(h_b, h_f, mlp_w1[0].astype(bf16), mlp_w2[0].astype(bf16),
                               ln_g[0, 1].reshape(1, D), ln_b[0, 1].reshape(1, D))

    cmp_t, sel_t, win_t, sel_pg, win_pg = kv_project_feature_major(h_b, kvw.T, Bp, Lp)
    kv_s = matmul(h_b[NP:], kvw, f32)
    cmp_s, sel_s, win_s = kv_s[:, 0:KV_ROW], kv_s[:, KV_ROW:2 * KV_ROW], kv_s[:, 2 * KV_ROW:3 * KV_ROW]
    kct_p, vc_p = compress_kv(cmp_t, None, *cmp_w)
    kct_s, vc_s = compress_kv(_feature_major(cache_cmp_kv), page_table, *cmp_w)

    q = matmul(h_b, qw[:, :A_Q_DIM], f32)
    gates = matmul(h_b, gate_w, f32, act="sigmoid")
    tbl, tz0, tz1 = bias_table(rel_bias.T)
    overlap = _overlap_matrix()
    o_p = nsa_attention_prompt(q, gates, Bp, Lp, kct_p, vc_p, tbl, tz0, tz1, overlap.T, sel_pg, win_pg)
    o_s = nsa_attention_sample(q[NP:], gates[NP:], 0, Bs, Ls, past_len, kct_s, vc_s, tbl, overlap,
                               _feature_major(cache_sel_kv), page_table, sel_s, _feature_major(cache_win_kv), win_s)
    o = jnp.concatenate([o_p, o_s.astype(bf16)], axis=0)
    h_f, h_b = matmul_residual_ln(o, o_w[0].astype(bf16), h_f, ln_g[1, 0].reshape(1, D), ln_b[1, 0].reshape(1, D))
    out_p, out_s = mlp_residual_ln(h_b, h_f, mlp_w1[1].astype(bf16), mlp_w2[1].astype(bf16),
                                   ln_g[1, 1].reshape(1, D), ln_b[1, 1].reshape(1, D), split_rows=NP)

    kv_shape = (2, A_N_KV, A_HEAD_DIM)
    n_keep = min(WINDOW, Lp)
    s_win = jnp.concatenate([cache_win_kv[:, Ls:], win_s.reshape((Bs, Ls) + kv_shape)], axis=1)
    return (
        out_p.reshape(Bp, Lp, D), out_s.reshape(Bs, Ls, D),
        p_ssm[None], p_conv[None],
        _token_major(cmp_t), _token_major(sel_t), _token_major(win_t[:, :, Lp - n_keep:]),
        s_ssm[None], s_conv[None],
        cmp_s.reshape((Bs, Ls) + kv_shape), sel_s.reshape((Bs, Ls) + kv_shape), s_win,
    )
```

```python
import functools
import math

import jax
import jax.numpy as jnp
import numpy as np
from jax import lax
from jax.experimental import pallas as pl
from jax.experimental.pallas import tpu as pltpu

f32 = jnp.float32
bf16 = jnp.bfloat16
i32 = jnp.int32

D_MODEL = 1024
DEPTH = 2
DN_ALPHA = (2.0 * DEPTH) ** 0.25
LN_EPS = 1e-5
RMS_EPS = 1e-5
D_FF = 4 * D_MODEL
M_D_INNER = 2 * D_MODEL
M_HEAD_DIM = 64
M_N_HEADS = M_D_INNER // M_HEAD_DIM
M_N_GROUPS = 4
M_HPG = M_N_HEADS // M_N_GROUPS
M_D_STATE = 128
M_CONV = 4
M_CHUNK = 128
M_CONV_DIM = M_D_INNER + 2 * M_N_GROUPS * M_D_STATE
A_HEAD_DIM = 64
A_N_HEADS = D_MODEL // A_HEAD_DIM
A_N_KV = 4
A_GROUP = A_N_HEADS // A_N_KV
A_Q_DIM = A_N_HEADS * A_HEAD_DIM
KV_HALF = A_N_KV * A_HEAD_DIM
KV_ROW = 2 * KV_HALF
CMP_LEN = 32
CMP_STRIDE = 16
CMP_RATIO = CMP_LEN // CMP_STRIDE
CMP_HIDDEN = 2 * A_HEAD_DIM
SEL_BLOCK = 64
SEL_TOPN = 16
WINDOW = 512
Q_BLOCK = 128
N_BUCKETS = 32
MAX_DISTANCE = 128
MASK_VALUE = -1e30
FORCE_SCORE = 1e3
PAGE_SIZE = 128

LANES = 128
SUBLANES = 8
VMEM_LIMIT = 56 * 1024 * 1024

_HI = lax.Precision.HIGHEST


def _cparams(sem):
    return pltpu.CompilerParams(dimension_semantics=sem, vmem_limit_bytes=VMEM_LIMIT)


def _dot(a, b):
    return jnp.dot(a, b, preferred_element_type=f32)


def _dot_nt(a, b):
    return lax.dot_general(a, b, (((1,), (1,)), ((), ())), preferred_element_type=f32)


def _dot_tn(a, b):
    return lax.dot_general(a, b, (((0,), (0,)), ((), ())), preferred_element_type=f32)


def _silu(x):
    return x * (1.0 / (1.0 + jnp.exp(-x)))


def _layer_norm(x, g, b):
    mu = jnp.mean(x, axis=-1, keepdims=True)
    xc = x - mu
    var = jnp.mean(xc * xc, axis=-1, keepdims=True)
    return xc * lax.rsqrt(var + LN_EPS) * g + b


def _mm_body(x_ref, w_ref, o_ref, *, act):
    y = _dot(x_ref[...], w_ref[...])
    if act == "sigmoid":
        y = 1.0 / (1.0 + jnp.exp(-y))
    o_ref[...] = y.astype(o_ref.dtype)


def matmul(x, w, out_dtype, act=None, tm=1024, tn=1024):
    M, K = x.shape
    N = w.shape[1]
    tn = next(t for t in (tn, 512, 256, LANES) if N % t == 0)
    return pl.pallas_call(
        functools.partial(_mm_body, act=act),
        out_shape=jax.ShapeDtypeStruct((M, N), out_dtype),
        grid=(M // tm, N // tn),
        in_specs=[pl.BlockSpec((tm, K), lambda i, j: (i, 0)), pl.BlockSpec((K, tn), lambda i, j: (0, j))],
        out_specs=pl.BlockSpec((tm, tn), lambda i, j: (i, j)),
        compiler_params=_cparams(("parallel", "parallel")),
        name="matmul",
    )(x, w)


def _mm_res_ln_body(x0_ref, x1_ref, w_ref, r_ref, g_ref, b_ref, of_ref, ob_ref, *, n_head_blocks):
    i = pl.program_id(0)
    w = w_ref[...]
    acc = lax.cond(i < n_head_blocks, lambda: _dot(x0_ref[...], w), lambda: _dot(x1_ref[...].astype(bf16), w))
    h = _layer_norm(DN_ALPHA * r_ref[...] + acc, g_ref[...], b_ref[...])
    of_ref[...] = h
    ob_ref[...] = h.astype(bf16)


def matmul_residual_ln(x_head, x_tail, w, resid, g, b, tm=512):
    M0, K = x_head.shape
    M1 = x_tail.shape[0]
    N = w.shape[1]
    assert M0 % tm == 0 and M1 % tm == 0
    n_head = M0 // tm
    row_map = lambda i: (i, 0)
    const = lambda i: (0, 0)
    return pl.pallas_call(
        functools.partial(_mm_res_ln_body, n_head_blocks=n_head),
        out_shape=(jax.ShapeDtypeStruct((M0 + M1, N), f32), jax.ShapeDtypeStruct((M0 + M1, N), bf16)),
        grid=((M0 + M1) // tm,),
        in_specs=[
            pl.BlockSpec((tm, K), lambda i: (jnp.minimum(i, n_head - 1), 0)),
            pl.BlockSpec((tm, K), lambda i: (jnp.maximum(i - n_head, 0), 0)),
            pl.BlockSpec((K, N), const),
            pl.BlockSpec((tm, N), row_map),
            pl.BlockSpec((1, N), const),
            pl.BlockSpec((1, N), const),
        ],
        out_specs=(pl.BlockSpec((tm, N), row_map), pl.BlockSpec((tm, N), row_map)),
        compiler_params=_cparams(("arbitrary",)),
        name="matmul_residual_ln",
    )(x_head, x_tail, w, resid, g, b)


def _mlp_body(hb_ref, hf_ref, w1_ref, w2_ref, g_ref, b_ref, o0_ref, o1_ref, acc_ref, *, n_head_blocks):
    i = pl.program_id(0)
    j = pl.program_id(1)

    @pl.when(j == 0)
    def _():
        acc_ref[...] = jnp.zeros_like(acc_ref)

    u = jnp.maximum(_dot(hb_ref[...], w1_ref[...]), 0.0)
    acc_ref[...] += _dot((u * u).astype(bf16), w2_ref[...])
    last = j == pl.num_programs(1) - 1

    def result():
        return _layer_norm(DN_ALPHA * hf_ref[...] + acc_ref[...], g_ref[...], b_ref[...])

    if n_head_blocks is None:
        @pl.when(last)
        def _():
            h = result()
            o0_ref[...] = h
            o1_ref[...] = h.astype(bf16)
    else:
        @pl.when(last & (i < n_head_blocks))
        def _():
            o0_ref[...] = result()

        @pl.when(last & (i >= n_head_blocks))
        def _():
            o1_ref[...] = result()


def mlp_residual_ln(hb, hf, w1, w2, g, b, split_rows=None, tm=1024, tf=1024):
    M, D = hb.shape
    F = w1.shape[1]
    row_map = lambda i, j: (i, 0)
    if split_rows is None:
        n_head = None
        out_shape = (jax.ShapeDtypeStruct((M, D), f32), jax.ShapeDtypeStruct((M, D), bf16))
        out_specs = (pl.BlockSpec((tm, D), row_map), pl.BlockSpec((tm, D), row_map))
    else:
        assert split_rows % tm == 0 and (M - split_rows) % tm == 0
        n_head = split_rows // tm
        out_shape = (jax.ShapeDtypeStruct((split_rows, D), f32), jax.ShapeDtypeStruct((M - split_rows, D), f32))
        out_specs = (pl.BlockSpec((tm, D), lambda i, j: (jnp.minimum(i, n_head - 1), 0)),
                     pl.BlockSpec((tm, D), lambda i, j: (jnp.maximum(i - n_head, 0), 0)))
    return pl.pallas_call(
        functools.partial(_mlp_body, n_head_blocks=n_head),
        out_shape=out_shape,
        grid=(M // tm, F // tf),
        in_specs=[
            pl.BlockSpec((tm, D), row_map),
            pl.BlockSpec((tm, D), row_map),
            pl.BlockSpec((D, tf), lambda i, j: (0, j)),
            pl.BlockSpec((tf, D), lambda i, j: (j, 0)),
            pl.BlockSpec((1, D), lambda i, j: (0, 0)),
            pl.BlockSpec((1, D), lambda i, j: (0, 0)),
        ],
        out_specs=out_specs,
        scratch_shapes=[pltpu.VMEM((tm, D), f32)],
        compiler_params=_cparams(("arbitrary", "arbitrary")),
        name="mlp_residual_ln",
    )(hb, hf, w1, w2, g, b)


def _kv_project_body(wt_ref, h_ref, cmp_ref, sel_ref, win_ref, selp_ref, winp_ref):
    res = _dot_nt(wt_ref[...], h_ref[...])
    tm = h_ref.shape[0]
    cmp_ref[0] = res[0:KV_ROW]
    sel_ref[0] = res[KV_ROW:2 * KV_ROW]
    win_ref[0] = res[2 * KV_ROW:3 * KV_ROW]
    for k in range(tm // PAGE_SIZE):
        cols = slice(k * PAGE_SIZE, (k + 1) * PAGE_SIZE)
        selp_ref[0, k] = res[KV_ROW:2 * KV_ROW, cols].astype(bf16)
        winp_ref[0, k] = res[2 * KV_ROW:3 * KV_ROW, cols].astype(bf16)


def kv_project_feature_major(h_b, w_t, Bn, L, tm=512):
    nj = L // tm
    pages_per_step = tm // PAGE_SIZE
    fm = jax.ShapeDtypeStruct((Bn, KV_ROW, L), f32)
    pg = jax.ShapeDtypeStruct((Bn, L // PAGE_SIZE, KV_ROW, PAGE_SIZE), bf16)
    fm_spec = pl.BlockSpec((1, KV_ROW, tm), lambda b, j: (b, 0, j))
    pg_spec = pl.BlockSpec((1, pages_per_step, KV_ROW, PAGE_SIZE), lambda b, j: (b, j, 0, 0))
    return pl.pallas_call(
        _kv_project_body,
        out_shape=(fm, fm, fm, pg, pg),
        grid=(Bn, nj),
        in_specs=[
            pl.BlockSpec(w_t.shape, lambda b, j: (0, 0)),
            pl.BlockSpec((tm, h_b.shape[1]), lambda b, j: (b * nj + j, 0)),
        ],
        out_specs=(fm_spec, fm_spec, fm_spec, pg_spec, pg_spec),
        compiler_params=_cparams(("parallel", "parallel")),
        name="kv_project_feature_major",
    )(w_t, h_b)


SSD_SEQS_PER_STEP = 4
SSD_SEQS_PER_STEP_CHUNKED = 2


def _ssd_body(*refs, Q, BB, nc, has_init):
    n_in = 1 if nc == 1 else BB
    xbc_refs, z_refs, dt_refs = refs[0:n_in], refs[n_in:2 * n_in], refs[2 * n_in:3 * n_in]
    refs = refs[3 * n_in:]
    if has_init:
        conv0_ref, h0_ref = refs[0:2]
        refs = refs[2:]
    (cw_ref, cb_ref, dtb_ref, alog_ref, dskx_ref, nw_ref, e64_ref, eq_ref,
     y_ref, hout_ref, cout_ref, st_ref, xpad_ref, xc_ref, ybuf_ref, xdt_ref, xds_ref, ecx_ref, acx_ref) = refs

    def rows_of(row_refs, bb, cols=slice(None)):
        return row_refs[0][bb * Q:(bb + 1) * Q, cols] if nc == 1 else row_refs[bb][:, cols]

    def store_y(bb, cols, value):
        if nc == 1:
            y_ref[bb * Q:(bb + 1) * Q, cols] = value
        else:
            y_ref[bb, :, cols] = value

    c = pl.program_id(1)
    single_chunk = nc == 1
    P, N, R, G = M_HEAD_DIM, M_D_STATE, M_HPG, M_N_GROUPS
    PAD = SUBLANES

    per_head_state = single_chunk and has_init

    def first_chunk():
        for bb in range(BB):
            xpad_ref[bb, 0:PAD, :] = jnp.zeros((PAD, M_CONV_DIM), f32)
            if has_init:
                xpad_ref[bb, PAD - (M_CONV - 1):PAD, :] = conv0_ref[bb]
            if per_head_state:
                continue
            if has_init:
                for g in range(G):
                    for r in range(R):
                        st_ref[bb, g, :, r * P:(r + 1) * P] = h0_ref[bb, g * R + r].T
            else:
                st_ref[bb] = jnp.zeros(st_ref.shape[1:], f32)

    def last_chunk():
        if per_head_state:
            return
        for bb in range(BB):
            for g in range(G):
                for r in range(R):
                    hout_ref[bb, g * R + r] = st_ref[bb, g, :, r * P:(r + 1) * P].T

    if single_chunk:
        first_chunk()
    else:
        pl.when(c == 0)(first_chunk)

    ri = lax.broadcasted_iota(i32, (Q, Q), 0)
    ci = lax.broadcasted_iota(i32, (Q, Q), 1)
    tril = ri >= ci
    GW = M_D_INNER // G
    lane = lax.broadcasted_iota(i32, (Q, LANES), 1)
    third = M_N_HEADS
    assert 3 * third <= LANES

    def spread(v, e_ref):
        hi = v.astype(bf16).astype(f32)
        r1 = v - hi
        mid = r1.astype(bf16).astype(f32)
        lo = r1 - mid
        packed = jnp.where(lane < third, hi, jnp.where(lane < 2 * third, pltpu.roll(mid, third, 1),
                                                       jnp.where(lane < 3 * third, pltpu.roll(lo, 2 * third, 1), 0.0)))
        return _dot(packed.astype(bf16), e_ref[...])

    seqs = range(BB)
    a_cum, a_cum_t, e_last = [], [], []
    for bb in seqs:
        xpad_ref[bb, PAD:PAD + Q, :] = rows_of(xbc_refs, bb)
        acc = cb_ref[...] + xpad_ref[bb, pl.ds(PAD - 3, Q), :] * cw_ref[0:1, :]
        for k in range(1, M_CONV):
            acc = acc + xpad_ref[bb, pl.ds(PAD - 3 + k, Q), :] * cw_ref[k:k + 1, :]
        xc_ref[bb] = _silu(acc)
        cout_ref[bb] = xpad_ref[bb, pl.ds(Q + PAD - 3, 3), :]
        xpad_ref[bb, 0:PAD, :] = xpad_ref[bb, pl.ds(Q, PAD), :]

        xdt = rows_of(dt_refs, bb) + dtb_ref[...]
        dt = jnp.maximum(xdt, 0.0) + jnp.log1p(jnp.exp(-jnp.abs(xdt)))
        a = dt * (-jnp.exp(alog_ref[...]))
        a_cum.append(jnp.dot(tril.astype(f32), a, preferred_element_type=f32, precision=_HI))
        a_cum_t.append(a_cum[bb].T)
        a_last = a_cum[bb][Q - 1:Q, :]
        e_last.append(jnp.exp(a_last))
        xdt_all = xc_ref[bb, :, 0:M_D_INNER] * spread(dt, e64_ref)
        xdt_ref[bb] = xdt_all
        xds_ref[bb] = xdt_all * spread(jnp.exp(a_last - a_cum[bb]), e64_ref)
        ecx_ref[bb] = spread(jnp.exp(a_cum[bb]), e64_ref)
        if Q == LANES:
            acx_ref[bb] = spread(a_cum[bb], eq_ref)

    for g in range(G):
        gcols = slice(g * GW, (g + 1) * GW)
        bg = [xc_ref[bb, :, M_D_INNER + g * N:M_D_INNER + (g + 1) * N].astype(bf16) for bb in seqs]
        cg = [xc_ref[bb, :, M_D_INNER + G * N + g * N:M_D_INNER + G * N + (g + 1) * N].astype(bf16) for bb in seqs]
        gmat = [_dot_nt(cg[bb], bg[bb]) for bb in seqs]
        for r in range(R):
            h = g * R + r
            hcols = slice(h * P, (h + 1) * P)
            for bb in seqs:
                col = acx_ref[bb, :, h * Q:(h + 1) * Q] if Q == LANES else a_cum[bb][:, h:h + 1]
                row = a_cum_t[bb][h:h + 1, :]
                lmat = jnp.exp(jnp.where(tril, col - row, -jnp.inf))
                ydiag = _dot((gmat[bb] * lmat).astype(bf16), xdt_ref[bb, :, hcols].astype(bf16))
                if per_head_state:
                    h_in = h0_ref[bb, h]
                    hout_ref[bb, h] = (h_in * e_last[bb][:, h:h + 1]
                                       + _dot_tn(xds_ref[bb, :, hcols].astype(bf16), bg[bb]))
                    ydiag = ydiag + _dot_nt(cg[bb], h_in.astype(bf16)) * ecx_ref[bb, :, hcols]
                ybuf_ref[bb, :, hcols] = ydiag
        for bb in seqs:
            extra = xc_ref[bb, :, gcols] * dskx_ref[:, gcols]
            if not per_head_state:
                extra = extra + _dot(cg[bb], st_ref[bb, g].astype(bf16)) * ecx_ref[bb, :, gcols]
                new = _dot_tn(bg[bb], xds_ref[bb, :, gcols].astype(bf16))
                st_ref[bb, g] = st_ref[bb, g] * ecx_ref[bb, Q - 1:Q, gcols] + new
            ybuf_ref[bb, :, gcols] += extra

    for g in range(G):
        cols = slice(g * GW, (g + 1) * GW)
        for bb in seqs:
            yg = ybuf_ref[bb, :, cols] * _silu(rows_of(z_refs, bb, cols))
            ms = jnp.mean(yg * yg, axis=-1, keepdims=True)
            store_y(bb, cols, (yg * lax.rsqrt(ms + RMS_EPS) * nw_ref[:, cols]).astype(y_ref.dtype))

    if single_chunk:
        last_chunk()
    else:
        pl.when(c == pl.num_programs(1) - 1)(last_chunk)


def _spread_matrix(width):
    rows = np.arange(LANES)[:, None]
    cols = np.arange(M_N_HEADS * width)[None, :]
    return jnp.asarray((rows < 3 * M_N_HEADS) & (rows % M_N_HEADS == cols // width), bf16)


def ssd_mixer_core(xbc, z, dt, row0, Bn, L, conv0, h0, conv_w, conv_b, dt_bias, a_log, d_skip_x, norm_w):
    Q = M_CHUNK if L % M_CHUNK == 0 else L
    nc = L // Q
    has_init = h0 is not None
    BB = next(n for n in ((SSD_SEQS_PER_STEP if nc == 1 else SSD_SEQS_PER_STEP_CHUNKED), 1) if Bn % n == 0)
    const2 = lambda b, c: (0, 0)
    if nc == 1:
        blk0 = row0 // (BB * Q)
        row_maps = [lambda b, c: (blk0 + b, 0)]
        y_shape, y_block, y_map = (Bn * L, M_D_INNER), (BB * Q, M_D_INNER), (lambda b, c: (b, 0))
    else:
        blk0 = row0 // Q
        row_maps = [functools.partial(lambda b, c, bb: (blk0 + (b * BB + bb) * nc + c, 0), bb=bb) for bb in range(BB)]
        y_shape, y_block, y_map = (Bn, L, M_D_INNER), (BB, Q, M_D_INNER), (lambda b, c: (b, c, 0))
    rows_per_block = BB * Q if nc == 1 else Q
    in_specs, args = [], []
    for arr, width in ((xbc, M_CONV_DIM), (z, M_D_INNER), (dt, LANES)):
        in_specs += [pl.BlockSpec((rows_per_block, width), m) for m in row_maps]
        args += [arr] * len(row_maps)
    if has_init:
        in_specs += [
            pl.BlockSpec((BB, M_CONV - 1, M_CONV_DIM), lambda b, c: (b, 0, 0)),
            pl.BlockSpec((BB, M_N_HEADS, M_HEAD_DIM, M_D_STATE), lambda b, c: (b, 0, 0, 0)),
        ]
        args += [conv0, h0]
    e64 = _spread_matrix(M_HEAD_DIM)
    eq = _spread_matrix(Q if Q == LANES else SUBLANES)
    in_specs += [
        pl.BlockSpec((M_CONV, M_CONV_DIM), const2),
        pl.BlockSpec((1, M_CONV_DIM), const2),
        pl.BlockSpec((1, LANES), const2),
        pl.BlockSpec((1, LANES), const2),
        pl.BlockSpec((1, M_D_INNER), const2),
        pl.BlockSpec((1, M_D_INNER), const2),
        pl.BlockSpec(e64.shape, const2),
        pl.BlockSpec(eq.shape, const2),
    ]
    args += [conv_w, conv_b, dt_bias, a_log, d_skip_x, norm_w, e64, eq]
    y_dtype = bf16 if y_block[-2] % 16 == 0 else f32
    y, new_ssm, new_conv = pl.pallas_call(
        functools.partial(_ssd_body, Q=Q, BB=BB, nc=nc, has_init=has_init),
        out_shape=(
            jax.ShapeDtypeStruct(y_shape, y_dtype),
            jax.ShapeDtypeStruct((Bn, M_N_HEADS, M_HEAD_DIM, M_D_STATE), f32),
            jax.ShapeDtypeStruct((Bn, M_CONV - 1, M_CONV_DIM), f32),
        ),
        grid=(Bn // BB, nc),
        in_specs=in_specs,
        out_specs=(
            pl.BlockSpec(y_block, y_map),
            pl.BlockSpec((BB, M_N_HEADS, M_HEAD_DIM, M_D_STATE), lambda b, c: (b, 0, 0, 0)),
            pl.BlockSpec((BB, M_CONV - 1, M_CONV_DIM), lambda b, c: (b, 0, 0)),
        ),
        scratch_shapes=[
            pltpu.VMEM((1, 1, SUBLANES, LANES) if nc == 1 and has_init
                       else (BB, M_N_GROUPS, M_D_STATE, M_HPG * M_HEAD_DIM), f32),
            pltpu.VMEM((BB, Q + SUBLANES, M_CONV_DIM), f32),
            pltpu.VMEM((BB, Q, M_CONV_DIM), f32),
            pltpu.VMEM((BB, Q, M_D_INNER), f32),
            pltpu.VMEM((BB, Q, M_D_INNER), f32),
            pltpu.VMEM((BB, Q, M_D_INNER), f32),
            pltpu.VMEM((BB, Q, M_D_INNER), f32),
            pltpu.VMEM((BB, Q, M_N_HEADS * Q) if Q == LANES else (1, SUBLANES, LANES), f32),
        ],
        compiler_params=_cparams(("parallel", "arbitrary")),
        name="ssd_mixer_core",
    )(*args)
    return y.reshape(Bn * L, M_D_INNER), new_ssm, new_conv


N_PAGES = 16
N_CHUNKS = N_PAGES * PAGE_SIZE // CMP_STRIDE
N_SLABS = KV_ROW // LANES


def _compress_body(*refs, paged):
    if paged:
        refs = refs[1:]
    pages = refs[:N_PAGES]
    wbd_ref, pe_ref, w1f_ref, w2_ref, w2t_ref, kct_ref, vc_ref, xs_ref = refs[N_PAGES:]
    H = CMP_HIDDEN
    n_steps = CMP_STRIDE // 2
    pages_per_step = N_PAGES // n_steps

    def to_token_major(sl, page_range):
        for p in page_range:
            xs_ref[sl, p * PAGE_SIZE:(p + 1) * PAGE_SIZE, :] = pages[p][0, sl * LANES:(sl + 1) * LANES, :].T

    to_token_major(0, range(N_PAGES))
    row = lax.broadcasted_iota(i32, (N_CHUNKS, H), 0)
    for kv in range(2):
        pe_term = _dot(pe_ref[kv], w1f_ref[kv])[0:1, :]
        for gp in range(A_N_KV // 2):
            sl = kv * (A_N_KV // 2) + gp
            acc = jnp.zeros((N_CHUNKS, 2 * CMP_RATIO * H), f32)
            for jp in range(n_steps):
                x = jnp.concatenate(
                    [xs_ref[sl, pl.ds(2 * jp + k, N_CHUNKS, stride=CMP_STRIDE), :] for k in range(2)], axis=1).astype(bf16)
                acc = acc + _dot(x, wbd_ref[kv, jp])
                if sl + 1 < N_SLABS:
                    to_token_major(sl + 1, range(jp * pages_per_step, (jp + 1) * pages_per_step))
            for gi in range(2):
                g = gp * 2 + gi
                p0 = acc[:, gi * CMP_RATIO * H:gi * CMP_RATIO * H + H]
                p1 = acc[:, gi * CMP_RATIO * H + H:(gi + 1) * CMP_RATIO * H]
                p1_next = jnp.where(row == N_CHUNKS - 1, 0.0, pltpu.roll(p1, N_CHUNKS - 1, 0))
                hid = _silu(p0 + p1_next + pe_term).astype(bf16)
                if kv == 0:
                    kct_ref[0, g * A_HEAD_DIM:(g + 1) * A_HEAD_DIM, :] = _dot_nt(w2t_ref[...], hid).astype(kct_ref.dtype)
                else:
                    vc_ref[0, :, g * A_HEAD_DIM:(g + 1) * A_HEAD_DIM] = _dot(hid, w2_ref[...]).astype(vc_ref.dtype)


def compress_kv(pages, page_table, wbd, pe_rows, w1_flat, w2_v, w2t_k):
    paged = page_table is not None
    Bn = page_table.shape[0] if paged else pages.shape[0]
    if paged:
        page_specs = [
            pl.BlockSpec((1, KV_ROW, PAGE_SIZE), functools.partial(lambda b, pt, j: (pt[b, j], 0, 0), j=j))
            for j in range(N_PAGES)
        ]
        const = lambda nd: (lambda b, pt: (0,) * nd)
        out_map = lambda b, pt: (b, 0, 0)
    else:
        page_specs = [
            pl.BlockSpec((1, KV_ROW, PAGE_SIZE), functools.partial(lambda b, j: (b, 0, j), j=j)) for j in range(N_PAGES)
        ]
        const = lambda nd: (lambda b: (0,) * nd)
        out_map = lambda b: (b, 0, 0)
    in_specs = page_specs + [
        pl.BlockSpec(wbd.shape, const(4)),
        pl.BlockSpec(pe_rows.shape, const(3)),
        pl.BlockSpec(w1_flat.shape, const(3)),
        pl.BlockSpec(w2_v.shape, const(2)),
        pl.BlockSpec(w2t_k.shape, const(2)),
    ]
    out_shape = (jax.ShapeDtypeStruct((Bn, KV_HALF, N_CHUNKS), bf16), jax.ShapeDtypeStruct((Bn, N_CHUNKS, KV_HALF), bf16))
    out_specs = (pl.BlockSpec((1, KV_HALF, N_CHUNKS), out_map), pl.BlockSpec((1, N_CHUNKS, KV_HALF), out_map))
    scratch = [pltpu.VMEM((N_SLABS, N_PAGES * PAGE_SIZE, LANES), f32)]
    body = functools.partial(_compress_body, paged=paged)
    args = ([pages] * N_PAGES) + [wbd, pe_rows, w1_flat, w2_v, w2t_k]
    if paged:
        return pl.pallas_call(
            body, out_shape=out_shape,
            grid_spec=pltpu.PrefetchScalarGridSpec(
                num_scalar_prefetch=1, grid=(Bn,), in_specs=in_specs, out_specs=out_specs, scratch_shapes=scratch),
            compiler_params=_cparams(("parallel",)), name="compress_kv_paged",
        )(page_table, *args)
    return pl.pallas_call(
        body, out_shape=out_shape, grid=(Bn,), in_specs=in_specs, out_specs=out_specs, scratch_shapes=scratch,
        compiler_params=_cparams(("parallel",)), name="compress_kv",
    )(*args)


def _bias_table_body(rb_ref, tbl_ref, tz0_ref, tz1_ref):
    n = lax.broadcasted_iota(i32, (A_N_HEADS, LANES), 1)
    max_exact = N_BUCKETS // 2
    large = max_exact + (jnp.log(jnp.maximum(n, max_exact).astype(f32) / max_exact)
                         / math.log(MAX_DISTANCE / max_exact) * (N_BUCKETS - max_exact)).astype(i32)
    bucket = jnp.where(n < max_exact, n, jnp.minimum(large, N_BUCKETS - 1))
    tbl = jnp.zeros((A_N_HEADS, LANES), f32)
    for b in range(N_BUCKETS):
        tbl = jnp.where(bucket == b, rb_ref[:, b:b + 1], tbl)
    tbl_ref[...] = tbl
    dist = lax.broadcasted_iota(i32, (LANES, LANES), 0) - lax.broadcasted_iota(i32, (LANES, LANES), 1)
    for h in range(A_N_HEADS):
        row = jnp.broadcast_to(tbl[h:h + 1, :], (LANES, LANES))
        own = jnp.take_along_axis(row, jnp.clip(dist, 0, MAX_DISTANCE - 1), axis=1)
        tz0_ref[h] = jnp.where(dist >= 0, own, MASK_VALUE)
        tz1_ref[h] = jnp.take_along_axis(row, jnp.minimum(dist + LANES, MAX_DISTANCE - 1), axis=1)


def bias_table(rel_bias_t):
    assert MAX_DISTANCE <= LANES
    tile = jax.ShapeDtypeStruct((A_N_HEADS, LANES, LANES), f32)
    return pl.pallas_call(
        _bias_table_body, out_shape=(jax.ShapeDtypeStruct((A_N_HEADS, LANES), f32), tile, tile), name="bias_table",
    )(rel_bias_t)


TILE = 128
WIN_TILES = WINDOW // TILE


WIDE = 2 * TILE
N_SLOTS = N_PAGES * TILE // WIDE
BIG = -MASK_VALUE


SEL, WIN = 0, 1
FAR, NEAR = 0, 1
WIN_SLOT0 = N_SLOTS


def _attn_prompt_body(q_ref, gate_ref, kct_ref, vc_ref, tbl_ref, tz0_ref, tz1_ref, ovlt_ref, selp_ref, winp_ref, o_ref,
                      s_scr, m_scr, l_scr, acc_scr, lhs_scr, oc_scr, fb_scr, *, qb, n_sel):
    i = pl.program_id(1)
    R = A_GROUP * qb
    dh = A_HEAD_DIM
    pos0 = i * qb
    odd = (i % 2) == 1
    td = i // 2
    scale = dh ** -0.5
    n_rank = SUBLANES * (-(-n_sel // SUBLANES))

    lane = lax.broadcasted_iota(i32, (R, TILE), 1)
    q_in_blk = jnp.concatenate([lax.broadcasted_iota(i32, (qb, TILE), 0)] * A_GROUP, axis=0)
    qpos = pos0 + q_in_blk
    neg_tile = jnp.full((R, TILE), MASK_VALUE, f32)

    s_idx = lax.broadcasted_iota(i32, (n_rank, qb), 0)
    s_qpos = pos0 + lax.broadcasted_iota(i32, (n_rank, qb), 1)
    blk = s_qpos // SEL_BLOCK
    sel_valid = s_idx * SEL_BLOCK <= s_qpos
    sel_forced = (s_idx == 0) | (s_idx == blk) | (s_idx == blk - 1)

    f_row = lax.broadcasted_iota(i32, (TILE - dh, WIDE), 0)
    flag_rows = [jnp.where(f_row == k, BIG, 0.0).astype(bf16) for k in range(2)]
    zero_flag = jnp.zeros((TILE - dh, WIDE), bf16)
    zero_drop = jnp.zeros((TILE, WIDE), bf16)
    b_row = lax.broadcasted_iota(i32, (TILE, WIDE), 0)
    b_col = lax.broadcasted_iota(i32, (TILE, WIDE), 1)
    f_lane = lax.broadcasted_iota(i32, (R, TILE - dh), 1)
    flags = jnp.where(f_lane == 0, jnp.where(td < 1, -1.0, 0.0),
                      jnp.where(f_lane == 1, jnp.where(td < 2, -1.0, 0.0), 0.0)).astype(bf16)
    win_thr = q_in_blk + jnp.where(odd, TILE, 0)

    def pair(ref, T, g, half):
        rows = slice(half * KV_HALF + g * dh, half * KV_HALF + (g + 1) * dh)
        return jnp.concatenate([ref[0, 2 * T, rows, :], ref[0, 2 * T + 1, rows, :]], axis=1)

    def drop_rows(T):
        return jnp.where(b_row == T * (WIDE // SEL_BLOCK) + b_col // SEL_BLOCK, BIG, 0.0).astype(bf16)

    def store_scores(g, br, slot, s, left, right, kind, first=False):
        s_l = s[:, 0:TILE] if left is None else s[:, 0:TILE] + left
        s_r = s[:, TILE:WIDE] if right is None else s[:, TILE:WIDE] + right
        s_scr[g, slot, :, 0:TILE] = s_l
        s_scr[g, slot, :, TILE:WIDE] = s_r
        mx = jnp.maximum(s_l, s_r)
        m_scr[g, br, kind] = mx if first else jnp.maximum(m_scr[g, br, kind], mx)

    def accumulate(g, br, slot, kind, v_t, first=False):
        shift = m_scr[g, br, kind]
        p_l = jnp.exp(s_scr[g, slot, :, 0:TILE] - shift)
        p_r = jnp.exp(s_scr[g, slot, :, TILE:WIDE] - shift)
        pv = _dot_nt(jnp.concatenate([p_l, p_r], axis=1).astype(bf16), v_t)
        if first:
            l_scr[g, br] = p_l + p_r
            acc_scr[g, br] = pv
        else:
            l_scr[g, br] += p_l + p_r
            acc_scr[g, br] += pv

    def row_max(g, br):
        fb = fb_scr[g][:, 0:1]
        m = jnp.maximum(jnp.max(m_scr[g, br, FAR], axis=1, keepdims=True) + fb,
                        jnp.max(m_scr[g, br, NEAR], axis=1, keepdims=True))
        m_scr[g, br, FAR] = jnp.broadcast_to(m - fb, (R, TILE))
        m_scr[g, br, NEAR] = jnp.broadcast_to(m, (R, TILE))

    def result(g, br):
        return acc_scr[g, br] / jnp.sum(l_scr[g, br], axis=1, keepdims=True)

    def near_tiles(g):
        tz0 = tz0_ref[g * A_GROUP:(g + 1) * A_GROUP].reshape(R, TILE)
        tz1 = tz1_ref[g * A_GROUP:(g + 1) * A_GROUP].reshape(R, TILE)
        return jnp.where(odd, fb_scr[g], tz1), jnp.where(odd, tz1, tz0), jnp.where(odd, tz0, neg_tile)

    t_prev = jnp.maximum(td - 1, 0)
    t_first = jnp.maximum(td - 2, 0)
    n_far = jnp.maximum(td - 1, 0)

    for g in range(A_N_KV):
        heads = [g * A_GROUP + r for r in range(A_GROUP)]
        qg = jnp.concatenate([q_ref[:, h * dh:(h + 1) * dh] for h in heads], axis=0)
        qg = (qg.astype(f32) * scale).astype(bf16)
        tbl_g = jnp.concatenate([jnp.broadcast_to(tbl_ref[h:h + 1, :], (qb, TILE)) for h in heads], axis=0)
        fb_scr[g] = jnp.concatenate(
            [jnp.broadcast_to(tbl_ref[h:h + 1, MAX_DISTANCE - 1:MAX_DISTANCE], (qb, TILE)) for h in heads], axis=0)

        dist_c = qpos - (lane * CMP_STRIDE + (CMP_LEN - 1))
        s_c = _dot(qg, kct_ref[0, g * dh:(g + 1) * dh, :])
        s_c = s_c + jnp.take_along_axis(tbl_g, jnp.clip(dist_c, 0, MAX_DISTANCE - 1), axis=1)
        s_c = jnp.where(dist_c >= 0, s_c, MASK_VALUE)
        m_c = jnp.max(s_c, axis=1, keepdims=True)
        p_c = jnp.where(dist_c >= 0, jnp.exp(s_c - m_c), 0.0)
        l_c = jnp.sum(p_c, axis=1, keepdims=True)
        l_c = jnp.where(l_c == 0.0, 1.0, l_c)
        oc_scr[g] = _dot(p_c.astype(bf16), vc_ref[0, :, g * dh:(g + 1) * dh]) / l_c
        p_c = p_c / l_c

        p_sum = p_c[0:qb]
        for r in range(1, A_GROUP):
            p_sum = p_sum + p_c[r * qb:(r + 1) * qb]
        imp_t = lax.dot_general(ovlt_ref[0:n_rank, :], p_sum, (((1,), (1,)), ((), ())),
                                preferred_element_type=f32, precision=_HI)
        score = jnp.where(sel_valid, imp_t + jnp.where(sel_forced, FORCE_SCORE, 0.0), -1.0)
        score = jnp.where(s_idx < n_sel, score, -3.0)
        rank = jnp.zeros((n_rank, qb), f32)
        for s2 in range(n_sel):
            other = score[s2:s2 + 1, :]
            rank = rank + jnp.where((other > score) | ((other == score) & (s_idx > s2)), 1.0, 0.0)
        dropped_t = jnp.where(rank < float(min(SEL_TOPN, n_sel)), 0.0, -1.0)
        dropped = jnp.concatenate([dropped_t, jnp.zeros((TILE - n_rank, qb), f32)], axis=0).T
        lhs_scr[g, SEL] = jnp.concatenate([qg, flags, jnp.concatenate([dropped] * A_GROUP, axis=0).astype(bf16)], axis=1)
        lhs_scr[g, WIN] = jnp.concatenate([qg, flags, jnp.zeros((R, TILE), bf16)], axis=1)
        m_scr[g, SEL, FAR] = neg_tile

    too_old_l = jnp.where(lane > win_thr, 0.0, MASK_VALUE)
    too_old_r = jnp.where(lane + TILE > win_thr, 0.0, MASK_VALUE)
    for g in range(A_N_KV):
        prev_right, diag_left, diag_right = near_tiles(g)
        lhs_win = lhs_scr[g, WIN]
        s = _dot(lhs_win, jnp.concatenate([pair(winp_ref, t_first, g, 0), flag_rows[1], zero_drop], axis=0))
        store_scores(g, WIN, WIN_SLOT0, s, too_old_l, too_old_r, FAR, first=True)
        s = _dot(lhs_win, jnp.concatenate([pair(winp_ref, t_prev, g, 0), flag_rows[0], zero_drop], axis=0))
        store_scores(g, WIN, WIN_SLOT0 + 1, s, fb_scr[g], prev_right, NEAR, first=True)
        s = _dot(lhs_win, jnp.concatenate([pair(winp_ref, td, g, 0), zero_flag, zero_drop], axis=0))
        store_scores(g, WIN, WIN_SLOT0 + 2, s, diag_left, diag_right, NEAR)
        lhs_sel = lhs_scr[g, SEL]
        s = _dot(lhs_sel, jnp.concatenate([pair(selp_ref, t_prev, g, 0), flag_rows[0], drop_rows(t_prev)], axis=0))
        store_scores(g, SEL, N_SLOTS - 2, s, fb_scr[g], prev_right, NEAR, first=True)
        s = _dot(lhs_sel, jnp.concatenate([pair(selp_ref, td, g, 0), zero_flag, drop_rows(td)], axis=0))
        store_scores(g, SEL, N_SLOTS - 1, s, diag_left, diag_right, NEAR)

    def far_scores(T, carry):
        drop = drop_rows(T)
        for g in range(A_N_KV):
            s = _dot(lhs_scr[g, SEL], jnp.concatenate([pair(selp_ref, T, g, 0), zero_flag, drop], axis=0))
            store_scores(g, SEL, T, s, None, None, FAR)
        return carry

    lax.fori_loop(0, n_far, far_scores, 0)

    for g in range(A_N_KV):
        row_max(g, WIN)
        row_max(g, SEL)
    for g in range(A_N_KV):
        accumulate(g, WIN, WIN_SLOT0, FAR, pair(winp_ref, t_first, g, 1), first=True)
        accumulate(g, WIN, WIN_SLOT0 + 1, NEAR, pair(winp_ref, t_prev, g, 1))
        accumulate(g, WIN, WIN_SLOT0 + 2, NEAR, pair(winp_ref, td, g, 1))
        accumulate(g, SEL, N_SLOTS - 2, NEAR, pair(selp_ref, t_prev, g, 1), first=True)
        accumulate(g, SEL, N_SLOTS - 1, NEAR, pair(selp_ref, td, g, 1))

    def far_accumulate(T, carry):
        for g in range(A_N_KV):
            accumulate(g, SEL, T, FAR, pair(selp_ref, T, g, 1))
        return carry

    lax.fori_loop(0, n_far, far_accumulate, 0)

    for g in range(A_N_KV):
        o_c, o_s, o_w = oc_scr[g], result(g, SEL), result(g, WIN)
        for r in range(A_GROUP):
            h = g * A_GROUP + r
            rows = slice(r * qb, (r + 1) * qb)
            gt = gate_ref[:, 3 * h:3 * h + 3]
            o_h = gt[:, 0:1] * o_c[rows] + gt[:, 1:2] * o_s[rows] + gt[:, 2:3] * o_w[rows]
            o_ref[:, h * dh:(h + 1) * dh] = o_h.astype(o_ref.dtype)


def nsa_attention_prompt(q, gates, Bn, L, kct, vc, tbl, tz0, tz1, overlap_t, sel_pages, win_pages):
    qb = Q_BLOCK
    nqb = L // qb
    n_sel = -(-L // SEL_BLOCK)
    n_tiles = L // TILE
    assert n_tiles == N_PAGES and qb == TILE
    R = A_GROUP * qb
    row_map = lambda b, i: (b * nqb + i, 0)
    seq_map3 = lambda b, i: (b, 0, 0)
    seq_map4 = lambda b, i: (b, 0, 0, 0)
    const2 = lambda b, i: (0, 0)
    const3 = lambda b, i: (0, 0, 0)
    return pl.pallas_call(
        functools.partial(_attn_prompt_body, qb=qb, n_sel=n_sel),
        out_shape=jax.ShapeDtypeStruct((Bn * L, A_Q_DIM), bf16),
        grid=(Bn, nqb),
        in_specs=[
            pl.BlockSpec((qb, A_Q_DIM), row_map),
            pl.BlockSpec((qb, LANES), row_map),
            pl.BlockSpec((1, KV_HALF, N_CHUNKS), seq_map3),
            pl.BlockSpec((1, N_CHUNKS, KV_HALF), seq_map3),
            pl.BlockSpec((A_N_HEADS, LANES), const2),
            pl.BlockSpec((A_N_HEADS, TILE, TILE), const3),
            pl.BlockSpec((A_N_HEADS, TILE, TILE), const3),
            pl.BlockSpec((TILE, TILE), const2),
            pl.BlockSpec((1, n_tiles, KV_ROW, TILE), seq_map4),
            pl.BlockSpec((1, n_tiles, KV_ROW, TILE), seq_map4),
        ],
        out_specs=pl.BlockSpec((qb, A_Q_DIM), row_map),
        scratch_shapes=[
            pltpu.VMEM((A_N_KV, N_SLOTS + 3, R, WIDE), f32),
            pltpu.VMEM((A_N_KV, 2, 2, R, TILE), f32),
            pltpu.VMEM((A_N_KV, 2, R, TILE), f32),
            pltpu.VMEM((A_N_KV, 2, R, A_HEAD_DIM), f32),
            pltpu.VMEM((A_N_KV, 2, R, WIDE), bf16),
            pltpu.VMEM((A_N_KV, R, A_HEAD_DIM), f32),
            pltpu.VMEM((A_N_KV, R, TILE), f32),
        ],
        compiler_params=_cparams(("parallel", "arbitrary")),
        name="nsa_attention_prompt",
    )(q, gates, kct, vc, tbl, tz0, tz1, overlap_t, sel_pages, win_pages)


def _attn_sample_body(pt_ref, q_ref, gate_ref, kct_ref, vc_ref, tbl_ref, ovl_ref, *refs, qb, start, n_sel):
    del pt_ref
    sel_pages = refs[:N_PAGES]
    selnew_ref, winpast_ref, winnew_ref, o_ref, s_scr = refs[N_PAGES:]
    dh = A_HEAD_DIM
    RG = A_GROUP * qb
    R = A_N_KV * RG
    scale = dh ** -0.5
    n_win_past = WINDOW // TILE
    assert R == TILE and n_sel <= SEL_BLOCK and start == N_PAGES * TILE

    qs = q_ref[...] * scale
    blocks = []
    for g in range(A_N_KV):
        qg = jnp.concatenate([qs[:, (g * A_GROUP + r) * dh:(g * A_GROUP + r + 1) * dh] for r in range(A_GROUP)], axis=0)
        parts = [qg if gg == g else jnp.zeros((RG, dh), f32) for gg in range(A_N_KV)]
        blocks.append(jnp.concatenate(parts, axis=1))
    qbd = jnp.concatenate(blocks, axis=0).astype(bf16)

    tbl_rows = jnp.concatenate([jnp.broadcast_to(tbl_ref[h:h + 1, :], (qb, LANES)) for h in range(A_N_HEADS)], axis=0)
    far_bias = tbl_rows[:, MAX_DISTANCE - 1:MAX_DISTANCE]
    lane = lax.broadcasted_iota(i32, (R, TILE), 1)
    row = lax.broadcasted_iota(i32, (R, TILE), 0)
    qpos = start + row % qb

    def near_bias(dist):
        return jnp.take_along_axis(tbl_rows, jnp.clip(dist, 0, MAX_DISTANCE - 1), axis=1)

    def softmax_rows(s):
        m = jnp.max(s, axis=1, keepdims=True)
        p = jnp.where(s > 0.5 * MASK_VALUE, jnp.exp(s - m), 0.0)
        l = jnp.sum(p, axis=1, keepdims=True)
        return p, jnp.where(l == 0.0, 1.0, l)

    def pad_rows(x):
        return jnp.concatenate([x, jnp.zeros((TILE - x.shape[0], x.shape[1]), x.dtype)], axis=0).astype(bf16)

    def branch_scores(tiles, new_ref, extra, col0):
        n = len(tiles)
        for j, (kv_t, kind) in enumerate(tiles):
            s = _dot(qbd, kv_t(0).astype(bf16)) + extra(j)
            dist = qpos - (start - (n - j) * TILE + lane)
            s = s + (far_bias if kind == "far" else near_bias(dist))
            if kind == "edge":
                s = jnp.where(dist < WINDOW, s, MASK_VALUE)
            s_scr[:, col0 + j * TILE:col0 + (j + 1) * TILE] = s
        dist = qpos - (start + lane)
        s = _dot_nt(qbd, pad_rows(new_ref[:, 0:KV_HALF])) + extra(n) + near_bias(dist)
        s_scr[:, col0 + n * TILE:col0 + (n + 1) * TILE] = jnp.where(dist >= 0, s, MASK_VALUE)

    def branch_output(tiles, new_ref, col0):
        n = len(tiles)
        p, l = softmax_rows(s_scr[:, col0:col0 + (n + 1) * TILE])
        p = p.astype(bf16)
        o = _dot(p[:, n * TILE:(n + 1) * TILE], pad_rows(new_ref[:, KV_HALF:KV_ROW]))
        for j, (kv_t, _) in enumerate(tiles):
            o = o + _dot_nt(p[:, j * TILE:(j + 1) * TILE], kv_t(1).astype(bf16))
        return o / l

    def page_getter(ref, cols=slice(None)):
        return lambda half: ref[0, half * KV_HALF:(half + 1) * KV_HALF, cols]

    sel_tiles = [(page_getter(sel_pages[t]), "near" if t == N_PAGES - 1 else "far") for t in range(N_PAGES)]
    win_tiles = [(page_getter(winpast_ref, slice(j * TILE, (j + 1) * TILE)),
                  "edge" if j == 0 else ("near" if j == n_win_past - 1 else "far")) for j in range(n_win_past)]
    win_col0 = (N_PAGES + 1) * TILE

    def sel_scores(extra):
        branch_scores(sel_tiles, selnew_ref, extra, 0)

    branch_scores(win_tiles, winnew_ref, lambda j: 0.0, win_col0)

    dist_c = qpos - (lane * CMP_STRIDE + (CMP_LEN - 1))
    s_c = _dot(qbd, kct_ref[0]) + near_bias(dist_c)
    p_c, l_c = softmax_rows(jnp.where(dist_c >= 0, s_c, MASK_VALUE))
    o_c = _dot(p_c.astype(bf16), vc_ref[0]) / l_c
    p_c = p_c / l_c

    p_sum = []
    for g in range(A_N_KV):
        acc = p_c[g * RG:g * RG + qb]
        for r in range(1, A_GROUP):
            acc = acc + p_c[g * RG + r * qb:g * RG + (r + 1) * qb]
        p_sum.append(acc)
    p_sum = jnp.concatenate(p_sum, axis=0)
    imp = jnp.dot(p_sum, ovl_ref[...], preferred_element_type=f32, precision=_HI)
    s_lane = lax.broadcasted_iota(i32, (A_N_KV * qb, TILE), 1)
    s_qpos = start + lax.broadcasted_iota(i32, (A_N_KV * qb, TILE), 0) % qb
    blk = s_qpos // SEL_BLOCK
    valid = s_lane * SEL_BLOCK <= s_qpos
    forced = (s_lane == 0) | (s_lane == blk) | (s_lane == blk - 1)
    score = jnp.where(valid, imp + jnp.where(forced, FORCE_SCORE, 0.0), -1.0)
    score = jnp.where(s_lane < n_sel, score, -3.0)
    rank = jnp.zeros(score.shape, f32)
    for s2 in range(n_sel):
        col = score[:, s2:s2 + 1]
        rank = rank + jnp.where((col > score) | ((col == score) & (s_lane > s2)), 1.0, 0.0)
    not_chosen = jnp.where(rank < float(min(SEL_TOPN, n_sel)), 0.0, -1.0)
    drop = jnp.concatenate(
        [not_chosen[g * qb:(g + 1) * qb] for g in range(A_N_KV) for _ in range(A_GROUP)], axis=0)
    drop = drop[:, 0:SEL_BLOCK].astype(bf16)
    b_row = lax.broadcasted_iota(i32, (SEL_BLOCK, TILE), 0)
    b_col = lax.broadcasted_iota(i32, (SEL_BLOCK, TILE), 1)

    def drop_unselected(t):
        expand = jnp.where(b_row == t * (TILE // SEL_BLOCK) + b_col // SEL_BLOCK, -MASK_VALUE, 0.0).astype(bf16)
        return _dot(drop, expand)

    sel_scores(lambda j: drop_unselected(j))
    o_w = branch_output(win_tiles, winnew_ref, win_col0)
    o_s = branch_output(sel_tiles, selnew_ref, 0)

    for g in range(A_N_KV):
        for r in range(A_GROUP):
            h = g * A_GROUP + r
            rows = slice(g * RG + r * qb, g * RG + (r + 1) * qb)
            cols = slice(g * dh, (g + 1) * dh)
            gt = gate_ref[:, 3 * h:3 * h + 3]
            o_h = gt[:, 0:1] * o_c[rows, cols] + gt[:, 1:2] * o_s[rows, cols] + gt[:, 2:3] * o_w[rows, cols]
            o_ref[:, h * dh:(h + 1) * dh] = o_h.astype(o_ref.dtype)


def nsa_attention_sample(q, gates, row0, Bn, L, start, kct, vc, tbl, overlap, sel_pages, page_table, sel_new,
                         win_past, win_new):
    qb = L
    n_sel = -(-(start + L) // SEL_BLOCK)
    blk0 = row0 // qb
    row_map = lambda b, pt: (blk0 + b, 0)
    seq_map = lambda b, pt: (b, 0, 0)
    page_specs = [
        pl.BlockSpec((1, KV_ROW, PAGE_SIZE), functools.partial(lambda b, pt, j: (pt[b, j], 0, 0), j=j))
        for j in range(N_PAGES)
    ]
    return pl.pallas_call(
        functools.partial(_attn_sample_body, qb=qb, start=start, n_sel=n_sel),
        out_shape=jax.ShapeDtypeStruct((Bn * L, A_Q_DIM), f32),
        grid_spec=pltpu.PrefetchScalarGridSpec(
            num_scalar_prefetch=1,
            grid=(Bn,),
            in_specs=[
                pl.BlockSpec((qb, A_Q_DIM), row_map),
                pl.BlockSpec((qb, LANES), row_map),
                pl.BlockSpec((1, KV_HALF, N_CHUNKS), seq_map),
                pl.BlockSpec((1, N_CHUNKS, KV_HALF), seq_map),
                pl.BlockSpec((A_N_HEADS, LANES), lambda b, pt: (0, 0)),
                pl.BlockSpec((TILE, TILE), lambda b, pt: (0, 0)),
            ] + page_specs + [
                pl.BlockSpec((qb, KV_ROW), row_map),
                pl.BlockSpec((1, KV_ROW, WINDOW), seq_map),
                pl.BlockSpec((qb, KV_ROW), row_map),
            ],
            out_specs=pl.BlockSpec((qb, A_Q_DIM), lambda b, pt: (b, 0)),
            scratch_shapes=[pltpu.VMEM((TILE, (N_PAGES + 1 + WINDOW // TILE + 1) * TILE), f32)],
        ),
        compiler_params=_cparams(("parallel",)),
        name="nsa_attention_sample",
    )(page_table, q, gates, kct, vc, tbl, overlap, *([sel_pages] * N_PAGES), sel_new, win_past, win_new)


def _window_update_body(old_ref, new_ref, place_ref, o_ref, *, n_new):
    x = old_ref[0]
    shifted = pltpu.roll(x, WINDOW - n_new, 1)
    new_t = lax.dot_general(new_ref[...], place_ref[...], (((0,), (0,)), ((), ())),
                            preferred_element_type=f32, precision=_HI)
    lane = lax.broadcasted_iota(i32, (KV_ROW, TILE), 1)
    o_ref[0, :, 0:WINDOW - TILE] = shifted[:, 0:WINDOW - TILE]
    o_ref[0, :, WINDOW - TILE:WINDOW] = jnp.where(lane >= TILE - n_new, new_t, shifted[:, WINDOW - TILE:WINDOW])


def window_update(old_t, new_rows):
    Bn = old_t.shape[0]
    n_new = new_rows.shape[0] // Bn
    place = jnp.asarray(np.eye(n_new, TILE, k=TILE - n_new, dtype=np.float32))
    return pl.pallas_call(
        functools.partial(_window_update_body, n_new=n_new),
        out_shape=jax.ShapeDtypeStruct(old_t.shape, f32),
        grid=(Bn,),
        in_specs=[
            pl.BlockSpec((1, KV_ROW, WINDOW), lambda b: (b, 0, 0)),
            pl.BlockSpec((n_new, KV_ROW), lambda b: (b, 0)),
            pl.BlockSpec((n_new, TILE), lambda b: (0, 0)),
        ],
        out_specs=pl.BlockSpec((1, KV_ROW, WINDOW), lambda b: (b, 0, 0)),
        compiler_params=_cparams(("parallel",)),
        name="window_update",
    )(old_t, new_rows, place)


def _pad_lanes(v):
    return jnp.pad(v, (0, LANES - v.shape[0])).reshape(1, LANES)


def _overlap_matrix():
    n_cmp = N_CHUNKS - CMP_RATIO + 1
    c = np.arange(TILE)[:, None] * CMP_STRIDE
    s = np.arange(TILE)[None, :] * SEL_BLOCK
    ov = (c < s + SEL_BLOCK) & (c + CMP_LEN > s) & (np.arange(TILE)[:, None] < n_cmp)
    return jnp.asarray(ov.astype(np.float32))


def _feature_major(x):
    lead = x.shape[:-4]
    n = len(lead)
    return jnp.transpose(x, tuple(range(n)) + (n + 1, n + 2, n + 3, n)).reshape(lead + (KV_ROW, x.shape[-4]))


def _token_major(x_t):
    B, _, T = x_t.shape
    return jnp.transpose(x_t.reshape(B, 2, A_N_KV, A_HEAD_DIM, T), (0, 4, 1, 2, 3))


def kernel(x_prompt, x_sample, state_ssm, state_conv, cache_cmp_kv, cache_sel_kv, cache_win_kv, page_table, ln_g, ln_b, m_in_w, m_conv_w, m_conv_b, m_dt_bias, m_a_log, m_d, m_norm_w, m_out_w, kv_w, cmp_w1, cmp_pe, cmp_w2, q_w, o_w, rel_bias, mlp_w1, mlp_w2):
    Bp, Lp, D = x_prompt.shape
    Bs, Ls, _ = x_sample.shape
    NP, NS = Bp * Lp, Bs * Ls
    past_len = page_table.shape[1] * PAGE_SIZE
    assert past_len == N_PAGES * PAGE_SIZE and Lp == N_PAGES * PAGE_SIZE and cache_win_kv.shape[1] == WINDOW

    in_w = m_in_w[0].astype(bf16)
    z_w = in_w[:, :M_D_INNER]
    xbc_w = in_w[:, M_D_INNER:M_D_INNER + M_CONV_DIM]
    dt_w = jnp.pad(in_w[:, M_D_INNER + M_CONV_DIM:], ((0, 0), (0, LANES - M_N_HEADS)))
    kvw = kv_w.astype(bf16)
    qw = q_w[0].astype(bf16)
    gate_w = jnp.pad(qw[:, A_Q_DIM:], ((0, 0), (0, LANES - 3 * A_N_HEADS)))
    w1b = cmp_w1.astype(bf16)
    w_j = jnp.transpose(w1b, (0, 2, 3, 1, 4)).reshape(2, CMP_STRIDE, A_HEAD_DIM, CMP_RATIO * CMP_HIDDEN)
    zeros = jnp.zeros_like(w_j)
    wbd = jnp.concatenate([jnp.concatenate([w_j, zeros], axis=3), jnp.concatenate([zeros, w_j], axis=3)], axis=2)
    wbd = wbd.reshape(2, CMP_STRIDE // 2, 2 * LANES, 2 * CMP_RATIO * CMP_HIDDEN)
    pe_rows = jnp.broadcast_to(cmp_pe.astype(bf16).reshape(2, 1, CMP_LEN * A_HEAD_DIM), (2, SUBLANES, CMP_LEN * A_HEAD_DIM))
    w1_flat = w1b.reshape(2, CMP_LEN * A_HEAD_DIM, CMP_HIDDEN)
    cmp_w = (wbd, pe_rows, w1_flat, cmp_w2[1].astype(bf16), cmp_w2[0].T.astype(bf16))

    x = jnp.concatenate([x_prompt.reshape(NP, D), x_sample.reshape(NS, D)], axis=0)
    xb = x.astype(bf16)
    z = matmul(xb, z_w, f32)
    xbc = matmul(xb, xbc_w, f32)
    dt = matmul(xb, dt_w, f32)
    ssd_w = (m_conv_w[0], m_conv_b[0].reshape(1, -1), _pad_lanes(m_dt_bias[0]), _pad_lanes(m_a_log[0]),
             jnp.repeat(m_d[0], M_HEAD_DIM).reshape(1, -1), m_norm_w[0].reshape(1, -1))
    y_p, p_ssm, p_conv = ssd_mixer_core(xbc, z, dt, 0, Bp, Lp, None, None, *ssd_w)
    y_s, s_ssm, s_conv = ssd_mixer_core(xbc, z, dt, NP, Bs, Ls, state_conv[0], state_ssm[0], *ssd_w)
    h_f, h_b = matmul_residual_ln(y_p, y_s, m_out_w[0].astype(bf16), x,
                                  ln_g[0, 0].reshape(1, D), ln_b[0, 0].reshape(1, D))
    h_f, h_b = mlp_residual_ln(h_b, h_f, mlp_w1[0].astype(bf16), mlp_w2[0].astype(bf16),
                               ln_g[0, 1].reshape(1, D), ln_b[0, 1].reshape(1, D))

    cmp_t, sel_t, win_t, sel_pg, win_pg = kv_project_feature_major(h_b, kvw.T, Bp, Lp)
    kv_s = matmul(h_b[NP:], kvw, f32)
    cmp_s, sel_s, win_s = kv_s[:, 0:KV_ROW], kv_s[:, KV_ROW:2 * KV_ROW], kv_s[:, 2 * KV_ROW:3 * KV_ROW]
    kct_p, vc_p = compress_kv(cmp_t, None, *cmp_w)
    kct_s, vc_s = compress_kv(_feature_major(cache_cmp_kv), page_table, *cmp_w)

    q = matmul(h_b, qw[:, :A_Q_DIM], f32)
    gates = matmul(h_b, gate_w, f32, act="sigmoid")
    tbl, tz0, tz1 = bias_table(rel_bias.T)
    overlap = _overlap_matrix()
    o_p = nsa_attention_prompt(q, gates, Bp, Lp, kct_p, vc_p, tbl, tz0, tz1, overlap.T, sel_pg, win_pg)
    win_cache_t = _feature_major(cache_win_kv)
    o_s = nsa_attention_sample(q[NP:], gates[NP:], 0, Bs, Ls, past_len, kct_s, vc_s, tbl, overlap,
                               _feature_major(cache_sel_kv), page_table, sel_s, win_cache_t, win_s)
    h_f, h_b = matmul_residual_ln(o_p, o_s, o_w[0].astype(bf16), h_f,
                                  ln_g[1, 0].reshape(1, D), ln_b[1, 0].reshape(1, D))
    out_p, out_s = mlp_residual_ln(h_b, h_f, mlp_w1[1].astype(bf16), mlp_w2[1].astype(bf16),
                                   ln_g[1, 1].reshape(1, D), ln_b[1, 1].reshape(1, D), split_rows=NP)

    kv_shape = (2, A_N_KV, A_HEAD_DIM)
    n_keep = min(WINDOW, Lp)
    s_win = _token_major(window_update(win_cache_t, win_s))
    return (
        out_p.reshape(Bp, Lp, D), out_s.reshape(Bs, Ls, D),
        p_ssm[None], p_conv[None],
        _token_major(cmp_t), _token_major(sel_t), _token_major(win_t[:, :, Lp - n_keep:]),
        s_ssm[None], s_conv[None],
        cmp_s.reshape((Bs, Ls) + kv_shape), sel_s.reshape((Bs, Ls) + kv_shape), s_win,
    )
```

```python
import functools
import math

import jax
import jax.numpy as jnp
import numpy as np
from jax import lax
from jax.experimental import pallas as pl
from jax.experimental.pallas import tpu as pltpu

f32 = jnp.float32
bf16 = jnp.bfloat16
i32 = jnp.int32

D_MODEL = 1024
DEPTH = 2
DN_ALPHA = (2.0 * DEPTH) ** 0.25
LN_EPS = 1e-5
RMS_EPS = 1e-5
D_FF = 4 * D_MODEL
M_D_INNER = 2 * D_MODEL
M_HEAD_DIM = 64
M_N_HEADS = M_D_INNER // M_HEAD_DIM
M_N_GROUPS = 4
M_HPG = M_N_HEADS // M_N_GROUPS
M_D_STATE = 128
M_CONV = 4
M_CHUNK = 128
M_CONV_DIM = M_D_INNER + 2 * M_N_GROUPS * M_D_STATE
A_HEAD_DIM = 64
A_N_HEADS = D_MODEL // A_HEAD_DIM
A_N_KV = 4
A_GROUP = A_N_HEADS // A_N_KV
A_Q_DIM = A_N_HEADS * A_HEAD_DIM
KV_HALF = A_N_KV * A_HEAD_DIM
KV_ROW = 2 * KV_HALF
CMP_LEN = 32
CMP_STRIDE = 16
CMP_RATIO = CMP_LEN // CMP_STRIDE
CMP_HIDDEN = 2 * A_HEAD_DIM
SEL_BLOCK = 64
SEL_TOPN = 16
WINDOW = 512
Q_BLOCK = 128
N_BUCKETS = 32
MAX_DISTANCE = 128
MASK_VALUE = -1e30
FORCE_SCORE = 1e3
PAGE_SIZE = 128

LANES = 128
SUBLANES = 8
VMEM_LIMIT = 56 * 1024 * 1024

_HI = lax.Precision.HIGHEST


def _cparams(sem):
    return pltpu.CompilerParams(dimension_semantics=sem, vmem_limit_bytes=VMEM_LIMIT)


def _dot(a, b):
    return jnp.dot(a, b, preferred_element_type=f32)


def _dot_nt(a, b):
    return lax.dot_general(a, b, (((1,), (1,)), ((), ())), preferred_element_type=f32)


def _dot_tn(a, b):
    return lax.dot_general(a, b, (((0,), (0,)), ((), ())), preferred_element_type=f32)


def _silu(x):
    return x * (1.0 / (1.0 + jnp.exp(-x)))


def _layer_norm(x, g, b):
    mu = jnp.mean(x, axis=-1, keepdims=True)
    xc = x - mu
    var = jnp.mean(xc * xc, axis=-1, keepdims=True)
    return xc * lax.rsqrt(var + LN_EPS) * g + b


def _mm_body(x_ref, w_ref, o_ref, *, act):
    y = _dot(x_ref[...], w_ref[...])
    if act == "sigmoid":
        y = 1.0 / (1.0 + jnp.exp(-y))
    o_ref[...] = y.astype(o_ref.dtype)


def matmul(x, w, out_dtype, act=None, tm=1024, tn=1024):
    M, K = x.shape
    N = w.shape[1]
    tn = next(t for t in (tn, 512, 256, LANES) if N % t == 0)
    return pl.pallas_call(
        functools.partial(_mm_body, act=act),
        out_shape=jax.ShapeDtypeStruct((M, N), out_dtype),
        grid=(M // tm, N // tn),
        in_specs=[pl.BlockSpec((tm, K), lambda i, j: (i, 0)), pl.BlockSpec((K, tn), lambda i, j: (0, j))],
        out_specs=pl.BlockSpec((tm, tn), lambda i, j: (i, j)),
        compiler_params=_cparams(("parallel", "parallel")),
        name="matmul",
    )(x, w)


def _mm_res_ln_body(x0_ref, x1_ref, w_ref, r_ref, g_ref, b_ref, of_ref, ob_ref, *, n_head_blocks):
    i = pl.program_id(0)
    w = w_ref[...]
    acc = lax.cond(i < n_head_blocks, lambda: _dot(x0_ref[...], w), lambda: _dot(x1_ref[...].astype(bf16), w))
    h = _layer_norm(DN_ALPHA * r_ref[...] + acc, g_ref[...], b_ref[...])
    of_ref[...] = h
    ob_ref[...] = h.astype(bf16)


def matmul_residual_ln(x_head, x_tail, w, resid, g, b, tm=512):
    M0, K = x_head.shape
    M1 = x_tail.shape[0]
    N = w.shape[1]
    assert M0 % tm == 0 and M1 % tm == 0
    n_head = M0 // tm
    row_map = lambda i: (i, 0)
    const = lambda i: (0, 0)
    return pl.pallas_call(
        functools.partial(_mm_res_ln_body, n_head_blocks=n_head),
        out_shape=(jax.ShapeDtypeStruct((M0 + M1, N), f32), jax.ShapeDtypeStruct((M0 + M1, N), bf16)),
        grid=((M0 + M1) // tm,),
        in_specs=[
            pl.BlockSpec((tm, K), lambda i: (jnp.minimum(i, n_head - 1), 0)),
            pl.BlockSpec((tm, K), lambda i: (jnp.maximum(i - n_head, 0), 0)),
            pl.BlockSpec((K, N), const),
            pl.BlockSpec((tm, N), row_map),
            pl.BlockSpec((1, N), const),
            pl.BlockSpec((1, N), const),
        ],
        out_specs=(pl.BlockSpec((tm, N), row_map), pl.BlockSpec((tm, N), row_map)),
        compiler_params=_cparams(("arbitrary",)),
        name="matmul_residual_ln",
    )(x_head, x_tail, w, resid, g, b)


def _mlp_body(hb_ref, hf_ref, w1_ref, w2_ref, g_ref, b_ref, o0_ref, o1_ref, acc_ref, *, n_head_blocks):
    i = pl.program_id(0)
    j = pl.program_id(1)

    @pl.when(j == 0)
    def _():
        acc_ref[...] = jnp.zeros_like(acc_ref)

    u = jnp.maximum(_dot(hb_ref[...], w1_ref[...]), 0.0)
    acc_ref[...] += _dot((u * u).astype(bf16), w2_ref[...])
    last = j == pl.num_programs(1) - 1

    def result():
        return _layer_norm(DN_ALPHA * hf_ref[...] + acc_ref[...], g_ref[...], b_ref[...])

    if n_head_blocks is None:
        @pl.when(last)
        def _():
            h = result()
            o0_ref[...] = h
            o1_ref[...] = h.astype(bf16)
    else:
        @pl.when(last & (i < n_head_blocks))
        def _():
            o0_ref[...] = result()

        @pl.when(last & (i >= n_head_blocks))
        def _():
            o1_ref[...] = result()


def mlp_residual_ln(hb, hf, w1, w2, g, b, split_rows=None, tm=1024, tf=1024):
    M, D = hb.shape
    F = w1.shape[1]
    row_map = lambda i, j: (i, 0)
    if split_rows is None:
        n_head = None
        out_shape = (jax.ShapeDtypeStruct((M, D), f32), jax.ShapeDtypeStruct((M, D), bf16))
        out_specs = (pl.BlockSpec((tm, D), row_map), pl.BlockSpec((tm, D), row_map))
    else:
        assert split_rows % tm == 0 and (M - split_rows) % tm == 0
        n_head = split_rows // tm
        out_shape = (jax.ShapeDtypeStruct((split_rows, D), f32), jax.ShapeDtypeStruct((M - split_rows, D), f32))
        out_specs = (pl.BlockSpec((tm, D), lambda i, j: (jnp.minimum(i, n_head - 1), 0)),
                     pl.BlockSpec((tm, D), lambda i, j: (jnp.maximum(i - n_head, 0), 0)))
    return pl.pallas_call(
        functools.partial(_mlp_body, n_head_blocks=n_head),
        out_shape=out_shape,
        grid=(M // tm, F // tf),
        in_specs=[
            pl.BlockSpec((tm, D), row_map),
            pl.BlockSpec((tm, D), row_map),
            pl.BlockSpec((D, tf), lambda i, j: (0, j)),
            pl.BlockSpec((tf, D), lambda i, j: (j, 0)),
            pl.BlockSpec((1, D), lambda i, j: (0, 0)),
            pl.BlockSpec((1, D), lambda i, j: (0, 0)),
        ],
        out_specs=out_specs,
        scratch_shapes=[pltpu.VMEM((tm, D), f32)],
        compiler_params=_cparams(("arbitrary", "arbitrary")),
        name="mlp_residual_ln",
    )(hb, hf, w1, w2, g, b)


def _kv_project_body(wt_ref, h_ref, cmp_ref, sel_ref, win_ref, selp_ref, winp_ref):
    res = _dot_nt(wt_ref[...], h_ref[...])
    tm = h_ref.shape[0]
    cmp_ref[0] = res[0:KV_ROW]
    sel_ref[0] = res[KV_ROW:2 * KV_ROW]
    win_ref[0] = res[2 * KV_ROW:3 * KV_ROW]
    for k in range(tm // PAGE_SIZE):
        cols = slice(k * PAGE_SIZE, (k + 1) * PAGE_SIZE)
        selp_ref[0, k] = res[KV_ROW:2 * KV_ROW, cols].astype(bf16)
        winp_ref[0, k] = res[2 * KV_ROW:3 * KV_ROW, cols].astype(bf16)


def kv_project_feature_major(h_b, w_t, Bn, L, tm=512):
    nj = L // tm
    pages_per_step = tm // PAGE_SIZE
    fm = jax.ShapeDtypeStruct((Bn, KV_ROW, L), f32)
    pg = jax.ShapeDtypeStruct((Bn, L // PAGE_SIZE, KV_ROW, PAGE_SIZE), bf16)
    fm_spec = pl.BlockSpec((1, KV_ROW, tm), lambda b, j: (b, 0, j))
    pg_spec = pl.BlockSpec((1, pages_per_step, KV_ROW, PAGE_SIZE), lambda b, j: (b, j, 0, 0))
    return pl.pallas_call(
        _kv_project_body,
        out_shape=(fm, fm, fm, pg, pg),
        grid=(Bn, nj),
        in_specs=[
            pl.BlockSpec(w_t.shape, lambda b, j: (0, 0)),
            pl.BlockSpec((tm, h_b.shape[1]), lambda b, j: (b * nj + j, 0)),
        ],
        out_specs=(fm_spec, fm_spec, fm_spec, pg_spec, pg_spec),
        compiler_params=_cparams(("parallel", "parallel")),
        name="kv_project_feature_major",
    )(w_t, h_b)


SSD_SEQS_PER_STEP = 4
SSD_SEQS_PER_STEP_CHUNKED = 2


def _ssd_body(*refs, Q, BB, nc, has_init):
    n_in = 1 if nc == 1 else BB
    xbc_refs, z_refs, dt_refs = refs[0:n_in], refs[n_in:2 * n_in], refs[2 * n_in:3 * n_in]
    refs = refs[3 * n_in:]
    if has_init:
        conv0_ref, h0_ref = refs[0:2]
        refs = refs[2:]
    (cw_ref, cb_ref, dtb_ref, alog_ref, dskx_ref, nw_ref, e64_ref, eq_ref,
     y_ref, hout_ref, cout_ref, st_ref, xpad_ref, xc_ref, ybuf_ref, xdt_ref, xds_ref, ecx_ref, acx_ref) = refs

    def rows_of(row_refs, bb, cols=slice(None)):
        return row_refs[0][bb * Q:(bb + 1) * Q, cols] if nc == 1 else row_refs[bb][:, cols]

    def store_y(bb, cols, value):
        if nc == 1:
            y_ref[bb * Q:(bb + 1) * Q, cols] = value
        else:
            y_ref[bb, :, cols] = value

    c = pl.program_id(1)
    single_chunk = nc == 1
    P, N, R, G = M_HEAD_DIM, M_D_STATE, M_HPG, M_N_GROUPS
    PAD = SUBLANES

    per_head_state = single_chunk and has_init

    def first_chunk():
        for bb in range(BB):
            xpad_ref[bb, 0:PAD, :] = jnp.zeros((PAD, M_CONV_DIM), f32)
            if has_init:
                xpad_ref[bb, PAD - (M_CONV - 1):PAD, :] = conv0_ref[bb]
            if per_head_state:
                continue
            if has_init:
                for g in range(G):
                    for r in range(R):
                        st_ref[bb, g, :, r * P:(r + 1) * P] = h0_ref[bb, g * R + r].T
            else:
                st_ref[bb] = jnp.zeros(st_ref.shape[1:], f32)

    def last_chunk():
        if per_head_state:
            return
        for bb in range(BB):
            for g in range(G):
                for r in range(R):
                    hout_ref[bb, g * R + r] = st_ref[bb, g, :, r * P:(r + 1) * P].T

    if single_chunk:
        first_chunk()
    else:
        pl.when(c == 0)(first_chunk)

    ri = lax.broadcasted_iota(i32, (Q, Q), 0)
    ci = lax.broadcasted_iota(i32, (Q, Q), 1)
    tril = ri >= ci
    GW = M_D_INNER // G
    lane = lax.broadcasted_iota(i32, (Q, LANES), 1)
    third = M_N_HEADS
    assert 3 * third <= LANES

    def spread(v, e_ref):
        hi = v.astype(bf16).astype(f32)
        r1 = v - hi
        mid = r1.astype(bf16).astype(f32)
        lo = r1 - mid
        packed = jnp.where(lane < third, hi, jnp.where(lane < 2 * third, pltpu.roll(mid, third, 1),
                                                       jnp.where(lane < 3 * third, pltpu.roll(lo, 2 * third, 1), 0.0)))
        return _dot(packed.astype(bf16), e_ref[...])

    seqs = range(BB)
    a_cum, a_cum_t, e_last = [], [], []
    for bb in seqs:
        xpad_ref[bb, PAD:PAD + Q, :] = rows_of(xbc_refs, bb)
        acc = cb_ref[...] + xpad_ref[bb, pl.ds(PAD - 3, Q), :] * cw_ref[0:1, :]
        for k in range(1, M_CONV):
            acc = acc + xpad_ref[bb, pl.ds(PAD - 3 + k, Q), :] * cw_ref[k:k + 1, :]
        xc_ref[bb] = _silu(acc)
        cout_ref[bb] = xpad_ref[bb, pl.ds(Q + PAD - 3, 3), :]
        xpad_ref[bb, 0:PAD, :] = xpad_ref[bb, pl.ds(Q, PAD), :]

        xdt = rows_of(dt_refs, bb) + dtb_ref[...]
        dt = jnp.maximum(xdt, 0.0) + jnp.log1p(jnp.exp(-jnp.abs(xdt)))
        a = dt * (-jnp.exp(alog_ref[...]))
        a_cum.append(jnp.dot(tril.astype(f32), a, preferred_element_type=f32, precision=_HI))
        a_cum_t.append(a_cum[bb].T)
        a_last = a_cum[bb][Q - 1:Q, :]
        e_last.append(jnp.exp(a_last))
        xdt_all = xc_ref[bb, :, 0:M_D_INNER] * spread(dt, e64_ref)
        xdt_ref[bb] = xdt_all
        xds_ref[bb] = xdt_all * spread(jnp.exp(a_last - a_cum[bb]), e64_ref)
        ecx_ref[bb] = spread(jnp.exp(a_cum[bb]), e64_ref)
        if Q == LANES:
            acx_ref[bb] = spread(a_cum[bb], eq_ref)

    for g in range(G):
        gcols = slice(g * GW, (g + 1) * GW)
        bg = [xc_ref[bb, :, M_D_INNER + g * N:M_D_INNER + (g + 1) * N].astype(bf16) for bb in seqs]
        cg = [xc_ref[bb, :, M_D_INNER + G * N + g * N:M_D_INNER + G * N + (g + 1) * N].astype(bf16) for bb in seqs]
        gmat = [_dot_nt(cg[bb], bg[bb]) for bb in seqs]
        for r in range(R):
            h = g * R + r
            hcols = slice(h * P, (h + 1) * P)
            for bb in seqs:
                col = acx_ref[bb, :, h * Q:(h + 1) * Q] if Q == LANES else a_cum[bb][:, h:h + 1]
                row = a_cum_t[bb][h:h + 1, :]
                lmat = jnp.exp(jnp.where(tril, col - row, -jnp.inf))
                ydiag = _dot((gmat[bb] * lmat).astype(bf16), xdt_ref[bb, :, hcols].astype(bf16))
                if per_head_state:
                    h_in = h0_ref[bb, h]
                    hout_ref[bb, h] = (h_in * e_last[bb][:, h:h + 1]
                                       + _dot_tn(xds_ref[bb, :, hcols].astype(bf16), bg[bb]))
                    ydiag = ydiag + _dot_nt(cg[bb], h_in.astype(bf16)) * ecx_ref[bb, :, hcols]
                ybuf_ref[bb, :, hcols] = ydiag
        for bb in seqs:
            extra = xc_ref[bb, :, gcols] * dskx_ref[:, gcols]
            if not per_head_state:
                extra = extra + _dot(cg[bb], st_ref[bb, g].astype(bf16)) * ecx_ref[bb, :, gcols]
                new = _dot_tn(bg[bb], xds_ref[bb, :, gcols].astype(bf16))
                st_ref[bb, g] = st_ref[bb, g] * ecx_ref[bb, Q - 1:Q, gcols] + new
            ybuf_ref[bb, :, gcols] += extra

    for g in range(G):
        cols = slice(g * GW, (g + 1) * GW)
        for bb in seqs:
            yg = ybuf_ref[bb, :, cols] * _silu(rows_of(z_refs, bb, cols))
            ms = jnp.mean(yg * yg, axis=-1, keepdims=True)
            store_y(bb, cols, (yg * lax.rsqrt(ms + RMS_EPS) * nw_ref[:, cols]).astype(y_ref.dtype))

    if single_chunk:
        last_chunk()
    else:
        pl.when(c == pl.num_programs(1) - 1)(last_chunk)


def _spread_matrix(width):
    rows = np.arange(LANES)[:, None]
    cols = np.arange(M_N_HEADS * width)[None, :]
    return jnp.asarray((rows < 3 * M_N_HEADS) & (rows % M_N_HEADS == cols // width), bf16)


def ssd_mixer_core(xbc, z, dt, row0, Bn, L, conv0, h0, conv_w, conv_b, dt_bias, a_log, d_skip_x, norm_w):
    Q = M_CHUNK if L % M_CHUNK == 0 else L
    nc = L // Q
    has_init = h0 is not None
    BB = next(n for n in ((SSD_SEQS_PER_STEP if nc == 1 else SSD_SEQS_PER_STEP_CHUNKED), 1) if Bn % n == 0)
    const2 = lambda b, c: (0, 0)
    if nc == 1:
        blk0 = row0 // (BB * Q)
        row_maps = [lambda b, c: (blk0 + b, 0)]
        y_shape, y_block, y_map = (Bn * L, M_D_INNER), (BB * Q, M_D_INNER), (lambda b, c: (b, 0))
    else:
        blk0 = row0 // Q
        row_maps = [functools.partial(lambda b, c, bb: (blk0 + (b * BB + bb) * nc + c, 0), bb=bb) for bb in range(BB)]
        y_shape, y_block, y_map = (Bn, L, M_D_INNER), (BB, Q, M_D_INNER), (lambda b, c: (b, c, 0))
    rows_per_block = BB * Q if nc == 1 else Q
    in_specs, args = [], []
    for arr, width in ((xbc, M_CONV_DIM), (z, M_D_INNER), (dt, LANES)):
        in_specs += [pl.BlockSpec((rows_per_block, width), m) for m in row_maps]
        args += [arr] * len(row_maps)
    if has_init:
        in_specs += [
            pl.BlockSpec((BB, M_CONV - 1, M_CONV_DIM), lambda b, c: (b, 0, 0)),
            pl.BlockSpec((BB, M_N_HEADS, M_HEAD_DIM, M_D_STATE), lambda b, c: (b, 0, 0, 0)),
        ]
        args += [conv0, h0]
    e64 = _spread_matrix(M_HEAD_DIM)
    eq = _spread_matrix(Q if Q == LANES else SUBLANES)
    in_specs += [
        pl.BlockSpec((M_CONV, M_CONV_DIM), const2),
        pl.BlockSpec((1, M_CONV_DIM), const2),
        pl.BlockSpec((1, LANES), const2),
        pl.BlockSpec((1, LANES), const2),
        pl.BlockSpec((1, M_D_INNER), const2),
        pl.BlockSpec((1, M_D_INNER), const2),
        pl.BlockSpec(e64.shape, const2),
        pl.BlockSpec(eq.shape, const2),
    ]
    args += [conv_w, conv_b, dt_bias, a_log, d_skip_x, norm_w, e64, eq]
    y_dtype = bf16 if y_block[-2] % 16 == 0 else f32
    y, new_ssm, new_conv = pl.pallas_call(
        functools.partial(_ssd_body, Q=Q, BB=BB, nc=nc, has_init=has_init),
        out_shape=(
            jax.ShapeDtypeStruct(y_shape, y_dtype),
            jax.ShapeDtypeStruct((Bn, M_N_HEADS, M_HEAD_DIM, M_D_STATE), f32),
            jax.ShapeDtypeStruct((Bn, M_CONV - 1, M_CONV_DIM), f32),
        ),
        grid=(Bn // BB, nc),
        in_specs=in_specs,
        out_specs=(
            pl.BlockSpec(y_block, y_map),
            pl.BlockSpec((BB, M_N_HEADS, M_HEAD_DIM, M_D_STATE), lambda b, c: (b, 0, 0, 0)),
            pl.BlockSpec((BB, M_CONV - 1, M_CONV_DIM), lambda b, c: (b, 0, 0)),
        ),
        scratch_shapes=[
            pltpu.VMEM((1, 1, SUBLANES, LANES) if nc == 1 and has_init
                       else (BB, M_N_GROUPS, M_D_STATE, M_HPG * M_HEAD_DIM), f32),
            pltpu.VMEM((BB, Q + SUBLANES, M_CONV_DIM), f32),
            pltpu.VMEM((BB, Q, M_CONV_DIM), f32),
            pltpu.VMEM((BB, Q, M_D_INNER), f32),
            pltpu.VMEM((BB, Q, M_D_INNER), f32),
            pltpu.VMEM((BB, Q, M_D_INNER), f32),
            pltpu.VMEM((BB, Q, M_D_INNER), f32),
            pltpu.VMEM((BB, Q, M_N_HEADS * Q) if Q == LANES else (1, SUBLANES, LANES), f32),
        ],
        compiler_params=_cparams(("parallel", "arbitrary")),
        name="ssd_mixer_core",
    )(*args)
    return y.reshape(Bn * L, M_D_INNER), new_ssm, new_conv


N_PAGES = 16
N_CHUNKS = N_PAGES * PAGE_SIZE // CMP_STRIDE
N_SLABS = KV_ROW // LANES


def _compress_body(*refs, paged):
    if paged:
        refs = refs[1:]
    pages = refs[:N_PAGES]
    wbd_ref, pe_ref, w1f_ref, w2_ref, w2t_ref, kct_ref, vc_ref, xs_ref = refs[N_PAGES:]
    H = CMP_HIDDEN
    n_steps = CMP_STRIDE // 2
    pages_per_step = N_PAGES // n_steps

    def to_token_major(sl, page_range):
        for p in page_range:
            xs_ref[sl, p * PAGE_SIZE:(p + 1) * PAGE_SIZE, :] = pages[p][0, sl * LANES:(sl + 1) * LANES, :].T

    to_token_major(0, range(N_PAGES))
    row = lax.broadcasted_iota(i32, (N_CHUNKS, H), 0)
    for kv in range(2):
        pe_term = _dot(pe_ref[kv], w1f_ref[kv])[0:1, :]
        for gp in range(A_N_KV // 2):
            sl = kv * (A_N_KV // 2) + gp
            acc = jnp.zeros((N_CHUNKS, 2 * CMP_RATIO * H), f32)
            for jp in range(n_steps):
                x = jnp.concatenate(
                    [xs_ref[sl, pl.ds(2 * jp + k, N_CHUNKS, stride=CMP_STRIDE), :] for k in range(2)], axis=1).astype(bf16)
                acc = acc + _dot(x, wbd_ref[kv, jp])
                if sl + 1 < N_SLABS:
                    to_token_major(sl + 1, range(jp * pages_per_step, (jp + 1) * pages_per_step))
            for gi in range(2):
                g = gp * 2 + gi
                p0 = acc[:, gi * CMP_RATIO * H:gi * CMP_RATIO * H + H]
                p1 = acc[:, gi * CMP_RATIO * H + H:(gi + 1) * CMP_RATIO * H]
                p1_next = jnp.where(row == N_CHUNKS - 1, 0.0, pltpu.roll(p1, N_CHUNKS - 1, 0))
                hid = _silu(p0 + p1_next + pe_term).astype(bf16)
                if kv == 0:
                    kct_ref[0, g * A_HEAD_DIM:(g + 1) * A_HEAD_DIM, :] = _dot_nt(w2t_ref[...], hid).astype(kct_ref.dtype)
                else:
                    vc_ref[0, :, g * A_HEAD_DIM:(g + 1) * A_HEAD_DIM] = _dot(hid, w2_ref[...]).astype(vc_ref.dtype)


def compress_kv(pages, page_table, wbd, pe_rows, w1_flat, w2_v, w2t_k):
    paged = page_table is not None
    Bn = page_table.shape[0] if paged else pages.shape[0]
    if paged:
        page_specs = [
            pl.BlockSpec((1, KV_ROW, PAGE_SIZE), functools.partial(lambda b, pt, j: (pt[b, j], 0, 0), j=j))
            for j in range(N_PAGES)
        ]
        const = lambda nd: (lambda b, pt: (0,) * nd)
        out_map = lambda b, pt: (b, 0, 0)
    else:
        page_specs = [
            pl.BlockSpec((1, KV_ROW, PAGE_SIZE), functools.partial(lambda b, j: (b, 0, j), j=j)) for j in range(N_PAGES)
        ]
        const = lambda nd: (lambda b: (0,) * nd)
        out_map = lambda b: (b, 0, 0)
    in_specs = page_specs + [
        pl.BlockSpec(wbd.shape, const(4)),
        pl.BlockSpec(pe_rows.shape, const(3)),
        pl.BlockSpec(w1_flat.shape, const(3)),
        pl.BlockSpec(w2_v.shape, const(2)),
        pl.BlockSpec(w2t_k.shape, const(2)),
    ]
    out_shape = (jax.ShapeDtypeStruct((Bn, KV_HALF, N_CHUNKS), bf16), jax.ShapeDtypeStruct((Bn, N_CHUNKS, KV_HALF), bf16))
    out_specs = (pl.BlockSpec((1, KV_HALF, N_CHUNKS), out_map), pl.BlockSpec((1, N_CHUNKS, KV_HALF), out_map))
    scratch = [pltpu.VMEM((N_SLABS, N_PAGES * PAGE_SIZE, LANES), f32)]
    body = functools.partial(_compress_body, paged=paged)
    args = ([pages] * N_PAGES) + [wbd, pe_rows, w1_flat, w2_v, w2t_k]
    if paged:
        return pl.pallas_call(
            body, out_shape=out_shape,
            grid_spec=pltpu.PrefetchScalarGridSpec(
                num_scalar_prefetch=1, grid=(Bn,), in_specs=in_specs, out_specs=out_specs, scratch_shapes=scratch),
            compiler_params=_cparams(("parallel",)), name="compress_kv_paged",
        )(page_table, *args)
    return pl.pallas_call(
        body, out_shape=out_shape, grid=(Bn,), in_specs=in_specs, out_specs=out_specs, scratch_shapes=scratch,
        compiler_params=_cparams(("parallel",)), name="compress_kv",
    )(*args)


def _bias_table_body(rb_ref, tbl_ref, tz0_ref, tz1_ref):
    n = lax.broadcasted_iota(i32, (A_N_HEADS, LANES), 1)
    max_exact = N_BUCKETS // 2
    large = max_exact + jnp.floor(jnp.log(jnp.maximum(n, max_exact).astype(f32) / max_exact)
                                  / math.log(MAX_DISTANCE / max_exact) * (N_BUCKETS - max_exact))
    bucket = jnp.where(n < max_exact, n.astype(f32), jnp.minimum(large, N_BUCKETS - 1.0))
    tbl = jnp.zeros((A_N_HEADS, LANES), f32)
    for b in range(N_BUCKETS):
        tbl = jnp.where(bucket == float(b), rb_ref[:, b:b + 1], tbl)
    tbl_ref[...] = tbl
    dist = lax.broadcasted_iota(i32, (LANES, LANES), 0) - lax.broadcasted_iota(i32, (LANES, LANES), 1)
    for h in range(A_N_HEADS):
        row = jnp.broadcast_to(tbl[h:h + 1, :], (LANES, LANES))
        own = jnp.take_along_axis(row, jnp.clip(dist, 0, MAX_DISTANCE - 1), axis=1)
        tz0_ref[h] = jnp.where(dist >= 0, own, MASK_VALUE)
        tz1_ref[h] = jnp.take_along_axis(row, jnp.minimum(dist + LANES, MAX_DISTANCE - 1), axis=1)


def bias_table(rel_bias_t):
    assert MAX_DISTANCE <= LANES
    tile = jax.ShapeDtypeStruct((A_N_HEADS, LANES, LANES), f32)
    return pl.pallas_call(
        _bias_table_body, out_shape=(jax.ShapeDtypeStruct((A_N_HEADS, LANES), f32), tile, tile), name="bias_table",
    )(rel_bias_t)


TILE = 128
WIN_TILES = WINDOW // TILE


WIDE = 2 * TILE
N_SLOTS = N_PAGES * TILE // WIDE
BIG = -MASK_VALUE


SEL, WIN = 0, 1
FAR, NEAR = 0, 1
WIN_SLOT0 = N_SLOTS


def _attn_prompt_body(q_ref, gate_ref, kct_ref, vc_ref, tbl_ref, tz0_ref, tz1_ref, ovlt_ref, selp_ref, winp_ref, o_ref,
                      s_scr, m_scr, l_scr, acc_scr, lhs_scr, oc_scr, fb_scr, *, qb, n_sel):
    i = pl.program_id(1)
    R = A_GROUP * qb
    dh = A_HEAD_DIM
    pos0 = i * qb
    odd = (i % 2) == 1
    td = i // 2
    scale = dh ** -0.5
    n_rank = SUBLANES * (-(-n_sel // SUBLANES))

    lane = lax.broadcasted_iota(i32, (R, TILE), 1)
    q_in_blk = jnp.concatenate([lax.broadcasted_iota(i32, (qb, TILE), 0)] * A_GROUP, axis=0)
    qpos = pos0 + q_in_blk
    neg_tile = jnp.full((R, TILE), MASK_VALUE, f32)

    s_idx = lax.broadcasted_iota(i32, (n_rank, qb), 0)
    s_qpos = pos0 + lax.broadcasted_iota(i32, (n_rank, qb), 1)
    blk = s_qpos // SEL_BLOCK
    sel_valid = s_idx * SEL_BLOCK <= s_qpos
    sel_forced = (s_idx == 0) | (s_idx == blk) | (s_idx == blk - 1)

    f_row = lax.broadcasted_iota(i32, (TILE - dh, WIDE), 0)
    flag_rows = [jnp.where(f_row == k, BIG, 0.0).astype(bf16) for k in range(2)]
    zero_flag = jnp.zeros((TILE - dh, WIDE), bf16)
    zero_drop = jnp.zeros((TILE, WIDE), bf16)
    b_row = lax.broadcasted_iota(i32, (TILE, WIDE), 0)
    b_col = lax.broadcasted_iota(i32, (TILE, WIDE), 1)
    f_lane = lax.broadcasted_iota(i32, (R, TILE - dh), 1)
    flags = jnp.where(f_lane == 0, jnp.where(td < 1, -1.0, 0.0),
                      jnp.where(f_lane == 1, jnp.where(td < 2, -1.0, 0.0), 0.0)).astype(bf16)
    win_thr = q_in_blk + jnp.where(odd, TILE, 0)

    def pair(ref, T, g, half):
        rows = slice(half * KV_HALF + g * dh, half * KV_HALF + (g + 1) * dh)
        return jnp.concatenate([ref[0, 2 * T, rows, :], ref[0, 2 * T + 1, rows, :]], axis=1)

    def drop_rows(T):
        return jnp.where(b_row == T * (WIDE // SEL_BLOCK) + b_col // SEL_BLOCK, BIG, 0.0).astype(bf16)

    def store_scores(g, br, slot, s, left, right, kind, first=False):
        s_l = s[:, 0:TILE] if left is None else s[:, 0:TILE] + left
        s_r = s[:, TILE:WIDE] if right is None else s[:, TILE:WIDE] + right
        s_scr[g, slot, :, 0:TILE] = s_l
        s_scr[g, slot, :, TILE:WIDE] = s_r
        mx = jnp.maximum(s_l, s_r)
        m_scr[g, br, kind] = mx if first else jnp.maximum(m_scr[g, br, kind], mx)

    def accumulate(g, br, slot, kind, v_t, first=False):
        shift = m_scr[g, br, kind]
        p_l = jnp.exp(s_scr[g, slot, :, 0:TILE] - shift)
        p_r = jnp.exp(s_scr[g, slot, :, TILE:WIDE] - shift)
        pv = _dot_nt(jnp.concatenate([p_l, p_r], axis=1).astype(bf16), v_t)
        if first:
            l_scr[g, br] = p_l + p_r
            acc_scr[g, br] = pv
        else:
            l_scr[g, br] += p_l + p_r
            acc_scr[g, br] += pv

    def row_max(g, br):
        fb = fb_scr[g]
        m = jnp.max(jnp.maximum(m_scr[g, br, FAR] + fb, m_scr[g, br, NEAR]), axis=1, keepdims=True)
        m_b = jnp.broadcast_to(m, (R, TILE))
        m_scr[g, br, FAR] = m_b - fb
        m_scr[g, br, NEAR] = m_b

    def result(g, br):
        return acc_scr[g, br] / jnp.sum(l_scr[g, br], axis=1, keepdims=True)

    def near_tiles(g):
        tz0 = tz0_ref[g * A_GROUP:(g + 1) * A_GROUP].reshape(R, TILE)
        tz1 = tz1_ref[g * A_GROUP:(g + 1) * A_GROUP].reshape(R, TILE)
        return jnp.where(odd, fb_scr[g], tz1), jnp.where(odd, tz1, tz0), jnp.where(odd, tz0, neg_tile)

    t_prev = jnp.maximum(td - 1, 0)
    t_first = jnp.maximum(td - 2, 0)
    n_far = jnp.maximum(td - 1, 0)

    for g in range(A_N_KV):
        heads = [g * A_GROUP + r for r in range(A_GROUP)]
        qg = jnp.concatenate([q_ref[:, h * dh:(h + 1) * dh] for h in heads], axis=0)
        qg = (qg.astype(f32) * scale).astype(bf16)
        tbl_g = jnp.concatenate([jnp.broadcast_to(tbl_ref[h:h + 1, :], (qb, TILE)) for h in heads], axis=0)
        fb_scr[g] = jnp.concatenate(
            [jnp.broadcast_to(tbl_ref[h:h + 1, MAX_DISTANCE - 1:MAX_DISTANCE], (qb, TILE)) for h in heads], axis=0)

        dist_c = qpos - (lane * CMP_STRIDE + (CMP_LEN - 1))
        s_c = _dot(qg, kct_ref[0, g * dh:(g + 1) * dh, :])
        s_c = s_c + jnp.take_along_axis(tbl_g, jnp.clip(dist_c, 0, MAX_DISTANCE - 1), axis=1)
        s_c = jnp.where(dist_c >= 0, s_c, MASK_VALUE)
        m_c = jnp.max(s_c, axis=1, keepdims=True)
        p_c = jnp.where(dist_c >= 0, jnp.exp(s_c - m_c), 0.0)
        l_c = jnp.sum(p_c, axis=1, keepdims=True)
        l_c = jnp.where(l_c == 0.0, 1.0, l_c)
        oc_scr[g] = _dot(p_c.astype(bf16), vc_ref[0, :, g * dh:(g + 1) * dh]) / l_c
        p_c = p_c / l_c

        p_sum = p_c[0:qb]
        for r in range(1, A_GROUP):
            p_sum = p_sum + p_c[r * qb:(r + 1) * qb]
        imp_t = lax.dot_general(ovlt_ref[0:n_rank, :], p_sum, (((1,), (1,)), ((), ())),
                                preferred_element_type=f32, precision=_HI)
        score = jnp.where(sel_valid, imp_t + jnp.where(sel_forced, FORCE_SCORE, 0.0), -1.0)
        score = jnp.where(s_idx < n_sel, score, -3.0)
        rank = jnp.zeros((n_rank, qb), f32)
        for s2 in range(n_sel):
            other = score[s2:s2 + 1, :]
            rank = rank + jnp.where((other > score) | ((other == score) & (s_idx > s2)), 1.0, 0.0)
        dropped_t = jnp.where(rank < float(min(SEL_TOPN, n_sel)), 0.0, -1.0)
        dropped = jnp.concatenate([dropped_t, jnp.zeros((TILE - n_rank, qb), f32)], axis=0).T
        lhs_scr[g, SEL] = jnp.concatenate([qg, flags, jnp.concatenate([dropped] * A_GROUP, axis=0).astype(bf16)], axis=1)
        lhs_scr[g, WIN] = jnp.concatenate([qg, flags, jnp.zeros((R, TILE), bf16)], axis=1)
        m_scr[g, SEL, FAR] = neg_tile

    too_old_l = jnp.where(lane > win_thr, 0.0, MASK_VALUE)
    too_old_r = jnp.where(lane + TILE > win_thr, 0.0, MASK_VALUE)
    for g in range(A_N_KV):
        prev_right, diag_left, diag_right = near_tiles(g)
        lhs_win = lhs_scr[g, WIN]
        s = _dot(lhs_win, jnp.concatenate([pair(winp_ref, t_first, g, 0), flag_rows[1], zero_drop], axis=0))
        store_scores(g, WIN, WIN_SLOT0, s, too_old_l, too_old_r, FAR, first=True)
        s = _dot(lhs_win, jnp.concatenate([pair(winp_ref, t_prev, g, 0), flag_rows[0], zero_drop], axis=0))
        store_scores(g, WIN, WIN_SLOT0 + 1, s, fb_scr[g], prev_right, NEAR, first=True)
        s = _dot(lhs_win, jnp.concatenate([pair(winp_ref, td, g, 0), zero_flag, zero_drop], axis=0))
        store_scores(g, WIN, WIN_SLOT0 + 2, s, diag_left, diag_right, NEAR)
        lhs_sel = lhs_scr[g, SEL]
        s = _dot(lhs_sel, jnp.concatenate([pair(selp_ref, t_prev, g, 0), flag_rows[0], drop_rows(t_prev)], axis=0))
        store_scores(g, SEL, N_SLOTS - 2, s, fb_scr[g], prev_right, NEAR, first=True)
        s = _dot(lhs_sel, jnp.concatenate([pair(selp_ref, td, g, 0), zero_flag, drop_rows(td)], axis=0))
        store_scores(g, SEL, N_SLOTS - 1, s, diag_left, diag_right, NEAR)

    def far_scores(T, carry):
        drop = drop_rows(T)
        for g in range(A_N_KV):
            s = _dot(lhs_scr[g, SEL], jnp.concatenate([pair(selp_ref, T, g, 0), zero_flag, drop], axis=0))
            store_scores(g, SEL, T, s, None, None, FAR)
        return carry

    lax.fori_loop(0, n_far, far_scores, 0)

    for g in range(A_N_KV):
        row_max(g, WIN)
        row_max(g, SEL)
    for g in range(A_N_KV):
        accumulate(g, WIN, WIN_SLOT0, FAR, pair(winp_ref, t_first, g, 1), first=True)
        accumulate(g, WIN, WIN_SLOT0 + 1, NEAR, pair(winp_ref, t_prev, g, 1))
        accumulate(g, WIN, WIN_SLOT0 + 2, NEAR, pair(winp_ref, td, g, 1))
        accumulate(g, SEL, N_SLOTS - 2, NEAR, pair(selp_ref, t_prev, g, 1), first=True)
        accumulate(g, SEL, N_SLOTS - 1, NEAR, pair(selp_ref, td, g, 1))

    def far_accumulate(T, carry):
        for g in range(A_N_KV):
            accumulate(g, SEL, T, FAR, pair(selp_ref, T, g, 1))
        return carry

    lax.fori_loop(0, n_far, far_accumulate, 0)

    for g in range(A_N_KV):
        o_c, o_s, o_w = oc_scr[g], result(g, SEL), result(g, WIN)
        for r in range(A_GROUP):
            h = g * A_GROUP + r
            rows = slice(r * qb, (r + 1) * qb)
            gt = gate_ref[:, 3 * h:3 * h + 3]
            o_h = gt[:, 0:1] * o_c[rows] + gt[:, 1:2] * o_s[rows] + gt[:, 2:3] * o_w[rows]
            o_ref[:, h * dh:(h + 1) * dh] = o_h.astype(o_ref.dtype)


def nsa_attention_prompt(q, gates, Bn, L, kct, vc, tbl, tz0, tz1, overlap_t, sel_pages, win_pages):
    qb = Q_BLOCK
    nqb = L // qb
    n_sel = -(-L // SEL_BLOCK)
    n_tiles = L // TILE
    assert n_tiles == N_PAGES and qb == TILE
    R = A_GROUP * qb
    row_map = lambda b, i: (b * nqb + i, 0)
    seq_map3 = lambda b, i: (b, 0, 0)
    seq_map4 = lambda b, i: (b, 0, 0, 0)
    const2 = lambda b, i: (0, 0)
    const3 = lambda b, i: (0, 0, 0)
    return pl.pallas_call(
        functools.partial(_attn_prompt_body, qb=qb, n_sel=n_sel),
        out_shape=jax.ShapeDtypeStruct((Bn * L, A_Q_DIM), bf16),
        grid=(Bn, nqb),
        in_specs=[
            pl.BlockSpec((qb, A_Q_DIM), row_map),
            pl.BlockSpec((qb, LANES), row_map),
            pl.BlockSpec((1, KV_HALF, N_CHUNKS), seq_map3),
            pl.BlockSpec((1, N_CHUNKS, KV_HALF), seq_map3),
            pl.BlockSpec((A_N_HEADS, LANES), const2),
            pl.BlockSpec((A_N_HEADS, TILE, TILE), const3),
            pl.BlockSpec((A_N_HEADS, TILE, TILE), const3),
            pl.BlockSpec((TILE, TILE), const2),
            pl.BlockSpec((1, n_tiles, KV_ROW, TILE), seq_map4),
            pl.BlockSpec((1, n_tiles, KV_ROW, TILE), seq_map4),
        ],
        out_specs=pl.BlockSpec((qb, A_Q_DIM), row_map),
        scratch_shapes=[
            pltpu.VMEM((A_N_KV, N_SLOTS + 3, R, WIDE), f32),
            pltpu.VMEM((A_N_KV, 2, 2, R, TILE), f32),
            pltpu.VMEM((A_N_KV, 2, R, TILE), f32),
            pltpu.VMEM((A_N_KV, 2, R, A_HEAD_DIM), f32),
            pltpu.VMEM((A_N_KV, 2, R, WIDE), bf16),
            pltpu.VMEM((A_N_KV, R, A_HEAD_DIM), f32),
            pltpu.VMEM((A_N_KV, R, TILE), f32),
        ],
        compiler_params=_cparams(("parallel", "arbitrary")),
        name="nsa_attention_prompt",
    )(q, gates, kct, vc, tbl, tz0, tz1, overlap_t, sel_pages, win_pages)


def _attn_sample_body(pt_ref, q_ref, gate_ref, kct_ref, vc_ref, tbl_ref, ovl_ref, *refs, qb, start, n_sel):
    del pt_ref
    sel_pages = refs[:N_PAGES]
    selnew_ref, winpast_ref, winnew_ref, o_ref, s_scr = refs[N_PAGES:]
    dh = A_HEAD_DIM
    RG = A_GROUP * qb
    R = A_N_KV * RG
    scale = dh ** -0.5
    n_win_past = WINDOW // TILE
    assert R == TILE and n_sel <= SEL_BLOCK and start == N_PAGES * TILE

    qs = q_ref[...] * scale
    blocks = []
    for g in range(A_N_KV):
        qg = jnp.concatenate([qs[:, (g * A_GROUP + r) * dh:(g * A_GROUP + r + 1) * dh] for r in range(A_GROUP)], axis=0)
        parts = [qg if gg == g else jnp.zeros((RG, dh), f32) for gg in range(A_N_KV)]
        blocks.append(jnp.concatenate(parts, axis=1))
    qbd = jnp.concatenate(blocks, axis=0).astype(bf16)

    tbl_rows = jnp.concatenate([jnp.broadcast_to(tbl_ref[h:h + 1, :], (qb, LANES)) for h in range(A_N_HEADS)], axis=0)
    far_bias = tbl_rows[:, MAX_DISTANCE - 1:MAX_DISTANCE]
    lane = lax.broadcasted_iota(i32, (R, TILE), 1)
    row = lax.broadcasted_iota(i32, (R, TILE), 0)
    qpos = start + row % qb

    def near_bias(dist):
        return jnp.take_along_axis(tbl_rows, jnp.clip(dist, 0, MAX_DISTANCE - 1), axis=1)

    def softmax_rows(s):
        m = jnp.max(s, axis=1, keepdims=True)
        p = jnp.where(s > 0.5 * MASK_VALUE, jnp.exp(s - m), 0.0)
        l = jnp.sum(p, axis=1, keepdims=True)
        return p, jnp.where(l == 0.0, 1.0, l)

    def pad_rows(x):
        return jnp.concatenate([x, jnp.zeros((TILE - x.shape[0], x.shape[1]), x.dtype)], axis=0).astype(bf16)

    def branch_scores(tiles, new_ref, extra, col0):
        n = len(tiles)
        for j, (kv_t, kind) in enumerate(tiles):
            s = _dot(qbd, kv_t(0).astype(bf16)) + extra(j)
            dist = qpos - (start - (n - j) * TILE + lane)
            s = s + (far_bias if kind == "far" else near_bias(dist))
            if kind == "edge":
                s = jnp.where(dist < WINDOW, s, MASK_VALUE)
            s_scr[:, col0 + j * TILE:col0 + (j + 1) * TILE] = s
        dist = qpos - (start + lane)
        s = _dot_nt(qbd, pad_rows(new_ref[:, 0:KV_HALF])) + extra(n) + near_bias(dist)
        s_scr[:, col0 + n * TILE:col0 + (n + 1) * TILE] = jnp.where(dist >= 0, s, MASK_VALUE)

    def branch_output(tiles, new_ref, col0):
        n = len(tiles)
        p, l = softmax_rows(s_scr[:, col0:col0 + (n + 1) * TILE])
        p = p.astype(bf16)
        o = _dot(p[:, n * TILE:(n + 1) * TILE], pad_rows(new_ref[:, KV_HALF:KV_ROW]))
        for j, (kv_t, _) in enumerate(tiles):
            o = o + _dot_nt(p[:, j * TILE:(j + 1) * TILE], kv_t(1).astype(bf16))
        return o / l

    def page_getter(ref, cols=slice(None)):
        return lambda half: ref[0, half * KV_HALF:(half + 1) * KV_HALF, cols]

    sel_tiles = [(page_getter(sel_pages[t]), "near" if t == N_PAGES - 1 else "far") for t in range(N_PAGES)]
    win_tiles = [(page_getter(winpast_ref, slice(j * TILE, (j + 1) * TILE)),
                  "edge" if j == 0 else ("near" if j == n_win_past - 1 else "far")) for j in range(n_win_past)]
    win_col0 = (N_PAGES + 1) * TILE

    def sel_scores(extra):
        branch_scores(sel_tiles, selnew_ref, extra, 0)

    branch_scores(win_tiles, winnew_ref, lambda j: 0.0, win_col0)

    dist_c = qpos - (lane * CMP_STRIDE + (CMP_LEN - 1))
    s_c = _dot(qbd, kct_ref[0]) + near_bias(dist_c)
    p_c, l_c = softmax_rows(jnp.where(dist_c >= 0, s_c, MASK_VALUE))
    o_c = _dot(p_c.astype(bf16), vc_ref[0]) / l_c
    p_c = p_c / l_c

    p_sum = []
    for g in range(A_N_KV):
        acc = p_c[g * RG:g * RG + qb]
        for r in range(1, A_GROUP):
            acc = acc + p_c[g * RG + r * qb:g * RG + (r + 1) * qb]
        p_sum.append(acc)
    p_sum = jnp.concatenate(p_sum, axis=0)
    imp = jnp.dot(p_sum, ovl_ref[...], preferred_element_type=f32, precision=_HI)
    s_lane = lax.broadcasted_iota(i32, (A_N_KV * qb, TILE), 1)
    s_qpos = start + lax.broadcasted_iota(i32, (A_N_KV * qb, TILE), 0) % qb
    blk = s_qpos // SEL_BLOCK
    valid = s_lane * SEL_BLOCK <= s_qpos
    forced = (s_lane == 0) | (s_lane == blk) | (s_lane == blk - 1)
    score = jnp.where(valid, imp + jnp.where(forced, FORCE_SCORE, 0.0), -1.0)
    score = jnp.where(s_lane < n_sel, score, -3.0)
    rank = jnp.zeros(score.shape, f32)
    for s2 in range(n_sel):
        col = score[:, s2:s2 + 1]
        rank = rank + jnp.where((col > score) | ((col == score) & (s_lane > s2)), 1.0, 0.0)
    not_chosen = jnp.where(rank < float(min(SEL_TOPN, n_sel)), 0.0, -1.0)
    drop = jnp.concatenate(
        [not_chosen[g * qb:(g + 1) * qb] for g in range(A_N_KV) for _ in range(A_GROUP)], axis=0)
    drop = drop[:, 0:SEL_BLOCK].astype(bf16)
    b_row = lax.broadcasted_iota(i32, (SEL_BLOCK, TILE), 0)
    b_col = lax.broadcasted_iota(i32, (SEL_BLOCK, TILE), 1)

    def drop_unselected(t):
        expand = jnp.where(b_row == t * (TILE // SEL_BLOCK) + b_col // SEL_BLOCK, -MASK_VALUE, 0.0).astype(bf16)
        return _dot(drop, expand)

    sel_scores(lambda j: drop_unselected(j))
    o_w = branch_output(win_tiles, winnew_ref, win_col0)
    o_s = branch_output(sel_tiles, selnew_ref, 0)

    for g in range(A_N_KV):
        for r in range(A_GROUP):
            h = g * A_GROUP + r
            rows = slice(g * RG + r * qb, g * RG + (r + 1) * qb)
            cols = slice(g * dh, (g + 1) * dh)
            gt = gate_ref[:, 3 * h:3 * h + 3]
            o_h = gt[:, 0:1] * o_c[rows, cols] + gt[:, 1:2] * o_s[rows, cols] + gt[:, 2:3] * o_w[rows, cols]
            o_ref[:, h * dh:(h + 1) * dh] = o_h.astype(o_ref.dtype)


def nsa_attention_sample(q, gates, row0, Bn, L, start, kct, vc, tbl, overlap, sel_pages, page_table, sel_new,
                         win_past, win_new):
    qb = L
    n_sel = -(-(start + L) // SEL_BLOCK)
    blk0 = row0 // qb
    row_map = lambda b, pt: (blk0 + b, 0)
    seq_map = lambda b, pt: (b, 0, 0)
    page_specs = [
        pl.BlockSpec((1, KV_ROW, PAGE_SIZE), functools.partial(lambda b, pt, j: (pt[b, j], 0, 0), j=j))
        for j in range(N_PAGES)
    ]
    return pl.pallas_call(
        functools.partial(_attn_sample_body, qb=qb, start=start, n_sel=n_sel),
        out_shape=jax.ShapeDtypeStruct((Bn * L, A_Q_DIM), f32),
        grid_spec=pltpu.PrefetchScalarGridSpec(
            num_scalar_prefetch=1,
            grid=(Bn,),
            in_specs=[
                pl.BlockSpec((qb, A_Q_DIM), row_map),
                pl.BlockSpec((qb, LANES), row_map),
                pl.BlockSpec((1, KV_HALF, N_CHUNKS), seq_map),
                pl.BlockSpec((1, N_CHUNKS, KV_HALF), seq_map),
                pl.BlockSpec((A_N_HEADS, LANES), lambda b, pt: (0, 0)),
                pl.BlockSpec((TILE, TILE), lambda b, pt: (0, 0)),
            ] + page_specs + [
                pl.BlockSpec((qb, KV_ROW), row_map),
                pl.BlockSpec((1, KV_ROW, WINDOW), seq_map),
                pl.BlockSpec((qb, KV_ROW), row_map),
            ],
            out_specs=pl.BlockSpec((qb, A_Q_DIM), lambda b, pt: (b, 0)),
            scratch_shapes=[pltpu.VMEM((TILE, (N_PAGES + 1 + WINDOW // TILE + 1) * TILE), f32)],
        ),
        compiler_params=_cparams(("parallel",)),
        name="nsa_attention_sample",
    )(page_table, q, gates, kct, vc, tbl, overlap, *([sel_pages] * N_PAGES), sel_new, win_past, win_new)


WINDOW_UPDATE_SEQS_PER_STEP = 4


def _window_update_body(old_ref, new_ref, place_ref, o_ref, *, n_new):
    lane = lax.broadcasted_iota(i32, (KV_ROW, TILE), 1)
    for bb in range(old_ref.shape[0]):
        x = old_ref[bb]
        shifted = pltpu.roll(x, WINDOW - n_new, 1)
        new_t = lax.dot_general(new_ref[bb * n_new:(bb + 1) * n_new, :], place_ref[...], (((0,), (0,)), ((), ())),
                                preferred_element_type=f32, precision=_HI)
        o_ref[bb, :, 0:WINDOW - TILE] = shifted[:, 0:WINDOW - TILE]
        o_ref[bb, :, WINDOW - TILE:WINDOW] = jnp.where(lane >= TILE - n_new, new_t, shifted[:, WINDOW - TILE:WINDOW])


def window_update(old_t, new_rows):
    Bn = old_t.shape[0]
    n_new = new_rows.shape[0] // Bn
    BB = next(n for n in (WINDOW_UPDATE_SEQS_PER_STEP, 1) if Bn % n == 0)
    place = jnp.asarray(np.eye(n_new, TILE, k=TILE - n_new, dtype=np.float32))
    return pl.pallas_call(
        functools.partial(_window_update_body, n_new=n_new),
        out_shape=jax.ShapeDtypeStruct(old_t.shape, f32),
        grid=(Bn // BB,),
        in_specs=[
            pl.BlockSpec((BB, KV_ROW, WINDOW), lambda b: (b, 0, 0)),
            pl.BlockSpec((BB * n_new, KV_ROW), lambda b: (b, 0)),
            pl.BlockSpec((n_new, TILE), lambda b: (0, 0)),
        ],
        out_specs=pl.BlockSpec((BB, KV_ROW, WINDOW), lambda b: (b, 0, 0)),
        compiler_params=_cparams(("parallel",)),
        name="window_update",
    )(old_t, new_rows, place)


def _pad_lanes(v):
    return jnp.pad(v, (0, LANES - v.shape[0])).reshape(1, LANES)


def _overlap_matrix():
    n_cmp = N_CHUNKS - CMP_RATIO + 1
    c = np.arange(TILE)[:, None] * CMP_STRIDE
    s = np.arange(TILE)[None, :] * SEL_BLOCK
    ov = (c < s + SEL_BLOCK) & (c + CMP_LEN > s) & (np.arange(TILE)[:, None] < n_cmp)
    return jnp.asarray(ov.astype(np.float32))


def _feature_major(x):
    lead = x.shape[:-4]
    n = len(lead)
    return jnp.transpose(x, tuple(range(n)) + (n + 1, n + 2, n + 3, n)).reshape(lead + (KV_ROW, x.shape[-4]))


def _token_major(x_t):
    B, _, T = x_t.shape
    return jnp.transpose(x_t.reshape(B, 2, A_N_KV, A_HEAD_DIM, T), (0, 4, 1, 2, 3))


def kernel(x_prompt, x_sample, state_ssm, state_conv, cache_cmp_kv, cache_sel_kv, cache_win_kv, page_table, ln_g, ln_b, m_in_w, m_conv_w, m_conv_b, m_dt_bias, m_a_log, m_d, m_norm_w, m_out_w, kv_w, cmp_w1, cmp_pe, cmp_w2, q_w, o_w, rel_bias, mlp_w1, mlp_w2):
    Bp, Lp, D = x_prompt.shape
    Bs, Ls, _ = x_sample.shape
    NP, NS = Bp * Lp, Bs * Ls
    past_len = page_table.shape[1] * PAGE_SIZE
    assert past_len == N_PAGES * PAGE_SIZE and Lp == N_PAGES * PAGE_SIZE and cache_win_kv.shape[1] == WINDOW

    in_w = m_in_w[0].astype(bf16)
    z_w = in_w[:, :M_D_INNER]
    xbc_w = in_w[:, M_D_INNER:M_D_INNER + M_CONV_DIM]
    dt_w = jnp.pad(in_w[:, M_D_INNER + M_CONV_DIM:], ((0, 0), (0, LANES - M_N_HEADS)))
    kvw = kv_w.astype(bf16)
    qw = q_w[0].astype(bf16)
    gate_w = jnp.pad(qw[:, A_Q_DIM:], ((0, 0), (0, LANES - 3 * A_N_HEADS)))
    w1b = cmp_w1.astype(bf16)
    w_j = jnp.transpose(w1b, (0, 2, 3, 1, 4)).reshape(2, CMP_STRIDE, A_HEAD_DIM, CMP_RATIO * CMP_HIDDEN)
    zeros = jnp.zeros_like(w_j)
    wbd = jnp.concatenate([jnp.concatenate([w_j, zeros], axis=3), jnp.concatenate([zeros, w_j], axis=3)], axis=2)
    wbd = wbd.reshape(2, CMP_STRIDE // 2, 2 * LANES, 2 * CMP_RATIO * CMP_HIDDEN)
    pe_rows = jnp.broadcast_to(cmp_pe.astype(bf16).reshape(2, 1, CMP_LEN * A_HEAD_DIM), (2, SUBLANES, CMP_LEN * A_HEAD_DIM))
    w1_flat = w1b.reshape(2, CMP_LEN * A_HEAD_DIM, CMP_HIDDEN)
    cmp_w = (wbd, pe_rows, w1_flat, cmp_w2[1].astype(bf16), cmp_w2[0].T.astype(bf16))

    x = jnp.concatenate([x_prompt.reshape(NP, D), x_sample.reshape(NS, D)], axis=0)
    xb = x.astype(bf16)
    z = matmul(xb, z_w, f32)
    xbc = matmul(xb, xbc_w, f32)
    dt = matmul(xb, dt_w, f32)
    ssd_w = (m_conv_w[0], m_conv_b[0].reshape(1, -1), _pad_lanes(m_dt_bias[0]), _pad_lanes(m_a_log[0]),
             jnp.repeat(m_d[0], M_HEAD_DIM).reshape(1, -1), m_norm_w[0].reshape(1, -1))
    y_p, p_ssm, p_conv = ssd_mixer_core(xbc, z, dt, 0, Bp, Lp, None, None, *ssd_w)
    y_s, s_ssm, s_conv = ssd_mixer_core(xbc, z, dt, NP, Bs, Ls, state_conv[0], state_ssm[0], *ssd_w)
    h_f, h_b = matmul_residual_ln(y_p, y_s, m_out_w[0].astype(bf16), x,
                                  ln_g[0, 0].reshape(1, D), ln_b[0, 0].reshape(1, D))
    h_f, h_b = mlp_residual_ln(h_b, h_f, mlp_w1[0].astype(bf16), mlp_w2[0].astype(bf16),
                               ln_g[0, 1].reshape(1, D), ln_b[0, 1].reshape(1, D))

    cmp_t, sel_t, win_t, sel_pg, win_pg = kv_project_feature_major(h_b, kvw.T, Bp, Lp)
    kv_s = matmul(h_b[NP:], kvw, f32)
    cmp_s, sel_s, win_s = kv_s[:, 0:KV_ROW], kv_s[:, KV_ROW:2 * KV_ROW], kv_s[:, 2 * KV_ROW:3 * KV_ROW]
    kct_p, vc_p = compress_kv(cmp_t, None, *cmp_w)
    kct_s, vc_s = compress_kv(_feature_major(cache_cmp_kv), page_table, *cmp_w)

    q = matmul(h_b, qw[:, :A_Q_DIM], f32)
    gates = matmul(h_b, gate_w, f32, act="sigmoid")
    tbl, tz0, tz1 = bias_table(rel_bias.T)
    overlap = _overlap_matrix()
    o_p = nsa_attention_prompt(q, gates, Bp, Lp, kct_p, vc_p, tbl, tz0, tz1, overlap.T, sel_pg, win_pg)
    win_cache_t = _feature_major(cache_win_kv)
    o_s = nsa_attention_sample(q[NP:], gates[NP:], 0, Bs, Ls, past_len, kct_s, vc_s, tbl, overlap,
                               _feature_major(cache_sel_kv), page_table, sel_s, win_cache_t, win_s)
    h_f, h_b = matmul_residual_ln(o_p, o_s, o_w[0].astype(bf16), h_f,
                                  ln_g[1, 0].reshape(1, D), ln_b[1, 0].reshape(1, D))
    out_p, out_s = mlp_residual_ln(h_b, h_f, mlp_w1[1].astype(bf16), mlp_w2[1].astype(bf16),
                                   ln_g[1, 1].reshape(1, D), ln_b[1, 1].reshape(1, D), split_rows=NP)

    kv_shape = (2, A_N_KV, A_HEAD_DIM)
    n_keep = min(WINDOW, Lp)
    s_win = _token_major(window_update(win_cache_t, win_s))
    return (
        out_p.reshape(Bp, Lp, D), out_s.reshape(Bs, Ls, D),
        p_ssm[None], p_conv[None],
        _token_major(cmp_t), _token_major(sel_t), _token_major(win_t[:, :, Lp - n_keep:]),
        s_ssm[None], s_conv[None],
        cmp_s.reshape((Bs, Ls) + kv_shape), sel_s.reshape((Bs, Ls) + kv_shape), s_win,
    )
```

```python
import functools
import math

import jax
import jax.numpy as jnp
import numpy as np
from jax import lax
from jax.experimental import pallas as pl
from jax.experimental.pallas import tpu as pltpu

f32 = jnp.float32
bf16 = jnp.bfloat16
i32 = jnp.int32

D_MODEL = 1024
DEPTH = 2
DN_ALPHA = (2.0 * DEPTH) ** 0.25
LN_EPS = 1e-5
RMS_EPS = 1e-5
D_FF = 4 * D_MODEL
M_D_INNER = 2 * D_MODEL
M_HEAD_DIM = 64
M_N_HEADS = M_D_INNER // M_HEAD_DIM
M_N_GROUPS = 4
M_HPG = M_N_HEADS // M_N_GROUPS
M_D_STATE = 128
M_CONV = 4
M_CHUNK = 128
M_CONV_DIM = M_D_INNER + 2 * M_N_GROUPS * M_D_STATE
A_HEAD_DIM = 64
A_N_HEADS = D_MODEL // A_HEAD_DIM
A_N_KV = 4
A_GROUP = A_N_HEADS // A_N_KV
A_Q_DIM = A_N_HEADS * A_HEAD_DIM
KV_HALF = A_N_KV * A_HEAD_DIM
KV_ROW = 2 * KV_HALF
CMP_LEN = 32
CMP_STRIDE = 16
CMP_RATIO = CMP_LEN // CMP_STRIDE
CMP_HIDDEN = 2 * A_HEAD_DIM
SEL_BLOCK = 64
SEL_TOPN = 16
WINDOW = 512
Q_BLOCK = 128
N_BUCKETS = 32
MAX_DISTANCE = 128
MASK_VALUE = -1e30
FORCE_SCORE = 1e3
PAGE_SIZE = 128

LANES = 128
SUBLANES = 8
VMEM_LIMIT = 56 * 1024 * 1024

_HI = lax.Precision.HIGHEST


def _cparams(sem):
    return pltpu.CompilerParams(dimension_semantics=sem, vmem_limit_bytes=VMEM_LIMIT)


def _dot(a, b):
    return jnp.dot(a, b, preferred_element_type=f32)


def _dot_nt(a, b):
    return lax.dot_general(a, b, (((1,), (1,)), ((), ())), preferred_element_type=f32)


def _dot_tn(a, b):
    return lax.dot_general(a, b, (((0,), (0,)), ((), ())), preferred_element_type=f32)


def _silu(x):
    return x * (1.0 / (1.0 + jnp.exp(-x)))


def _layer_norm(x, g, b):
    mu = jnp.mean(x, axis=-1, keepdims=True)
    xc = x - mu
    var = jnp.mean(xc * xc, axis=-1, keepdims=True)
    return xc * lax.rsqrt(var + LN_EPS) * g + b


def _mm_body(x_ref, w_ref, o_ref, *, act):
    y = _dot(x_ref[...], w_ref[...])
    if act == "sigmoid":
        y = 1.0 / (1.0 + jnp.exp(-y))
    o_ref[...] = y.astype(o_ref.dtype)


def matmul(x, w, out_dtype, act=None, tm=1024, tn=1024):
    M, K = x.shape
    N = w.shape[1]
    tn = next(t for t in (tn, 512, 256, LANES) if N % t == 0)
    return pl.pallas_call(
        functools.partial(_mm_body, act=act),
        out_shape=jax.ShapeDtypeStruct((M, N), out_dtype),
        grid=(M // tm, N // tn),
        in_specs=[pl.BlockSpec((tm, K), lambda i, j: (i, 0)), pl.BlockSpec((K, tn), lambda i, j: (0, j))],
        out_specs=pl.BlockSpec((tm, tn), lambda i, j: (i, j)),
        compiler_params=_cparams(("parallel", "parallel")),
        name="matmul",
    )(x, w)


def _mm_res_ln_body(x0_ref, x1_ref, w_ref, r0_ref, r1_ref, g_ref, b_ref, of_ref, ob_ref, *, n_head_blocks):
    i = pl.program_id(0)
    w = w_ref[...]
    y = lax.cond(i < n_head_blocks,
                 lambda: DN_ALPHA * r0_ref[...] + _dot(x0_ref[...], w),
                 lambda: DN_ALPHA * r1_ref[...] + _dot(x1_ref[...].astype(bf16), w))
    h = _layer_norm(y, g_ref[...], b_ref[...])
    of_ref[...] = h
    ob_ref[...] = h.astype(bf16)


def matmul_residual_ln(x_head, x_tail, w, resid_head, resid_tail, g, b, tm=1024):
    M0, K = x_head.shape
    M1 = x_tail.shape[0]
    N = w.shape[1]
    assert M0 % tm == 0 and M1 % tm == 0 and resid_head.shape[0] >= M0 and resid_tail.shape[0] == M1
    n_head = M0 // tm
    row_map = lambda i: (i, 0)
    head_map = lambda i: (jnp.minimum(i, n_head - 1), 0)
    tail_map = lambda i: (jnp.maximum(i - n_head, 0), 0)
    const = lambda i: (0, 0)
    once = pl.Buffered(1) if M1 == tm else None
    return pl.pallas_call(
        functools.partial(_mm_res_ln_body, n_head_blocks=n_head),
        out_shape=(jax.ShapeDtypeStruct((M0 + M1, N), f32), jax.ShapeDtypeStruct((M0 + M1, N), bf16)),
        grid=((M0 + M1) // tm,),
        in_specs=[
            pl.BlockSpec((tm, K), head_map),
            pl.BlockSpec((tm, K), tail_map, pipeline_mode=once),
            pl.BlockSpec((K, N), const, pipeline_mode=pl.Buffered(1)),
            pl.BlockSpec((tm, N), head_map),
            pl.BlockSpec((tm, N), tail_map, pipeline_mode=once),
            pl.BlockSpec((1, N), const),
            pl.BlockSpec((1, N), const),
        ],
        out_specs=(pl.BlockSpec((tm, N), row_map), pl.BlockSpec((tm, N), row_map)),
        compiler_params=_cparams(("arbitrary",)),
        name="matmul_residual_ln",
    )(x_head, x_tail, w, resid_head, resid_tail, g, b)


def _mlp_body(hb_ref, hf_ref, w1_ref, w2_ref, g_ref, b_ref, o0_ref, o1_ref, acc_ref, *, n_head_blocks):
    i = pl.program_id(0)
    j = pl.program_id(1)

    @pl.when(j == 0)
    def _():
        acc_ref[...] = jnp.zeros_like(acc_ref)

    u = jnp.maximum(_dot(hb_ref[...], w1_ref[...]), 0.0)
    acc_ref[...] += _dot((u * u).astype(bf16), w2_ref[...])
    last = j == pl.num_programs(1) - 1

    def result():
        return _layer_norm(DN_ALPHA * hf_ref[...] + acc_ref[...], g_ref[...], b_ref[...])

    if n_head_blocks is None:
        @pl.when(last)
        def _():
            h = result()
            o0_ref[...] = h
            o1_ref[...] = h.astype(bf16)
    else:
        @pl.when(last & (i < n_head_blocks))
        def _():
            o0_ref[...] = result()

        @pl.when(last & (i >= n_head_blocks))
        def _():
            o1_ref[...] = result()


def mlp_residual_ln(hb, hf, w1, w2, g, b, split_rows=None, tm=1024, tf=1024):
    M, D = hb.shape
    F = w1.shape[1]
    row_map = lambda i, j: (i, 0)
    if split_rows is None:
        n_head = None
        out_shape = (jax.ShapeDtypeStruct((M, D), f32), jax.ShapeDtypeStruct((M, D), bf16))
        out_specs = (pl.BlockSpec((tm, D), row_map), pl.BlockSpec((tm, D), row_map))
    else:
        assert split_rows % tm == 0 and (M - split_rows) % tm == 0
        n_head = split_rows // tm
        out_shape = (jax.ShapeDtypeStruct((split_rows, D), f32), jax.ShapeDtypeStruct((M - split_rows, D), f32))
        out_specs = (pl.BlockSpec((tm, D), lambda i, j: (jnp.minimum(i, n_head - 1), 0)),
                     pl.BlockSpec((tm, D), lambda i, j: (jnp.maximum(i - n_head, 0), 0)))
    return pl.pallas_call(
        functools.partial(_mlp_body, n_head_blocks=n_head),
        out_shape=out_shape,
        grid=(M // tm, F // tf),
        in_specs=[
            pl.BlockSpec((tm, D), row_map),
            pl.BlockSpec((tm, D), row_map),
            pl.BlockSpec((D, tf), lambda i, j: (0, j)),
            pl.BlockSpec((tf, D), lambda i, j: (j, 0)),
            pl.BlockSpec((1, D), lambda i, j: (0, 0)),
            pl.BlockSpec((1, D), lambda i, j: (0, 0)),
        ],
        out_specs=out_specs,
        scratch_shapes=[pltpu.VMEM((tm, D), f32)],
        compiler_params=_cparams(("arbitrary", "arbitrary")),
        name="mlp_residual_ln",
    )(hb, hf, w1, w2, g, b)


def _kv_project_body(wt_ref, h_ref, cmp_ref, sel_ref, win_ref, selp_ref, winp_ref):
    res = _dot_nt(wt_ref[...], h_ref[...])
    tm = h_ref.shape[0]
    cmp_ref[0] = res[0:KV_ROW]
    sel_ref[0] = res[KV_ROW:2 * KV_ROW]
    win_ref[0] = res[2 * KV_ROW:3 * KV_ROW]
    for k in range(tm // PAGE_SIZE):
        cols = slice(k * PAGE_SIZE, (k + 1) * PAGE_SIZE)
        selp_ref[0, k] = res[KV_ROW:2 * KV_ROW, cols].astype(bf16)
        winp_ref[0, k] = res[2 * KV_ROW:3 * KV_ROW, cols].astype(bf16)


def kv_project_feature_major(h_b, w_t, Bn, L, tm=512):
    nj = L // tm
    pages_per_step = tm // PAGE_SIZE
    fm = jax.ShapeDtypeStruct((Bn, KV_ROW, L), f32)
    pg = jax.ShapeDtypeStruct((Bn, L // PAGE_SIZE, KV_ROW, PAGE_SIZE), bf16)
    fm_spec = pl.BlockSpec((1, KV_ROW, tm), lambda b, j: (b, 0, j))
    pg_spec = pl.BlockSpec((1, pages_per_step, KV_ROW, PAGE_SIZE), lambda b, j: (b, j, 0, 0))
    return pl.pallas_call(
        _kv_project_body,
        out_shape=(fm, fm, fm, pg, pg),
        grid=(Bn, nj),
        in_specs=[
            pl.BlockSpec(w_t.shape, lambda b, j: (0, 0)),
            pl.BlockSpec((tm, h_b.shape[1]), lambda b, j: (b * nj + j, 0)),
        ],
        out_specs=(fm_spec, fm_spec, fm_spec, pg_spec, pg_spec),
        compiler_params=_cparams(("parallel", "parallel")),
        name="kv_project_feature_major",
    )(w_t, h_b)


SSD_SEQS_PER_STEP = 4
SSD_SEQS_PER_STEP_CHUNKED = 2


def _ssd_body(*refs, Q, BB, nc, has_init):
    n_in = 1 if nc == 1 else BB
    xbc_refs, z_refs, dt_refs = refs[0:n_in], refs[n_in:2 * n_in], refs[2 * n_in:3 * n_in]
    refs = refs[3 * n_in:]
    if has_init:
        conv0_ref, h0_ref = refs[0:2]
        refs = refs[2:]
    (cw_ref, cb_ref, dtb_ref, alog_ref, dskx_ref, nw_ref, e64_ref, eq_ref,
     y_ref, hout_ref, cout_ref, st_ref, xpad_ref, xc_ref, ybuf_ref, xdt_ref, xds_ref, ecx_ref, acx_ref) = refs

    def rows_of(row_refs, bb, cols=slice(None)):
        return row_refs[0][bb * Q:(bb + 1) * Q, cols] if nc == 1 else row_refs[bb][:, cols]

    def store_y(bb, cols, value):
        if nc == 1:
            y_ref[bb * Q:(bb + 1) * Q, cols] = value
        else:
            y_ref[bb, :, cols] = value

    c = pl.program_id(1)
    single_chunk = nc == 1
    P, N, R, G = M_HEAD_DIM, M_D_STATE, M_HPG, M_N_GROUPS
    PAD = SUBLANES

    per_head_state = single_chunk and has_init

    def first_chunk():
        for bb in range(BB):
            xpad_ref[bb, 0:PAD, :] = jnp.zeros((PAD, M_CONV_DIM), f32)
            if has_init:
                xpad_ref[bb, PAD - (M_CONV - 1):PAD, :] = conv0_ref[bb]
            if per_head_state:
                continue
            if has_init:
                for g in range(G):
                    for r in range(R):
                        st_ref[bb, g, :, r * P:(r + 1) * P] = h0_ref[bb, g * R + r].T
            else:
                st_ref[bb] = jnp.zeros(st_ref.shape[1:], f32)

    def last_chunk():
        if per_head_state:
            return
        for bb in range(BB):
            for g in range(G):
                for r in range(R):
                    hout_ref[bb, g * R + r] = st_ref[bb, g, :, r * P:(r + 1) * P].T

    if single_chunk:
        first_chunk()
    else:
        pl.when(c == 0)(first_chunk)

    ri = lax.broadcasted_iota(i32, (Q, Q), 0)
    ci = lax.broadcasted_iota(i32, (Q, Q), 1)
    tril = ri >= ci
    GW = M_D_INNER // G
    lane = lax.broadcasted_iota(i32, (Q, LANES), 1)
    third = M_N_HEADS
    assert 3 * third <= LANES

    def spread(v, e_ref):
        hi = v.astype(bf16).astype(f32)
        r1 = v - hi
        mid = r1.astype(bf16).astype(f32)
        lo = r1 - mid
        packed = jnp.where(lane < third, hi, jnp.where(lane < 2 * third, pltpu.roll(mid, third, 1),
                                                       jnp.where(lane < 3 * third, pltpu.roll(lo, 2 * third, 1), 0.0)))
        return _dot(packed.astype(bf16), e_ref[...])

    seqs = range(BB)
    a_cum, a_cum_t, e_last = [], [], []
    for bb in seqs:
        xpad_ref[bb, PAD:PAD + Q, :] = rows_of(xbc_refs, bb)
        acc = cb_ref[...] + xpad_ref[bb, pl.ds(PAD - 3, Q), :] * cw_ref[0:1, :]
        for k in range(1, M_CONV):
            acc = acc + xpad_ref[bb, pl.ds(PAD - 3 + k, Q), :] * cw_ref[k:k + 1, :]
        xc_ref[bb] = _silu(acc)
        cout_ref[bb] = xpad_ref[bb, pl.ds(Q + PAD - 3, 3), :]
        xpad_ref[bb, 0:PAD, :] = xpad_ref[bb, pl.ds(Q, PAD), :]

        xdt = rows_of(dt_refs, bb) + dtb_ref[...]
        dt = jnp.maximum(xdt, 0.0) + jnp.log1p(jnp.exp(-jnp.abs(xdt)))
        a = dt * (-jnp.exp(alog_ref[...]))
        a_cum.append(jnp.dot(tril.astype(f32), a, preferred_element_type=f32, precision=_HI))
        a_cum_t.append(a_cum[bb].T)
        a_last = a_cum[bb][Q - 1:Q, :]
        e_last.append(jnp.exp(a_last))
        xdt_all = xc_ref[bb, :, 0:M_D_INNER] * spread(dt, e64_ref)
        xdt_ref[bb] = xdt_all
        xds_ref[bb] = xdt_all * spread(jnp.exp(a_last - a_cum[bb]), e64_ref)
        ecx_ref[bb] = spread(jnp.exp(a_cum[bb]), e64_ref)
        if Q == LANES:
            acx_ref[bb] = spread(a_cum[bb], eq_ref)

    for g in range(G):
        gcols = slice(g * GW, (g + 1) * GW)
        bg = [xc_ref[bb, :, M_D_INNER + g * N:M_D_INNER + (g + 1) * N].astype(bf16) for bb in seqs]
        cg = [xc_ref[bb, :, M_D_INNER + G * N + g * N:M_D_INNER + G * N + (g + 1) * N].astype(bf16) for bb in seqs]
        gmat = [_dot_nt(cg[bb], bg[bb]) for bb in seqs]
        for r in range(R):
            h = g * R + r
            hcols = slice(h * P, (h + 1) * P)
            for bb in seqs:
                col = acx_ref[bb, :, h * Q:(h + 1) * Q] if Q == LANES else a_cum[bb][:, h:h + 1]
                row = a_cum_t[bb][h:h + 1, :]
                lmat = jnp.exp(jnp.where(tril, col - row, -jnp.inf))
                ydiag = _dot((gmat[bb] * lmat).astype(bf16), xdt_ref[bb, :, hcols].astype(bf16))
                if per_head_state:
                    h_in = h0_ref[bb, h]
                    hout_ref[bb, h] = (h_in * e_last[bb][:, h:h + 1]
                                       + _dot_tn(xds_ref[bb, :, hcols].astype(bf16), bg[bb]))
                    ydiag = ydiag + _dot_nt(cg[bb], h_in.astype(bf16)) * ecx_ref[bb, :, hcols]
                ybuf_ref[bb, :, hcols] = ydiag
        for bb in seqs:
            extra = xc_ref[bb, :, gcols] * dskx_ref[:, gcols]
            if not per_head_state:
                extra = extra + _dot(cg[bb], st_ref[bb, g].astype(bf16)) * ecx_ref[bb, :, gcols]
                new = _dot_tn(bg[bb], xds_ref[bb, :, gcols].astype(bf16))
                st_ref[bb, g] = st_ref[bb, g] * ecx_ref[bb, Q - 1:Q, gcols] + new
            ybuf_ref[bb, :, gcols] += extra

    for g in range(G):
        cols = slice(g * GW, (g + 1) * GW)
        for bb in seqs:
            yg = ybuf_ref[bb, :, cols] * _silu(rows_of(z_refs, bb, cols))
            ms = jnp.mean(yg * yg, axis=-1, keepdims=True)
            store_y(bb, cols, (yg * lax.rsqrt(ms + RMS_EPS) * nw_ref[:, cols]).astype(y_ref.dtype))

    if single_chunk:
        last_chunk()
    else:
        pl.when(c == pl.num_programs(1) - 1)(last_chunk)


def _spread_matrix(width):
    rows = np.arange(LANES)[:, None]
    cols = np.arange(M_N_HEADS * width)[None, :]
    return jnp.asarray((rows < 3 * M_N_HEADS) & (rows % M_N_HEADS == cols // width), bf16)


def ssd_mixer_core(xbc, z, dt, row0, Bn, L, conv0, h0, conv_w, conv_b, dt_bias, a_log, d_skip_x, norm_w):
    Q = M_CHUNK if L % M_CHUNK == 0 else L
    nc = L // Q
    has_init = h0 is not None
    BB = next(n for n in ((SSD_SEQS_PER_STEP if nc == 1 else SSD_SEQS_PER_STEP_CHUNKED), 1) if Bn % n == 0)
    const2 = lambda b, c: (0, 0)
    if nc == 1:
        blk0 = row0 // (BB * Q)
        row_maps = [lambda b, c: (blk0 + b, 0)]
        y_shape, y_block, y_map = (Bn * L, M_D_INNER), (BB * Q, M_D_INNER), (lambda b, c: (b, 0))
    else:
        blk0 = row0 // Q
        row_maps = [functools.partial(lambda b, c, bb: (blk0 + (b * BB + bb) * nc + c, 0), bb=bb) for bb in range(BB)]
        y_shape, y_block, y_map = (Bn, L, M_D_INNER), (BB, Q, M_D_INNER), (lambda b, c: (b, c, 0))
    rows_per_block = BB * Q if nc == 1 else Q
    in_specs, args = [], []
    for arr, width in ((xbc, M_CONV_DIM), (z, M_D_INNER), (dt, LANES)):
        in_specs += [pl.BlockSpec((rows_per_block, width), m) for m in row_maps]
        args += [arr] * len(row_maps)
    if has_init:
        in_specs += [
            pl.BlockSpec((BB, M_CONV - 1, M_CONV_DIM), lambda b, c: (b, 0, 0)),
            pl.BlockSpec((BB, M_N_HEADS, M_HEAD_DIM, M_D_STATE), lambda b, c: (b, 0, 0, 0)),
        ]
        args += [conv0, h0]
    e64 = _spread_matrix(M_HEAD_DIM)
    eq = _spread_matrix(Q if Q == LANES else SUBLANES)
    in_specs += [
        pl.BlockSpec((M_CONV, M_CONV_DIM), const2),
        pl.BlockSpec((1, M_CONV_DIM), const2),
        pl.BlockSpec((1, LANES), const2),
        pl.BlockSpec((1, LANES), const2),
        pl.BlockSpec((1, M_D_INNER), const2),
        pl.BlockSpec((1, M_D_INNER), const2),
        pl.BlockSpec(e64.shape, const2),
        pl.BlockSpec(eq.shape, const2),
    ]
    args += [conv_w, conv_b, dt_bias, a_log, d_skip_x, norm_w, e64, eq]
    y_dtype = bf16 if y_block[-2] % 16 == 0 else f32
    y, new_ssm, new_conv = pl.pallas_call(
        functools.partial(_ssd_body, Q=Q, BB=BB, nc=nc, has_init=has_init),
        out_shape=(
            jax.ShapeDtypeStruct(y_shape, y_dtype),
            jax.ShapeDtypeStruct((Bn, M_N_HEADS, M_HEAD_DIM, M_D_STATE), f32),
            jax.ShapeDtypeStruct((Bn, M_CONV - 1, M_CONV_DIM), f32),
        ),
        grid=(Bn // BB, nc),
        in_specs=in_specs,
        out_specs=(
            pl.BlockSpec(y_block, y_map),
            pl.BlockSpec((BB, M_N_HEADS, M_HEAD_DIM, M_D_STATE), lambda b, c: (b, 0, 0, 0)),
            pl.BlockSpec((BB, M_CONV - 1, M_CONV_DIM), lambda b, c: (b, 0, 0)),
        ),
        scratch_shapes=[
            pltpu.VMEM((1, 1, SUBLANES, LANES) if nc == 1 and has_init
                       else (BB, M_N_GROUPS, M_D_STATE, M_HPG * M_HEAD_DIM), f32),
            pltpu.VMEM((BB, Q + SUBLANES, M_CONV_DIM), f32),
            pltpu.VMEM((BB, Q, M_CONV_DIM), f32),
            pltpu.VMEM((BB, Q, M_D_INNER), f32),
            pltpu.VMEM((BB, Q, M_D_INNER), f32),
            pltpu.VMEM((BB, Q, M_D_INNER), f32),
            pltpu.VMEM((BB, Q, M_D_INNER), f32),
            pltpu.VMEM((BB, Q, M_N_HEADS * Q) if Q == LANES else (1, SUBLANES, LANES), f32),
        ],
        compiler_params=_cparams(("parallel", "arbitrary")),
        name="ssd_mixer_core",
    )(*args)
    return y.reshape(Bn * L, M_D_INNER), new_ssm, new_conv


N_PAGES = 16
N_CHUNKS = N_PAGES * PAGE_SIZE // CMP_STRIDE
N_SLABS = KV_ROW // LANES


def _compress_body(*refs, paged):
    if paged:
        refs = refs[1:]
    pages = refs[:N_PAGES]
    wbd_ref, pe_ref, w1f_ref, w2_ref, w2t_ref, kct_ref, vc_ref, xs_ref = refs[N_PAGES:]
    H = CMP_HIDDEN
    n_steps = CMP_STRIDE // 2
    pages_per_step = N_PAGES // n_steps

    def to_token_major(sl, page_range):
        for p in page_range:
            xs_ref[sl, p * PAGE_SIZE:(p + 1) * PAGE_SIZE, :] = pages[p][0, sl * LANES:(sl + 1) * LANES, :].T

    to_token_major(0, range(N_PAGES))
    row = lax.broadcasted_iota(i32, (N_CHUNKS, H), 0)
    for kv in range(2):
        pe_term = _dot(pe_ref[kv], w1f_ref[kv])[0:1, :]
        for gp in range(A_N_KV // 2):
            sl = kv * (A_N_KV // 2) + gp
            acc = jnp.zeros((N_CHUNKS, 2 * CMP_RATIO * H), f32)
            for jp in range(n_steps):
                x = jnp.concatenate(
                    [xs_ref[sl, pl.ds(2 * jp + k, N_CHUNKS, stride=CMP_STRIDE), :] for k in range(2)], axis=1).astype(bf16)
                acc = acc + _dot(x, wbd_ref[kv, jp])
                if sl + 1 < N_SLABS:
                    to_token_major(sl + 1, range(jp * pages_per_step, (jp + 1) * pages_per_step))
            for gi in range(2):
                g = gp * 2 + gi
                p0 = acc[:, gi * CMP_RATIO * H:gi * CMP_RATIO * H + H]
                p1 = acc[:, gi * CMP_RATIO * H + H:(gi + 1) * CMP_RATIO * H]
                p1_next = jnp.where(row == N_CHUNKS - 1, 0.0, pltpu.roll(p1, N_CHUNKS - 1, 0))
                hid = _silu(p0 + p1_next + pe_term).astype(bf16)
                if kv == 0:
                    kct_ref[0, g * A_HEAD_DIM:(g + 1) * A_HEAD_DIM, :] = _dot_nt(w2t_ref[...], hid).astype(kct_ref.dtype)
                else:
                    vc_ref[0, :, g * A_HEAD_DIM:(g + 1) * A_HEAD_DIM] = _dot(hid, w2_ref[...]).astype(vc_ref.dtype)


def compress_kv(pages, page_table, wbd, pe_rows, w1_flat, w2_v, w2t_k):
    paged = page_table is not None
    Bn = page_table.shape[0] if paged else pages.shape[0]
    if paged:
        page_specs = [
            pl.BlockSpec((1, KV_ROW, PAGE_SIZE), functools.partial(lambda b, pt, j: (pt[b, j], 0, 0), j=j))
            for j in range(N_PAGES)
        ]
        const = lambda nd: (lambda b, pt: (0,) * nd)
        out_map = lambda b, pt: (b, 0, 0)
    else:
        page_specs = [
            pl.BlockSpec((1, KV_ROW, PAGE_SIZE), functools.partial(lambda b, j: (b, 0, j), j=j)) for j in range(N_PAGES)
        ]
        const = lambda nd: (lambda b: (0,) * nd)
        out_map = lambda b: (b, 0, 0)
    in_specs = page_specs + [
        pl.BlockSpec(wbd.shape, const(4)),
        pl.BlockSpec(pe_rows.shape, const(3)),
        pl.BlockSpec(w1_flat.shape, const(3)),
        pl.BlockSpec(w2_v.shape, const(2)),
        pl.BlockSpec(w2t_k.shape, const(2)),
    ]
    out_shape = (jax.ShapeDtypeStruct((Bn, KV_HALF, N_CHUNKS), bf16), jax.ShapeDtypeStruct((Bn, N_CHUNKS, KV_HALF), bf16))
    out_specs = (pl.BlockSpec((1, KV_HALF, N_CHUNKS), out_map), pl.BlockSpec((1, N_CHUNKS, KV_HALF), out_map))
    scratch = [pltpu.VMEM((N_SLABS, N_PAGES * PAGE_SIZE, LANES), f32)]
    body = functools.partial(_compress_body, paged=paged)
    args = ([pages] * N_PAGES) + [wbd, pe_rows, w1_flat, w2_v, w2t_k]
    if paged:
        return pl.pallas_call(
            body, out_shape=out_shape,
            grid_spec=pltpu.PrefetchScalarGridSpec(
                num_scalar_prefetch=1, grid=(Bn,), in_specs=in_specs, out_specs=out_specs, scratch_shapes=scratch),
            compiler_params=_cparams(("parallel",)), name="compress_kv_paged",
        )(page_table, *args)
    return pl.pallas_call(
        body, out_shape=out_shape, grid=(Bn,), in_specs=in_specs, out_specs=out_specs, scratch_shapes=scratch,
        compiler_params=_cparams(("parallel",)), name="compress_kv",
    )(*args)


def _bias_table_body(rb_ref, tbl_ref, tz0_ref, tz1_ref):
    n = lax.broadcasted_iota(i32, (A_N_HEADS, LANES), 1)
    max_exact = N_BUCKETS // 2
    large = max_exact + jnp.floor(jnp.log(jnp.maximum(n, max_exact).astype(f32) / max_exact)
                                  / math.log(MAX_DISTANCE / max_exact) * (N_BUCKETS - max_exact))
    bucket = jnp.where(n < max_exact, n.astype(f32), jnp.minimum(large, N_BUCKETS - 1.0))
    tbl = jnp.zeros((A_N_HEADS, LANES), f32)
    for b in range(N_BUCKETS):
        tbl = jnp.where(bucket == float(b), rb_ref[:, b:b + 1], tbl)
    tbl_ref[...] = tbl
    dist = lax.broadcasted_iota(i32, (LANES, LANES), 0) - lax.broadcasted_iota(i32, (LANES, LANES), 1)
    for h in range(A_N_HEADS):
        row = jnp.broadcast_to(tbl[h:h + 1, :], (LANES, LANES))
        own = jnp.take_along_axis(row, jnp.clip(dist, 0, MAX_DISTANCE - 1), axis=1)
        tz0_ref[h] = jnp.where(dist >= 0, own, MASK_VALUE)
        tz1_ref[h] = jnp.take_along_axis(row, jnp.minimum(dist + LANES, MAX_DISTANCE - 1), axis=1)


def bias_table(rel_bias_t):
    assert MAX_DISTANCE <= LANES
    tile = jax.ShapeDtypeStruct((A_N_HEADS, LANES, LANES), f32)
    return pl.pallas_call(
        _bias_table_body, out_shape=(jax.ShapeDtypeStruct((A_N_HEADS, LANES), f32), tile, tile), name="bias_table",
    )(rel_bias_t)


TILE = 128
WIN_TILES = WINDOW // TILE


WIDE = 2 * TILE
N_SLOTS = N_PAGES * TILE // WIDE
BIG = -MASK_VALUE


SEL, WIN = 0, 1
FAR, NEAR = 0, 1
WIN_SLOT0 = N_SLOTS


def _attn_prompt_body(q_ref, gate_ref, kct_ref, vc_ref, tbl_ref, tz0_ref, tz1_ref, ovlt_ref, selp_ref, winp_ref, o_ref,
                      s_scr, m_scr, l_scr, acc_scr, lhs_scr, oc_scr, fb_scr, *, qb, n_sel):
    i = pl.program_id(1)
    R = A_GROUP * qb
    dh = A_HEAD_DIM
    pos0 = i * qb
    odd = (i % 2) == 1
    td = i // 2
    scale = dh ** -0.5
    n_rank = SUBLANES * (-(-n_sel // SUBLANES))

    lane = lax.broadcasted_iota(i32, (R, TILE), 1)
    q_in_blk = jnp.concatenate([lax.broadcasted_iota(i32, (qb, TILE), 0)] * A_GROUP, axis=0)
    qpos = pos0 + q_in_blk
    neg_tile = jnp.full((R, TILE), MASK_VALUE, f32)

    s_idx = lax.broadcasted_iota(i32, (n_rank, qb), 0)
    s_qpos = pos0 + lax.broadcasted_iota(i32, (n_rank, qb), 1)
    blk = s_qpos // SEL_BLOCK
    sel_valid = s_idx * SEL_BLOCK <= s_qpos
    sel_forced = (s_idx == 0) | (s_idx == blk) | (s_idx == blk - 1)

    f_row = lax.broadcasted_iota(i32, (TILE - dh, WIDE), 0)
    flag_rows = [jnp.where(f_row == k, BIG, 0.0).astype(bf16) for k in range(2)]
    zero_flag = jnp.zeros((TILE - dh, WIDE), bf16)
    zero_drop = jnp.zeros((TILE, WIDE), bf16)
    b_row = lax.broadcasted_iota(i32, (TILE, WIDE), 0)
    b_col = lax.broadcasted_iota(i32, (TILE, WIDE), 1)
    f_lane = lax.broadcasted_iota(i32, (R, TILE - dh), 1)
    flags = jnp.where(f_lane == 0, jnp.where(td < 1, -1.0, 0.0),
                      jnp.where(f_lane == 1, jnp.where(td < 2, -1.0, 0.0), 0.0)).astype(bf16)
    win_thr = q_in_blk + jnp.where(odd, TILE, 0)

    def pair(ref, T, g, half):
        rows = slice(half * KV_HALF + g * dh, half * KV_HALF + (g + 1) * dh)
        return jnp.concatenate([ref[0, 2 * T, rows, :], ref[0, 2 * T + 1, rows, :]], axis=1)

    def drop_rows(T):
        return jnp.where(b_row == T * (WIDE // SEL_BLOCK) + b_col // SEL_BLOCK, BIG, 0.0).astype(bf16)

    def store_scores(g, br, slot, s, left, right, kind, first=False):
        s_l = s[:, 0:TILE] if left is None else s[:, 0:TILE] + left
        s_r = s[:, TILE:WIDE] if right is None else s[:, TILE:WIDE] + right
        s_scr[g, slot, :, 0:TILE] = s_l
        s_scr[g, slot, :, TILE:WIDE] = s_r
        mx = jnp.maximum(s_l, s_r)
        m_scr[g, br, kind] = mx if first else jnp.maximum(m_scr[g, br, kind], mx)

    def accumulate(g, br, slot, kind, v_t, first=False):
        shift = m_scr[g, br, kind]
        p_l = jnp.exp(s_scr[g, slot, :, 0:TILE] - shift)
        p_r = jnp.exp(s_scr[g, slot, :, TILE:WIDE] - shift)
        pv = _dot_nt(jnp.concatenate([p_l, p_r], axis=1).astype(bf16), v_t)
        if first:
            l_scr[g, br] = p_l + p_r
            acc_scr[g, br] = pv
        else:
            l_scr[g, br] += p_l + p_r
            acc_scr[g, br] += pv

    def row_max(g, br):
        fb = fb_scr[g]
        m = jnp.max(jnp.maximum(m_scr[g, br, FAR] + fb, m_scr[g, br, NEAR]), axis=1, keepdims=True)
        m_b = jnp.broadcast_to(m, (R, TILE))
        m_scr[g, br, FAR] = m_b - fb
        m_scr[g, br, NEAR] = m_b

    def result(g, br):
        return acc_scr[g, br] / jnp.sum(l_scr[g, br], axis=1, keepdims=True)

    def near_tiles(g):
        tz0 = tz0_ref[g * A_GROUP:(g + 1) * A_GROUP].reshape(R, TILE)
        tz1 = tz1_ref[g * A_GROUP:(g + 1) * A_GROUP].reshape(R, TILE)
        return jnp.where(odd, fb_scr[g], tz1), jnp.where(odd, tz1, tz0), jnp.where(odd, tz0, neg_tile)

    t_prev = jnp.maximum(td - 1, 0)
    t_first = jnp.maximum(td - 2, 0)
    n_far = jnp.maximum(td - 1, 0)

    for g in range(A_N_KV):
        heads = [g * A_GROUP + r for r in range(A_GROUP)]
        qg = jnp.concatenate([q_ref[:, h * dh:(h + 1) * dh] for h in heads], axis=0)
        qg = (qg.astype(f32) * scale).astype(bf16)
        tbl_g = jnp.concatenate([jnp.broadcast_to(tbl_ref[h:h + 1, :], (qb, TILE)) for h in heads], axis=0)
        fb_scr[g] = jnp.concatenate(
            [jnp.broadcast_to(tbl_ref[h:h + 1, MAX_DISTANCE - 1:MAX_DISTANCE], (qb, TILE)) for h in heads], axis=0)

        dist_c = qpos - (lane * CMP_STRIDE + (CMP_LEN - 1))
        s_c = _dot(qg, kct_ref[0, g * dh:(g + 1) * dh, :])
        s_c = s_c + jnp.take_along_axis(tbl_g, jnp.clip(dist_c, 0, MAX_DISTANCE - 1), axis=1)
        s_c = jnp.where(dist_c >= 0, s_c, MASK_VALUE)
        m_c = jnp.max(s_c, axis=1, keepdims=True)
        p_c = jnp.where(dist_c >= 0, jnp.exp(s_c - m_c), 0.0)
        l_c = jnp.sum(p_c, axis=1, keepdims=True)
        l_c = jnp.where(l_c == 0.0, 1.0, l_c)
        oc_scr[g] = _dot(p_c.astype(bf16), vc_ref[0, :, g * dh:(g + 1) * dh]) / l_c
        p_c = p_c / l_c

        p_sum = p_c[0:qb]
        for r in range(1, A_GROUP):
            p_sum = p_sum + p_c[r * qb:(r + 1) * qb]
        imp_t = lax.dot_general(ovlt_ref[0:n_rank, :], p_sum, (((1,), (1,)), ((), ())),
                                preferred_element_type=f32, precision=_HI)
        score = jnp.where(sel_valid, imp_t + jnp.where(sel_forced, FORCE_SCORE, 0.0), -1.0)
        score = jnp.where(s_idx < n_sel, score, -3.0)
        rank = jnp.zeros((n_rank, qb), f32)
        for s2 in range(n_sel):
            other = score[s2:s2 + 1, :]
            rank = rank + jnp.where((other > score) | ((other == score) & (s_idx > s2)), 1.0, 0.0)
        dropped_t = jnp.where(rank < float(min(SEL_TOPN, n_sel)), 0.0, -1.0)
        dropped = jnp.concatenate([dropped_t, jnp.zeros((TILE - n_rank, qb), f32)], axis=0).T
        lhs_scr[g, SEL] = jnp.concatenate([qg, flags, jnp.concatenate([dropped] * A_GROUP, axis=0).astype(bf16)], axis=1)
        lhs_scr[g, WIN] = jnp.concatenate([qg, flags, jnp.zeros((R, TILE), bf16)], axis=1)
        m_scr[g, SEL, FAR] = neg_tile

    too_old_l = jnp.where(lane > win_thr, 0.0, MASK_VALUE)
    too_old_r = jnp.where(lane + TILE > win_thr, 0.0, MASK_VALUE)
    for g in range(A_N_KV):
        prev_right, diag_left, diag_right = near_tiles(g)
        lhs_win = lhs_scr[g, WIN]
        s = _dot(lhs_win, jnp.concatenate([pair(winp_ref, t_first, g, 0), flag_rows[1], zero_drop], axis=0))
        store_scores(g, WIN, WIN_SLOT0, s, too_old_l, too_old_r, FAR, first=True)
        s = _dot(lhs_win, jnp.concatenate([pair(winp_ref, t_prev, g, 0), flag_rows[0], zero_drop], axis=0))
        store_scores(g, WIN, WIN_SLOT0 + 1, s, fb_scr[g], prev_right, NEAR, first=True)
        s = _dot(lhs_win, jnp.concatenate([pair(winp_ref, td, g, 0), zero_flag, zero_drop], axis=0))
        store_scores(g, WIN, WIN_SLOT0 + 2, s, diag_left, diag_right, NEAR)
        lhs_sel = lhs_scr[g, SEL]
        s = _dot(lhs_sel, jnp.concatenate([pair(selp_ref, t_prev, g, 0), flag_rows[0], drop_rows(t_prev)], axis=0))
        store_scores(g, SEL, N_SLOTS - 2, s, fb_scr[g], prev_right, NEAR, first=True)
        s = _dot(lhs_sel, jnp.concatenate([pair(selp_ref, td, g, 0), zero_flag, drop_rows(td)], axis=0))
        store_scores(g, SEL, N_SLOTS - 1, s, diag_left, diag_right, NEAR)

    def far_scores(T, carry):
        drop = drop_rows(T)
        for g in range(A_N_KV):
            s = _dot(lhs_scr[g, SEL], jnp.concatenate([pair(selp_ref, T, g, 0), zero_flag, drop], axis=0))
            store_scores(g, SEL, T, s, None, None, FAR)
        return carry

    lax.fori_loop(0, n_far, far_scores, 0)

    for g in range(A_N_KV):
        row_max(g, WIN)
        row_max(g, SEL)
    for g in range(A_N_KV):
        accumulate(g, WIN, WIN_SLOT0, FAR, pair(winp_ref, t_first, g, 1), first=True)
        accumulate(g, WIN, WIN_SLOT0 + 1, NEAR, pair(winp_ref, t_prev, g, 1))
        accumulate(g, WIN, WIN_SLOT0 + 2, NEAR, pair(winp_ref, td, g, 1))
        accumulate(g, SEL, N_SLOTS - 2, NEAR, pair(selp_ref, t_prev, g, 1), first=True)
        accumulate(g, SEL, N_SLOTS - 1, NEAR, pair(selp_ref, td, g, 1))

    def far_accumulate(T, carry):
        for g in range(A_N_KV):
            accumulate(g, SEL, T, FAR, pair(selp_ref, T, g, 1))
        return carry

    lax.fori_loop(0, n_far, far_accumulate, 0)

    for g in range(A_N_KV):
        o_c, o_s, o_w = oc_scr[g], result(g, SEL), result(g, WIN)
        for r in range(A_GROUP):
            h = g * A_GROUP + r
            rows = slice(r * qb, (r + 1) * qb)
            gt = gate_ref[:, 3 * h:3 * h + 3]
            o_h = gt[:, 0:1] * o_c[rows] + gt[:, 1:2] * o_s[rows] + gt[:, 2:3] * o_w[rows]
            o_ref[:, h * dh:(h + 1) * dh] = o_h.astype(o_ref.dtype)


def nsa_attention_prompt(q, gates, Bn, L, kct, vc, tbl, tz0, tz1, overlap_t, sel_pages, win_pages):
    qb = Q_BLOCK
    nqb = L // qb
    n_sel = -(-L // SEL_BLOCK)
    n_tiles = L // TILE
    assert n_tiles == N_PAGES and qb == TILE
    R = A_GROUP * qb
    row_map = lambda b, i: (b * nqb + i, 0)
    seq_map3 = lambda b, i: (b, 0, 0)
    seq_map4 = lambda b, i: (b, 0, 0, 0)
    const2 = lambda b, i: (0, 0)
    const3 = lambda b, i: (0, 0, 0)
    return pl.pallas_call(
        functools.partial(_attn_prompt_body, qb=qb, n_sel=n_sel),
        out_shape=jax.ShapeDtypeStruct((Bn * L, A_Q_DIM), bf16),
        grid=(Bn, nqb),
        in_specs=[
            pl.BlockSpec((qb, A_Q_DIM), row_map),
            pl.BlockSpec((qb, LANES), row_map),
            pl.BlockSpec((1, KV_HALF, N_CHUNKS), seq_map3),
            pl.BlockSpec((1, N_CHUNKS, KV_HALF), seq_map3),
            pl.BlockSpec((A_N_HEADS, LANES), const2),
            pl.BlockSpec((A_N_HEADS, TILE, TILE), const3),
            pl.BlockSpec((A_N_HEADS, TILE, TILE), const3),
            pl.BlockSpec((TILE, TILE), const2),
            pl.BlockSpec((1, n_tiles, KV_ROW, TILE), seq_map4),
            pl.BlockSpec((1, n_tiles, KV_ROW, TILE), seq_map4),
        ],
        out_specs=pl.BlockSpec((qb, A_Q_DIM), row_map),
        scratch_shapes=[
            pltpu.VMEM((A_N_KV, N_SLOTS + 3, R, WIDE), f32),
            pltpu.VMEM((A_N_KV, 2, 2, R, TILE), f32),
            pltpu.VMEM((A_N_KV, 2, R, TILE), f32),
            pltpu.VMEM((A_N_KV, 2, R, A_HEAD_DIM), f32),
            pltpu.VMEM((A_N_KV, 2, R, WIDE), bf16),
            pltpu.VMEM((A_N_KV, R, A_HEAD_DIM), f32),
            pltpu.VMEM((A_N_KV, R, TILE), f32),
        ],
        compiler_params=_cparams(("parallel", "arbitrary")),
        name="nsa_attention_prompt",
    )(q, gates, kct, vc, tbl, tz0, tz1, overlap_t, sel_pages, win_pages)


def _attn_sample_body(pt_ref, q_ref, gate_ref, kct_ref, vc_ref, tbl_ref, ovl_ref, *refs, qb, start, n_sel):
    del pt_ref
    sel_pages = refs[:N_PAGES]
    selnew_ref, winpast_ref, winnew_ref, o_ref, s_scr = refs[N_PAGES:]
    dh = A_HEAD_DIM
    RG = A_GROUP * qb
    R = A_N_KV * RG
    scale = dh ** -0.5
    n_win_past = WINDOW // TILE
    assert R == TILE and n_sel <= SEL_BLOCK and start == N_PAGES * TILE

    qs = q_ref[...] * scale
    blocks = []
    for g in range(A_N_KV):
        qg = jnp.concatenate([qs[:, (g * A_GROUP + r) * dh:(g * A_GROUP + r + 1) * dh] for r in range(A_GROUP)], axis=0)
        parts = [qg if gg == g else jnp.zeros((RG, dh), f32) for gg in range(A_N_KV)]
        blocks.append(jnp.concatenate(parts, axis=1))
    qbd = jnp.concatenate(blocks, axis=0).astype(bf16)

    tbl_rows = jnp.concatenate([jnp.broadcast_to(tbl_ref[h:h + 1, :], (qb, LANES)) for h in range(A_N_HEADS)], axis=0)
    far_bias = tbl_rows[:, MAX_DISTANCE - 1:MAX_DISTANCE]
    lane = lax.broadcasted_iota(i32, (R, TILE), 1)
    row = lax.broadcasted_iota(i32, (R, TILE), 0)
    qpos = start + row % qb

    def near_bias(dist):
        return jnp.take_along_axis(tbl_rows, jnp.clip(dist, 0, MAX_DISTANCE - 1), axis=1)

    def softmax_rows(s):
        m = jnp.max(s, axis=1, keepdims=True)
        p = jnp.where(s > 0.5 * MASK_VALUE, jnp.exp(s - m), 0.0)
        l = jnp.sum(p, axis=1, keepdims=True)
        return p, jnp.where(l == 0.0, 1.0, l)

    def pad_rows(x):
        return jnp.concatenate([x, jnp.zeros((TILE - x.shape[0], x.shape[1]), x.dtype)], axis=0).astype(bf16)

    def branch_scores(tiles, new_ref, extra, col0):
        n = len(tiles)
        for j, (kv_t, kind) in enumerate(tiles):
            s = _dot(qbd, kv_t(0).astype(bf16)) + extra(j)
            dist = qpos - (start - (n - j) * TILE + lane)
            s = s + (far_bias if kind == "far" else near_bias(dist))
            if kind == "edge":
                s = jnp.where(dist < WINDOW, s, MASK_VALUE)
            s_scr[:, col0 + j * TILE:col0 + (j + 1) * TILE] = s
        dist = qpos - (start + lane)
        s = _dot_nt(qbd, pad_rows(new_ref[:, 0:KV_HALF])) + extra(n) + near_bias(dist)
        s_scr[:, col0 + n * TILE:col0 + (n + 1) * TILE] = jnp.where(dist >= 0, s, MASK_VALUE)

    def branch_output(tiles, new_ref, col0):
        n = len(tiles)
        p, l = softmax_rows(s_scr[:, col0:col0 + (n + 1) * TILE])
        p = p.astype(bf16)
        o = _dot(p[:, n * TILE:(n + 1) * TILE], pad_rows(new_ref[:, KV_HALF:KV_ROW]))
        for j, (kv_t, _) in enumerate(tiles):
            o = o + _dot_nt(p[:, j * TILE:(j + 1) * TILE], kv_t(1).astype(bf16))
        return o / l

    def page_getter(ref, cols=slice(None)):
        return lambda half: ref[0, half * KV_HALF:(half + 1) * KV_HALF, cols]

    sel_tiles = [(page_getter(sel_pages[t]), "near" if t == N_PAGES - 1 else "far") for t in range(N_PAGES)]
    win_tiles = [(page_getter(winpast_ref, slice(j * TILE, (j + 1) * TILE)),
                  "edge" if j == 0 else ("near" if j == n_win_past - 1 else "far")) for j in range(n_win_past)]
    win_col0 = (N_PAGES + 1) * TILE

    def sel_scores(extra):
        branch_scores(sel_tiles, selnew_ref, extra, 0)

    branch_scores(win_tiles, winnew_ref, lambda j: 0.0, win_col0)

    dist_c = qpos - (lane * CMP_STRIDE + (CMP_LEN - 1))
    s_c = _dot(qbd, kct_ref[0]) + near_bias(dist_c)
    p_c, l_c = softmax_rows(jnp.where(dist_c >= 0, s_c, MASK_VALUE))
    o_c = _dot(p_c.astype(bf16), vc_ref[0]) / l_c
    p_c = p_c / l_c

    p_sum = []
    for g in range(A_N_KV):
        acc = p_c[g * RG:g * RG + qb]
        for r in range(1, A_GROUP):
            acc = acc + p_c[g * RG + r * qb:g * RG + (r + 1) * qb]
        p_sum.append(acc)
    p_sum = jnp.concatenate(p_sum, axis=0)
    imp = jnp.dot(p_sum, ovl_ref[...], preferred_element_type=f32, precision=_HI)
    s_lane = lax.broadcasted_iota(i32, (A_N_KV * qb, TILE), 1)
    s_qpos = start + lax.broadcasted_iota(i32, (A_N_KV * qb, TILE), 0) % qb
    blk = s_qpos // SEL_BLOCK
    valid = s_lane * SEL_BLOCK <= s_qpos
    forced = (s_lane == 0) | (s_lane == blk) | (s_lane == blk - 1)
    score = jnp.where(valid, imp + jnp.where(forced, FORCE_SCORE, 0.0), -1.0)
    score = jnp.where(s_lane < n_sel, score, -3.0)
    rank = jnp.zeros(score.shape, f32)
    for s2 in range(n_sel):
        col = score[:, s2:s2 + 1]
        rank = rank + jnp.where((col > score) | ((col == score) & (s_lane > s2)), 1.0, 0.0)
    not_chosen = jnp.where(rank < float(min(SEL_TOPN, n_sel)), 0.0, -1.0)
    drop = jnp.concatenate(
        [not_chosen[g * qb:(g + 1) * qb] for g in range(A_N_KV) for _ in range(A_GROUP)], axis=0)
    drop = drop[:, 0:SEL_BLOCK].astype(bf16)
    b_row = lax.broadcasted_iota(i32, (SEL_BLOCK, TILE), 0)
    b_col = lax.broadcasted_iota(i32, (SEL_BLOCK, TILE), 1)

    def drop_unselected(t):
        expand = jnp.where(b_row == t * (TILE // SEL_BLOCK) + b_col // SEL_BLOCK, -MASK_VALUE, 0.0).astype(bf16)
        return _dot(drop, expand)

    sel_scores(lambda j: drop_unselected(j))
    o_w = branch_output(win_tiles, winnew_ref, win_col0)
    o_s = branch_output(sel_tiles, selnew_ref, 0)

    for g in range(A_N_KV):
        for r in range(A_GROUP):
            h = g * A_GROUP + r
            rows = slice(g * RG + r * qb, g * RG + (r + 1) * qb)
            cols = slice(g * dh, (g + 1) * dh)
            gt = gate_ref[:, 3 * h:3 * h + 3]
            o_h = gt[:, 0:1] * o_c[rows, cols] + gt[:, 1:2] * o_s[rows, cols] + gt[:, 2:3] * o_w[rows, cols]
            o_ref[:, h * dh:(h + 1) * dh] = o_h.astype(o_ref.dtype)


def nsa_attention_sample(q, gates, row0, Bn, L, start, kct, vc, tbl, overlap, sel_pages, page_table, sel_new,
                         win_past, win_new):
    qb = L
    n_sel = -(-(start + L) // SEL_BLOCK)
    blk0 = row0 // qb
    row_map = lambda b, pt: (blk0 + b, 0)
    seq_map = lambda b, pt: (b, 0, 0)
    page_specs = [
        pl.BlockSpec((1, KV_ROW, PAGE_SIZE), functools.partial(lambda b, pt, j: (pt[b, j], 0, 0), j=j))
        for j in range(N_PAGES)
    ]
    return pl.pallas_call(
        functools.partial(_attn_sample_body, qb=qb, start=start, n_sel=n_sel),
        out_shape=jax.ShapeDtypeStruct((Bn * L, A_Q_DIM), f32),
        grid_spec=pltpu.PrefetchScalarGridSpec(
            num_scalar_prefetch=1,
            grid=(Bn,),
            in_specs=[
                pl.BlockSpec((qb, A_Q_DIM), row_map),
                pl.BlockSpec((qb, LANES), row_map),
                pl.BlockSpec((1, KV_HALF, N_CHUNKS), seq_map),
                pl.BlockSpec((1, N_CHUNKS, KV_HALF), seq_map),
                pl.BlockSpec((A_N_HEADS, LANES), lambda b, pt: (0, 0)),
                pl.BlockSpec((TILE, TILE), lambda b, pt: (0, 0)),
            ] + page_specs + [
                pl.BlockSpec((qb, KV_ROW), row_map),
                pl.BlockSpec((1, KV_ROW, WINDOW), seq_map),
                pl.BlockSpec((qb, KV_ROW), row_map),
            ],
            out_specs=pl.BlockSpec((qb, A_Q_DIM), lambda b, pt: (b, 0)),
            scratch_shapes=[pltpu.VMEM((TILE, (N_PAGES + 1 + WINDOW // TILE + 1) * TILE), f32)],
        ),
        compiler_params=_cparams(("parallel",)),
        name="nsa_attention_sample",
    )(page_table, q, gates, kct, vc, tbl, overlap, *([sel_pages] * N_PAGES), sel_new, win_past, win_new)


WINDOW_UPDATE_SEQS_PER_STEP = 4


def _window_update_body(old_ref, new_ref, place_ref, o_ref, *, n_new):
    lane = lax.broadcasted_iota(i32, (KV_ROW, TILE), 1)
    for bb in range(old_ref.shape[0]):
        x = old_ref[bb]
        shifted = pltpu.roll(x, WINDOW - n_new, 1)
        new_t = lax.dot_general(new_ref[bb * n_new:(bb + 1) * n_new, :], place_ref[...], (((0,), (0,)), ((), ())),
                                preferred_element_type=f32, precision=_HI)
        o_ref[bb, :, 0:WINDOW - TILE] = shifted[:, 0:WINDOW - TILE]
        o_ref[bb, :, WINDOW - TILE:WINDOW] = jnp.where(lane >= TILE - n_new, new_t, shifted[:, WINDOW - TILE:WINDOW])


def window_update(old_t, new_rows):
    Bn = old_t.shape[0]
    n_new = new_rows.shape[0] // Bn
    BB = next(n for n in (WINDOW_UPDATE_SEQS_PER_STEP, 1) if Bn % n == 0)
    place = jnp.asarray(np.eye(n_new, TILE, k=TILE - n_new, dtype=np.float32))
    return pl.pallas_call(
        functools.partial(_window_update_body, n_new=n_new),
        out_shape=jax.ShapeDtypeStruct(old_t.shape, f32),
        grid=(Bn // BB,),
        in_specs=[
            pl.BlockSpec((BB, KV_ROW, WINDOW), lambda b: (b, 0, 0)),
            pl.BlockSpec((BB * n_new, KV_ROW), lambda b: (b, 0)),
            pl.BlockSpec((n_new, TILE), lambda b: (0, 0)),
        ],
        out_specs=pl.BlockSpec((BB, KV_ROW, WINDOW), lambda b: (b, 0, 0)),
        compiler_params=_cparams(("parallel",)),
        name="window_update",
    )(old_t, new_rows, place)


def _pad_lanes(v):
    return jnp.pad(v, (0, LANES - v.shape[0])).reshape(1, LANES)


def _overlap_matrix():
    n_cmp = N_CHUNKS - CMP_RATIO + 1
    c = np.arange(TILE)[:, None] * CMP_STRIDE
    s = np.arange(TILE)[None, :] * SEL_BLOCK
    ov = (c < s + SEL_BLOCK) & (c + CMP_LEN > s) & (np.arange(TILE)[:, None] < n_cmp)
    return jnp.asarray(ov.astype(np.float32))


def _feature_major(x):
    lead = x.shape[:-4]
    n = len(lead)
    return jnp.transpose(x, tuple(range(n)) + (n + 1, n + 2, n + 3, n)).reshape(lead + (KV_ROW, x.shape[-4]))


def _token_major(x_t):
    B, _, T = x_t.shape
    return jnp.transpose(x_t.reshape(B, 2, A_N_KV, A_HEAD_DIM, T), (0, 4, 1, 2, 3))


def kernel(x_prompt, x_sample, state_ssm, state_conv, cache_cmp_kv, cache_sel_kv, cache_win_kv, page_table, ln_g, ln_b, m_in_w, m_conv_w, m_conv_b, m_dt_bias, m_a_log, m_d, m_norm_w, m_out_w, kv_w, cmp_w1, cmp_pe, cmp_w2, q_w, o_w, rel_bias, mlp_w1, mlp_w2):
    Bp, Lp, D = x_prompt.shape
    Bs, Ls, _ = x_sample.shape
    NP, NS = Bp * Lp, Bs * Ls
    past_len = page_table.shape[1] * PAGE_SIZE
    assert past_len == N_PAGES * PAGE_SIZE and Lp == N_PAGES * PAGE_SIZE and cache_win_kv.shape[1] == WINDOW

    in_w = m_in_w[0].astype(bf16)
    z_w = in_w[:, :M_D_INNER]
    xbc_w = in_w[:, M_D_INNER:M_D_INNER + M_CONV_DIM]
    dt_w = jnp.pad(in_w[:, M_D_INNER + M_CONV_DIM:], ((0, 0), (0, LANES - M_N_HEADS)))
    kvw = kv_w.astype(bf16)
    qw = q_w[0].astype(bf16)
    gate_w = jnp.pad(qw[:, A_Q_DIM:], ((0, 0), (0, LANES - 3 * A_N_HEADS)))
    w1b = cmp_w1.astype(bf16)
    w_j = jnp.transpose(w1b, (0, 2, 3, 1, 4)).reshape(2, CMP_STRIDE, A_HEAD_DIM, CMP_RATIO * CMP_HIDDEN)
    zeros = jnp.zeros_like(w_j)
    wbd = jnp.concatenate([jnp.concatenate([w_j, zeros], axis=3), jnp.concatenate([zeros, w_j], axis=3)], axis=2)
    wbd = wbd.reshape(2, CMP_STRIDE // 2, 2 * LANES, 2 * CMP_RATIO * CMP_HIDDEN)
    pe_rows = jnp.broadcast_to(cmp_pe.astype(bf16).reshape(2, 1, CMP_LEN * A_HEAD_DIM), (2, SUBLANES, CMP_LEN * A_HEAD_DIM))
    w1_flat = w1b.reshape(2, CMP_LEN * A_HEAD_DIM, CMP_HIDDEN)
    cmp_w = (wbd, pe_rows, w1_flat, cmp_w2[1].astype(bf16), cmp_w2[0].T.astype(bf16))

    x_p, x_s = x_prompt.reshape(NP, D), x_sample.reshape(NS, D)
    xb = jnp.concatenate([x_p.astype(bf16), x_s.astype(bf16)], axis=0)
    z = matmul(xb, z_w, f32)
    xbc = matmul(xb, xbc_w, f32)
    dt = matmul(xb, dt_w, f32)
    ssd_w = (m_conv_w[0], m_conv_b[0].reshape(1, -1), _pad_lanes(m_dt_bias[0]), _pad_lanes(m_a_log[0]),
             jnp.repeat(m_d[0], M_HEAD_DIM).reshape(1, -1), m_norm_w[0].reshape(1, -1))
    y_p, p_ssm, p_conv = ssd_mixer_core(xbc, z, dt, 0, Bp, Lp, None, None, *ssd_w)
    y_s, s_ssm, s_conv = ssd_mixer_core(xbc, z, dt, NP, Bs, Ls, state_conv[0], state_ssm[0], *ssd_w)
    h_f, h_b = matmul_residual_ln(y_p, y_s, m_out_w[0].astype(bf16), x_p, x_s,
                                  ln_g[0, 0].reshape(1, D), ln_b[0, 0].reshape(1, D))
    h_f, h_b = mlp_residual_ln(h_b, h_f, mlp_w1[0].astype(bf16), mlp_w2[0].astype(bf16),
                               ln_g[0, 1].reshape(1, D), ln_b[0, 1].reshape(1, D))

    cmp_t, sel_t, win_t, sel_pg, win_pg = kv_project_feature_major(h_b, kvw.T, Bp, Lp)
    kv_s = matmul(h_b[NP:], kvw, f32)
    cmp_s, sel_s, win_s = kv_s[:, 0:KV_ROW], kv_s[:, KV_ROW:2 * KV_ROW], kv_s[:, 2 * KV_ROW:3 * KV_ROW]
    kct_p, vc_p = compress_kv(cmp_t, None, *cmp_w)
    kct_s, vc_s = compress_kv(_feature_major(cache_cmp_kv), page_table, *cmp_w)

    q = matmul(h_b, qw[:, :A_Q_DIM], f32)
    gates = matmul(h_b, gate_w, f32, act="sigmoid")
    tbl, tz0, tz1 = bias_table(rel_bias.T)
    overlap = _overlap_matrix()
    o_p = nsa_attention_prompt(q, gates, Bp, Lp, kct_p, vc_p, tbl, tz0, tz1, overlap.T, sel_pg, win_pg)
    win_cache_t = _feature_major(cache_win_kv)
    o_s = nsa_attention_sample(q[NP:], gates[NP:], 0, Bs, Ls, past_len, kct_s, vc_s, tbl, overlap,
                               _feature_major(cache_sel_kv), page_table, sel_s, win_cache_t, win_s)
    h_f, h_b = matmul_residual_ln(o_p, o_s, o_w[0].astype(bf16), h_f, h_f[NP:],
                                  ln_g[1, 0].reshape(1, D), ln_b[1, 0].reshape(1, D))
    out_p, out_s = mlp_residual_ln(h_b, h_f, mlp_w1[1].astype(bf16), mlp_w2[1].astype(bf16),
                                   ln_g[1, 1].reshape(1, D), ln_b[1, 1].reshape(1, D), split_rows=NP)

    kv_shape = (2, A_N_KV, A_HEAD_DIM)
    n_keep = min(WINDOW, Lp)
    s_win = _token_major(window_update(win_cache_t, win_s))
    return (
        out_p.reshape(Bp, Lp, D), out_s.reshape(Bs, Ls, D),
        p_ssm[None], p_conv[None],
        _token_major(cmp_t), _token_major(sel_t), _token_major(win_t[:, :, Lp - n_keep:]),
        s_ssm[None], s_conv[None],
        cmp_s.reshape((Bs, Ls) + kv_shape), sel_s.reshape((Bs, Ls) + kv_shape), s_win,
    )
```

```python
import functools
import math

import jax
import jax.numpy as jnp
import numpy as np
from jax import lax
from jax.experimental import pallas as pl
from jax.experimental.pallas import tpu as pltpu

f32 = jnp.float32
bf16 = jnp.bfloat16
i32 = jnp.int32

D_MODEL = 1024
DEPTH = 2
DN_ALPHA = (2.0 * DEPTH) ** 0.25
LN_EPS = 1e-5
RMS_EPS = 1e-5
M_D_INNER = 2 * D_MODEL
M_HEAD_DIM = 64
M_N_HEADS = M_D_INNER // M_HEAD_DIM
M_N_GROUPS = 4
M_HPG = M_N_HEADS // M_N_GROUPS
M_D_STATE = 128
M_CONV = 4
M_CHUNK = 128
M_CONV_DIM = M_D_INNER + 2 * M_N_GROUPS * M_D_STATE
A_HEAD_DIM = 64
A_N_HEADS = D_MODEL // A_HEAD_DIM
A_N_KV = 4
A_GROUP = A_N_HEADS // A_N_KV
A_Q_DIM = A_N_HEADS * A_HEAD_DIM
KV_HALF = A_N_KV * A_HEAD_DIM
KV_ROW = 2 * KV_HALF
CMP_LEN = 32
CMP_STRIDE = 16
CMP_RATIO = CMP_LEN // CMP_STRIDE
CMP_HIDDEN = 2 * A_HEAD_DIM
SEL_BLOCK = 64
SEL_TOPN = 16
WINDOW = 512
Q_BLOCK = 128
N_BUCKETS = 32
MAX_DISTANCE = 128
MASK_VALUE = -1e30
FORCE_SCORE = 1e3
PAGE_SIZE = 128

LANES = 128
SUBLANES = 8
VMEM_LIMIT = 56 * 1024 * 1024

_HI = lax.Precision.HIGHEST


def _cparams(sem):
    return pltpu.CompilerParams(dimension_semantics=sem, vmem_limit_bytes=VMEM_LIMIT)


def _dot(a, b):
    return jnp.dot(a, b, preferred_element_type=f32)


def _dot_nt(a, b):
    return lax.dot_general(a, b, (((1,), (1,)), ((), ())), preferred_element_type=f32)


def _dot_tn(a, b):
    return lax.dot_general(a, b, (((0,), (0,)), ((), ())), preferred_element_type=f32)


def _silu(x):
    return x * (1.0 / (1.0 + jnp.exp(-x)))


def _layer_norm(x, g, b):
    mu = jnp.mean(x, axis=-1, keepdims=True)
    xc = x - mu
    var = jnp.mean(xc * xc, axis=-1, keepdims=True)
    return xc * lax.rsqrt(var + LN_EPS) * g + b


def _mm_body(x_ref, w_ref, o_ref, *, act):
    y = _dot(x_ref[...], w_ref[...])
    if act == "sigmoid":
        y = 1.0 / (1.0 + jnp.exp(-y))
    o_ref[...] = y.astype(o_ref.dtype)


def matmul(x, w, out_dtype, act=None, tm=1024, tn=1024):
    M, K = x.shape
    N = w.shape[1]
    tn = next(t for t in (tn, 512, 256, LANES) if N % t == 0)
    return pl.pallas_call(
        functools.partial(_mm_body, act=act),
        out_shape=jax.ShapeDtypeStruct((M, N), out_dtype),
        grid=(M // tm, N // tn),
        in_specs=[pl.BlockSpec((tm, K), lambda i, j: (i, 0)), pl.BlockSpec((K, tn), lambda i, j: (0, j))],
        out_specs=pl.BlockSpec((tm, tn), lambda i, j: (i, j)),
        compiler_params=_cparams(("parallel", "parallel")),
        name="matmul",
    )(x, w)


def _mm_res_ln_body(x0_ref, x1_ref, w_ref, r0_ref, r1_ref, g_ref, b_ref, of_ref, ob_ref, *, n_head_blocks):
    i = pl.program_id(0)
    w = w_ref[...]
    y = lax.cond(i < n_head_blocks,
                 lambda: DN_ALPHA * r0_ref[...] + _dot(x0_ref[...], w),
                 lambda: DN_ALPHA * r1_ref[...] + _dot(x1_ref[...].astype(bf16), w))
    h = _layer_norm(y, g_ref[...], b_ref[...])
    of_ref[...] = h
    ob_ref[...] = h.astype(bf16)


def matmul_residual_ln(x_head, x_tail, w, resid_head, resid_tail, g, b, tm=1024):
    M0, K = x_head.shape
    M1 = x_tail.shape[0]
    N = w.shape[1]
    assert M0 % tm == 0 and M1 % tm == 0 and resid_head.shape[0] >= M0 and resid_tail.shape[0] == M1
    n_head = M0 // tm
    row_map = lambda i: (i, 0)
    head_map = lambda i: (jnp.minimum(i, n_head - 1), 0)
    tail_map = lambda i: (jnp.maximum(i - n_head, 0), 0)
    const = lambda i: (0, 0)
    once = pl.Buffered(1) if M1 == tm else None
    return pl.pallas_call(
        functools.partial(_mm_res_ln_body, n_head_blocks=n_head),
        out_shape=(jax.ShapeDtypeStruct((M0 + M1, N), f32), jax.ShapeDtypeStruct((M0 + M1, N), bf16)),
        grid=((M0 + M1) // tm,),
        in_specs=[
            pl.BlockSpec((tm, K), head_map),
            pl.BlockSpec((tm, K), tail_map, pipeline_mode=once),
            pl.BlockSpec((K, N), const, pipeline_mode=pl.Buffered(1)),
            pl.BlockSpec((tm, N), head_map),
            pl.BlockSpec((tm, N), tail_map, pipeline_mode=once),
            pl.BlockSpec((1, N), const),
            pl.BlockSpec((1, N), const),
        ],
        out_specs=(pl.BlockSpec((tm, N), row_map), pl.BlockSpec((tm, N), row_map)),
        compiler_params=_cparams(("arbitrary",)),
        name="matmul_residual_ln",
    )(x_head, x_tail, w, resid_head, resid_tail, g, b)


def _mlp_body(hb_ref, hf_ref, w1_ref, w2_ref, g_ref, b_ref, o0_ref, o1_ref, acc_ref, *, n_head_blocks):
    i = pl.program_id(0)
    j = pl.program_id(1)

    @pl.when(j == 0)
    def _():
        acc_ref[...] = jnp.zeros_like(acc_ref)

    u = jnp.maximum(_dot(hb_ref[...], w1_ref[...]), 0.0)
    acc_ref[...] += _dot((u * u).astype(bf16), w2_ref[...])
    last = j == pl.num_programs(1) - 1

    def result():
        return _layer_norm(DN_ALPHA * hf_ref[...] + acc_ref[...], g_ref[...], b_ref[...])

    if n_head_blocks is None:
        @pl.when(last)
        def _():
            h = result()
            o0_ref[...] = h
            o1_ref[...] = h.astype(bf16)
    else:
        @pl.when(last & (i < n_head_blocks))
        def _():
            o0_ref[...] = result()

        @pl.when(last & (i >= n_head_blocks))
        def _():
            o1_ref[...] = result()


def mlp_residual_ln(hb, hf, w1, w2, g, b, split_rows=None, tm=1024, tf=1024):
    M, D = hb.shape
    F = w1.shape[1]
    row_map = lambda i, j: (i, 0)
    if split_rows is None:
        n_head = None
        out_shape = (jax.ShapeDtypeStruct((M, D), f32), jax.ShapeDtypeStruct((M, D), bf16))
        out_specs = (pl.BlockSpec((tm, D), row_map), pl.BlockSpec((tm, D), row_map))
    else:
        assert split_rows % tm == 0 and (M - split_rows) % tm == 0
        n_head = split_rows // tm
        out_shape = (jax.ShapeDtypeStruct((split_rows, D), f32), jax.ShapeDtypeStruct((M - split_rows, D), f32))
        out_specs = (pl.BlockSpec((tm, D), lambda i, j: (jnp.minimum(i, n_head - 1), 0)),
                     pl.BlockSpec((tm, D), lambda i, j: (jnp.maximum(i - n_head, 0), 0)))
    return pl.pallas_call(
        functools.partial(_mlp_body, n_head_blocks=n_head),
        out_shape=out_shape,
        grid=(M // tm, F // tf),
        in_specs=[
            pl.BlockSpec((tm, D), row_map),
            pl.BlockSpec((tm, D), row_map),
            pl.BlockSpec((D, tf), lambda i, j: (0, j)),
            pl.BlockSpec((tf, D), lambda i, j: (j, 0)),
            pl.BlockSpec((1, D), lambda i, j: (0, 0)),
            pl.BlockSpec((1, D), lambda i, j: (0, 0)),
        ],
        out_specs=out_specs,
        scratch_shapes=[pltpu.VMEM((tm, D), f32)],
        compiler_params=_cparams(("arbitrary", "arbitrary")),
        name="mlp_residual_ln",
    )(hb, hf, w1, w2, g, b)


def _kv_project_body(wt_ref, h_ref, cmp_ref, sel_ref, win_ref, selp_ref, winp_ref):
    res = _dot_nt(wt_ref[...], h_ref[...])
    tm = h_ref.shape[0]
    cmp_ref[0] = res[0:KV_ROW]
    sel_ref[0] = res[KV_ROW:2 * KV_ROW]
    win_ref[0] = res[2 * KV_ROW:3 * KV_ROW]
    for k in range(tm // PAGE_SIZE):
        cols = slice(k * PAGE_SIZE, (k + 1) * PAGE_SIZE)
        selp_ref[0, k] = res[KV_ROW:2 * KV_ROW, cols].astype(bf16)
        winp_ref[0, k] = res[2 * KV_ROW:3 * KV_ROW, cols].astype(bf16)


def kv_project_feature_major(h_b, w_t, Bn, L, tm=512):
    nj = L // tm
    pages_per_step = tm // PAGE_SIZE
    fm = jax.ShapeDtypeStruct((Bn, KV_ROW, L), f32)
    pg = jax.ShapeDtypeStruct((Bn, L // PAGE_SIZE, KV_ROW, PAGE_SIZE), bf16)
    fm_spec = pl.BlockSpec((1, KV_ROW, tm), lambda b, j: (b, 0, j))
    pg_spec = pl.BlockSpec((1, pages_per_step, KV_ROW, PAGE_SIZE), lambda b, j: (b, j, 0, 0))
    return pl.pallas_call(
        _kv_project_body,
        out_shape=(fm, fm, fm, pg, pg),
        grid=(Bn, nj),
        in_specs=[
            pl.BlockSpec(w_t.shape, lambda b, j: (0, 0)),
            pl.BlockSpec((tm, h_b.shape[1]), lambda b, j: (b * nj + j, 0)),
        ],
        out_specs=(fm_spec, fm_spec, fm_spec, pg_spec, pg_spec),
        compiler_params=_cparams(("parallel", "parallel")),
        name="kv_project_feature_major",
    )(w_t, h_b)


SSD_SEQS_PER_STEP = 4
SSD_SEQS_PER_STEP_CHUNKED = 2


def _ssd_body(*refs, Q, BB, nc, has_init):
    n_in = 1 if nc == 1 else BB
    xbc_refs, z_refs, dt_refs = refs[0:n_in], refs[n_in:2 * n_in], refs[2 * n_in:3 * n_in]
    refs = refs[3 * n_in:]
    if has_init:
        conv0_ref, h0_ref = refs[0:2]
        refs = refs[2:]
    (cw_ref, cb_ref, dtb_ref, alog_ref, dskx_ref, nw_ref, e64_ref, eq_ref,
     y_ref, hout_ref, cout_ref, st_ref, xpad_ref, xc_ref, ybuf_ref, xdt_ref, xds_ref, ecx_ref, acx_ref) = refs

    def rows_of(row_refs, bb, cols=slice(None)):
        return row_refs[0][bb * Q:(bb + 1) * Q, cols] if nc == 1 else row_refs[bb][:, cols]

    def store_y(bb, cols, value):
        if nc == 1:
            y_ref[bb * Q:(bb + 1) * Q, cols] = value
        else:
            y_ref[bb, :, cols] = value

    c = pl.program_id(1)
    single_chunk = nc == 1
    P, N, R, G = M_HEAD_DIM, M_D_STATE, M_HPG, M_N_GROUPS
    PAD = SUBLANES

    per_head_state = single_chunk and has_init

    def first_chunk():
        for bb in range(BB):
            xpad_ref[bb, 0:PAD, :] = jnp.zeros((PAD, M_CONV_DIM), f32)
            if has_init:
                xpad_ref[bb, PAD - (M_CONV - 1):PAD, :] = conv0_ref[bb]
            if per_head_state:
                continue
            if has_init:
                for g in range(G):
                    for r in range(R):
                        st_ref[bb, g, :, r * P:(r + 1) * P] = h0_ref[bb, g * R + r].T
            else:
                st_ref[bb] = jnp.zeros(st_ref.shape[1:], f32)

    def last_chunk():
        if per_head_state:
            return
        for bb in range(BB):
            for g in range(G):
                for r in range(R):
                    hout_ref[bb, g * R + r] = st_ref[bb, g, :, r * P:(r + 1) * P].T

    if single_chunk:
        first_chunk()
    else:
        pl.when(c == 0)(first_chunk)

    ri = lax.broadcasted_iota(i32, (Q, Q), 0)
    ci = lax.broadcasted_iota(i32, (Q, Q), 1)
    tril = ri >= ci
    GW = M_D_INNER // G
    lane = lax.broadcasted_iota(i32, (Q, LANES), 1)
    third = M_N_HEADS
    assert 3 * third <= LANES

    def spread(v, e_ref):
        hi = v.astype(bf16).astype(f32)
        r1 = v - hi
        mid = r1.astype(bf16).astype(f32)
        lo = r1 - mid
        packed = jnp.where(lane < third, hi, jnp.where(lane < 2 * third, pltpu.roll(mid, third, 1),
                                                       jnp.where(lane < 3 * third, pltpu.roll(lo, 2 * third, 1), 0.0)))
        return _dot(packed.astype(bf16), e_ref[...])

    seqs = range(BB)
    a_cum, a_cum_t, e_last = [], [], []
    for bb in seqs:
        xpad_ref[bb, PAD:PAD + Q, :] = rows_of(xbc_refs, bb)
        acc = cb_ref[...] + xpad_ref[bb, pl.ds(PAD - 3, Q), :] * cw_ref[0:1, :]
        for k in range(1, M_CONV):
            acc = acc + xpad_ref[bb, pl.ds(PAD - 3 + k, Q), :] * cw_ref[k:k + 1, :]
        xc_ref[bb] = _silu(acc)
        cout_ref[bb] = xpad_ref[bb, pl.ds(Q + PAD - 3, 3), :]
        xpad_ref[bb, 0:PAD, :] = xpad_ref[bb, pl.ds(Q, PAD), :]

        xdt = rows_of(dt_refs, bb) + dtb_ref[...]
        dt = jnp.maximum(xdt, 0.0) + jnp.log1p(jnp.exp(-jnp.abs(xdt)))
        a = dt * (-jnp.exp(alog_ref[...]))
        a_cum.append(jnp.dot(tril.astype(f32), a, preferred_element_type=f32, precision=_HI))
        a_cum_t.append(a_cum[bb].T)
        a_last = a_cum[bb][Q - 1:Q, :]
        e_last.append(jnp.exp(a_last))
        xdt_all = xc_ref[bb, :, 0:M_D_INNER] * spread(dt, e64_ref)
        xdt_ref[bb] = xdt_all
        xds_ref[bb] = xdt_all * spread(jnp.exp(a_last - a_cum[bb]), e64_ref)
        ecx_ref[bb] = spread(jnp.exp(a_cum[bb]), e64_ref)
        if Q == LANES:
            acx_ref[bb] = spread(a_cum[bb], eq_ref)

    for g in range(G):
        gcols = slice(g * GW, (g + 1) * GW)
        bg = [xc_ref[bb, :, M_D_INNER + g * N:M_D_INNER + (g + 1) * N].astype(bf16) for bb in seqs]
        cg = [xc_ref[bb, :, M_D_INNER + G * N + g * N:M_D_INNER + G * N + (g + 1) * N].astype(bf16) for bb in seqs]
        gmat = [_dot_nt(cg[bb], bg[bb]) for bb in seqs]
        for r in range(R):
            h = g * R + r
            hcols = slice(h * P, (h + 1) * P)
            for bb in seqs:
                col = acx_ref[bb, :, h * Q:(h + 1) * Q] if Q == LANES else a_cum[bb][:, h:h + 1]
                row = a_cum_t[bb][h:h + 1, :]
                lmat = jnp.exp(jnp.where(tril, col - row, -jnp.inf))
                ydiag = _dot((gmat[bb] * lmat).astype(bf16), xdt_ref[bb, :, hcols].astype(bf16))
                if per_head_state:
                    h_in = h0_ref[bb, h]
                    hout_ref[bb, h] = (h_in * e_last[bb][:, h:h + 1]
                                       + _dot_tn(xds_ref[bb, :, hcols].astype(bf16), bg[bb]))
                    ydiag = ydiag + _dot_nt(cg[bb], h_in.astype(bf16)) * ecx_ref[bb, :, hcols]
                ybuf_ref[bb, :, hcols] = ydiag
        for bb in seqs:
            extra = xc_ref[bb, :, gcols] * dskx_ref[:, gcols]
            if not per_head_state:
                extra = extra + _dot(cg[bb], st_ref[bb, g].astype(bf16)) * ecx_ref[bb, :, gcols]
                new = _dot_tn(bg[bb], xds_ref[bb, :, gcols].astype(bf16))
                st_ref[bb, g] = st_ref[bb, g] * ecx_ref[bb, Q - 1:Q, gcols] + new
            ybuf_ref[bb, :, gcols] += extra

    for g in range(G):
        cols = slice(g * GW, (g + 1) * GW)
        for bb in seqs:
            yg = ybuf_ref[bb, :, cols] * _silu(rows_of(z_refs, bb, cols))
            ms = jnp.mean(yg * yg, axis=-1, keepdims=True)
            store_y(bb, cols, (yg * lax.rsqrt(ms + RMS_EPS) * nw_ref[:, cols]).astype(y_ref.dtype))

    if single_chunk:
        last_chunk()
    else:
        pl.when(c == pl.num_programs(1) - 1)(last_chunk)


def _spread_matrix(width):
    rows = np.arange(LANES)[:, None]
    cols = np.arange(M_N_HEADS * width)[None, :]
    return jnp.asarray((rows < 3 * M_N_HEADS) & (rows % M_N_HEADS == cols // width), bf16)


def ssd_mixer_core(xbc, z, dt, row0, Bn, L, conv0, h0, conv_w, conv_b, dt_bias, a_log, d_skip_x, norm_w):
    Q = M_CHUNK if L % M_CHUNK == 0 else L
    nc = L // Q
    has_init = h0 is not None
    BB = next(n for n in ((SSD_SEQS_PER_STEP if nc == 1 else SSD_SEQS_PER_STEP_CHUNKED), 1) if Bn % n == 0)
    const2 = lambda b, c: (0, 0)
    if nc == 1:
        blk0 = row0 // (BB * Q)
        row_maps = [lambda b, c: (blk0 + b, 0)]
        y_shape, y_block, y_map = (Bn * L, M_D_INNER), (BB * Q, M_D_INNER), (lambda b, c: (b, 0))
    else:
        blk0 = row0 // Q
        row_maps = [functools.partial(lambda b, c, bb: (blk0 + (b * BB + bb) * nc + c, 0), bb=bb) for bb in range(BB)]
        y_shape, y_block, y_map = (Bn, L, M_D_INNER), (BB, Q, M_D_INNER), (lambda b, c: (b, c, 0))
    rows_per_block = BB * Q if nc == 1 else Q
    in_specs, args = [], []
    for arr, width in ((xbc, M_CONV_DIM), (z, M_D_INNER), (dt, LANES)):
        in_specs += [pl.BlockSpec((rows_per_block, width), m) for m in row_maps]
        args += [arr] * len(row_maps)
    if has_init:
        in_specs += [
            pl.BlockSpec((BB, M_CONV - 1, M_CONV_DIM), lambda b, c: (b, 0, 0)),
            pl.BlockSpec((BB, M_N_HEADS, M_HEAD_DIM, M_D_STATE), lambda b, c: (b, 0, 0, 0)),
        ]
        args += [conv0, h0]
    e64 = _spread_matrix(M_HEAD_DIM)
    eq = _spread_matrix(Q if Q == LANES else SUBLANES)
    in_specs += [
        pl.BlockSpec((M_CONV, M_CONV_DIM), const2),
        pl.BlockSpec((1, M_CONV_DIM), const2),
        pl.BlockSpec((1, LANES), const2),
        pl.BlockSpec((1, LANES), const2),
        pl.BlockSpec((1, M_D_INNER), const2),
        pl.BlockSpec((1, M_D_INNER), const2),
        pl.BlockSpec(e64.shape, const2),
        pl.BlockSpec(eq.shape, const2),
    ]
    args += [conv_w, conv_b, dt_bias, a_log, d_skip_x, norm_w, e64, eq]
    y_dtype = bf16 if y_block[-2] % 16 == 0 else f32
    y, new_ssm, new_conv = pl.pallas_call(
        functools.partial(_ssd_body, Q=Q, BB=BB, nc=nc, has_init=has_init),
        out_shape=(
            jax.ShapeDtypeStruct(y_shape, y_dtype),
            jax.ShapeDtypeStruct((Bn, M_N_HEADS, M_HEAD_DIM, M_D_STATE), f32),
            jax.ShapeDtypeStruct((Bn, M_CONV - 1, M_CONV_DIM), f32),
        ),
        grid=(Bn // BB, nc),
        in_specs=in_specs,
        out_specs=(
            pl.BlockSpec(y_block, y_map),
            pl.BlockSpec((BB, M_N_HEADS, M_HEAD_DIM, M_D_STATE), lambda b, c: (b, 0, 0, 0)),
            pl.BlockSpec((BB, M_CONV - 1, M_CONV_DIM), lambda b, c: (b, 0, 0)),
        ),
        scratch_shapes=[
            pltpu.VMEM((1, 1, SUBLANES, LANES) if nc == 1 and has_init
                       else (BB, M_N_GROUPS, M_D_STATE, M_HPG * M_HEAD_DIM), f32),
            pltpu.VMEM((BB, Q + SUBLANES, M_CONV_DIM), f32),
            pltpu.VMEM((BB, Q, M_CONV_DIM), f32),
            pltpu.VMEM((BB, Q, M_D_INNER), f32),
            pltpu.VMEM((BB, Q, M_D_INNER), f32),
            pltpu.VMEM((BB, Q, M_D_INNER), f32),
            pltpu.VMEM((BB, Q, M_D_INNER), f32),
            pltpu.VMEM((BB, Q, M_N_HEADS * Q) if Q == LANES else (1, SUBLANES, LANES), f32),
        ],
        compiler_params=_cparams(("parallel", "arbitrary")),
        name="ssd_mixer_core",
    )(*args)
    return y.reshape(Bn * L, M_D_INNER), new_ssm, new_conv


N_PAGES = 16
N_CHUNKS = N_PAGES * PAGE_SIZE // CMP_STRIDE
N_SLABS = KV_ROW // LANES


def _compress_body(*refs, paged):
    if paged:
        refs = refs[1:]
    pages = refs[:N_PAGES]
    wbd_ref, pe_ref, w1f_ref, w2_ref, w2t_ref, kct_ref, vc_ref, xs_ref = refs[N_PAGES:]
    H = CMP_HIDDEN
    n_steps = CMP_STRIDE // 2
    pages_per_step = N_PAGES // n_steps

    def to_token_major(sl, page_range):
        for p in page_range:
            xs_ref[sl, p * PAGE_SIZE:(p + 1) * PAGE_SIZE, :] = pages[p][0, sl * LANES:(sl + 1) * LANES, :].T

    to_token_major(0, range(N_PAGES))
    row = lax.broadcasted_iota(i32, (N_CHUNKS, H), 0)
    for kv in range(2):
        pe_term = _dot(pe_ref[kv], w1f_ref[kv])[0:1, :]
        for gp in range(A_N_KV // 2):
            sl = kv * (A_N_KV // 2) + gp
            acc = jnp.zeros((N_CHUNKS, 2 * CMP_RATIO * H), f32)
            for jp in range(n_steps):
                x = jnp.concatenate(
                    [xs_ref[sl, pl.ds(2 * jp + k, N_CHUNKS, stride=CMP_STRIDE), :] for k in range(2)], axis=1).astype(bf16)
                acc = acc + _dot(x, wbd_ref[kv, jp])
                if sl + 1 < N_SLABS:
                    to_token_major(sl + 1, range(jp * pages_per_step, (jp + 1) * pages_per_step))
            for gi in range(2):
                g = gp * 2 + gi
                p0 = acc[:, gi * CMP_RATIO * H:gi * CMP_RATIO * H + H]
                p1 = acc[:, gi * CMP_RATIO * H + H:(gi + 1) * CMP_RATIO * H]
                p1_next = jnp.where(row == N_CHUNKS - 1, 0.0, pltpu.roll(p1, N_CHUNKS - 1, 0))
                hid = _silu(p0 + p1_next + pe_term).astype(bf16)
                if kv == 0:
                    kct_ref[0, g * A_HEAD_DIM:(g + 1) * A_HEAD_DIM, :] = _dot_nt(w2t_ref[...], hid).astype(kct_ref.dtype)
                else:
                    vc_ref[0, :, g * A_HEAD_DIM:(g + 1) * A_HEAD_DIM] = _dot(hid, w2_ref[...]).astype(vc_ref.dtype)


def compress_kv(pages, page_table, wbd, pe_rows, w1_flat, w2_v, w2t_k):
    paged = page_table is not None
    Bn = page_table.shape[0] if paged else pages.shape[0]
    if paged:
        page_specs = [
            pl.BlockSpec((1, KV_ROW, PAGE_SIZE), functools.partial(lambda b, pt, j: (pt[b, j], 0, 0), j=j))
            for j in range(N_PAGES)
        ]
        const = lambda nd: (lambda b, pt: (0,) * nd)
        out_map = lambda b, pt: (b, 0, 0)
    else:
        page_specs = [
            pl.BlockSpec((1, KV_ROW, PAGE_SIZE), functools.partial(lambda b, j: (b, 0, j), j=j)) for j in range(N_PAGES)
        ]
        const = lambda nd: (lambda b: (0,) * nd)
        out_map = lambda b: (b, 0, 0)
    in_specs = page_specs + [
        pl.BlockSpec(wbd.shape, const(4)),
        pl.BlockSpec(pe_rows.shape, const(3)),
        pl.BlockSpec(w1_flat.shape, const(3)),
        pl.BlockSpec(w2_v.shape, const(2)),
        pl.BlockSpec(w2t_k.shape, const(2)),
    ]
    out_shape = (jax.ShapeDtypeStruct((Bn, KV_HALF, N_CHUNKS), bf16), jax.ShapeDtypeStruct((Bn, N_CHUNKS, KV_HALF), bf16))
    out_specs = (pl.BlockSpec((1, KV_HALF, N_CHUNKS), out_map), pl.BlockSpec((1, N_CHUNKS, KV_HALF), out_map))
    scratch = [pltpu.VMEM((N_SLABS, N_PAGES * PAGE_SIZE, LANES), f32)]
    body = functools.partial(_compress_body, paged=paged)
    args = ([pages] * N_PAGES) + [wbd, pe_rows, w1_flat, w2_v, w2t_k]
    if paged:
        return pl.pallas_call(
            body, out_shape=out_shape,
            grid_spec=pltpu.PrefetchScalarGridSpec(
                num_scalar_prefetch=1, grid=(Bn,), in_specs=in_specs, out_specs=out_specs, scratch_shapes=scratch),
            compiler_params=_cparams(("parallel",)), name="compress_kv_paged",
        )(page_table, *args)
    return pl.pallas_call(
        body, out_shape=out_shape, grid=(Bn,), in_specs=in_specs, out_specs=out_specs, scratch_shapes=scratch,
        compiler_params=_cparams(("parallel",)), name="compress_kv",
    )(*args)


def _bias_table_body(rb_ref, tbl_ref, tz0_ref, tz1_ref):
    n = lax.broadcasted_iota(i32, (A_N_HEADS, LANES), 1)
    max_exact = N_BUCKETS // 2
    large = max_exact + jnp.floor(jnp.log(jnp.maximum(n, max_exact).astype(f32) / max_exact)
                                  / math.log(MAX_DISTANCE / max_exact) * (N_BUCKETS - max_exact))
    bucket = jnp.where(n < max_exact, n.astype(f32), jnp.minimum(large, N_BUCKETS - 1.0))
    tbl = jnp.zeros((A_N_HEADS, LANES), f32)
    for b in range(N_BUCKETS):
        tbl = jnp.where(bucket == float(b), rb_ref[:, b:b + 1], tbl)
    tbl_ref[...] = tbl
    dist = lax.broadcasted_iota(i32, (LANES, LANES), 0) - lax.broadcasted_iota(i32, (LANES, LANES), 1)
    for h in range(A_N_HEADS):
        row = jnp.broadcast_to(tbl[h:h + 1, :], (LANES, LANES))
        own = jnp.take_along_axis(row, jnp.clip(dist, 0, MAX_DISTANCE - 1), axis=1)
        tz0_ref[h] = jnp.where(dist >= 0, own, MASK_VALUE)
        tz1_ref[h] = jnp.take_along_axis(row, jnp.minimum(dist + LANES, MAX_DISTANCE - 1), axis=1)


def bias_table(rel_bias_t):
    assert MAX_DISTANCE <= LANES
    tile = jax.ShapeDtypeStruct((A_N_HEADS, LANES, LANES), f32)
    return pl.pallas_call(
        _bias_table_body, out_shape=(jax.ShapeDtypeStruct((A_N_HEADS, LANES), f32), tile, tile), name="bias_table",
    )(rel_bias_t)


TILE = 128


WIDE = 2 * TILE
N_SLOTS = N_PAGES * TILE // WIDE
BIG = -MASK_VALUE


SEL, WIN = 0, 1
FAR, NEAR = 0, 1
WIN_SLOT0 = N_SLOTS


def _attn_prompt_body(q_ref, gate_ref, kct_ref, vc_ref, tbl_ref, tz0_ref, tz1_ref, ovlt_ref, selp_ref, winp_ref, o_ref,
                      s_scr, m_scr, l_scr, acc_scr, lhs_scr, oc_scr, fb_scr, *, qb, n_sel):
    i = pl.program_id(1)
    R = A_GROUP * qb
    dh = A_HEAD_DIM
    pos0 = i * qb
    odd = (i % 2) == 1
    td = i // 2
    scale = dh ** -0.5
    n_rank = SUBLANES * (-(-n_sel // SUBLANES))

    lane = lax.broadcasted_iota(i32, (R, TILE), 1)
    q_in_blk = jnp.concatenate([lax.broadcasted_iota(i32, (qb, TILE), 0)] * A_GROUP, axis=0)
    qpos = pos0 + q_in_blk
    neg_tile = jnp.full((R, TILE), MASK_VALUE, f32)

    s_idx = lax.broadcasted_iota(i32, (n_rank, qb), 0)
    s_qpos = pos0 + lax.broadcasted_iota(i32, (n_rank, qb), 1)
    blk = s_qpos // SEL_BLOCK
    sel_valid = s_idx * SEL_BLOCK <= s_qpos
    sel_forced = (s_idx == 0) | (s_idx == blk) | (s_idx == blk - 1)

    f_row = lax.broadcasted_iota(i32, (TILE - dh, WIDE), 0)
    flag_rows = [jnp.where(f_row == k, BIG, 0.0).astype(bf16) for k in range(2)]
    zero_flag = jnp.zeros((TILE - dh, WIDE), bf16)
    zero_drop = jnp.zeros((TILE, WIDE), bf16)
    b_row = lax.broadcasted_iota(i32, (TILE, WIDE), 0)
    b_col = lax.broadcasted_iota(i32, (TILE, WIDE), 1)
    f_lane = lax.broadcasted_iota(i32, (R, TILE - dh), 1)
    flags = jnp.where(f_lane == 0, jnp.where(td < 1, -1.0, 0.0),
                      jnp.where(f_lane == 1, jnp.where(td < 2, -1.0, 0.0), 0.0)).astype(bf16)
    win_thr = q_in_blk + jnp.where(odd, TILE, 0)

    def pair(ref, T, g, half):
        rows = slice(half * KV_HALF + g * dh, half * KV_HALF + (g + 1) * dh)
        return jnp.concatenate([ref[0, 2 * T, rows, :], ref[0, 2 * T + 1, rows, :]], axis=1)

    def drop_rows(T):
        return jnp.where(b_row == T * (WIDE // SEL_BLOCK) + b_col // SEL_BLOCK, BIG, 0.0).astype(bf16)

    def store_scores(g, br, slot, s, left, right, kind, first=False):
        s_l = s[:, 0:TILE] if left is None else s[:, 0:TILE] + left
        s_r = s[:, TILE:WIDE] if right is None else s[:, TILE:WIDE] + right
        s_scr[g, slot, :, 0:TILE] = s_l
        s_scr[g, slot, :, TILE:WIDE] = s_r
        mx = jnp.maximum(s_l, s_r)
        m_scr[g, br, kind] = mx if first else jnp.maximum(m_scr[g, br, kind], mx)

    def accumulate(g, br, slot, kind, v_t, first=False):
        shift = m_scr[g, br, kind]
        p_l = jnp.exp(s_scr[g, slot, :, 0:TILE] - shift)
        p_r = jnp.exp(s_scr[g, slot, :, TILE:WIDE] - shift)
        pv = _dot_nt(jnp.concatenate([p_l, p_r], axis=1).astype(bf16), v_t)
        if first:
            l_scr[g, br] = p_l + p_r
            acc_scr[g, br] = pv
        else:
            l_scr[g, br] += p_l + p_r
            acc_scr[g, br] += pv

    def row_max(g, br):
        fb = fb_scr[g]
        m = jnp.max(jnp.maximum(m_scr[g, br, FAR] + fb, m_scr[g, br, NEAR]), axis=1, keepdims=True)
        m_b = jnp.broadcast_to(m, (R, TILE))
        m_scr[g, br, FAR] = m_b - fb
        m_scr[g, br, NEAR] = m_b

    def result(g, br):
        return acc_scr[g, br] / jnp.sum(l_scr[g, br], axis=1, keepdims=True)

    def near_tiles(g):
        tz0 = tz0_ref[g * A_GROUP:(g + 1) * A_GROUP].reshape(R, TILE)
        tz1 = tz1_ref[g * A_GROUP:(g + 1) * A_GROUP].reshape(R, TILE)
        return jnp.where(odd, fb_scr[g], tz1), jnp.where(odd, tz1, tz0), jnp.where(odd, tz0, neg_tile)

    t_prev = jnp.maximum(td - 1, 0)
    t_first = jnp.maximum(td - 2, 0)
    n_far = jnp.maximum(td - 1, 0)

    scores = []
    for g in range(A_N_KV):
        heads = [g * A_GROUP + r for r in range(A_GROUP)]
        qg = jnp.concatenate([q_ref[:, h * dh:(h + 1) * dh] for h in heads], axis=0)
        qg = (qg.astype(f32) * scale).astype(bf16)
        tbl_g = jnp.concatenate([jnp.broadcast_to(tbl_ref[h:h + 1, :], (qb, TILE)) for h in heads], axis=0)
        fb_scr[g] = jnp.concatenate(
            [jnp.broadcast_to(tbl_ref[h:h + 1, MAX_DISTANCE - 1:MAX_DISTANCE], (qb, TILE)) for h in heads], axis=0)

        dist_c = qpos - (lane * CMP_STRIDE + (CMP_LEN - 1))
        s_c = _dot(qg, kct_ref[0, g * dh:(g + 1) * dh, :])
        s_c = s_c + jnp.take_along_axis(tbl_g, jnp.clip(dist_c, 0, MAX_DISTANCE - 1), axis=1)
        s_c = jnp.where(dist_c >= 0, s_c, MASK_VALUE)
        m_c = jnp.max(s_c, axis=1, keepdims=True)
        p_c = jnp.where(dist_c >= 0, jnp.exp(s_c - m_c), 0.0)
        l_c = jnp.sum(p_c, axis=1, keepdims=True)
        l_c = jnp.where(l_c == 0.0, 1.0, l_c)
        oc_scr[g] = _dot(p_c.astype(bf16), vc_ref[0, :, g * dh:(g + 1) * dh]) / l_c
        p_c = p_c / l_c

        p_sum = p_c[0:qb]
        for r in range(1, A_GROUP):
            p_sum = p_sum + p_c[r * qb:(r + 1) * qb]
        imp_t = lax.dot_general(ovlt_ref[0:n_rank, :], p_sum, (((1,), (1,)), ((), ())),
                                preferred_element_type=f32, precision=_HI)
        score = jnp.where(sel_valid, imp_t + jnp.where(sel_forced, FORCE_SCORE, 0.0), -1.0)
        scores.append(jnp.where(s_idx < n_sel, score, -3.0))
        lhs_scr[g, WIN] = jnp.concatenate([qg, flags, jnp.zeros((R, TILE), bf16)], axis=1)
        m_scr[g, SEL, FAR] = neg_tile

    too_old_l = jnp.where(lane > win_thr, 0.0, MASK_VALUE)
    too_old_r = jnp.where(lane + TILE > win_thr, 0.0, MASK_VALUE)
    for g in range(A_N_KV):
        prev_right, diag_left, diag_right = near_tiles(g)
        lhs_win = lhs_scr[g, WIN]
        s = _dot(lhs_win, jnp.concatenate([pair(winp_ref, t_first, g, 0), flag_rows[1], zero_drop], axis=0))
        store_scores(g, WIN, WIN_SLOT0, s, too_old_l, too_old_r, FAR, first=True)
        s = _dot(lhs_win, jnp.concatenate([pair(winp_ref, t_prev, g, 0), flag_rows[0], zero_drop], axis=0))
        store_scores(g, WIN, WIN_SLOT0 + 1, s, fb_scr[g], prev_right, NEAR, first=True)
        s = _dot(lhs_win, jnp.concatenate([pair(winp_ref, td, g, 0), zero_flag, zero_drop], axis=0))
        store_scores(g, WIN, WIN_SLOT0 + 2, s, diag_left, diag_right, NEAR)

    ranks = [jnp.zeros((n_rank, qb), f32) for _ in range(A_N_KV)]
    for s2 in range(n_sel):
        for g in range(A_N_KV):
            other = scores[g][s2:s2 + 1, :]
            ranks[g] = ranks[g] + jnp.where((other > scores[g]) | ((other == scores[g]) & (s_idx > s2)), 1.0, 0.0)
    for g in range(A_N_KV):
        dropped_t = jnp.where(ranks[g] < float(min(SEL_TOPN, n_sel)), 0.0, -1.0)
        dropped = jnp.concatenate([dropped_t, jnp.zeros((TILE - n_rank, qb), f32)], axis=0).T
        lhs_scr[g, SEL] = jnp.concatenate(
            [lhs_scr[g, WIN, :, 0:TILE], jnp.concatenate([dropped] * A_GROUP, axis=0).astype(bf16)], axis=1)

    for g in range(A_N_KV):
        prev_right, diag_left, diag_right = near_tiles(g)
        lhs_sel = lhs_scr[g, SEL]
        s = _dot(lhs_sel, jnp.concatenate([pair(selp_ref, t_prev, g, 0), flag_rows[0], drop_rows(t_prev)], axis=0))
        store_scores(g, SEL, N_SLOTS - 2, s, fb_scr[g], prev_right, NEAR, first=True)
        s = _dot(lhs_sel, jnp.concatenate([pair(selp_ref, td, g, 0), zero_flag, drop_rows(td)], axis=0))
        store_scores(g, SEL, N_SLOTS - 1, s, diag_left, diag_right, NEAR)

    def far_scores(T, carry):
        drop = drop_rows(T)
        for g in range(A_N_KV):
            s = _dot(lhs_scr[g, SEL], jnp.concatenate([pair(selp_ref, T, g, 0), zero_flag, drop], axis=0))
            store_scores(g, SEL, T, s, None, None, FAR)
        return carry

    lax.fori_loop(0, n_far, far_scores, 0)

    for g in range(A_N_KV):
        row_max(g, WIN)
        row_max(g, SEL)
    for g in range(A_N_KV):
        accumulate(g, WIN, WIN_SLOT0, FAR, pair(winp_ref, t_first, g, 1), first=True)
        accumulate(g, WIN, WIN_SLOT0 + 1, NEAR, pair(winp_ref, t_prev, g, 1))
        accumulate(g, WIN, WIN_SLOT0 + 2, NEAR, pair(winp_ref, td, g, 1))
        accumulate(g, SEL, N_SLOTS - 2, NEAR, pair(selp_ref, t_prev, g, 1), first=True)
        accumulate(g, SEL, N_SLOTS - 1, NEAR, pair(selp_ref, td, g, 1))

    def far_accumulate(T, carry):
        for g in range(A_N_KV):
            accumulate(g, SEL, T, FAR, pair(selp_ref, T, g, 1))
        return carry

    lax.fori_loop(0, n_far, far_accumulate, 0)

    for g in range(A_N_KV):
        o_c, o_s, o_w = oc_scr[g], result(g, SEL), result(g, WIN)
        for r in range(A_GROUP):
            h = g * A_GROUP + r
            rows = slice(r * qb, (r + 1) * qb)
            gt = gate_ref[:, 3 * h:3 * h + 3]
            o_h = gt[:, 0:1] * o_c[rows] + gt[:, 1:2] * o_s[rows] + gt[:, 2:3] * o_w[rows]
            o_ref[:, h * dh:(h + 1) * dh] = o_h.astype(o_ref.dtype)


def nsa_attention_prompt(q, gates, Bn, L, kct, vc, tbl, tz0, tz1, overlap_t, sel_pages, win_pages):
    qb = Q_BLOCK
    nqb = L // qb
    n_sel = -(-L // SEL_BLOCK)
    n_tiles = L // TILE
    assert n_tiles == N_PAGES and qb == TILE
    R = A_GROUP * qb
    row_map = lambda b, i: (b * nqb + i, 0)
    seq_map3 = lambda b, i: (b, 0, 0)
    seq_map4 = lambda b, i: (b, 0, 0, 0)
    const2 = lambda b, i: (0, 0)
    const3 = lambda b, i: (0, 0, 0)
    return pl.pallas_call(
        functools.partial(_attn_prompt_body, qb=qb, n_sel=n_sel),
        out_shape=jax.ShapeDtypeStruct((Bn * L, A_Q_DIM), bf16),
        grid=(Bn, nqb),
        in_specs=[
            pl.BlockSpec((qb, A_Q_DIM), row_map),
            pl.BlockSpec((qb, LANES), row_map),
            pl.BlockSpec((1, KV_HALF, N_CHUNKS), seq_map3),
            pl.BlockSpec((1, N_CHUNKS, KV_HALF), seq_map3),
            pl.BlockSpec((A_N_HEADS, LANES), const2),
            pl.BlockSpec((A_N_HEADS, TILE, TILE), const3),
            pl.BlockSpec((A_N_HEADS, TILE, TILE), const3),
            pl.BlockSpec((TILE, TILE), const2),
            pl.BlockSpec((1, n_tiles, KV_ROW, TILE), seq_map4),
            pl.BlockSpec((1, n_tiles, KV_ROW, TILE), seq_map4),
        ],
        out_specs=pl.BlockSpec((qb, A_Q_DIM), row_map),
        scratch_shapes=[
            pltpu.VMEM((A_N_KV, N_SLOTS + 3, R, WIDE), f32),
            pltpu.VMEM((A_N_KV, 2, 2, R, TILE), f32),
            pltpu.VMEM((A_N_KV, 2, R, TILE), f32),
            pltpu.VMEM((A_N_KV, 2, R, A_HEAD_DIM), f32),
            pltpu.VMEM((A_N_KV, 2, R, WIDE), bf16),
            pltpu.VMEM((A_N_KV, R, A_HEAD_DIM), f32),
            pltpu.VMEM((A_N_KV, R, TILE), f32),
        ],
        compiler_params=_cparams(("parallel", "arbitrary")),
        name="nsa_attention_prompt",
    )(q, gates, kct, vc, tbl, tz0, tz1, overlap_t, sel_pages, win_pages)


def _attn_sample_body(pt_ref, q_ref, gate_ref, kct_ref, vc_ref, tbl_ref, ovl_ref, *refs, qb, start, n_sel):
    del pt_ref
    sel_pages = refs[:N_PAGES]
    selnew_ref, winpast_ref, winnew_ref, o_ref, s_scr = refs[N_PAGES:]
    dh = A_HEAD_DIM
    RG = A_GROUP * qb
    R = A_N_KV * RG
    scale = dh ** -0.5
    n_win_past = WINDOW // TILE
    assert R == TILE and n_sel <= SEL_BLOCK and start == N_PAGES * TILE

    qs = q_ref[...] * scale
    blocks = []
    for g in range(A_N_KV):
        qg = jnp.concatenate([qs[:, (g * A_GROUP + r) * dh:(g * A_GROUP + r + 1) * dh] for r in range(A_GROUP)], axis=0)
        parts = [qg if gg == g else jnp.zeros((RG, dh), f32) for gg in range(A_N_KV)]
        blocks.append(jnp.concatenate(parts, axis=1))
    qbd = jnp.concatenate(blocks, axis=0).astype(bf16)

    tbl_rows = jnp.concatenate([jnp.broadcast_to(tbl_ref[h:h + 1, :], (qb, LANES)) for h in range(A_N_HEADS)], axis=0)
    far_bias = tbl_rows[:, MAX_DISTANCE - 1:MAX_DISTANCE]
    lane = lax.broadcasted_iota(i32, (R, TILE), 1)
    row = lax.broadcasted_iota(i32, (R, TILE), 0)
    qpos = start + row % qb

    def near_bias(dist):
        return jnp.take_along_axis(tbl_rows, jnp.clip(dist, 0, MAX_DISTANCE - 1), axis=1)

    def softmax_rows(s):
        m = jnp.max(s, axis=1, keepdims=True)
        p = jnp.where(s > 0.5 * MASK_VALUE, jnp.exp(s - m), 0.0)
        l = jnp.sum(p, axis=1, keepdims=True)
        return p, jnp.where(l == 0.0, 1.0, l)

    def pad_rows(x):
        return jnp.concatenate([x, jnp.zeros((TILE - x.shape[0], x.shape[1]), x.dtype)], axis=0).astype(bf16)

    def branch_scores(tiles, new_ref, extra, col0):
        n = len(tiles)
        for j, (kv_t, kind) in enumerate(tiles):
            s = _dot(qbd, kv_t(0).astype(bf16)) + extra(j)
            dist = qpos - (start - (n - j) * TILE + lane)
            s = s + (near_bias(dist) if kind == "near" else far_bias)
            if kind == "edge":
                s = jnp.where(dist < WINDOW, s, MASK_VALUE)
            s_scr[:, col0 + j * TILE:col0 + (j + 1) * TILE] = s
        dist = qpos - (start + lane)
        s = _dot_nt(qbd, pad_rows(new_ref[:, 0:KV_HALF])) + extra(n) + near_bias(dist)
        s_scr[:, col0 + n * TILE:col0 + (n + 1) * TILE] = jnp.where(dist >= 0, s, MASK_VALUE)

    def branch_output(tiles, new_ref, col0):
        n = len(tiles)
        p, l = softmax_rows(s_scr[:, col0:col0 + (n + 1) * TILE])
        p = p.astype(bf16)
        o = _dot(p[:, n * TILE:(n + 1) * TILE], pad_rows(new_ref[:, KV_HALF:KV_ROW]))
        for j, (kv_t, _) in enumerate(tiles):
            o = o + _dot_nt(p[:, j * TILE:(j + 1) * TILE], kv_t(1).astype(bf16))
        return o / l

    def page_getter(ref, cols=slice(None)):
        return lambda half: ref[0, half * KV_HALF:(half + 1) * KV_HALF, cols]

    sel_tiles = [(page_getter(sel_pages[t]), "near" if t == N_PAGES - 1 else "far") for t in range(N_PAGES)]
    win_tiles = [(page_getter(winpast_ref, slice(j * TILE, (j + 1) * TILE)),
                  "edge" if j == 0 else ("near" if j == n_win_past - 1 else "far")) for j in range(n_win_past)]
    win_col0 = (N_PAGES + 1) * TILE

    def sel_scores(extra):
        branch_scores(sel_tiles, selnew_ref, extra, 0)

    branch_scores(win_tiles, winnew_ref, lambda j: 0.0, win_col0)

    dist_c = qpos - (lane * CMP_STRIDE + (CMP_LEN - 1))
    s_c = _dot(qbd, kct_ref[0]) + near_bias(dist_c)
    p_c, l_c = softmax_rows(jnp.where(dist_c >= 0, s_c, MASK_VALUE))
    o_c = _dot(p_c.astype(bf16), vc_ref[0]) / l_c
    p_c = p_c / l_c

    p_sum = []
    for g in range(A_N_KV):
        acc = p_c[g * RG:g * RG + qb]
        for r in range(1, A_GROUP):
            acc = acc + p_c[g * RG + r * qb:g * RG + (r + 1) * qb]
        p_sum.append(acc)
    p_sum = jnp.concatenate(p_sum, axis=0)
    imp = jnp.dot(p_sum, ovl_ref[...], preferred_element_type=f32, precision=_HI)
    s_lane = lax.broadcasted_iota(i32, (A_N_KV * qb, TILE), 1)
    s_qpos = start + lax.broadcasted_iota(i32, (A_N_KV * qb, TILE), 0) % qb
    blk = s_qpos // SEL_BLOCK
    valid = s_lane * SEL_BLOCK <= s_qpos
    forced = (s_lane == 0) | (s_lane == blk) | (s_lane == blk - 1)
    score = jnp.where(valid, imp + jnp.where(forced, FORCE_SCORE, 0.0), -1.0)
    score = jnp.where(s_lane < n_sel, score, -3.0)
    rank = jnp.zeros(score.shape, f32)
    for s2 in range(n_sel):
        col = score[:, s2:s2 + 1]
        rank = rank + jnp.where((col > score) | ((col == score) & (s_lane > s2)), 1.0, 0.0)
    not_chosen = jnp.where(rank < float(min(SEL_TOPN, n_sel)), 0.0, -1.0)
    drop = jnp.concatenate(
        [not_chosen[g * qb:(g + 1) * qb] for g in range(A_N_KV) for _ in range(A_GROUP)], axis=0)
    drop = drop[:, 0:SEL_BLOCK].astype(bf16)
    b_row = lax.broadcasted_iota(i32, (SEL_BLOCK, TILE), 0)
    b_col = lax.broadcasted_iota(i32, (SEL_BLOCK, TILE), 1)

    def drop_unselected(t):
        expand = jnp.where(b_row == t * (TILE // SEL_BLOCK) + b_col // SEL_BLOCK, -MASK_VALUE, 0.0).astype(bf16)
        return _dot(drop, expand)

    sel_scores(lambda j: drop_unselected(j))
    o_w = branch_output(win_tiles, winnew_ref, win_col0)
    o_s = branch_output(sel_tiles, selnew_ref, 0)

    for g in range(A_N_KV):
        for r in range(A_GROUP):
            h = g * A_GROUP + r
            rows = slice(g * RG + r * qb, g * RG + (r + 1) * qb)
            cols = slice(g * dh, (g + 1) * dh)
            gt = gate_ref[:, 3 * h:3 * h + 3]
            o_h = gt[:, 0:1] * o_c[rows, cols] + gt[:, 1:2] * o_s[rows, cols] + gt[:, 2:3] * o_w[rows, cols]
            o_ref[:, h * dh:(h + 1) * dh] = o_h.astype(o_ref.dtype)


def nsa_attention_sample(q, gates, row0, Bn, L, start, kct, vc, tbl, overlap, sel_pages, page_table, sel_new,
                         win_past, win_new):
    qb = L
    n_sel = -(-(start + L) // SEL_BLOCK)
    blk0 = row0 // qb
    row_map = lambda b, pt: (blk0 + b, 0)
    seq_map = lambda b, pt: (b, 0, 0)
    page_specs = [
        pl.BlockSpec((1, KV_ROW, PAGE_SIZE), functools.partial(lambda b, pt, j: (pt[b, j], 0, 0), j=j))
        for j in range(N_PAGES)
    ]
    return pl.pallas_call(
        functools.partial(_attn_sample_body, qb=qb, start=start, n_sel=n_sel),
        out_shape=jax.ShapeDtypeStruct((Bn * L, A_Q_DIM), f32),
        grid_spec=pltpu.PrefetchScalarGridSpec(
            num_scalar_prefetch=1,
            grid=(Bn,),
            in_specs=[
                pl.BlockSpec((qb, A_Q_DIM), row_map),
                pl.BlockSpec((qb, LANES), row_map),
                pl.BlockSpec((1, KV_HALF, N_CHUNKS), seq_map),
                pl.BlockSpec((1, N_CHUNKS, KV_HALF), seq_map),
                pl.BlockSpec((A_N_HEADS, LANES), lambda b, pt: (0, 0)),
                pl.BlockSpec((TILE, TILE), lambda b, pt: (0, 0)),
            ] + page_specs + [
                pl.BlockSpec((qb, KV_ROW), row_map),
                pl.BlockSpec((1, KV_ROW, WINDOW), seq_map),
                pl.BlockSpec((qb, KV_ROW), row_map),
            ],
            out_specs=pl.BlockSpec((qb, A_Q_DIM), lambda b, pt: (b, 0)),
            scratch_shapes=[pltpu.VMEM((TILE, (N_PAGES + 1 + WINDOW // TILE + 1) * TILE), f32)],
        ),
        compiler_params=_cparams(("parallel",)),
        name="nsa_attention_sample",
    )(page_table, q, gates, kct, vc, tbl, overlap, *([sel_pages] * N_PAGES), sel_new, win_past, win_new)


WINDOW_UPDATE_SEQS_PER_STEP = 4


def _window_update_body(old_ref, new_ref, place_ref, o_ref, *, n_new):
    lane = lax.broadcasted_iota(i32, (KV_ROW, TILE), 1)
    for bb in range(old_ref.shape[0]):
        x = old_ref[bb]
        shifted = pltpu.roll(x, WINDOW - n_new, 1)
        new_t = lax.dot_general(new_ref[bb * n_new:(bb + 1) * n_new, :], place_ref[...], (((0,), (0,)), ((), ())),
                                preferred_element_type=f32, precision=_HI)
        o_ref[bb, :, 0:WINDOW - TILE] = shifted[:, 0:WINDOW - TILE]
        o_ref[bb, :, WINDOW - TILE:WINDOW] = jnp.where(lane >= TILE - n_new, new_t, shifted[:, WINDOW - TILE:WINDOW])


def window_update(old_t, new_rows):
    Bn = old_t.shape[0]
    n_new = new_rows.shape[0] // Bn
    BB = next(n for n in (WINDOW_UPDATE_SEQS_PER_STEP, 1) if Bn % n == 0)
    place = jnp.asarray(np.eye(n_new, TILE, k=TILE - n_new, dtype=np.float32))
    return pl.pallas_call(
        functools.partial(_window_update_body, n_new=n_new),
        out_shape=jax.ShapeDtypeStruct(old_t.shape, f32),
        grid=(Bn // BB,),
        in_specs=[
            pl.BlockSpec((BB, KV_ROW, WINDOW), lambda b: (b, 0, 0)),
            pl.BlockSpec((BB * n_new, KV_ROW), lambda b: (b, 0)),
            pl.BlockSpec((n_new, TILE), lambda b: (0, 0)),
        ],
        out_specs=pl.BlockSpec((BB, KV_ROW, WINDOW), lambda b: (b, 0, 0)),
        compiler_params=_cparams(("parallel",)),
        name="window_update",
    )(old_t, new_rows, place)


def _pad_lanes(v):
    return jnp.pad(v, (0, LANES - v.shape[0])).reshape(1, LANES)


def _overlap_matrix():
    n_cmp = N_CHUNKS - CMP_RATIO + 1
    c = np.arange(TILE)[:, None] * CMP_STRIDE
    s = np.arange(TILE)[None, :] * SEL_BLOCK
    ov = (c < s + SEL_BLOCK) & (c + CMP_LEN > s) & (np.arange(TILE)[:, None] < n_cmp)
    return jnp.asarray(ov.astype(np.float32))


def _feature_major(x):
    lead = x.shape[:-4]
    n = len(lead)
    return jnp.transpose(x, tuple(range(n)) + (n + 1, n + 2, n + 3, n)).reshape(lead + (KV_ROW, x.shape[-4]))


def _token_major(x_t):
    B, _, T = x_t.shape
    return jnp.transpose(x_t.reshape(B, 2, A_N_KV, A_HEAD_DIM, T), (0, 4, 1, 2, 3))


def kernel(x_prompt, x_sample, state_ssm, state_conv, cache_cmp_kv, cache_sel_kv, cache_win_kv, page_table, ln_g, ln_b, m_in_w, m_conv_w, m_conv_b, m_dt_bias, m_a_log, m_d, m_norm_w, m_out_w, kv_w, cmp_w1, cmp_pe, cmp_w2, q_w, o_w, rel_bias, mlp_w1, mlp_w2):
    Bp, Lp, D = x_prompt.shape
    Bs, Ls, _ = x_sample.shape
    NP, NS = Bp * Lp, Bs * Ls
    past_len = page_table.shape[1] * PAGE_SIZE
    assert past_len == N_PAGES * PAGE_SIZE and Lp == N_PAGES * PAGE_SIZE and cache_win_kv.shape[1] == WINDOW

    in_w = m_in_w[0].astype(bf16)
    z_w = in_w[:, :M_D_INNER]
    xbc_w = in_w[:, M_D_INNER:M_D_INNER + M_CONV_DIM]
    dt_w = jnp.pad(in_w[:, M_D_INNER + M_CONV_DIM:], ((0, 0), (0, LANES - M_N_HEADS)))
    kvw = kv_w.astype(bf16)
    qw = q_w[0].astype(bf16)
    gate_w = jnp.pad(qw[:, A_Q_DIM:], ((0, 0), (0, LANES - 3 * A_N_HEADS)))
    w1b = cmp_w1.astype(bf16)
    w_j = jnp.transpose(w1b, (0, 2, 3, 1, 4)).reshape(2, CMP_STRIDE, A_HEAD_DIM, CMP_RATIO * CMP_HIDDEN)
    zeros = jnp.zeros_like(w_j)
    wbd = jnp.concatenate([jnp.concatenate([w_j, zeros], axis=3), jnp.concatenate([zeros, w_j], axis=3)], axis=2)
    wbd = wbd.reshape(2, CMP_STRIDE // 2, 2 * LANES, 2 * CMP_RATIO * CMP_HIDDEN)
    pe_rows = jnp.broadcast_to(cmp_pe.astype(bf16).reshape(2, 1, CMP_LEN * A_HEAD_DIM), (2, SUBLANES, CMP_LEN * A_HEAD_DIM))
    w1_flat = w1b.reshape(2, CMP_LEN * A_HEAD_DIM, CMP_HIDDEN)
    cmp_w = (wbd, pe_rows, w1_flat, cmp_w2[1].astype(bf16), cmp_w2[0].T.astype(bf16))

    x_p, x_s = x_prompt.reshape(NP, D), x_sample.reshape(NS, D)
    xb = jnp.concatenate([x_p.astype(bf16), x_s.astype(bf16)], axis=0)
    z = matmul(xb, z_w, f32)
    xbc = matmul(xb, xbc_w, f32)
    dt = matmul(xb, dt_w, f32)
    ssd_w = (m_conv_w[0], m_conv_b[0].reshape(1, -1), _pad_lanes(m_dt_bias[0]), _pad_lanes(m_a_log[0]),
             jnp.repeat(m_d[0], M_HEAD_DIM).reshape(1, -1), m_norm_w[0].reshape(1, -1))
    y_p, p_ssm, p_conv = ssd_mixer_core(xbc, z, dt, 0, Bp, Lp, None, None, *ssd_w)
    y_s, s_ssm, s_conv = ssd_mixer_core(xbc, z, dt, NP, Bs, Ls, state_conv[0], state_ssm[0], *ssd_w)
    h_f, h_b = matmul_residual_ln(y_p, y_s, m_out_w[0].astype(bf16), x_p, x_s,
                                  ln_g[0, 0].reshape(1, D), ln_b[0, 0].reshape(1, D))
    h_f, h_b = mlp_residual_ln(h_b, h_f, mlp_w1[0].astype(bf16), mlp_w2[0].astype(bf16),
                               ln_g[0, 1].reshape(1, D), ln_b[0, 1].reshape(1, D))

    cmp_t, sel_t, win_t, sel_pg, win_pg = kv_project_feature_major(h_b, kvw.T, Bp, Lp)
    kv_s = matmul(h_b[NP:], kvw, f32)
    cmp_s, sel_s, win_s = kv_s[:, 0:KV_ROW], kv_s[:, KV_ROW:2 * KV_ROW], kv_s[:, 2 * KV_ROW:3 * KV_ROW]
    kct_p, vc_p = compress_kv(cmp_t, None, *cmp_w)
    kct_s, vc_s = compress_kv(_feature_major(cache_cmp_kv), page_table, *cmp_w)

    q = matmul(h_b, qw[:, :A_Q_DIM], f32)
    gates = matmul(h_b, gate_w, f32, act="sigmoid")
    tbl, tz0, tz1 = bias_table(rel_bias.T)
    overlap = _overlap_matrix()
    o_p = nsa_attention_prompt(q, gates, Bp, Lp, kct_p, vc_p, tbl, tz0, tz1, overlap.T, sel_pg, win_pg)
    win_cache_t = _feature_major(cache_win_kv)
    o_s = nsa_attention_sample(q[NP:], gates[NP:], 0, Bs, Ls, past_len, kct_s, vc_s, tbl, overlap,
                               _feature_major(cache_sel_kv), page_table, sel_s, win_cache_t, win_s)
    h_f, h_b = matmul_residual_ln(o_p, o_s, o_w[0].astype(bf16), h_f, h_f[NP:],
                                  ln_g[1, 0].reshape(1, D), ln_b[1, 0].reshape(1, D))
    out_p, out_s = mlp_residual_ln(h_b, h_f, mlp_w1[1].astype(bf16), mlp_w2[1].astype(bf16),
                                   ln_g[1, 1].reshape(1, D), ln_b[1, 1].reshape(1, D), split_rows=NP)

    kv_shape = (2, A_N_KV, A_HEAD_DIM)
    n_keep = min(WINDOW, Lp)
    s_win = _token_major(window_update(win_cache_t, win_s))
    return (
        out_p.reshape(Bp, Lp, D), out_s.reshape(Bs, Ls, D),
        p_ssm[None], p_conv[None],
        _token_major(cmp_t), _token_major(sel_t), _token_major(win_t[:, :, Lp - n_keep:]),
        s_ssm[None], s_conv[None],
        cmp_s.reshape((Bs, Ls) + kv_shape), sel_s.reshape((Bs, Ls) + kv_shape), s_win,
    )
```

```python
import functools
import math

import jax
import jax.numpy as jnp
import numpy as np
from jax import lax
from jax.experimental import pallas as pl
from jax.experimental.pallas import tpu as pltpu

f32 = jnp.float32
bf16 = jnp.bfloat16
i32 = jnp.int32

D_MODEL = 1024
DEPTH = 2
DN_ALPHA = (2.0 * DEPTH) ** 0.25
LN_EPS = 1e-5
RMS_EPS = 1e-5
D_FF = 4 * D_MODEL
M_D_INNER = 2 * D_MODEL
M_HEAD_DIM = 64
M_N_HEADS = M_D_INNER // M_HEAD_DIM
M_N_GROUPS = 4
M_HPG = M_N_HEADS // M_N_GROUPS
M_D_STATE = 128
M_CONV = 4
M_CHUNK = 128
M_CONV_DIM = M_D_INNER + 2 * M_N_GROUPS * M_D_STATE
A_HEAD_DIM = 64
A_N_HEADS = D_MODEL // A_HEAD_DIM
A_N_KV = 4
A_GROUP = A_N_HEADS // A_N_KV
A_Q_DIM = A_N_HEADS * A_HEAD_DIM
KV_HALF = A_N_KV * A_HEAD_DIM
KV_ROW = 2 * KV_HALF
CMP_LEN = 32
CMP_STRIDE = 16
CMP_RATIO = CMP_LEN // CMP_STRIDE
CMP_HIDDEN = 2 * A_HEAD_DIM
SEL_BLOCK = 64
SEL_TOPN = 16
WINDOW = 512
Q_BLOCK = 128
N_BUCKETS = 32
MAX_DISTANCE = 128
MASK_VALUE = -1e30
FORCE_SCORE = 1e3
PAGE_SIZE = 128

LANES = 128
SUBLANES = 8
VMEM_LIMIT = 56 * 1024 * 1024

_HI = lax.Precision.HIGHEST


def _cparams(sem):
    return pltpu.CompilerParams(dimension_semantics=sem, vmem_limit_bytes=VMEM_LIMIT)


def _dot(a, b):
    return jnp.dot(a, b, preferred_element_type=f32)


def _dot_nt(a, b):
    return lax.dot_general(a, b, (((1,), (1,)), ((), ())), preferred_element_type=f32)


def _dot_tn(a, b):
    return lax.dot_general(a, b, (((0,), (0,)), ((), ())), preferred_element_type=f32)


def _silu(x):
    return x * (1.0 / (1.0 + jnp.exp(-x)))


def _layer_norm(x, g, b):
    mu = jnp.mean(x, axis=-1, keepdims=True)
    xc = x - mu
    var = jnp.mean(xc * xc, axis=-1, keepdims=True)
    return xc * lax.rsqrt(var + LN_EPS) * g + b


def _mm_body(x_ref, w_ref, o_ref, *, act):
    y = _dot(x_ref[...], w_ref[...])
    if act == "sigmoid":
        y = 1.0 / (1.0 + jnp.exp(-y))
    o_ref[...] = y.astype(o_ref.dtype)


def matmul(x, w, out_dtype, act=None, tm=1024, tn=1024):
    M, K = x.shape
    N = w.shape[1]
    tn = next(t for t in (tn, 512, 256, LANES) if N % t == 0)
    return pl.pallas_call(
        functools.partial(_mm_body, act=act),
        out_shape=jax.ShapeDtypeStruct((M, N), out_dtype),
        grid=(M // tm, N // tn),
        in_specs=[pl.BlockSpec((tm, K), lambda i, j: (i, 0)), pl.BlockSpec((K, tn), lambda i, j: (0, j))],
        out_specs=pl.BlockSpec((tm, tn), lambda i, j: (i, j)),
        compiler_params=_cparams(("parallel", "parallel")),
        name="matmul",
    )(x, w)


def _mm_res_ln_body(x0_ref, x1_ref, w_ref, r0_ref, r1_ref, g_ref, b_ref, of_ref, ob_ref, *, n_head_blocks):
    i = pl.program_id(0)
    w = w_ref[...]
    head = i < n_head_blocks
    x = jnp.where(head, x0_ref[...], x1_ref[...].astype(bf16))
    y = DN_ALPHA * jnp.where(head, r0_ref[...], r1_ref[...]) + _dot(x, w)
    h = _layer_norm(y, g_ref[...], b_ref[...])
    of_ref[...] = h
    ob_ref[...] = h.astype(bf16)


def matmul_residual_ln(x_head, x_tail, w, resid_head, resid_tail, g, b, tm=1024):
    M0, K = x_head.shape
    M1 = x_tail.shape[0]
    N = w.shape[1]
    assert M0 % tm == 0 and M1 % tm == 0 and resid_head.shape[0] >= M0 and resid_tail.shape[0] == M1
    n_head = M0 // tm
    row_map = lambda i: (i, 0)
    head_map = lambda i: (jnp.minimum(i, n_head - 1), 0)
    tail_map = lambda i: (jnp.maximum(i - n_head, 0), 0)
    const = lambda i: (0, 0)
    once = pl.Buffered(1) if M1 == tm else None
    return pl.pallas_call(
        functools.partial(_mm_res_ln_body, n_head_blocks=n_head),
        out_shape=(jax.ShapeDtypeStruct((M0 + M1, N), f32), jax.ShapeDtypeStruct((M0 + M1, N), bf16)),
        grid=((M0 + M1) // tm,),
        in_specs=[
            pl.BlockSpec((tm, K), head_map),
            pl.BlockSpec((tm, K), tail_map, pipeline_mode=once),
            pl.BlockSpec((K, N), const, pipeline_mode=pl.Buffered(1)),
            pl.BlockSpec((tm, N), head_map),
            pl.BlockSpec((tm, N), tail_map, pipeline_mode=once),
            pl.BlockSpec((1, N), const),
            pl.BlockSpec((1, N), const),
        ],
        out_specs=(pl.BlockSpec((tm, N), row_map), pl.BlockSpec((tm, N), row_map)),
        compiler_params=_cparams(("arbitrary",)),
        name="matmul_residual_ln",
    )(x_head, x_tail, w, resid_head, resid_tail, g, b)


def _mlp_body(hb_ref, hf_ref, w1_ref, w2_ref, g_ref, b_ref, o0_ref, o1_ref, acc_ref, *, n_head_blocks):
    i = pl.program_id(0)
    j = pl.program_id(1)

    @pl.when(j == 0)
    def _():
        acc_ref[...] = jnp.zeros_like(acc_ref)

    u = jnp.maximum(_dot(hb_ref[...], w1_ref[...]), 0.0)
    acc_ref[...] += _dot((u * u).astype(bf16), w2_ref[...])
    last = j == pl.num_programs(1) - 1

    def result():
        return _layer_norm(DN_ALPHA * hf_ref[...] + acc_ref[...], g_ref[...], b_ref[...])

    if n_head_blocks is None:
        @pl.when(last)
        def _():
            h = result()
            o0_ref[...] = h
            o1_ref[...] = h.astype(bf16)
    else:
        @pl.when(last & (i < n_head_blocks))
        def _():
            o0_ref[...] = result()

        @pl.when(last & (i >= n_head_blocks))
        def _():
            o1_ref[...] = result()


def mlp_residual_ln(hb, hf, w1, w2, g, b, split_rows=None, tm=1024, tf=1024):
    M, D = hb.shape
    F = w1.shape[1]
    row_map = lambda i, j: (i, 0)
    if split_rows is None:
        n_head = None
        out_shape = (jax.ShapeDtypeStruct((M, D), f32), jax.ShapeDtypeStruct((M, D), bf16))
        out_specs = (pl.BlockSpec((tm, D), row_map), pl.BlockSpec((tm, D), row_map))
    else:
        assert split_rows % tm == 0 and (M - split_rows) % tm == 0
        n_head = split_rows // tm
        out_shape = (jax.ShapeDtypeStruct((split_rows, D), f32), jax.ShapeDtypeStruct((M - split_rows, D), f32))
        out_specs = (pl.BlockSpec((tm, D), lambda i, j: (jnp.minimum(i, n_head - 1), 0)),
                     pl.BlockSpec((tm, D), lambda i, j: (jnp.maximum(i - n_head, 0), 0)))
    return pl.pallas_call(
        functools.partial(_mlp_body, n_head_blocks=n_head),
        out_shape=out_shape,
        grid=(M // tm, F // tf),
        in_specs=[
            pl.BlockSpec((tm, D), row_map),
            pl.BlockSpec((tm, D), row_map),
            pl.BlockSpec((D, tf), lambda i, j: (0, j)),
            pl.BlockSpec((tf, D), lambda i, j: (j, 0)),
            pl.BlockSpec((1, D), lambda i, j: (0, 0)),
            pl.BlockSpec((1, D), lambda i, j: (0, 0)),
        ],
        out_specs=out_specs,
        scratch_shapes=[pltpu.VMEM((tm, D), f32)],
        compiler_params=_cparams(("arbitrary", "arbitrary")),
        name="mlp_residual_ln",
    )(hb, hf, w1, w2, g, b)


def _kv_project_body(wt_ref, h_ref, cmp_ref, sel_ref, win_ref, selp_ref, winp_ref):
    res = _dot_nt(wt_ref[...], h_ref[...])
    tm = h_ref.shape[0]
    cmp_ref[0] = res[0:KV_ROW]
    sel_ref[0] = res[KV_ROW:2 * KV_ROW]
    win_ref[0] = res[2 * KV_ROW:3 * KV_ROW]
    for k in range(tm // PAGE_SIZE):
        cols = slice(k * PAGE_SIZE, (k + 1) * PAGE_SIZE)
        selp_ref[0, k] = res[KV_ROW:2 * KV_ROW, cols].astype(bf16)
        winp_ref[0, k] = res[2 * KV_ROW:3 * KV_ROW, cols].astype(bf16)


def kv_project_feature_major(h_b, w_t, Bn, L, tm=512):
    nj = L // tm
    pages_per_step = tm // PAGE_SIZE
    fm = jax.ShapeDtypeStruct((Bn, KV_ROW, L), f32)
    pg = jax.ShapeDtypeStruct((Bn, L // PAGE_SIZE, KV_ROW, PAGE_SIZE), bf16)
    fm_spec = pl.BlockSpec((1, KV_ROW, tm), lambda b, j: (b, 0, j))
    pg_spec = pl.BlockSpec((1, pages_per_step, KV_ROW, PAGE_SIZE), lambda b, j: (b, j, 0, 0))
    return pl.pallas_call(
        _kv_project_body,
        out_shape=(fm, fm, fm, pg, pg),
        grid=(Bn, nj),
        in_specs=[
            pl.BlockSpec(w_t.shape, lambda b, j: (0, 0)),
            pl.BlockSpec((tm, h_b.shape[1]), lambda b, j: (b * nj + j, 0)),
        ],
        out_specs=(fm_spec, fm_spec, fm_spec, pg_spec, pg_spec),
        compiler_params=_cparams(("parallel", "parallel")),
        name="kv_project_feature_major",
    )(w_t, h_b)


SSD_SEQS_PER_STEP = 4
SSD_SEQS_PER_STEP_CHUNKED = 2


def _ssd_body(*refs, Q, BB, nc, has_init):
    n_in = 1 if nc == 1 else BB
    xbc_refs, z_refs, dt_refs = refs[0:n_in], refs[n_in:2 * n_in], refs[2 * n_in:3 * n_in]
    refs = refs[3 * n_in:]
    if has_init:
        conv0_ref, h0_ref = refs[0:2]
        refs = refs[2:]
    (cw_ref, cb_ref, dtb_ref, alog_ref, dskx_ref, nw_ref, e64_ref, eq_ref,
     y_ref, hout_ref, cout_ref, st_ref, xpad_ref, xc_ref, ybuf_ref, xdt_ref, xds_ref, ecx_ref, acx_ref) = refs

    def rows_of(row_refs, bb, cols=slice(None)):
        return row_refs[0][bb * Q:(bb + 1) * Q, cols] if nc == 1 else row_refs[bb][:, cols]

    def store_y(bb, cols, value):
        if nc == 1:
            y_ref[bb * Q:(bb + 1) * Q, cols] = value
        else:
            y_ref[bb, :, cols] = value

    c = pl.program_id(1)
    single_chunk = nc == 1
    P, N, R, G = M_HEAD_DIM, M_D_STATE, M_HPG, M_N_GROUPS
    PAD = SUBLANES

    per_head_state = single_chunk and has_init

    def first_chunk():
        for bb in range(BB):
            xpad_ref[bb, 0:PAD, :] = jnp.zeros((PAD, M_CONV_DIM), f32)
            if has_init:
                xpad_ref[bb, PAD - (M_CONV - 1):PAD, :] = conv0_ref[bb]
            if per_head_state:
                continue
            if has_init:
                for g in range(G):
                    for r in range(R):
                        st_ref[bb, g, :, r * P:(r + 1) * P] = h0_ref[bb, g * R + r].T
            else:
                st_ref[bb] = jnp.zeros(st_ref.shape[1:], f32)

    def last_chunk():
        if per_head_state:
            return
        for bb in range(BB):
            for g in range(G):
                for r in range(R):
                    hout_ref[bb, g * R + r] = st_ref[bb, g, :, r * P:(r + 1) * P].T

    if single_chunk:
        first_chunk()
    else:
        pl.when(c == 0)(first_chunk)

    ri = lax.broadcasted_iota(i32, (Q, Q), 0)
    ci = lax.broadcasted_iota(i32, (Q, Q), 1)
    tril = ri >= ci
    GW = M_D_INNER // G
    lane = lax.broadcasted_iota(i32, (Q, LANES), 1)
    third = M_N_HEADS
    assert 3 * third <= LANES

    def spread(v, e_ref):
        hi = v.astype(bf16).astype(f32)
        r1 = v - hi
        mid = r1.astype(bf16).astype(f32)
        lo = r1 - mid
        packed = jnp.where(lane < third, hi, jnp.where(lane < 2 * third, pltpu.roll(mid, third, 1),
                                                       jnp.where(lane < 3 * third, pltpu.roll(lo, 2 * third, 1), 0.0)))
        return _dot(packed.astype(bf16), e_ref[...])

    seqs = range(BB)
    a_cum, a_cum_t, e_last = [], [], []
    for bb in seqs:
        xpad_ref[bb, PAD:PAD + Q, :] = rows_of(xbc_refs, bb)
        acc = cb_ref[...] + xpad_ref[bb, pl.ds(PAD - 3, Q), :] * cw_ref[0:1, :]
        for k in range(1, M_CONV):
            acc = acc + xpad_ref[bb, pl.ds(PAD - 3 + k, Q), :] * cw_ref[k:k + 1, :]
        xc_ref[bb] = _silu(acc)
        cout_ref[bb] = xpad_ref[bb, pl.ds(Q + PAD - 3, 3), :]
        xpad_ref[bb, 0:PAD, :] = xpad_ref[bb, pl.ds(Q, PAD), :]

        xdt = rows_of(dt_refs, bb) + dtb_ref[...]
        dt = jnp.maximum(xdt, 0.0) + jnp.log1p(jnp.exp(-jnp.abs(xdt)))
        a = dt * (-jnp.exp(alog_ref[...]))
        a_cum.append(jnp.dot(tril.astype(f32), a, preferred_element_type=f32, precision=_HI))
        a_cum_t.append(a_cum[bb].T)
        a_last = a_cum[bb][Q - 1:Q, :]
        e_last.append(jnp.exp(a_last))
        xdt_all = xc_ref[bb, :, 0:M_D_INNER] * spread(dt, e64_ref)
        xdt_ref[bb] = xdt_all
        xds_ref[bb] = xdt_all * spread(jnp.exp(a_last - a_cum[bb]), e64_ref)
        ecx_ref[bb] = spread(jnp.exp(a_cum[bb]), e64_ref)
        if Q == LANES:
            acx_ref[bb] = spread(a_cum[bb], eq_ref)

    for g in range(G):
        gcols = slice(g * GW, (g + 1) * GW)
        bg = [xc_ref[bb, :, M_D_INNER + g * N:M_D_INNER + (g + 1) * N].astype(bf16) for bb in seqs]
        cg = [xc_ref[bb, :, M_D_INNER + G * N + g * N:M_D_INNER + G * N + (g + 1) * N].astype(bf16) for bb in seqs]
        gmat = [_dot_nt(cg[bb], bg[bb]) for bb in seqs]
        for r in range(R):
            h = g * R + r
            hcols = slice(h * P, (h + 1) * P)
            for bb in seqs:
                col = acx_ref[bb, :, h * Q:(h + 1) * Q] if Q == LANES else a_cum[bb][:, h:h + 1]
                row = a_cum_t[bb][h:h + 1, :]
                lmat = jnp.exp(jnp.where(tril, col - row, -jnp.inf))
                ydiag = _dot((gmat[bb] * lmat).astype(bf16), xdt_ref[bb, :, hcols].astype(bf16))
                if per_head_state:
                    h_in = h0_ref[bb, h]
                    hout_ref[bb, h] = (h_in * e_last[bb][:, h:h + 1]
                                       + _dot_tn(xds_ref[bb, :, hcols].astype(bf16), bg[bb]))
                    ydiag = ydiag + _dot_nt(cg[bb], h_in.astype(bf16)) * ecx_ref[bb, :, hcols]
                ybuf_ref[bb, :, hcols] = ydiag
        for bb in seqs:
            extra = xc_ref[bb, :, gcols] * dskx_ref[:, gcols]
            if not per_head_state:
                extra = extra + _dot(cg[bb], st_ref[bb, g].astype(bf16)) * ecx_ref[bb, :, gcols]
                new = _dot_tn(bg[bb], xds_ref[bb, :, gcols].astype(bf16))
                st_ref[bb, g] = st_ref[bb, g] * ecx_ref[bb, Q - 1:Q, gcols] + new
            ybuf_ref[bb, :, gcols] += extra

    for g in range(G):
        cols = slice(g * GW, (g + 1) * GW)
        for bb in seqs:
            yg = ybuf_ref[bb, :, cols] * _silu(rows_of(z_refs, bb, cols))
            ms = jnp.mean(yg * yg, axis=-1, keepdims=True)
            store_y(bb, cols, (yg * lax.rsqrt(ms + RMS_EPS) * nw_ref[:, cols]).astype(y_ref.dtype))

    if single_chunk:
        last_chunk()
    else:
        pl.when(c == pl.num_programs(1) - 1)(last_chunk)


def _spread_matrix(width):
    rows = np.arange(LANES)[:, None]
    cols = np.arange(M_N_HEADS * width)[None, :]
    return jnp.asarray((rows < 3 * M_N_HEADS) & (rows % M_N_HEADS == cols // width), bf16)


def ssd_mixer_core(xbc, z, dt, row0, Bn, L, conv0, h0, conv_w, conv_b, dt_bias, a_log, d_skip_x, norm_w):
    Q = M_CHUNK if L % M_CHUNK == 0 else L
    nc = L // Q
    has_init = h0 is not None
    BB = next(n for n in ((SSD_SEQS_PER_STEP if nc == 1 else SSD_SEQS_PER_STEP_CHUNKED), 1) if Bn % n == 0)
    const2 = lambda b, c: (0, 0)
    if nc == 1:
        blk0 = row0 // (BB * Q)
        row_maps = [lambda b, c: (blk0 + b, 0)]
        y_shape, y_block, y_map = (Bn * L, M_D_INNER), (BB * Q, M_D_INNER), (lambda b, c: (b, 0))
    else:
        blk0 = row0 // Q
        row_maps = [functools.partial(lambda b, c, bb: (blk0 + (b * BB + bb) * nc + c, 0), bb=bb) for bb in range(BB)]
        y_shape, y_block, y_map = (Bn, L, M_D_INNER), (BB, Q, M_D_INNER), (lambda b, c: (b, c, 0))
    rows_per_block = BB * Q if nc == 1 else Q
    in_specs, args = [], []
    for arr, width in ((xbc, M_CONV_DIM), (z, M_D_INNER), (dt, LANES)):
        in_specs += [pl.BlockSpec((rows_per_block, width), m) for m in row_maps]
        args += [arr] * len(row_maps)
    if has_init:
        in_specs += [
            pl.BlockSpec((BB, M_CONV - 1, M_CONV_DIM), lambda b, c: (b, 0, 0)),
            pl.BlockSpec((BB, M_N_HEADS, M_HEAD_DIM, M_D_STATE), lambda b, c: (b, 0, 0, 0)),
        ]
        args += [conv0, h0]
    e64 = _spread_matrix(M_HEAD_DIM)
    eq = _spread_matrix(Q if Q == LANES else SUBLANES)
    in_specs += [
        pl.BlockSpec((M_CONV, M_CONV_DIM), const2),
        pl.BlockSpec((1, M_CONV_DIM), const2),
        pl.BlockSpec((1, LANES), const2),
        pl.BlockSpec((1, LANES), const2),
        pl.BlockSpec((1, M_D_INNER), const2),
        pl.BlockSpec((1, M_D_INNER), const2),
        pl.BlockSpec(e64.shape, const2),
        pl.BlockSpec(eq.shape, const2),
    ]
    args += [conv_w, conv_b, dt_bias, a_log, d_skip_x, norm_w, e64, eq]
    y_dtype = bf16 if y_block[-2] % 16 == 0 else f32
    y, new_ssm, new_conv = pl.pallas_call(
        functools.partial(_ssd_body, Q=Q, BB=BB, nc=nc, has_init=has_init),
        out_shape=(
            jax.ShapeDtypeStruct(y_shape, y_dtype),
            jax.ShapeDtypeStruct((Bn, M_N_HEADS, M_HEAD_DIM, M_D_STATE), f32),
            jax.ShapeDtypeStruct((Bn, M_CONV - 1, M_CONV_DIM), f32),
        ),
        grid=(Bn // BB, nc),
        in_specs=in_specs,
        out_specs=(
            pl.BlockSpec(y_block, y_map),
            pl.BlockSpec((BB, M_N_HEADS, M_HEAD_DIM, M_D_STATE), lambda b, c: (b, 0, 0, 0)),
            pl.BlockSpec((BB, M_CONV - 1, M_CONV_DIM), lambda b, c: (b, 0, 0)),
        ),
        scratch_shapes=[
            pltpu.VMEM((1, 1, SUBLANES, LANES) if nc == 1 and has_init
                       else (BB, M_N_GROUPS, M_D_STATE, M_HPG * M_HEAD_DIM), f32),
            pltpu.VMEM((BB, Q + SUBLANES, M_CONV_DIM), f32),
            pltpu.VMEM((BB, Q, M_CONV_DIM), f32),
            pltpu.VMEM((BB, Q, M_D_INNER), f32),
            pltpu.VMEM((BB, Q, M_D_INNER), f32),
            pltpu.VMEM((BB, Q, M_D_INNER), f32),
            pltpu.VMEM((BB, Q, M_D_INNER), f32),
            pltpu.VMEM((BB, Q, M_N_HEADS * Q) if Q == LANES else (1, SUBLANES, LANES), f32),
        ],
        compiler_params=_cparams(("parallel", "arbitrary")),
        name="ssd_mixer_core",
    )(*args)
    return y.reshape(Bn * L, M_D_INNER), new_ssm, new_conv


N_PAGES = 16
N_CHUNKS = N_PAGES * PAGE_SIZE // CMP_STRIDE
N_SLABS = KV_ROW // LANES


def _compress_body(*refs, paged):
    if paged:
        refs = refs[1:]
    pages = refs[:N_PAGES]
    wbd_ref, pe_ref, w1f_ref, w2_ref, w2t_ref, kct_ref, vc_ref, xs_ref = refs[N_PAGES:]
    H = CMP_HIDDEN
    n_steps = CMP_STRIDE // 2
    pages_per_step = N_PAGES // n_steps

    def to_token_major(sl, page_range):
        for p in page_range:
            xs_ref[sl, p * PAGE_SIZE:(p + 1) * PAGE_SIZE, :] = pages[p][0, sl * LANES:(sl + 1) * LANES, :].T

    to_token_major(0, range(N_PAGES))
    row = lax.broadcasted_iota(i32, (N_CHUNKS, H), 0)
    for kv in range(2):
        pe_term = _dot(pe_ref[kv], w1f_ref[kv])[0:1, :]
        for gp in range(A_N_KV // 2):
            sl = kv * (A_N_KV // 2) + gp
            acc = jnp.zeros((N_CHUNKS, 2 * CMP_RATIO * H), f32)
            for jp in range(n_steps):
                x = jnp.concatenate(
                    [xs_ref[sl, pl.ds(2 * jp + k, N_CHUNKS, stride=CMP_STRIDE), :] for k in range(2)], axis=1).astype(bf16)
                acc = acc + _dot(x, wbd_ref[kv, jp])
                if sl + 1 < N_SLABS:
                    to_token_major(sl + 1, range(jp * pages_per_step, (jp + 1) * pages_per_step))
            for gi in range(2):
                g = gp * 2 + gi
                p0 = acc[:, gi * CMP_RATIO * H:gi * CMP_RATIO * H + H]
                p1 = acc[:, gi * CMP_RATIO * H + H:(gi + 1) * CMP_RATIO * H]
                p1_next = jnp.where(row == N_CHUNKS - 1, 0.0, pltpu.roll(p1, N_CHUNKS - 1, 0))
                hid = _silu(p0 + p1_next + pe_term).astype(bf16)
                if kv == 0:
                    kct_ref[0, g * A_HEAD_DIM:(g + 1) * A_HEAD_DIM, :] = _dot_nt(w2t_ref[...], hid).astype(kct_ref.dtype)
                else:
                    vc_ref[0, :, g * A_HEAD_DIM:(g + 1) * A_HEAD_DIM] = _dot(hid, w2_ref[...]).astype(vc_ref.dtype)


def compress_kv(pages, page_table, wbd, pe_rows, w1_flat, w2_v, w2t_k):
    paged = page_table is not None
    Bn = page_table.shape[0] if paged else pages.shape[0]
    if paged:
        page_specs = [
            pl.BlockSpec((1, KV_ROW, PAGE_SIZE), functools.partial(lambda b, pt, j: (pt[b, j], 0, 0), j=j))
            for j in range(N_PAGES)
        ]
        const = lambda nd: (lambda b, pt: (0,) * nd)
        out_map = lambda b, pt: (b, 0, 0)
    else:
        page_specs = [
            pl.BlockSpec((1, KV_ROW, PAGE_SIZE), functools.partial(lambda b, j: (b, 0, j), j=j)) for j in range(N_PAGES)
        ]
        const = lambda nd: (lambda b: (0,) * nd)
        out_map = lambda b: (b, 0, 0)
    in_specs = page_specs + [
        pl.BlockSpec(wbd.shape, const(4)),
        pl.BlockSpec(pe_rows.shape, const(3)),
        pl.BlockSpec(w1_flat.shape, const(3)),
        pl.BlockSpec(w2_v.shape, const(2)),
        pl.BlockSpec(w2t_k.shape, const(2)),
    ]
    out_shape = (jax.ShapeDtypeStruct((Bn, KV_HALF, N_CHUNKS), bf16), jax.ShapeDtypeStruct((Bn, N_CHUNKS, KV_HALF), bf16))
    out_specs = (pl.BlockSpec((1, KV_HALF, N_CHUNKS), out_map), pl.BlockSpec((1, N_CHUNKS, KV_HALF), out_map))
    scratch = [pltpu.VMEM((N_SLABS, N_PAGES * PAGE_SIZE, LANES), f32)]
    body = functools.partial(_compress_body, paged=paged)
    args = ([pages] * N_PAGES) + [wbd, pe_rows, w1_flat, w2_v, w2t_k]
    if paged:
        return pl.pallas_call(
            body, out_shape=out_shape,
            grid_spec=pltpu.PrefetchScalarGridSpec(
                num_scalar_prefetch=1, grid=(Bn,), in_specs=in_specs, out_specs=out_specs, scratch_shapes=scratch),
            compiler_params=_cparams(("parallel",)), name="compress_kv_paged",
        )(page_table, *args)
    return pl.pallas_call(
        body, out_shape=out_shape, grid=(Bn,), in_specs=in_specs, out_specs=out_specs, scratch_shapes=scratch,
        compiler_params=_cparams(("parallel",)), name="compress_kv",
    )(*args)


def _bias_table_body(rb_ref, tbl_ref, tz0_ref, tz1_ref):
    n = lax.broadcasted_iota(i32, (A_N_HEADS, LANES), 1)
    max_exact = N_BUCKETS // 2
    large = max_exact + jnp.floor(jnp.log(jnp.maximum(n, max_exact).astype(f32) / max_exact)
                                  / math.log(MAX_DISTANCE / max_exact) * (N_BUCKETS - max_exact))
    bucket = jnp.where(n < max_exact, n.astype(f32), jnp.minimum(large, N_BUCKETS - 1.0))
    tbl = jnp.zeros((A_N_HEADS, LANES), f32)
    for b in range(N_BUCKETS):
        tbl = jnp.where(bucket == float(b), rb_ref[:, b:b + 1], tbl)
    tbl_ref[...] = tbl
    dist = lax.broadcasted_iota(i32, (LANES, LANES), 0) - lax.broadcasted_iota(i32, (LANES, LANES), 1)
    for h in range(A_N_HEADS):
        row = jnp.broadcast_to(tbl[h:h + 1, :], (LANES, LANES))
        own = jnp.take_along_axis(row, jnp.clip(dist, 0, MAX_DISTANCE - 1), axis=1)
        tz0_ref[h] = jnp.where(dist >= 0, own, MASK_VALUE)
        tz1_ref[h] = jnp.take_along_axis(row, jnp.minimum(dist + LANES, MAX_DISTANCE - 1), axis=1)


def bias_table(rel_bias_t):
    assert MAX_DISTANCE <= LANES
    tile = jax.ShapeDtypeStruct((A_N_HEADS, LANES, LANES), f32)
    return pl.pallas_call(
        _bias_table_body, out_shape=(jax.ShapeDtypeStruct((A_N_HEADS, LANES), f32), tile, tile), name="bias_table",
    )(rel_bias_t)


TILE = 128
WIN_TILES = WINDOW // TILE


WIDE = 2 * TILE
N_SLOTS = N_PAGES * TILE // WIDE
BIG = -MASK_VALUE


SEL, WIN = 0, 1
FAR, NEAR = 0, 1
WIN_SLOT0 = N_SLOTS


def _attn_prompt_body(q_ref, gate_ref, kct_ref, vc_ref, tbl_ref, tz0_ref, tz1_ref, ovlt_ref, selp_ref, winp_ref, o_ref,
                      s_scr, m_scr, l_scr, acc_scr, lhs_scr, oc_scr, fb_scr, *, qb, n_sel):
    i = pl.program_id(1)
    R = A_GROUP * qb
    dh = A_HEAD_DIM
    pos0 = i * qb
    odd = (i % 2) == 1
    td = i // 2
    scale = dh ** -0.5
    n_rank = SUBLANES * (-(-n_sel // SUBLANES))

    lane = lax.broadcasted_iota(i32, (R, TILE), 1)
    q_in_blk = jnp.concatenate([lax.broadcasted_iota(i32, (qb, TILE), 0)] * A_GROUP, axis=0)
    qpos = pos0 + q_in_blk
    neg_tile = jnp.full((R, TILE), MASK_VALUE, f32)

    s_idx = lax.broadcasted_iota(i32, (n_rank, qb), 0)
    s_qpos = pos0 + lax.broadcasted_iota(i32, (n_rank, qb), 1)
    blk = s_qpos // SEL_BLOCK
    sel_valid = s_idx * SEL_BLOCK <= s_qpos
    sel_forced = (s_idx == 0) | (s_idx == blk) | (s_idx == blk - 1)

    f_row = lax.broadcasted_iota(i32, (TILE - dh, WIDE), 0)
    flag_rows = [jnp.where(f_row == k, BIG, 0.0).astype(bf16) for k in range(2)]
    zero_flag = jnp.zeros((TILE - dh, WIDE), bf16)
    zero_drop = jnp.zeros((TILE, WIDE), bf16)
    b_row = lax.broadcasted_iota(i32, (TILE, WIDE), 0)
    b_col = lax.broadcasted_iota(i32, (TILE, WIDE), 1)
    f_lane = lax.broadcasted_iota(i32, (R, TILE - dh), 1)
    flags = jnp.where(f_lane == 0, jnp.where(td < 1, -1.0, 0.0),
                      jnp.where(f_lane == 1, jnp.where(td < 2, -1.0, 0.0), 0.0)).astype(bf16)
    win_thr = q_in_blk + jnp.where(odd, TILE, 0)

    def pair(ref, T, g, half):
        rows = slice(half * KV_HALF + g * dh, half * KV_HALF + (g + 1) * dh)
        return jnp.concatenate([ref[0, 2 * T, rows, :], ref[0, 2 * T + 1, rows, :]], axis=1)

    def drop_rows(T):
        return jnp.where(b_row == T * (WIDE // SEL_BLOCK) + b_col // SEL_BLOCK, BIG, 0.0).astype(bf16)

    def store_scores(g, br, slot, s, left, right, kind, first=False):
        s_l = s[:, 0:TILE] if left is None else s[:, 0:TILE] + left
        s_r = s[:, TILE:WIDE] if right is None else s[:, TILE:WIDE] + right
        s_scr[g, slot, :, 0:TILE] = s_l
        s_scr[g, slot, :, TILE:WIDE] = s_r
        mx = jnp.maximum(s_l, s_r)
        m_scr[g, br, kind] = mx if first else jnp.maximum(m_scr[g, br, kind], mx)

    def accumulate(g, br, slot, kind, v_t, first=False):
        shift = m_scr[g, br, kind]
        p_l = jnp.exp(s_scr[g, slot, :, 0:TILE] - shift)
        p_r = jnp.exp(s_scr[g, slot, :, TILE:WIDE] - shift)
        pv = _dot_nt(jnp.concatenate([p_l, p_r], axis=1).astype(bf16), v_t)
        if first:
            l_scr[g, br] = p_l + p_r
            acc_scr[g, br] = pv
        else:
            l_scr[g, br] += p_l + p_r
            acc_scr[g, br] += pv

    def row_max(g, br):
        fb = fb_scr[g]
        m = jnp.max(jnp.maximum(m_scr[g, br, FAR] + fb, m_scr[g, br, NEAR]), axis=1, keepdims=True)
        m_b = jnp.broadcast_to(m, (R, TILE))
        m_scr[g, br, FAR] = m_b - fb
        m_scr[g, br, NEAR] = m_b

    def result(g, br):
        return acc_scr[g, br] / jnp.sum(l_scr[g, br], axis=1, keepdims=True)

    def near_tiles(g):
        tz0 = tz0_ref[g * A_GROUP:(g + 1) * A_GROUP].reshape(R, TILE)
        tz1 = tz1_ref[g * A_GROUP:(g + 1) * A_GROUP].reshape(R, TILE)
        return jnp.where(odd, fb_scr[g], tz1), jnp.where(odd, tz1, tz0), jnp.where(odd, tz0, neg_tile)

    t_prev = jnp.maximum(td - 1, 0)
    t_first = jnp.maximum(td - 2, 0)
    n_far = jnp.maximum(td - 1, 0)

    for g in range(A_N_KV):
        heads = [g * A_GROUP + r for r in range(A_GROUP)]
        qg = jnp.concatenate([q_ref[:, h * dh:(h + 1) * dh] for h in heads], axis=0)
        qg = (qg.astype(f32) * scale).astype(bf16)
        tbl_g = jnp.concatenate([jnp.broadcast_to(tbl_ref[h:h + 1, :], (qb, TILE)) for h in heads], axis=0)
        fb_scr[g] = jnp.concatenate(
            [jnp.broadcast_to(tbl_ref[h:h + 1, MAX_DISTANCE - 1:MAX_DISTANCE], (qb, TILE)) for h in heads], axis=0)

        dist_c = qpos - (lane * CMP_STRIDE + (CMP_LEN - 1))
        s_c = _dot(qg, kct_ref[0, g * dh:(g + 1) * dh, :])
        s_c = s_c + jnp.take_along_axis(tbl_g, jnp.clip(dist_c, 0, MAX_DISTANCE - 1), axis=1)
        s_c = jnp.where(dist_c >= 0, s_c, MASK_VALUE)
        m_c = jnp.max(s_c, axis=1, keepdims=True)
        p_c = jnp.where(dist_c >= 0, jnp.exp(s_c - m_c), 0.0)
        l_c = jnp.sum(p_c, axis=1, keepdims=True)
        l_c = jnp.where(l_c == 0.0, 1.0, l_c)
        oc_scr[g] = _dot(p_c.astype(bf16), vc_ref[0, :, g * dh:(g + 1) * dh]) / l_c
        p_c = p_c / l_c

        p_sum = p_c[0:qb]
        for r in range(1, A_GROUP):
            p_sum = p_sum + p_c[r * qb:(r + 1) * qb]
        imp_t = lax.dot_general(ovlt_ref[0:n_rank, :], p_sum, (((1,), (1,)), ((), ())),
                                preferred_element_type=f32, precision=_HI)
        score = jnp.where(sel_valid, imp_t + jnp.where(sel_forced, FORCE_SCORE, 0.0), -1.0)
        score = jnp.where(s_idx < n_sel, score, -3.0)
        rank = jnp.zeros((n_rank, qb), f32)
        for s2 in range(n_sel):
            other = score[s2:s2 + 1, :]
            rank = rank + jnp.where((other > score) | ((other == score) & (s_idx > s2)), 1.0, 0.0)
        dropped_t = jnp.where(rank < float(min(SEL_TOPN, n_sel)), 0.0, -1.0)
        dropped = jnp.concatenate([dropped_t, jnp.zeros((TILE - n_rank, qb), f32)], axis=0).T
        lhs_scr[g, SEL] = jnp.concatenate([qg, flags, jnp.concatenate([dropped] * A_GROUP, axis=0).astype(bf16)], axis=1)
        lhs_scr[g, WIN] = jnp.concatenate([qg, flags, jnp.zeros((R, TILE), bf16)], axis=1)
        m_scr[g, SEL, FAR] = neg_tile

    too_old_l = jnp.where(lane > win_thr, 0.0, MASK_VALUE)
    too_old_r = jnp.where(lane + TILE > win_thr, 0.0, MASK_VALUE)
    for g in range(A_N_KV):
        prev_right, diag_left, diag_right = near_tiles(g)
        lhs_win = lhs_scr[g, WIN]
        s = _dot(lhs_win, jnp.concatenate([pair(winp_ref, t_first, g, 0), flag_rows[1], zero_drop], axis=0))
        store_scores(g, WIN, WIN_SLOT0, s, too_old_l, too_old_r, FAR, first=True)
        s = _dot(lhs_win, jnp.concatenate([pair(winp_ref, t_prev, g, 0), flag_rows[0], zero_drop], axis=0))
        store_scores(g, WIN, WIN_SLOT0 + 1, s, fb_scr[g], prev_right, NEAR, first=True)
        s = _dot(lhs_win, jnp.concatenate([pair(winp_ref, td, g, 0), zero_flag, zero_drop], axis=0))
        store_scores(g, WIN, WIN_SLOT0 + 2, s, diag_left, diag_right, NEAR)
        lhs_sel = lhs_scr[g, SEL]
        s = _dot(lhs_sel, jnp.concatenate([pair(selp_ref, t_prev, g, 0), flag_rows[0], drop_rows(t_prev)], axis=0))
        store_scores(g, SEL, N_SLOTS - 2, s, fb_scr[g], prev_right, NEAR, first=True)
        s = _dot(lhs_sel, jnp.concatenate([pair(selp_ref, td, g, 0), zero_flag, drop_rows(td)], axis=0))
        store_scores(g, SEL, N_SLOTS - 1, s, diag_left, diag_right, NEAR)

    def far_scores(T, carry):
        drop = drop_rows(T)
        for g in range(A_N_KV):
            s = _dot(lhs_scr[g, SEL], jnp.concatenate([pair(selp_ref, T, g, 0), zero_flag, drop], axis=0))
            store_scores(g, SEL, T, s, None, None, FAR)
        return carry

    lax.fori_loop(0, n_far, far_scores, 0)

    for g in range(A_N_KV):
        row_max(g, WIN)
        row_max(g, SEL)
    for g in range(A_N_KV):
        accumulate(g, WIN, WIN_SLOT0, FAR, pair(winp_ref, t_first, g, 1), first=True)
        accumulate(g, WIN, WIN_SLOT0 + 1, NEAR, pair(winp_ref, t_prev, g, 1))
        accumulate(g, WIN, WIN_SLOT0 + 2, NEAR, pair(winp_ref, td, g, 1))
        accumulate(g, SEL, N_SLOTS - 2, NEAR, pair(selp_ref, t_prev, g, 1), first=True)
        accumulate(g, SEL, N_SLOTS - 1, NEAR, pair(selp_ref, td, g, 1))

    def far_accumulate(T, carry):
        for g in range(A_N_KV):
            accumulate(g, SEL, T, FAR, pair(selp_ref, T, g, 1))
        return carry

    lax.fori_loop(0, n_far, far_accumulate, 0)

    for g in range(A_N_KV):
        o_c, o_s, o_w = oc_scr[g], result(g, SEL), result(g, WIN)
        for r in range(A_GROUP):
            h = g * A_GROUP + r
            rows = slice(r * qb, (r + 1) * qb)
            gt = gate_ref[:, 3 * h:3 * h + 3]
            o_h = gt[:, 0:1] * o_c[rows] + gt[:, 1:2] * o_s[rows] + gt[:, 2:3] * o_w[rows]
            o_ref[:, h * dh:(h + 1) * dh] = o_h.astype(o_ref.dtype)


def nsa_attention_prompt(q, gates, Bn, L, kct, vc, tbl, tz0, tz1, overlap_t, sel_pages, win_pages):
    qb = Q_BLOCK
    nqb = L // qb
    n_sel = -(-L // SEL_BLOCK)
    n_tiles = L // TILE
    assert n_tiles == N_PAGES and qb == TILE
    R = A_GROUP * qb
    row_map = lambda b, i: (b * nqb + i, 0)
    seq_map3 = lambda b, i: (b, 0, 0)
    seq_map4 = lambda b, i: (b, 0, 0, 0)
    const2 = lambda b, i: (0, 0)
    const3 = lambda b, i: (0, 0, 0)
    return pl.pallas_call(
        functools.partial(_attn_prompt_body, qb=qb, n_sel=n_sel),
        out_shape=jax.ShapeDtypeStruct((Bn * L, A_Q_DIM), bf16),
        grid=(Bn, nqb),
        in_specs=[
            pl.BlockSpec((qb, A_Q_DIM), row_map),
            pl.BlockSpec((qb, LANES), row_map),
            pl.BlockSpec((1, KV_HALF, N_CHUNKS), seq_map3),
            pl.BlockSpec((1, N_CHUNKS, KV_HALF), seq_map3),
            pl.BlockSpec((A_N_HEADS, LANES), const2),
            pl.BlockSpec((A_N_HEADS, TILE, TILE), const3),
            pl.BlockSpec((A_N_HEADS, TILE, TILE), const3),
            pl.BlockSpec((TILE, TILE), const2),
            pl.BlockSpec((1, n_tiles, KV_ROW, TILE), seq_map4),
            pl.BlockSpec((1, n_tiles, KV_ROW, TILE), seq_map4),
        ],
        out_specs=pl.BlockSpec((qb, A_Q_DIM), row_map),
        scratch_shapes=[
            pltpu.VMEM((A_N_KV, N_SLOTS + 3, R, WIDE), f32),
            pltpu.VMEM((A_N_KV, 2, 2, R, TILE), f32),
            pltpu.VMEM((A_N_KV, 2, R, TILE), f32),
            pltpu.VMEM((A_N_KV, 2, R, A_HEAD_DIM), f32),
            pltpu.VMEM((A_N_KV, 2, R, WIDE), bf16),
            pltpu.VMEM((A_N_KV, R, A_HEAD_DIM), f32),
            pltpu.VMEM((A_N_KV, R, TILE), f32),
        ],
        compiler_params=_cparams(("parallel", "arbitrary")),
        name="nsa_attention_prompt",
    )(q, gates, kct, vc, tbl, tz0, tz1, overlap_t, sel_pages, win_pages)


def _attn_sample_body(pt_ref, q_ref, gate_ref, kct_ref, vc_ref, tbl_ref, ovl_ref, *refs, qb, start, n_sel):
    del pt_ref
    sel_pages = refs[:N_PAGES]
    selnew_ref, winpast_ref, winnew_ref, o_ref, s_scr = refs[N_PAGES:]
    dh = A_HEAD_DIM
    RG = A_GROUP * qb
    R = A_N_KV * RG
    scale = dh ** -0.5
    n_win_past = WINDOW // TILE
    assert R == TILE and n_sel <= SEL_BLOCK and start == N_PAGES * TILE

    qs = q_ref[...] * scale
    blocks = []
    for g in range(A_N_KV):
        qg = jnp.concatenate([qs[:, (g * A_GROUP + r) * dh:(g * A_GROUP + r + 1) * dh] for r in range(A_GROUP)], axis=0)
        parts = [qg if gg == g else jnp.zeros((RG, dh), f32) for gg in range(A_N_KV)]
        blocks.append(jnp.concatenate(parts, axis=1))
    qbd = jnp.concatenate(blocks, axis=0).astype(bf16)

    tbl_rows = jnp.concatenate([jnp.broadcast_to(tbl_ref[h:h + 1, :], (qb, LANES)) for h in range(A_N_HEADS)], axis=0)
    far_bias = tbl_rows[:, MAX_DISTANCE - 1:MAX_DISTANCE]
    lane = lax.broadcasted_iota(i32, (R, TILE), 1)
    row = lax.broadcasted_iota(i32, (R, TILE), 0)
    qpos = start + row % qb

    def near_bias(dist):
        return jnp.take_along_axis(tbl_rows, jnp.clip(dist, 0, MAX_DISTANCE - 1), axis=1)

    def softmax_rows(s):
        m = jnp.max(s, axis=1, keepdims=True)
        p = jnp.where(s > 0.5 * MASK_VALUE, jnp.exp(s - m), 0.0)
        l = jnp.sum(p, axis=1, keepdims=True)
        return p, jnp.where(l == 0.0, 1.0, l)

    def pad_rows(x):
        return jnp.concatenate([x, jnp.zeros((TILE - x.shape[0], x.shape[1]), x.dtype)], axis=0).astype(bf16)

    def branch_scores(tiles, new_ref, extra, col0):
        n = len(tiles)
        for j, (kv_t, kind) in enumerate(tiles):
            s = _dot(qbd, kv_t(0).astype(bf16)) + extra(j)
            dist = qpos - (start - (n - j) * TILE + lane)
            s = s + (far_bias if kind == "far" else near_bias(dist))
            if kind == "edge":
                s = jnp.where(dist < WINDOW, s, MASK_VALUE)
            s_scr[:, col0 + j * TILE:col0 + (j + 1) * TILE] = s
        dist = qpos - (start + lane)
        s = _dot_nt(qbd, pad_rows(new_ref[:, 0:KV_HALF])) + extra(n) + near_bias(dist)
        s_scr[:, col0 + n * TILE:col0 + (n + 1) * TILE] = jnp.where(dist >= 0, s, MASK_VALUE)

    def branch_output(tiles, new_ref, col0):
        n = len(tiles)
        p, l = softmax_rows(s_scr[:, col0:col0 + (n + 1) * TILE])
        p = p.astype(bf16)
        o = _dot(p[:, n * TILE:(n + 1) * TILE], pad_rows(new_ref[:, KV_HALF:KV_ROW]))
        for j, (kv_t, _) in enumerate(tiles):
            o = o + _dot_nt(p[:, j * TILE:(j + 1) * TILE], kv_t(1).astype(bf16))
        return o / l

    def page_getter(ref, cols=slice(None)):
        return lambda half: ref[0, half * KV_HALF:(half + 1) * KV_HALF, cols]

    sel_tiles = [(page_getter(sel_pages[t]), "near" if t == N_PAGES - 1 else "far") for t in range(N_PAGES)]
    win_tiles = [(page_getter(winpast_ref, slice(j * TILE, (j + 1) * TILE)),
                  "edge" if j == 0 else ("near" if j == n_win_past - 1 else "far")) for j in range(n_win_past)]
    win_col0 = (N_PAGES + 1) * TILE

    def sel_scores(extra):
        branch_scores(sel_tiles, selnew_ref, extra, 0)

    branch_scores(win_tiles, winnew_ref, lambda j: 0.0, win_col0)

    dist_c = qpos - (lane * CMP_STRIDE + (CMP_LEN - 1))
    s_c = _dot(qbd, kct_ref[0]) + near_bias(dist_c)
    p_c, l_c = softmax_rows(jnp.where(dist_c >= 0, s_c, MASK_VALUE))
    o_c = _dot(p_c.astype(bf16), vc_ref[0]) / l_c
    p_c = p_c / l_c

    p_sum = []
    for g in range(A_N_KV):
        acc = p_c[g * RG:g * RG + qb]
        for r in range(1, A_GROUP):
            acc = acc + p_c[g * RG + r * qb:g * RG + (r + 1) * qb]
        p_sum.append(acc)
    p_sum = jnp.concatenate(p_sum, axis=0)
    imp = jnp.dot(p_sum, ovl_ref[...], preferred_element_type=f32, precision=_HI)
    s_lane = lax.broadcasted_iota(i32, (A_N_KV * qb, TILE), 1)
    s_qpos = start + lax.broadcasted_iota(i32, (A_N_KV * qb, TILE), 0) % qb
    blk = s_qpos // SEL_BLOCK
    valid = s_lane * SEL_BLOCK <= s_qpos
    forced = (s_lane == 0) | (s_lane == blk) | (s_lane == blk - 1)
    score = jnp.where(valid, imp + jnp.where(forced, FORCE_SCORE, 0.0), -1.0)
    score = jnp.where(s_lane < n_sel, score, -3.0)
    rank = jnp.zeros(score.shape, f32)
    for s2 in range(n_sel):
        col = score[:, s2:s2 + 1]
        rank = rank + jnp.where((col > score) | ((col == score) & (s_lane > s2)), 1.0, 0.0)
    not_chosen = jnp.where(rank < float(min(SEL_TOPN, n_sel)), 0.0, -1.0)
    drop = jnp.concatenate(
        [not_chosen[g * qb:(g + 1) * qb] for g in range(A_N_KV) for _ in range(A_GROUP)], axis=0)
    drop = drop[:, 0:SEL_BLOCK].astype(bf16)
    b_row = lax.broadcasted_iota(i32, (SEL_BLOCK, TILE), 0)
    b_col = lax.broadcasted_iota(i32, (SEL_BLOCK, TILE), 1)

    def drop_unselected(t):
        expand = jnp.where(b_row == t * (TILE // SEL_BLOCK) + b_col // SEL_BLOCK, -MASK_VALUE, 0.0).astype(bf16)
        return _dot(drop, expand)

    sel_scores(lambda j: drop_unselected(j))
    o_w = branch_output(win_tiles, winnew_ref, win_col0)
    o_s = branch_output(sel_tiles, selnew_ref, 0)

    for g in range(A_N_KV):
        for r in range(A_GROUP):
            h = g * A_GROUP + r
            rows = slice(g * RG + r * qb, g * RG + (r + 1) * qb)
            cols = slice(g * dh, (g + 1) * dh)
            gt = gate_ref[:, 3 * h:3 * h + 3]
            o_h = gt[:, 0:1] * o_c[rows, cols] + gt[:, 1:2] * o_s[rows, cols] + gt[:, 2:3] * o_w[rows, cols]
            o_ref[:, h * dh:(h + 1) * dh] = o_h.astype(o_ref.dtype)


def nsa_attention_sample(q, gates, row0, Bn, L, start, kct, vc, tbl, overlap, sel_pages, page_table, sel_new,
                         win_past, win_new):
    qb = L
    n_sel = -(-(start + L) // SEL_BLOCK)
    blk0 = row0 // qb
    row_map = lambda b, pt: (blk0 + b, 0)
    seq_map = lambda b, pt: (b, 0, 0)
    page_specs = [
        pl.BlockSpec((1, KV_ROW, PAGE_SIZE), functools.partial(lambda b, pt, j: (pt[b, j], 0, 0), j=j))
        for j in range(N_PAGES)
    ]
    return pl.pallas_call(
        functools.partial(_attn_sample_body, qb=qb, start=start, n_sel=n_sel),
        out_shape=jax.ShapeDtypeStruct((Bn * L, A_Q_DIM), f32),
        grid_spec=pltpu.PrefetchScalarGridSpec(
            num_scalar_prefetch=1,
            grid=(Bn,),
            in_specs=[
                pl.BlockSpec((qb, A_Q_DIM), row_map),
                pl.BlockSpec((qb, LANES), row_map),
                pl.BlockSpec((1, KV_HALF, N_CHUNKS), seq_map),
                pl.BlockSpec((1, N_CHUNKS, KV_HALF), seq_map),
                pl.BlockSpec((A_N_HEADS, LANES), lambda b, pt: (0, 0)),
                pl.BlockSpec((TILE, TILE), lambda b, pt: (0, 0)),
            ] + page_specs + [
                pl.BlockSpec((qb, KV_ROW), row_map),
                pl.BlockSpec((1, KV_ROW, WINDOW), seq_map),
                pl.BlockSpec((qb, KV_ROW), row_map),
            ],
            out_specs=pl.BlockSpec((qb, A_Q_DIM), lambda b, pt: (b, 0)),
            scratch_shapes=[pltpu.VMEM((TILE, (N_PAGES + 1 + WINDOW // TILE + 1) * TILE), f32)],
        ),
        compiler_params=_cparams(("parallel",)),
        name="nsa_attention_sample",
    )(page_table, q, gates, kct, vc, tbl, overlap, *([sel_pages] * N_PAGES), sel_new, win_past, win_new)


WINDOW_UPDATE_SEQS_PER_STEP = 4


def _window_update_body(old_ref, new_ref, place_ref, o_ref, *, n_new):
    lane = lax.broadcasted_iota(i32, (KV_ROW, TILE), 1)
    for bb in range(old_ref.shape[0]):
        x = old_ref[bb]
        shifted = pltpu.roll(x, WINDOW - n_new, 1)
        new_t = lax.dot_general(new_ref[bb * n_new:(bb + 1) * n_new, :], place_ref[...], (((0,), (0,)), ((), ())),
                                preferred_element_type=f32, precision=_HI)
        o_ref[bb, :, 0:WINDOW - TILE] = shifted[:, 0:WINDOW - TILE]
        o_ref[bb, :, WINDOW - TILE:WINDOW] = jnp.where(lane >= TILE - n_new, new_t, shifted[:, WINDOW - TILE:WINDOW])


def window_update(old_t, new_rows):
    Bn = old_t.shape[0]
    n_new = new_rows.shape[0] // Bn
    BB = next(n for n in (WINDOW_UPDATE_SEQS_PER_STEP, 1) if Bn % n == 0)
    place = jnp.asarray(np.eye(n_new, TILE, k=TILE - n_new, dtype=np.float32))
    return pl.pallas_call(
        functools.partial(_window_update_body, n_new=n_new),
        out_shape=jax.ShapeDtypeStruct(old_t.shape, f32),
        grid=(Bn // BB,),
        in_specs=[
            pl.BlockSpec((BB, KV_ROW, WINDOW), lambda b: (b, 0, 0)),
            pl.BlockSpec((BB * n_new, KV_ROW), lambda b: (b, 0)),
            pl.BlockSpec((n_new, TILE), lambda b: (0, 0)),
        ],
        out_specs=pl.BlockSpec((BB, KV_ROW, WINDOW), lambda b: (b, 0, 0)),
        compiler_params=_cparams(("parallel",)),
        name="window_update",
    )(old_t, new_rows, place)


def _pad_lanes(v):
    return jnp.pad(v, (0, LANES - v.shape[0])).reshape(1, LANES)


def _overlap_matrix():
    n_cmp = N_CHUNKS - CMP_RATIO + 1
    c = np.arange(TILE)[:, None] * CMP_STRIDE
    s = np.arange(TILE)[None, :] * SEL_BLOCK
    ov = (c < s + SEL_BLOCK) & (c + CMP_LEN > s) & (np.arange(TILE)[:, None] < n_cmp)
    return jnp.asarray(ov.astype(np.float32))


def _feature_major(x):
    lead = x.shape[:-4]
    n = len(lead)
    return jnp.transpose(x, tuple(range(n)) + (n + 1, n + 2, n + 3, n)).reshape(lead + (KV_ROW, x.shape[-4]))


def _token_major(x_t):
    B, _, T = x_t.shape
    return jnp.transpose(x_t.reshape(B, 2, A_N_KV, A_HEAD_DIM, T), (0, 4, 1, 2, 3))


def kernel(x_prompt, x_sample, state_ssm, state_conv, cache_cmp_kv, cache_sel_kv, cache_win_kv, page_table, ln_g, ln_b, m_in_w, m_conv_w, m_conv_b, m_dt_bias, m_a_log, m_d, m_norm_w, m_out_w, kv_w, cmp_w1, cmp_pe, cmp_w2, q_w, o_w, rel_bias, mlp_w1, mlp_w2):
    Bp, Lp, D = x_prompt.shape
    Bs, Ls, _ = x_sample.shape
    NP, NS = Bp * Lp, Bs * Ls
    past_len = page_table.shape[1] * PAGE_SIZE
    assert past_len == N_PAGES * PAGE_SIZE and Lp == N_PAGES * PAGE_SIZE and cache_win_kv.shape[1] == WINDOW

    in_w = m_in_w[0].astype(bf16)
    z_w = in_w[:, :M_D_INNER]
    xbc_w = in_w[:, M_D_INNER:M_D_INNER + M_CONV_DIM]
    dt_w = jnp.pad(in_w[:, M_D_INNER + M_CONV_DIM:], ((0, 0), (0, LANES - M_N_HEADS)))
    kvw = kv_w.astype(bf16)
    qw = q_w[0].astype(bf16)
    gate_w = jnp.pad(qw[:, A_Q_DIM:], ((0, 0), (0, LANES - 3 * A_N_HEADS)))
    w1b = cmp_w1.astype(bf16)
    w_j = jnp.transpose(w1b, (0, 2, 3, 1, 4)).reshape(2, CMP_STRIDE, A_HEAD_DIM, CMP_RATIO * CMP_HIDDEN)
    zeros = jnp.zeros_like(w_j)
    wbd = jnp.concatenate([jnp.concatenate([w_j, zeros], axis=3), jnp.concatenate([zeros, w_j], axis=3)], axis=2)
    wbd = wbd.reshape(2, CMP_STRIDE // 2, 2 * LANES, 2 * CMP_RATIO * CMP_HIDDEN)
    pe_rows = jnp.broadcast_to(cmp_pe.astype(bf16).reshape(2, 1, CMP_LEN * A_HEAD_DIM), (2, SUBLANES, CMP_LEN * A_HEAD_DIM))
    w1_flat = w1b.reshape(2, CMP_LEN * A_HEAD_DIM, CMP_HIDDEN)
    cmp_w = (wbd, pe_rows, w1_flat, cmp_w2[1].astype(bf16), cmp_w2[0].T.astype(bf16))

    x_p, x_s = x_prompt.reshape(NP, D), x_sample.reshape(NS, D)
    xb = jnp.concatenate([x_p.astype(bf16), x_s.astype(bf16)], axis=0)
    z = matmul(xb, z_w, f32)
    xbc = matmul(xb, xbc_w, f32)
    dt = matmul(xb, dt_w, f32)
    ssd_w = (m_conv_w[0], m_conv_b[0].reshape(1, -1), _pad_lanes(m_dt_bias[0]), _pad_lanes(m_a_log[0]),
             jnp.repeat(m_d[0], M_HEAD_DIM).reshape(1, -1), m_norm_w[0].reshape(1, -1))
    y_p, p_ssm, p_conv = ssd_mixer_core(xbc, z, dt, 0, Bp, Lp, None, None, *ssd_w)
    y_s, s_ssm, s_conv = ssd_mixer_core(xbc, z, dt, NP, Bs, Ls, state_conv[0], state_ssm[0], *ssd_w)
    h_f, h_b = matmul_residual_ln(y_p, y_s, m_out_w[0].astype(bf16), x_p, x_s,
                                  ln_g[0, 0].reshape(1, D), ln_b[0, 0].reshape(1, D))
    h_f, h_b = mlp_residual_ln(h_b, h_f, mlp_w1[0].astype(bf16), mlp_w2[0].astype(bf16),
                               ln_g[0, 1].reshape(1, D), ln_b[0, 1].reshape(1, D))

    cmp_t, sel_t, win_t, sel_pg, win_pg = kv_project_feature_major(h_b, kvw.T, Bp, Lp)
    kv_s = matmul(h_b[NP:], kvw, f32)
    cmp_s, sel_s, win_s = kv_s[:, 0:KV_ROW], kv_s[:, KV_ROW:2 * KV_ROW], kv_s[:, 2 * KV_ROW:3 * KV_ROW]
    kct_p, vc_p = compress_kv(cmp_t, None, *cmp_w)
    kct_s, vc_s = compress_kv(_feature_major(cache_cmp_kv), page_table, *cmp_w)

    q = matmul(h_b, qw[:, :A_Q_DIM], f32)
    gates = matmul(h_b, gate_w, f32, act="sigmoid")
    tbl, tz0, tz1 = bias_table(rel_bias.T)
    overlap = _overlap_matrix()
    o_p = nsa_attention_prompt(q, gates, Bp, Lp, kct_p, vc_p, tbl, tz0, tz1, overlap.T, sel_pg, win_pg)
    win_cache_t = _feature_major(cache_win_kv)
    o_s = nsa_attention_sample(q[NP:], gates[NP:], 0, Bs, Ls, past_len, kct_s, vc_s, tbl, overlap,
                               _feature_major(cache_sel_kv), page_table, sel_s, win_cache_t, win_s)
    h_f, h_b = matmul_residual_ln(o_p, o_s, o_w[0].astype(bf16), h_f, h_f[NP:],
                                  ln_g[1, 0].reshape(1, D), ln_b[1, 0].reshape(1, D))
    out_p, out_s = mlp_residual_ln(h_b, h_f, mlp_w1[1].astype(bf16), mlp_w2[1].astype(bf16),
                                   ln_g[1, 1].reshape(1, D), ln_b[1, 1].reshape(1, D), split_rows=NP)

    kv_shape = (2, A_N_KV, A_HEAD_DIM)
    n_keep = min(WINDOW, Lp)
    s_win = _token_major(window_update(win_cache_t, win_s))
    return (
        out_p.reshape(Bp, Lp, D), out_s.reshape(Bs, Ls, D),
        p_ssm[None], p_conv[None],
        _token_major(cmp_t), _token_major(sel_t), _token_major(win_t[:, :, Lp - n_keep:]),
        s_ssm[None], s_conv[None],
        cmp_s.reshape((Bs, Ls) + kv_shape), sel_s.reshape((Bs, Ls) + kv_shape), s_win,
    )
```

```python
import functools
import math

import jax
import jax.numpy as jnp
import numpy as np
from jax import lax
from jax.experimental import pallas as pl
from jax.experimental.pallas import tpu as pltpu

f32 = jnp.float32
bf16 = jnp.bfloat16
i32 = jnp.int32

D_MODEL = 1024
DEPTH = 2
DN_ALPHA = (2.0 * DEPTH) ** 0.25
LN_EPS = 1e-5
RMS_EPS = 1e-5
D_FF = 4 * D_MODEL
M_D_INNER = 2 * D_MODEL
M_HEAD_DIM = 64
M_N_HEADS = M_D_INNER // M_HEAD_DIM
M_N_GROUPS = 4
M_HPG = M_N_HEADS // M_N_GROUPS
M_D_STATE = 128
M_CONV = 4
M_CHUNK = 128
M_CONV_DIM = M_D_INNER + 2 * M_N_GROUPS * M_D_STATE
A_HEAD_DIM = 64
A_N_HEADS = D_MODEL // A_HEAD_DIM
A_N_KV = 4
A_GROUP = A_N_HEADS // A_N_KV
A_Q_DIM = A_N_HEADS * A_HEAD_DIM
KV_HALF = A_N_KV * A_HEAD_DIM
KV_ROW = 2 * KV_HALF
CMP_LEN = 32
CMP_STRIDE = 16
CMP_RATIO = CMP_LEN // CMP_STRIDE
CMP_HIDDEN = 2 * A_HEAD_DIM
SEL_BLOCK = 64
SEL_TOPN = 16
WINDOW = 512
Q_BLOCK = 128
N_BUCKETS = 32
MAX_DISTANCE = 128
MASK_VALUE = -1e30
FORCE_SCORE = 1e3
PAGE_SIZE = 128

LANES = 128
SUBLANES = 8
VMEM_LIMIT = 56 * 1024 * 1024

_HI = lax.Precision.HIGHEST


def _cparams(sem):
    return pltpu.CompilerParams(dimension_semantics=sem, vmem_limit_bytes=VMEM_LIMIT)


def _dot(a, b):
    return jnp.dot(a, b, preferred_element_type=f32)


def _dot_nt(a, b):
    return lax.dot_general(a, b, (((1,), (1,)), ((), ())), preferred_element_type=f32)


def _dot_tn(a, b):
    return lax.dot_general(a, b, (((0,), (0,)), ((), ())), preferred_element_type=f32)


def _silu(x):
    return x * (1.0 / (1.0 + jnp.exp(-x)))


def _layer_norm(x, g, b):
    mu = jnp.mean(x, axis=-1, keepdims=True)
    xc = x - mu
    var = jnp.mean(xc * xc, axis=-1, keepdims=True)
    return xc * lax.rsqrt(var + LN_EPS) * g + b


def _mm_body(x_ref, w_ref, o_ref, *, act):
    y = _dot(x_ref[...], w_ref[...])
    if act == "sigmoid":
        y = 1.0 / (1.0 + jnp.exp(-y))
    o_ref[...] = y.astype(o_ref.dtype)


def matmul(x, w, out_dtype, act=None, tm=1024, tn=1024):
    M, K = x.shape
    N = w.shape[1]
    tn = next(t for t in (tn, 512, 256, LANES) if N % t == 0)
    return pl.pallas_call(
        functools.partial(_mm_body, act=act),
        out_shape=jax.ShapeDtypeStruct((M, N), out_dtype),
        grid=(M // tm, N // tn),
        in_specs=[pl.BlockSpec((tm, K), lambda i, j: (i, 0)), pl.BlockSpec((K, tn), lambda i, j: (0, j))],
        out_specs=pl.BlockSpec((tm, tn), lambda i, j: (i, j)),
        compiler_params=_cparams(("parallel", "parallel")),
        name="matmul",
    )(x, w)


def _mm_res_ln_body(x0_ref, x1_ref, w_ref, r0_ref, r1_ref, g_ref, b_ref, of_ref, ob_ref, *, n_head_blocks):
    i = pl.program_id(0)
    w = w_ref[...]
    head = i < n_head_blocks
    x = jnp.where(head, x0_ref[...], x1_ref[...].astype(bf16))
    y = DN_ALPHA * jnp.where(head, r0_ref[...], r1_ref[...]) + _dot(x, w)
    h = _layer_norm(y, g_ref[...], b_ref[...])
    of_ref[...] = h
    ob_ref[...] = h.astype(bf16)


def matmul_residual_ln(x_head, x_tail, w, resid_head, resid_tail, g, b, tm=1024):
    M0, K = x_head.shape
    M1 = x_tail.shape[0]
    N = w.shape[1]
    assert M0 % tm == 0 and M1 % tm == 0 and resid_head.shape[0] >= M0 and resid_tail.shape[0] == M1
    n_head = M0 // tm
    row_map = lambda i: (i, 0)
    head_map = lambda i: (jnp.minimum(i, n_head - 1), 0)
    tail_map = lambda i: (jnp.maximum(i - n_head, 0), 0)
    const = lambda i: (0, 0)
    once = pl.Buffered(1) if M1 == tm else None
    return pl.pallas_call(
        functools.partial(_mm_res_ln_body, n_head_blocks=n_head),
        out_shape=(jax.ShapeDtypeStruct((M0 + M1, N), f32), jax.ShapeDtypeStruct((M0 + M1, N), bf16)),
        grid=((M0 + M1) // tm,),
        in_specs=[
            pl.BlockSpec((tm, K), head_map),
            pl.BlockSpec((tm, K), tail_map, pipeline_mode=once),
            pl.BlockSpec((K, N), const, pipeline_mode=pl.Buffered(1)),
            pl.BlockSpec((tm, N), head_map),
            pl.BlockSpec((tm, N), tail_map, pipeline_mode=once),
            pl.BlockSpec((1, N), const),
            pl.BlockSpec((1, N), const),
        ],
        out_specs=(pl.BlockSpec((tm, N), row_map), pl.BlockSpec((tm, N), row_map)),
        compiler_params=_cparams(("arbitrary",)),
        name="matmul_residual_ln",
    )(x_head, x_tail, w, resid_head, resid_tail, g, b)


def _mlp_body(hb_ref, hf_ref, w1_ref, w2_ref, g_ref, b_ref, o0_ref, o1_ref, acc_ref, *, n_head_blocks):
    i = pl.program_id(0)
    j = pl.program_id(1)

    @pl.when(j == 0)
    def _():
        acc_ref[...] = jnp.zeros_like(acc_ref)

    u = jnp.maximum(_dot(hb_ref[...], w1_ref[...]), 0.0)
    acc_ref[...] += _dot((u * u).astype(bf16), w2_ref[...])
    last = j == pl.num_programs(1) - 1

    def result():
        return _layer_norm(DN_ALPHA * hf_ref[...] + acc_ref[...], g_ref[...], b_ref[...])

    if n_head_blocks is None:
        @pl.when(last)
        def _():
            h = result()
            o0_ref[...] = h
            o1_ref[...] = h.astype(bf16)
    else:
        @pl.when(last & (i < n_head_blocks))
        def _():
            o0_ref[...] = result()

        @pl.when(last & (i >= n_head_blocks))
        def _():
            o1_ref[...] = result()


def mlp_residual_ln(hb, hf, w1, w2, g, b, split_rows=None, tm=1024, tf=1024):
    M, D = hb.shape
    F = w1.shape[1]
    row_map = lambda i, j: (i, 0)
    if split_rows is None:
        n_head = None
        out_shape = (jax.ShapeDtypeStruct((M, D), f32), jax.ShapeDtypeStruct((M, D), bf16))
        out_specs = (pl.BlockSpec((tm, D), row_map), pl.BlockSpec((tm, D), row_map))
    else:
        assert split_rows % tm == 0 and (M - split_rows) % tm == 0
        n_head = split_rows // tm
        out_shape = (jax.ShapeDtypeStruct((split_rows, D), f32), jax.ShapeDtypeStruct((M - split_rows, D), f32))
        out_specs = (pl.BlockSpec((tm, D), lambda i, j: (jnp.minimum(i, n_head - 1), 0)),
                     pl.BlockSpec((tm, D), lambda i, j: (jnp.maximum(i - n_head, 0), 0)))
    return pl.pallas_call(
        functools.partial(_mlp_body, n_head_blocks=n_head),
        out_shape=out_shape,
        grid=(M // tm, F // tf),
        in_specs=[
            pl.BlockSpec((tm, D), row_map),
            pl.BlockSpec((tm, D), row_map),
            pl.BlockSpec((D, tf), lambda i, j: (0, j)),
            pl.BlockSpec((tf, D), lambda i, j: (j, 0)),
            pl.BlockSpec((1, D), lambda i, j: (0, 0)),
            pl.BlockSpec((1, D), lambda i, j: (0, 0)),
        ],
        out_specs=out_specs,
        scratch_shapes=[pltpu.VMEM((tm, D), f32)],
        compiler_params=_cparams(("arbitrary", "arbitrary")),
        name="mlp_residual_ln",
    )(hb, hf, w1, w2, g, b)


def _kv_project_body(wt_ref, h_ref, cmp_ref, sel_ref, win_ref, selp_ref, winp_ref):
    res = _dot_nt(wt_ref[...], h_ref[...])
    tm = h_ref.shape[0]
    cmp_ref[0] = res[0:KV_ROW]
    sel_ref[0] = res[KV_ROW:2 * KV_ROW]
    win_ref[0] = res[2 * KV_ROW:3 * KV_ROW]
    for k in range(tm // PAGE_SIZE):
        cols = slice(k * PAGE_SIZE, (k + 1) * PAGE_SIZE)
        selp_ref[0, k] = res[KV_ROW:2 * KV_ROW, cols].astype(bf16)
        winp_ref[0, k] = res[2 * KV_ROW:3 * KV_ROW, cols].astype(bf16)


def kv_project_feature_major(h_b, w_t, Bn, L, tm=512):
    nj = L // tm
    pages_per_step = tm // PAGE_SIZE
    fm = jax.ShapeDtypeStruct((Bn, KV_ROW, L), f32)
    pg = jax.ShapeDtypeStruct((Bn, L // PAGE_SIZE, KV_ROW, PAGE_SIZE), bf16)
    fm_spec = pl.BlockSpec((1, KV_ROW, tm), lambda b, j: (b, 0, j))
    pg_spec = pl.BlockSpec((1, pages_per_step, KV_ROW, PAGE_SIZE), lambda b, j: (b, j, 0, 0))
    return pl.pallas_call(
        _kv_project_body,
        out_shape=(fm, fm, fm, pg, pg),
        grid=(Bn, nj),
        in_specs=[
            pl.BlockSpec(w_t.shape, lambda b, j: (0, 0)),
            pl.BlockSpec((tm, h_b.shape[1]), lambda b, j: (b * nj + j, 0)),
        ],
        out_specs=(fm_spec, fm_spec, fm_spec, pg_spec, pg_spec),
        compiler_params=_cparams(("parallel", "parallel")),
        name="kv_project_feature_major",
    )(w_t, h_b)


SSD_SEQS_PER_STEP = 4
SSD_SEQS_PER_STEP_CHUNKED = 2


def _ssd_body(*refs, Q, BB, nc, has_init):
    n_in = 1 if nc == 1 else BB
    xbc_refs, z_refs, dt_refs = refs[0:n_in], refs[n_in:2 * n_in], refs[2 * n_in:3 * n_in]
    refs = refs[3 * n_in:]
    if has_init:
        conv0_ref, h0_ref = refs[0:2]
        refs = refs[2:]
    (cw_ref, cb_ref, dtb_ref, alog_ref, dskx_ref, nw_ref, e64_ref, eq_ref,
     y_ref, hout_ref, cout_ref, st_ref, xpad_ref, xc_ref, ybuf_ref, xdt_ref, xds_ref, ecx_ref, acx_ref) = refs

    def rows_of(row_refs, bb, cols=slice(None)):
        return row_refs[0][bb * Q:(bb + 1) * Q, cols] if nc == 1 else row_refs[bb][:, cols]

    def store_y(bb, cols, value):
        if nc == 1:
            y_ref[bb * Q:(bb + 1) * Q, cols] = value
        else:
            y_ref[bb, :, cols] = value

    c = pl.program_id(1)
    single_chunk = nc == 1
    P, N, R, G = M_HEAD_DIM, M_D_STATE, M_HPG, M_N_GROUPS
    PAD = SUBLANES

    per_head_state = single_chunk and has_init

    def first_chunk():
        for bb in range(BB):
            xpad_ref[bb, 0:PAD, :] = jnp.zeros((PAD, M_CONV_DIM), f32)
            if has_init:
                xpad_ref[bb, PAD - (M_CONV - 1):PAD, :] = conv0_ref[bb]
            if per_head_state:
                continue
            if has_init:
                for g in range(G):
                    for r in range(R):
                        st_ref[bb, g, :, r * P:(r + 1) * P] = h0_ref[bb, g * R + r].T
            else:
                st_ref[bb] = jnp.zeros(st_ref.shape[1:], f32)

    def last_chunk():
        if per_head_state:
            return
        for bb in range(BB):
            for g in range(G):
                for r in range(R):
                    hout_ref[bb, g * R + r] = st_ref[bb, g, :, r * P:(r + 1) * P].T

    if single_chunk:
        first_chunk()
    else:
        pl.when(c == 0)(first_chunk)

    ri = lax.broadcasted_iota(i32, (Q, Q), 0)
    ci = lax.broadcasted_iota(i32, (Q, Q), 1)
    tril = ri >= ci
    GW = M_D_INNER // G
    lane = lax.broadcasted_iota(i32, (Q, LANES), 1)
    third = M_N_HEADS
    assert 3 * third <= LANES

    def spread(v, e_ref):
        hi = v.astype(bf16).astype(f32)
        r1 = v - hi
        mid = r1.astype(bf16).astype(f32)
        lo = r1 - mid
        packed = jnp.where(lane < third, hi, jnp.where(lane < 2 * third, pltpu.roll(mid, third, 1),
                                                       jnp.where(lane < 3 * third, pltpu.roll(lo, 2 * third, 1), 0.0)))
        return _dot(packed.astype(bf16), e_ref[...])

    seqs = range(BB)
    a_cum, a_cum_t, e_last = [], [], []
    for bb in seqs:
        xpad_ref[bb, PAD:PAD + Q, :] = rows_of(xbc_refs, bb)
        acc = cb_ref[...] + xpad_ref[bb, pl.ds(PAD - 3, Q), :] * cw_ref[0:1, :]
        for k in range(1, M_CONV):
            acc = acc + xpad_ref[bb, pl.ds(PAD - 3 + k, Q), :] * cw_ref[k:k + 1, :]
        xc_ref[bb] = _silu(acc)
        cout_ref[bb] = xpad_ref[bb, pl.ds(Q + PAD - 3, 3), :]
        xpad_ref[bb, 0:PAD, :] = xpad_ref[bb, pl.ds(Q, PAD), :]

        xdt = rows_of(dt_refs, bb) + dtb_ref[...]
        dt = jnp.maximum(xdt, 0.0) + jnp.log1p(jnp.exp(-jnp.abs(xdt)))
        a = dt * (-jnp.exp(alog_ref[...]))
        a_cum.append(jnp.dot(tril.astype(f32), a, preferred_element_type=f32, precision=_HI))
        a_cum_t.append(a_cum[bb].T)
        a_last = a_cum[bb][Q - 1:Q, :]
        e_last.append(jnp.exp(a_last))
        xdt_all = xc_ref[bb, :, 0:M_D_INNER] * spread(dt, e64_ref)
        xdt_ref[bb] = xdt_all
        xds_ref[bb] = xdt_all * spread(jnp.exp(a_last - a_cum[bb]), e64_ref)
        ecx_ref[bb] = spread(jnp.exp(a_cum[bb]), e64_ref)
        if Q == LANES:
            acx_ref[bb] = spread(a_cum[bb], eq_ref)

    for g in range(G):
        gcols = slice(g * GW, (g + 1) * GW)
        bg = [xc_ref[bb, :, M_D_INNER + g * N:M_D_INNER + (g + 1) * N].astype(bf16) for bb in seqs]
        cg = [xc_ref[bb, :, M_D_INNER + G * N + g * N:M_D_INNER + G * N + (g + 1) * N].astype(bf16) for bb in seqs]
        gmat = [_dot_nt(cg[bb], bg[bb]) for bb in seqs]
        for r in range(R):
            h = g * R + r
            hcols = slice(h * P, (h + 1) * P)
            for bb in seqs:
                col = acx_ref[bb, :, h * Q:(h + 1) * Q] if Q == LANES else a_cum[bb][:, h:h + 1]
                row = a_cum_t[bb][h:h + 1, :]
                lmat = jnp.exp(jnp.where(tril, col - row, -jnp.inf))
                ydiag = _dot((gmat[bb] * lmat).astype(bf16), xdt_ref[bb, :, hcols].astype(bf16))
                if per_head_state:
                    h_in = h0_ref[bb, h]
                    hout_ref[bb, h] = (h_in * e_last[bb][:, h:h + 1]
                                       + _dot_tn(xds_ref[bb, :, hcols].astype(bf16), bg[bb]))
                    ydiag = ydiag + _dot_nt(cg[bb], h_in.astype(bf16)) * ecx_ref[bb, :, hcols]
                ybuf_ref[bb, :, hcols] = ydiag
        for bb in seqs:
            extra = xc_ref[bb, :, gcols] * dskx_ref[:, gcols]
            if not per_head_state:
                extra = extra + _dot(cg[bb], st_ref[bb, g].astype(bf16)) * ecx_ref[bb, :, gcols]
                new = _dot_tn(bg[bb], xds_ref[bb, :, gcols].astype(bf16))
                st_ref[bb, g] = st_ref[bb, g] * ecx_ref[bb, Q - 1:Q, gcols] + new
            ybuf_ref[bb, :, gcols] += extra

    for g in range(G):
        cols = slice(g * GW, (g + 1) * GW)
        for bb in seqs:
            yg = ybuf_ref[bb, :, cols] * _silu(rows_of(z_refs, bb, cols))
            ms = jnp.mean(yg * yg, axis=-1, keepdims=True)
            store_y(bb, cols, (yg * lax.rsqrt(ms + RMS_EPS) * nw_ref[:, cols]).astype(y_ref.dtype))

    if single_chunk:
        last_chunk()
    else:
        pl.when(c == pl.num_programs(1) - 1)(last_chunk)


def _spread_matrix(width):
    rows = np.arange(LANES)[:, None]
    cols = np.arange(M_N_HEADS * width)[None, :]
    return jnp.asarray((rows < 3 * M_N_HEADS) & (rows % M_N_HEADS == cols // width), bf16)


def ssd_mixer_core(xbc, z, dt, row0, Bn, L, conv0, h0, conv_w, conv_b, dt_bias, a_log, d_skip_x, norm_w):
    Q = M_CHUNK if L % M_CHUNK == 0 else L
    nc = L // Q
    has_init = h0 is not None
    BB = next(n for n in ((SSD_SEQS_PER_STEP if nc == 1 else SSD_SEQS_PER_STEP_CHUNKED), 1) if Bn % n == 0)
    const2 = lambda b, c: (0, 0)
    if nc == 1:
        blk0 = row0 // (BB * Q)
        row_maps = [lambda b, c: (blk0 + b, 0)]
        y_shape, y_block, y_map = (Bn * L, M_D_INNER), (BB * Q, M_D_INNER), (lambda b, c: (b, 0))
    else:
        blk0 = row0 // Q
        row_maps = [functools.partial(lambda b, c, bb: (blk0 + (b * BB + bb) * nc + c, 0), bb=bb) for bb in range(BB)]
        y_shape, y_block, y_map = (Bn, L, M_D_INNER), (BB, Q, M_D_INNER), (lambda b, c: (b, c, 0))
    rows_per_block = BB * Q if nc == 1 else Q
    in_specs, args = [], []
    for arr, width in ((xbc, M_CONV_DIM), (z, M_D_INNER), (dt, LANES)):
        in_specs += [pl.BlockSpec((rows_per_block, width), m) for m in row_maps]
        args += [arr] * len(row_maps)
    if has_init:
        in_specs += [
            pl.BlockSpec((BB, M_CONV - 1, M_CONV_DIM), lambda b, c: (b, 0, 0)),
            pl.BlockSpec((BB, M_N_HEADS, M_HEAD_DIM, M_D_STATE), lambda b, c: (b, 0, 0, 0)),
        ]
        args += [conv0, h0]
    e64 = _spread_matrix(M_HEAD_DIM)
    eq = _spread_matrix(Q if Q == LANES else SUBLANES)
    in_specs += [
        pl.BlockSpec((M_CONV, M_CONV_DIM), const2),
        pl.BlockSpec((1, M_CONV_DIM), const2),
        pl.BlockSpec((1, LANES), const2),
        pl.BlockSpec((1, LANES), const2),
        pl.BlockSpec((1, M_D_INNER), const2),
        pl.BlockSpec((1, M_D_INNER), const2),
        pl.BlockSpec(e64.shape, const2),
        pl.BlockSpec(eq.shape, const2),
    ]
    args += [conv_w, conv_b, dt_bias, a_log, d_skip_x, norm_w, e64, eq]
    y_dtype = bf16 if y_block[-2] % 16 == 0 else f32
    y, new_ssm, new_conv = pl.pallas_call(
        functools.partial(_ssd_body, Q=Q, BB=BB, nc=nc, has_init=has_init),
        out_shape=(
            jax.ShapeDtypeStruct(y_shape, y_dtype),
            jax.ShapeDtypeStruct((Bn, M_N_HEADS, M_HEAD_DIM, M_D_STATE), f32),
            jax.ShapeDtypeStruct((Bn, M_CONV - 1, M_CONV_DIM), f32),
        ),
        grid=(Bn // BB, nc),
        in_specs=in_specs,
        out_specs=(
            pl.BlockSpec(y_block, y_map),
            pl.BlockSpec((BB, M_N_HEADS, M_HEAD_DIM, M_D_STATE), lambda b, c: (b, 0, 0, 0)),
            pl.BlockSpec((BB, M_CONV - 1, M_CONV_DIM), lambda b, c: (b, 0, 0)),
        ),
        scratch_shapes=[
            pltpu.VMEM((1, 1, SUBLANES, LANES) if nc == 1 and has_init
                       else (BB, M_N_GROUPS, M_D_STATE, M_HPG * M_HEAD_DIM), f32),
            pltpu.VMEM((BB, Q + SUBLANES, M_CONV_DIM), f32),
            pltpu.VMEM((BB, Q, M_CONV_DIM), f32),
            pltpu.VMEM((BB, Q, M_D_INNER), f32),
            pltpu.VMEM((BB, Q, M_D_INNER), f32),
            pltpu.VMEM((BB, Q, M_D_INNER), f32),
            pltpu.VMEM((BB, Q, M_D_INNER), f32),
            pltpu.VMEM((BB, Q, M_N_HEADS * Q) if Q == LANES else (1, SUBLANES, LANES), f32),
        ],
        compiler_params=_cparams(("parallel", "arbitrary")),
        name="ssd_mixer_core",
    )(*args)
    return y.reshape(Bn * L, M_D_INNER), new_ssm, new_conv


N_PAGES = 16
N_CHUNKS = N_PAGES * PAGE_SIZE // CMP_STRIDE
N_SLABS = KV_ROW // LANES


def _compress_body(*refs, paged):
    if paged:
        pt_ref, cache_ref = refs[0:2]
        wbd_ref, pe_ref, w1f_ref, w2_ref, w2t_ref, kct_ref, vc_ref, xs_ref, buf_ref, sem_ref = refs[2:]
        b = pl.program_id(0)
        slot = b % 2

        def page_copies(seq, to_slot):
            return [pltpu.make_async_copy(cache_ref.at[pt_ref[seq, j]], buf_ref.at[to_slot, j], sem_ref.at[to_slot])
                    for j in range(N_PAGES)]

        @pl.when(b == 0)
        def _():
            for cp in page_copies(0, 0):
                cp.start()

        @pl.when(b + 1 < pl.num_programs(0))
        def _():
            for cp in page_copies(b + 1, 1 - slot):
                cp.start()

        for cp in page_copies(b, slot):
            cp.wait()
        page = lambda p, rows: buf_ref[slot, p, rows, :]
    else:
        pages = refs[:N_PAGES]
        wbd_ref, pe_ref, w1f_ref, w2_ref, w2t_ref, kct_ref, vc_ref, xs_ref = refs[N_PAGES:]
        page = lambda p, rows: pages[p][0, rows, :]
    H = CMP_HIDDEN
    n_steps = CMP_STRIDE // 2
    pages_per_step = N_PAGES // n_steps

    def to_token_major(sl, page_range):
        for p in page_range:
            xs_ref[sl, p * PAGE_SIZE:(p + 1) * PAGE_SIZE, :] = page(p, slice(sl * LANES, (sl + 1) * LANES)).T

    to_token_major(0, range(N_PAGES))
    row = lax.broadcasted_iota(i32, (N_CHUNKS, H), 0)
    for kv in range(2):
        pe_term = _dot(pe_ref[kv], w1f_ref[kv])[0:1, :]
        for gp in range(A_N_KV // 2):
            sl = kv * (A_N_KV // 2) + gp
            acc = jnp.zeros((N_CHUNKS, 2 * CMP_RATIO * H), f32)
            for jp in range(n_steps):
                x = jnp.concatenate(
                    [xs_ref[sl, pl.ds(2 * jp + k, N_CHUNKS, stride=CMP_STRIDE), :] for k in range(2)], axis=1).astype(bf16)
                acc = acc + _dot(x, wbd_ref[kv, jp])
                if sl + 1 < N_SLABS:
                    to_token_major(sl + 1, range(jp * pages_per_step, (jp + 1) * pages_per_step))
            for gi in range(2):
                g = gp * 2 + gi
                p0 = acc[:, gi * CMP_RATIO * H:gi * CMP_RATIO * H + H]
                p1 = acc[:, gi * CMP_RATIO * H + H:(gi + 1) * CMP_RATIO * H]
                p1_next = jnp.where(row == N_CHUNKS - 1, 0.0, pltpu.roll(p1, N_CHUNKS - 1, 0))
                hid = _silu(p0 + p1_next + pe_term).astype(bf16)
                if kv == 0:
                    kct_ref[0, g * A_HEAD_DIM:(g + 1) * A_HEAD_DIM, :] = _dot_nt(w2t_ref[...], hid).astype(kct_ref.dtype)
                else:
                    vc_ref[0, :, g * A_HEAD_DIM:(g + 1) * A_HEAD_DIM] = _dot(hid, w2_ref[...]).astype(vc_ref.dtype)


def compress_kv(pages, page_table, wbd, pe_rows, w1_flat, w2_v, w2t_k):
    paged = page_table is not None
    Bn = page_table.shape[0] if paged else pages.shape[0]
    if paged:
        page_specs = [pl.BlockSpec(memory_space=pl.ANY)]
        const = lambda nd: (lambda b, pt: (0,) * nd)
        out_map = lambda b, pt: (b, 0, 0)
    else:
        page_specs = [
            pl.BlockSpec((1, KV_ROW, PAGE_SIZE), functools.partial(lambda b, j: (b, 0, j), j=j)) for j in range(N_PAGES)
        ]
        const = lambda nd: (lambda b: (0,) * nd)
        out_map = lambda b: (b, 0, 0)
    in_specs = page_specs + [
        pl.BlockSpec(wbd.shape, const(4)),
        pl.BlockSpec(pe_rows.shape, const(3)),
        pl.BlockSpec(w1_flat.shape, const(3)),
        pl.BlockSpec(w2_v.shape, const(2)),
        pl.BlockSpec(w2t_k.shape, const(2)),
    ]
    out_shape = (jax.ShapeDtypeStruct((Bn, KV_HALF, N_CHUNKS), bf16), jax.ShapeDtypeStruct((Bn, N_CHUNKS, KV_HALF), bf16))
    out_specs = (pl.BlockSpec((1, KV_HALF, N_CHUNKS), out_map), pl.BlockSpec((1, N_CHUNKS, KV_HALF), out_map))
    scratch = [pltpu.VMEM((N_SLABS, N_PAGES * PAGE_SIZE, LANES), f32)]
    body = functools.partial(_compress_body, paged=paged)
    args = ([pages] * N_PAGES) + [wbd, pe_rows, w1_flat, w2_v, w2t_k]
    if paged:
        scratch = scratch + [pltpu.VMEM((2, N_PAGES, KV_ROW, PAGE_SIZE), f32), pltpu.SemaphoreType.DMA((2,))]
        return pl.pallas_call(
            body, out_shape=out_shape,
            grid_spec=pltpu.PrefetchScalarGridSpec(
                num_scalar_prefetch=1, grid=(Bn,), in_specs=in_specs, out_specs=out_specs, scratch_shapes=scratch),
            compiler_params=_cparams(("arbitrary",)),
            name="compress_kv_paged",
        )(page_table, pages, wbd, pe_rows, w1_flat, w2_v, w2t_k)
    return pl.pallas_call(
        body, out_shape=out_shape, grid=(Bn,), in_specs=in_specs, out_specs=out_specs, scratch_shapes=scratch,
        compiler_params=_cparams(("parallel",)), name="compress_kv",
    )(*args)


def _bias_table_body(rb_ref, tbl_ref, tz0_ref, tz1_ref):
    n = lax.broadcasted_iota(i32, (A_N_HEADS, LANES), 1)
    max_exact = N_BUCKETS // 2
    large = max_exact + jnp.floor(jnp.log(jnp.maximum(n, max_exact).astype(f32) / max_exact)
                                  / math.log(MAX_DISTANCE / max_exact) * (N_BUCKETS - max_exact))
    bucket = jnp.where(n < max_exact, n.astype(f32), jnp.minimum(large, N_BUCKETS - 1.0))
    tbl = jnp.zeros((A_N_HEADS, LANES), f32)
    for b in range(N_BUCKETS):
        tbl = jnp.where(bucket == float(b), rb_ref[:, b:b + 1], tbl)
    tbl_ref[...] = tbl
    dist = lax.broadcasted_iota(i32, (LANES, LANES), 0) - lax.broadcasted_iota(i32, (LANES, LANES), 1)
    for h in range(A_N_HEADS):
        row = jnp.broadcast_to(tbl[h:h + 1, :], (LANES, LANES))
        own = jnp.take_along_axis(row, jnp.clip(dist, 0, MAX_DISTANCE - 1), axis=1)
        tz0_ref[h] = jnp.where(dist >= 0, own, MASK_VALUE)
        tz1_ref[h] = jnp.take_along_axis(row, jnp.minimum(dist + LANES, MAX_DISTANCE - 1), axis=1)


def bias_table(rel_bias_t):
    assert MAX_DISTANCE <= LANES
    tile = jax.ShapeDtypeStruct((A_N_HEADS, LANES, LANES), f32)
    return pl.pallas_call(
        _bias_table_body, out_shape=(jax.ShapeDtypeStruct((A_N_HEADS, LANES), f32), tile, tile), name="bias_table",
    )(rel_bias_t)


TILE = 128
WIN_TILES = WINDOW // TILE


WIDE = 2 * TILE
N_SLOTS = N_PAGES * TILE // WIDE
BIG = -MASK_VALUE


SEL, WIN = 0, 1
FAR, NEAR = 0, 1
WIN_SLOT0 = N_SLOTS


def _attn_prompt_body(q_ref, gate_ref, kct_ref, vc_ref, tbl_ref, tz0_ref, tz1_ref, ovlt_ref, selp_ref, winp_ref, o_ref,
                      s_scr, m_scr, l_scr, acc_scr, lhs_scr, oc_scr, fb_scr, *, qb, n_sel):
    i = pl.program_id(1)
    R = A_GROUP * qb
    dh = A_HEAD_DIM
    pos0 = i * qb
    odd = (i % 2) == 1
    td = i // 2
    scale = dh ** -0.5
    n_rank = SUBLANES * (-(-n_sel // SUBLANES))

    lane = lax.broadcasted_iota(i32, (R, TILE), 1)
    q_in_blk = jnp.concatenate([lax.broadcasted_iota(i32, (qb, TILE), 0)] * A_GROUP, axis=0)
    qpos = pos0 + q_in_blk
    neg_tile = jnp.full((R, TILE), MASK_VALUE, f32)

    s_idx = lax.broadcasted_iota(i32, (n_rank, qb), 0)
    s_qpos = pos0 + lax.broadcasted_iota(i32, (n_rank, qb), 1)
    blk = s_qpos // SEL_BLOCK
    sel_valid = s_idx * SEL_BLOCK <= s_qpos
    sel_forced = (s_idx == 0) | (s_idx == blk) | (s_idx == blk - 1)

    f_row = lax.broadcasted_iota(i32, (TILE - dh, WIDE), 0)
    flag_rows = [jnp.where(f_row == k, BIG, 0.0).astype(bf16) for k in range(2)]
    zero_flag = jnp.zeros((TILE - dh, WIDE), bf16)
    zero_drop = jnp.zeros((TILE, WIDE), bf16)
    b_row = lax.broadcasted_iota(i32, (TILE, WIDE), 0)
    b_col = lax.broadcasted_iota(i32, (TILE, WIDE), 1)
    f_lane = lax.broadcasted_iota(i32, (R, TILE - dh), 1)
    flags = jnp.where(f_lane == 0, jnp.where(td < 1, -1.0, 0.0),
                      jnp.where(f_lane == 1, jnp.where(td < 2, -1.0, 0.0), 0.0)).astype(bf16)
    win_thr = q_in_blk + jnp.where(odd, TILE, 0)

    def pair(ref, T, g, half):
        rows = slice(half * KV_HALF + g * dh, half * KV_HALF + (g + 1) * dh)
        return jnp.concatenate([ref[0, 2 * T, rows, :], ref[0, 2 * T + 1, rows, :]], axis=1)

    def drop_rows(T):
        return jnp.where(b_row == T * (WIDE // SEL_BLOCK) + b_col // SEL_BLOCK, BIG, 0.0).astype(bf16)

    def store_scores(g, br, slot, s, left, right, kind, first=False):
        s_l = s[:, 0:TILE] if left is None else s[:, 0:TILE] + left
        s_r = s[:, TILE:WIDE] if right is None else s[:, TILE:WIDE] + right
        s_scr[g, slot, :, 0:TILE] = s_l
        s_scr[g, slot, :, TILE:WIDE] = s_r
        mx = jnp.maximum(s_l, s_r)
        m_scr[g, br, kind] = mx if first else jnp.maximum(m_scr[g, br, kind], mx)

    def accumulate(g, br, slot, kind, v_t, first=False):
        shift = m_scr[g, br, kind]
        p_l = jnp.exp(s_scr[g, slot, :, 0:TILE] - shift)
        p_r = jnp.exp(s_scr[g, slot, :, TILE:WIDE] - shift)
        pv = _dot_nt(jnp.concatenate([p_l, p_r], axis=1).astype(bf16), v_t)
        if first:
            l_scr[g, br] = p_l + p_r
            acc_scr[g, br] = pv
        else:
            l_scr[g, br] += p_l + p_r
            acc_scr[g, br] += pv

    def row_max(g, br):
        fb = fb_scr[g]
        m = jnp.max(jnp.maximum(m_scr[g, br, FAR] + fb, m_scr[g, br, NEAR]), axis=1, keepdims=True)
        m_b = jnp.broadcast_to(m, (R, TILE))
        m_scr[g, br, FAR] = m_b - fb
        m_scr[g, br, NEAR] = m_b

    def result(g, br):
        return acc_scr[g, br] / jnp.sum(l_scr[g, br], axis=1, keepdims=True)

    def near_tiles(g):
        tz0 = tz0_ref[g * A_GROUP:(g + 1) * A_GROUP].reshape(R, TILE)
        tz1 = tz1_ref[g * A_GROUP:(g + 1) * A_GROUP].reshape(R, TILE)
        return jnp.where(odd, fb_scr[g], tz1), jnp.where(odd, tz1, tz0), jnp.where(odd, tz0, neg_tile)

    t_prev = jnp.maximum(td - 1, 0)
    t_first = jnp.maximum(td - 2, 0)
    n_far = jnp.maximum(td - 1, 0)

    for g in range(A_N_KV):
        heads = [g * A_GROUP + r for r in range(A_GROUP)]
        qg = jnp.concatenate([q_ref[:, h * dh:(h + 1) * dh] for h in heads], axis=0)
        qg = (qg.astype(f32) * scale).astype(bf16)
        tbl_g = jnp.concatenate([jnp.broadcast_to(tbl_ref[h:h + 1, :], (qb, TILE)) for h in heads], axis=0)
        fb_scr[g] = jnp.concatenate(
            [jnp.broadcast_to(tbl_ref[h:h + 1, MAX_DISTANCE - 1:MAX_DISTANCE], (qb, TILE)) for h in heads], axis=0)

        dist_c = qpos - (lane * CMP_STRIDE + (CMP_LEN - 1))
        s_c = _dot(qg, kct_ref[0, g * dh:(g + 1) * dh, :])
        s_c = s_c + jnp.take_along_axis(tbl_g, jnp.clip(dist_c, 0, MAX_DISTANCE - 1), axis=1)
        s_c = jnp.where(dist_c >= 0, s_c, MASK_VALUE)
        m_c = jnp.max(s_c, axis=1, keepdims=True)
        p_c = jnp.where(dist_c >= 0, jnp.exp(s_c - m_c), 0.0)
        l_c = jnp.sum(p_c, axis=1, keepdims=True)
        l_c = jnp.where(l_c == 0.0, 1.0, l_c)
        oc_scr[g] = _dot(p_c.astype(bf16), vc_ref[0, :, g * dh:(g + 1) * dh]) / l_c
        p_c = p_c / l_c

        p_sum = p_c[0:qb]
        for r in range(1, A_GROUP):
            p_sum = p_sum + p_c[r * qb:(r + 1) * qb]
        imp_t = lax.dot_general(ovlt_ref[0:n_rank, :], p_sum, (((1,), (1,)), ((), ())),
                                preferred_element_type=f32, precision=_HI)
        score = jnp.where(sel_valid, imp_t + jnp.where(sel_forced, FORCE_SCORE, 0.0), -1.0)
        score = jnp.where(s_idx < n_sel, score, -3.0)
        rank = jnp.zeros((n_rank, qb), f32)
        for s2 in range(n_sel):
            other = score[s2:s2 + 1, :]
            rank = rank + jnp.where((other > score) | ((other == score) & (s_idx > s2)), 1.0, 0.0)
        dropped_t = jnp.where(rank < float(min(SEL_TOPN, n_sel)), 0.0, -1.0)
        dropped = jnp.concatenate([dropped_t, jnp.zeros((TILE - n_rank, qb), f32)], axis=0).T
        lhs_scr[g, SEL] = jnp.concatenate([qg, flags, jnp.concatenate([dropped] * A_GROUP, axis=0).astype(bf16)], axis=1)
        lhs_scr[g, WIN] = jnp.concatenate([qg, flags, jnp.zeros((R, TILE), bf16)], axis=1)
        m_scr[g, SEL, FAR] = neg_tile

    too_old_l = jnp.where(lane > win_thr, 0.0, MASK_VALUE)
    too_old_r = jnp.where(lane + TILE > win_thr, 0.0, MASK_VALUE)
    for g in range(A_N_KV):
        prev_right, diag_left, diag_right = near_tiles(g)
        lhs_win = lhs_scr[g, WIN]
        s = _dot(lhs_win, jnp.concatenate([pair(winp_ref, t_first, g, 0), flag_rows[1], zero_drop], axis=0))
        store_scores(g, WIN, WIN_SLOT0, s, too_old_l, too_old_r, FAR, first=True)
        s = _dot(lhs_win, jnp.concatenate([pair(winp_ref, t_prev, g, 0), flag_rows[0], zero_drop], axis=0))
        store_scores(g, WIN, WIN_SLOT0 + 1, s, fb_scr[g], prev_right, NEAR, first=True)
        s = _dot(lhs_win, jnp.concatenate([pair(winp_ref, td, g, 0), zero_flag, zero_drop], axis=0))
        store_scores(g, WIN, WIN_SLOT0 + 2, s, diag_left, diag_right, NEAR)
        lhs_sel = lhs_scr[g, SEL]
        s = _dot(lhs_sel, jnp.concatenate([pair(selp_ref, t_prev, g, 0), flag_rows[0], drop_rows(t_prev)], axis=0))
        store_scores(g, SEL, N_SLOTS - 2, s, fb_scr[g], prev_right, NEAR, first=True)
        s = _dot(lhs_sel, jnp.concatenate([pair(selp_ref, td, g, 0), zero_flag, drop_rows(td)], axis=0))
        store_scores(g, SEL, N_SLOTS - 1, s, diag_left, diag_right, NEAR)

    def far_scores(T, carry):
        drop = drop_rows(T)
        for g in range(A_N_KV):
            s = _dot(lhs_scr[g, SEL], jnp.concatenate([pair(selp_ref, T, g, 0), zero_flag, drop], axis=0))
            store_scores(g, SEL, T, s, None, None, FAR)
        return carry

    lax.fori_loop(0, n_far, far_scores, 0)

    for g in range(A_N_KV):
        row_max(g, WIN)
        row_max(g, SEL)
    for g in range(A_N_KV):
        accumulate(g, WIN, WIN_SLOT0, FAR, pair(winp_ref, t_first, g, 1), first=True)
        accumulate(g, WIN, WIN_SLOT0 + 1, NEAR, pair(winp_ref, t_prev, g, 1))
        accumulate(g, WIN, WIN_SLOT0 + 2, NEAR, pair(winp_ref, td, g, 1))
        accumulate(g, SEL, N_SLOTS - 2, NEAR, pair(selp_ref, t_prev, g, 1), first=True)
        accumulate(g, SEL, N_SLOTS - 1, NEAR, pair(selp_ref, td, g, 1))

    def far_accumulate(T, carry):
        for g in range(A_N_KV):
            accumulate(g, SEL, T, FAR, pair(selp_ref, T, g, 1))
        return carry

    lax.fori_loop(0, n_far, far_accumulate, 0)

    for g in range(A_N_KV):
        o_c, o_s, o_w = oc_scr[g], result(g, SEL), result(g, WIN)
        for r in range(A_GROUP):
            h = g * A_GROUP + r
            rows = slice(r * qb, (r + 1) * qb)
            gt = gate_ref[:, 3 * h:3 * h + 3]
            o_h = gt[:, 0:1] * o_c[rows] + gt[:, 1:2] * o_s[rows] + gt[:, 2:3] * o_w[rows]
            o_ref[:, h * dh:(h + 1) * dh] = o_h.astype(o_ref.dtype)


def nsa_attention_prompt(q, gates, Bn, L, kct, vc, tbl, tz0, tz1, overlap_t, sel_pages, win_pages):
    qb = Q_BLOCK
    nqb = L // qb
    n_sel = -(-L // SEL_BLOCK)
    n_tiles = L // TILE
    assert n_tiles == N_PAGES and qb == TILE
    R = A_GROUP * qb
    row_map = lambda b, i: (b * nqb + i, 0)
    seq_map3 = lambda b, i: (b, 0, 0)
    seq_map4 = lambda b, i: (b, 0, 0, 0)
    const2 = lambda b, i: (0, 0)
    const3 = lambda b, i: (0, 0, 0)
    return pl.pallas_call(
        functools.partial(_attn_prompt_body, qb=qb, n_sel=n_sel),
        out_shape=jax.ShapeDtypeStruct((Bn * L, A_Q_DIM), bf16),
        grid=(Bn, nqb),
        in_specs=[
            pl.BlockSpec((qb, A_Q_DIM), row_map),
            pl.BlockSpec((qb, LANES), row_map),
            pl.BlockSpec((1, KV_HALF, N_CHUNKS), seq_map3),
            pl.BlockSpec((1, N_CHUNKS, KV_HALF), seq_map3),
            pl.BlockSpec((A_N_HEADS, LANES), const2),
            pl.BlockSpec((A_N_HEADS, TILE, TILE), const3),
            pl.BlockSpec((A_N_HEADS, TILE, TILE), const3),
            pl.BlockSpec((TILE, TILE), const2),
            pl.BlockSpec((1, n_tiles, KV_ROW, TILE), seq_map4),
            pl.BlockSpec((1, n_tiles, KV_ROW, TILE), seq_map4),
        ],
        out_specs=pl.BlockSpec((qb, A_Q_DIM), row_map),
        scratch_shapes=[
            pltpu.VMEM((A_N_KV, N_SLOTS + 3, R, WIDE), f32),
            pltpu.VMEM((A_N_KV, 2, 2, R, TILE), f32),
            pltpu.VMEM((A_N_KV, 2, R, TILE), f32),
            pltpu.VMEM((A_N_KV, 2, R, A_HEAD_DIM), f32),
            pltpu.VMEM((A_N_KV, 2, R, WIDE), bf16),
            pltpu.VMEM((A_N_KV, R, A_HEAD_DIM), f32),
            pltpu.VMEM((A_N_KV, R, TILE), f32),
        ],
        compiler_params=_cparams(("parallel", "arbitrary")),
        name="nsa_attention_prompt",
    )(q, gates, kct, vc, tbl, tz0, tz1, overlap_t, sel_pages, win_pages)


def _attn_sample_body(pt_ref, q_ref, gate_ref, kct_ref, vc_ref, tbl_ref, ovl_ref, *refs, qb, start, n_sel):
    del pt_ref
    sel_pages = refs[:N_PAGES]
    selnew_ref, winpast_ref, winnew_ref, o_ref, s_scr = refs[N_PAGES:]
    dh = A_HEAD_DIM
    RG = A_GROUP * qb
    R = A_N_KV * RG
    scale = dh ** -0.5
    n_win_past = WINDOW // TILE
    assert R == TILE and n_sel <= SEL_BLOCK and start == N_PAGES * TILE

    qs = q_ref[...] * scale
    blocks = []
    for g in range(A_N_KV):
        qg = jnp.concatenate([qs[:, (g * A_GROUP + r) * dh:(g * A_GROUP + r + 1) * dh] for r in range(A_GROUP)], axis=0)
        parts = [qg if gg == g else jnp.zeros((RG, dh), f32) for gg in range(A_N_KV)]
        blocks.append(jnp.concatenate(parts, axis=1))
    qbd = jnp.concatenate(blocks, axis=0).astype(bf16)

    tbl_rows = jnp.concatenate([jnp.broadcast_to(tbl_ref[h:h + 1, :], (qb, LANES)) for h in range(A_N_HEADS)], axis=0)
    far_bias = tbl_rows[:, MAX_DISTANCE - 1:MAX_DISTANCE]
    lane = lax.broadcasted_iota(i32, (R, TILE), 1)
    row = lax.broadcasted_iota(i32, (R, TILE), 0)
    qpos = start + row % qb

    def near_bias(dist):
        return jnp.take_along_axis(tbl_rows, jnp.clip(dist, 0, MAX_DISTANCE - 1), axis=1)

    def softmax_rows(s):
        m = jnp.max(s, axis=1, keepdims=True)
        p = jnp.where(s > 0.5 * MASK_VALUE, jnp.exp(s - m), 0.0)
        l = jnp.sum(p, axis=1, keepdims=True)
        return p, jnp.where(l == 0.0, 1.0, l)

    def pad_rows(x):
        return jnp.concatenate([x, jnp.zeros((TILE - x.shape[0], x.shape[1]), x.dtype)], axis=0).astype(bf16)

    def branch_scores(tiles, new_ref, extra, col0):
        n = len(tiles)
        for j, (kv_t, kind) in enumerate(tiles):
            s = _dot(qbd, kv_t(0).astype(bf16)) + extra(j)
            dist = qpos - (start - (n - j) * TILE + lane)
            s = s + (far_bias if kind == "far" else near_bias(dist))
            if kind == "edge":
                s = jnp.where(dist < WINDOW, s, MASK_VALUE)
            s_scr[:, col0 + j * TILE:col0 + (j + 1) * TILE] = s
        dist = qpos - (start + lane)
        s = _dot_nt(qbd, pad_rows(new_ref[:, 0:KV_HALF])) + extra(n) + near_bias(dist)
        s_scr[:, col0 + n * TILE:col0 + (n + 1) * TILE] = jnp.where(dist >= 0, s, MASK_VALUE)

    def branch_output(tiles, new_ref, col0):
        n = len(tiles)
        p, l = softmax_rows(s_scr[:, col0:col0 + (n + 1) * TILE])
        p = p.astype(bf16)
        o = _dot(p[:, n * TILE:(n + 1) * TILE], pad_rows(new_ref[:, KV_HALF:KV_ROW]))
        for j, (kv_t, _) in enumerate(tiles):
            o = o + _dot_nt(p[:, j * TILE:(j + 1) * TILE], kv_t(1).astype(bf16))
        return o / l

    def page_getter(ref, cols=slice(None)):
        return lambda half: ref[0, half * KV_HALF:(half + 1) * KV_HALF, cols]

    sel_tiles = [(page_getter(sel_pages[t]), "near" if t == N_PAGES - 1 else "far") for t in range(N_PAGES)]
    win_tiles = [(page_getter(winpast_ref, slice(j * TILE, (j + 1) * TILE)),
                  "edge" if j == 0 else ("near" if j == n_win_past - 1 else "far")) for j in range(n_win_past)]
    win_col0 = (N_PAGES + 1) * TILE

    def sel_scores(extra):
        branch_scores(sel_tiles, selnew_ref, extra, 0)

    branch_scores(win_tiles, winnew_ref, lambda j: 0.0, win_col0)

    dist_c = qpos - (lane * CMP_STRIDE + (CMP_LEN - 1))
    s_c = _dot(qbd, kct_ref[0]) + near_bias(dist_c)
    p_c, l_c = softmax_rows(jnp.where(dist_c >= 0, s_c, MASK_VALUE))
    o_c = _dot(p_c.astype(bf16), vc_ref[0]) / l_c
    p_c = p_c / l_c

    p_sum = []
    for g in range(A_N_KV):
        acc = p_c[g * RG:g * RG + qb]
        for r in range(1, A_GROUP):
            acc = acc + p_c[g * RG + r * qb:g * RG + (r + 1) * qb]
        p_sum.append(acc)
    p_sum = jnp.concatenate(p_sum, axis=0)
    imp = jnp.dot(p_sum, ovl_ref[...], preferred_element_type=f32, precision=_HI)
    s_lane = lax.broadcasted_iota(i32, (A_N_KV * qb, TILE), 1)
    s_qpos = start + lax.broadcasted_iota(i32, (A_N_KV * qb, TILE), 0) % qb
    blk = s_qpos // SEL_BLOCK
    valid = s_lane * SEL_BLOCK <= s_qpos
    forced = (s_lane == 0) | (s_lane == blk) | (s_lane == blk - 1)
    score = jnp.where(valid, imp + jnp.where(forced, FORCE_SCORE, 0.0), -1.0)
    score = jnp.where(s_lane < n_sel, score, -3.0)
    rank = jnp.zeros(score.shape, f32)
    for s2 in range(n_sel):
        col = score[:, s2:s2 + 1]
        rank = rank + jnp.where((col > score) | ((col == score) & (s_lane > s2)), 1.0, 0.0)
    not_chosen = jnp.where(rank < float(min(SEL_TOPN, n_sel)), 0.0, -1.0)
    drop = jnp.concatenate(
        [not_chosen[g * qb:(g + 1) * qb] for g in range(A_N_KV) for _ in range(A_GROUP)], axis=0)
    drop = drop[:, 0:SEL_BLOCK].astype(bf16)
    b_row = lax.broadcasted_iota(i32, (SEL_BLOCK, TILE), 0)
    b_col = lax.broadcasted_iota(i32, (SEL_BLOCK, TILE), 1)

    def drop_unselected(t):
        expand = jnp.where(b_row == t * (TILE // SEL_BLOCK) + b_col // SEL_BLOCK, -MASK_VALUE, 0.0).astype(bf16)
        return _dot(drop, expand)

    sel_scores(lambda j: drop_unselected(j))
    o_w = branch_output(win_tiles, winnew_ref, win_col0)
    o_s = branch_output(sel_tiles, selnew_ref, 0)

    for g in range(A_N_KV):
        for r in range(A_GROUP):
            h = g * A_GROUP + r
            rows = slice(g * RG + r * qb, g * RG + (r + 1) * qb)
            cols = slice(g * dh, (g + 1) * dh)
            gt = gate_ref[:, 3 * h:3 * h + 3]
            o_h = gt[:, 0:1] * o_c[rows, cols] + gt[:, 1:2] * o_s[rows, cols] + gt[:, 2:3] * o_w[rows, cols]
            o_ref[:, h * dh:(h + 1) * dh] = o_h.astype(o_ref.dtype)


def nsa_attention_sample(q, gates, row0, Bn, L, start, kct, vc, tbl, overlap, sel_pages, page_table, sel_new,
                         win_past, win_new):
    qb = L
    n_sel = -(-(start + L) // SEL_BLOCK)
    blk0 = row0 // qb
    row_map = lambda b, pt: (blk0 + b, 0)
    seq_map = lambda b, pt: (b, 0, 0)
    page_specs = [
        pl.BlockSpec((1, KV_ROW, PAGE_SIZE), functools.partial(lambda b, pt, j: (pt[b, j], 0, 0), j=j))
        for j in range(N_PAGES)
    ]
    return pl.pallas_call(
        functools.partial(_attn_sample_body, qb=qb, start=start, n_sel=n_sel),
        out_shape=jax.ShapeDtypeStruct((Bn * L, A_Q_DIM), f32),
        grid_spec=pltpu.PrefetchScalarGridSpec(
            num_scalar_prefetch=1,
            grid=(Bn,),
            in_specs=[
                pl.BlockSpec((qb, A_Q_DIM), row_map),
                pl.BlockSpec((qb, LANES), row_map),
                pl.BlockSpec((1, KV_HALF, N_CHUNKS), seq_map),
                pl.BlockSpec((1, N_CHUNKS, KV_HALF), seq_map),
                pl.BlockSpec((A_N_HEADS, LANES), lambda b, pt: (0, 0)),
                pl.BlockSpec((TILE, TILE), lambda b, pt: (0, 0)),
            ] + page_specs + [
                pl.BlockSpec((qb, KV_ROW), row_map),
                pl.BlockSpec((1, KV_ROW, WINDOW), seq_map),
                pl.BlockSpec((qb, KV_ROW), row_map),
            ],
            out_specs=pl.BlockSpec((qb, A_Q_DIM), lambda b, pt: (b, 0)),
            scratch_shapes=[pltpu.VMEM((TILE, (N_PAGES + 1 + WINDOW // TILE + 1) * TILE), f32)],
        ),
        compiler_params=_cparams(("parallel",)),
        name="nsa_attention_sample",
    )(page_table, q, gates, kct, vc, tbl, overlap, *([sel_pages] * N_PAGES), sel_new, win_past, win_new)


WINDOW_UPDATE_SEQS_PER_STEP = 4


def _window_update_body(old_ref, new_ref, place_ref, o_ref, *, n_new):
    lane = lax.broadcasted_iota(i32, (KV_ROW, TILE), 1)
    for bb in range(old_ref.shape[0]):
        x = old_ref[bb]
        shifted = pltpu.roll(x, WINDOW - n_new, 1)
        new_t = lax.dot_general(new_ref[bb * n_new:(bb + 1) * n_new, :], place_ref[...], (((0,), (0,)), ((), ())),
                                preferred_element_type=f32, precision=_HI)
        o_ref[bb, :, 0:WINDOW - TILE] = shifted[:, 0:WINDOW - TILE]
        o_ref[bb, :, WINDOW - TILE:WINDOW] = jnp.where(lane >= TILE - n_new, new_t, shifted[:, WINDOW - TILE:WINDOW])


def window_update(old_t, new_rows):
    Bn = old_t.shape[0]
    n_new = new_rows.shape[0] // Bn
    BB = next(n for n in (WINDOW_UPDATE_SEQS_PER_STEP, 1) if Bn % n == 0)
    place = jnp.asarray(np.eye(n_new, TILE, k=TILE - n_new, dtype=np.float32))
    return pl.pallas_call(
        functools.partial(_window_update_body, n_new=n_new),
        out_shape=jax.ShapeDtypeStruct(old_t.shape, f32),
        grid=(Bn // BB,),
        in_specs=[
            pl.BlockSpec((BB, KV_ROW, WINDOW), lambda b: (b, 0, 0)),
            pl.BlockSpec((BB * n_new, KV_ROW), lambda b: (b, 0)),
            pl.BlockSpec((n_new, TILE), lambda b: (0, 0)),
        ],
        out_specs=pl.BlockSpec((BB, KV_ROW, WINDOW), lambda b: (b, 0, 0)),
        compiler_params=_cparams(("parallel",)),
        name="window_update",
    )(old_t, new_rows, place)


def _pad_lanes(v):
    return jnp.pad(v, (0, LANES - v.shape[0])).reshape(1, LANES)


def _overlap_matrix():
    n_cmp = N_CHUNKS - CMP_RATIO + 1
    c = np.arange(TILE)[:, None] * CMP_STRIDE
    s = np.arange(TILE)[None, :] * SEL_BLOCK
    ov = (c < s + SEL_BLOCK) & (c + CMP_LEN > s) & (np.arange(TILE)[:, None] < n_cmp)
    return jnp.asarray(ov.astype(np.float32))


def _feature_major(x):
    lead = x.shape[:-4]
    n = len(lead)
    return jnp.transpose(x, tuple(range(n)) + (n + 1, n + 2, n + 3, n)).reshape(lead + (KV_ROW, x.shape[-4]))


def _token_major(x_t):
    B, _, T = x_t.shape
    return jnp.transpose(x_t.reshape(B, 2, A_N_KV, A_HEAD_DIM, T), (0, 4, 1, 2, 3))


def kernel(x_prompt, x_sample, state_ssm, state_conv, cache_cmp_kv, cache_sel_kv, cache_win_kv, page_table, ln_g, ln_b, m_in_w, m_conv_w, m_conv_b, m_dt_bias, m_a_log, m_d, m_norm_w, m_out_w, kv_w, cmp_w1, cmp_pe, cmp_w2, q_w, o_w, rel_bias, mlp_w1, mlp_w2):
    Bp, Lp, D = x_prompt.shape
    Bs, Ls, _ = x_sample.shape
    NP, NS = Bp * Lp, Bs * Ls
    past_len = page_table.shape[1] * PAGE_SIZE
    assert past_len == N_PAGES * PAGE_SIZE and Lp == N_PAGES * PAGE_SIZE and cache_win_kv.shape[1] == WINDOW

    in_w = m_in_w[0].astype(bf16)
    z_w = in_w[:, :M_D_INNER]
    xbc_w = in_w[:, M_D_INNER:M_D_INNER + M_CONV_DIM]
    dt_w = jnp.pad(in_w[:, M_D_INNER + M_CONV_DIM:], ((0, 0), (0, LANES - M_N_HEADS)))
    kvw = kv_w.astype(bf16)
    qw = q_w[0].astype(bf16)
    gate_w = jnp.pad(qw[:, A_Q_DIM:], ((0, 0), (0, LANES - 3 * A_N_HEADS)))
    w1b = cmp_w1.astype(bf16)
    w_j = jnp.transpose(w1b, (0, 2, 3, 1, 4)).reshape(2, CMP_STRIDE, A_HEAD_DIM, CMP_RATIO * CMP_HIDDEN)
    zeros = jnp.zeros_like(w_j)
    wbd = jnp.concatenate([jnp.concatenate([w_j, zeros], axis=3), jnp.concatenate([zeros, w_j], axis=3)], axis=2)
    wbd = wbd.reshape(2, CMP_STRIDE // 2, 2 * LANES, 2 * CMP_RATIO * CMP_HIDDEN)
    pe_rows = jnp.broadcast_to(cmp_pe.astype(bf16).reshape(2, 1, CMP_LEN * A_HEAD_DIM), (2, SUBLANES, CMP_LEN * A_HEAD_DIM))
    w1_flat = w1b.reshape(2, CMP_LEN * A_HEAD_DIM, CMP_HIDDEN)
    cmp_w = (wbd, pe_rows, w1_flat, cmp_w2[1].astype(bf16), cmp_w2[0].T.astype(bf16))

    x_p, x_s = x_prompt.reshape(NP, D), x_sample.reshape(NS, D)
    xb = jnp.concatenate([x_p.astype(bf16), x_s.astype(bf16)], axis=0)
    z = matmul(xb, z_w, f32)
    xbc = matmul(xb, xbc_w, f32)
    dt = matmul(xb, dt_w, f32)
    ssd_w = (m_conv_w[0], m_conv_b[0].reshape(1, -1), _pad_lanes(m_dt_bias[0]), _pad_lanes(m_a_log[0]),
             jnp.repeat(m_d[0], M_HEAD_DIM).reshape(1, -1), m_norm_w[0].reshape(1, -1))
    y_p, p_ssm, p_conv = ssd_mixer_core(xbc, z, dt, 0, Bp, Lp, None, None, *ssd_w)
    y_s, s_ssm, s_conv = ssd_mixer_core(xbc, z, dt, NP, Bs, Ls, state_conv[0], state_ssm[0], *ssd_w)
    h_f, h_b = matmul_residual_ln(y_p, y_s, m_out_w[0].astype(bf16), x_p, x_s,
                                  ln_g[0, 0].reshape(1, D), ln_b[0, 0].reshape(1, D))
    h_f, h_b = mlp_residual_ln(h_b, h_f, mlp_w1[0].astype(bf16), mlp_w2[0].astype(bf16),
                               ln_g[0, 1].reshape(1, D), ln_b[0, 1].reshape(1, D))

    cmp_t, sel_t, win_t, sel_pg, win_pg = kv_project_feature_major(h_b, kvw.T, Bp, Lp)
    kv_s = matmul(h_b[NP:], kvw, f32)
    cmp_s, sel_s, win_s = kv_s[:, 0:KV_ROW], kv_s[:, KV_ROW:2 * KV_ROW], kv_s[:, 2 * KV_ROW:3 * KV_ROW]
    kct_p, vc_p = compress_kv(cmp_t, None, *cmp_w)
    kct_s, vc_s = compress_kv(_feature_major(cache_cmp_kv), page_table, *cmp_w)

    q = matmul(h_b, qw[:, :A_Q_DIM], f32)
    gates = matmul(h_b, gate_w, f32, act="sigmoid")
    tbl, tz0, tz1 = bias_table(rel_bias.T)
    overlap = _overlap_matrix()
    o_p = nsa_attention_prompt(q, gates, Bp, Lp, kct_p, vc_p, tbl, tz0, tz1, overlap.T, sel_pg, win_pg)
    win_cache_t = _feature_major(cache_win_kv)
    o_s = nsa_attention_sample(q[NP:], gates[NP:], 0, Bs, Ls, past_len, kct_s, vc_s, tbl, overlap,
                               _feature_major(cache_sel_kv), page_table, sel_s, win_cache_t, win_s)
    h_f, h_b = matmul_residual_ln(o_p, o_s, o_w[0].astype(bf16), h_f, h_f[NP:],
                                  ln_g[1, 0].reshape(1, D), ln_b[1, 0].reshape(1, D))
    out_p, out_s = mlp_residual_ln(h_b, h_f, mlp_w1[1].astype(bf16), mlp_w2[1].astype(bf16),
                                   ln_g[1, 1].reshape(1, D), ln_b[1, 1].reshape(1, D), split_rows=NP)

    kv_shape = (2, A_N_KV, A_HEAD_DIM)
    n_keep = min(WINDOW, Lp)
    s_win = _token_major(window_update(win_cache_t, win_s))
    return (
        out_p.reshape(Bp, Lp, D), out_s.reshape(Bs, Ls, D),
        p_ssm[None], p_conv[None],
        _token_major(cmp_t), _token_major(sel_t), _token_major(win_t[:, :, Lp - n_keep:]),
        s_ssm[None], s_conv[None],
        cmp_s.reshape((Bs, Ls) + kv_shape), sel_s.reshape((Bs, Ls) + kv_shape), s_win,
    )
```

```python
import functools
import math

import jax
import jax.numpy as jnp
import numpy as np
from jax import lax
from jax.experimental import pallas as pl
from jax.experimental.pallas import tpu as pltpu

f32 = jnp.float32
bf16 = jnp.bfloat16
i32 = jnp.int32

D_MODEL = 1024
DEPTH = 2
DN_ALPHA = (2.0 * DEPTH) ** 0.25
LN_EPS = 1e-5
RMS_EPS = 1e-5
D_FF = 4 * D_MODEL
M_D_INNER = 2 * D_MODEL
M_HEAD_DIM = 64
M_N_HEADS = M_D_INNER // M_HEAD_DIM
M_N_GROUPS = 4
M_HPG = M_N_HEADS // M_N_GROUPS
M_D_STATE = 128
M_CONV = 4
M_CHUNK = 128
M_CONV_DIM = M_D_INNER + 2 * M_N_GROUPS * M_D_STATE
A_HEAD_DIM = 64
A_N_HEADS = D_MODEL // A_HEAD_DIM
A_N_KV = 4
A_GROUP = A_N_HEADS // A_N_KV
A_Q_DIM = A_N_HEADS * A_HEAD_DIM
KV_HALF = A_N_KV * A_HEAD_DIM
KV_ROW = 2 * KV_HALF
CMP_LEN = 32
CMP_STRIDE = 16
CMP_RATIO = CMP_LEN // CMP_STRIDE
CMP_HIDDEN = 2 * A_HEAD_DIM
SEL_BLOCK = 64
SEL_TOPN = 16
WINDOW = 512
Q_BLOCK = 128
N_BUCKETS = 32
MAX_DISTANCE = 128
MASK_VALUE = -1e30
FORCE_SCORE = 1e3
PAGE_SIZE = 128

LANES = 128
SUBLANES = 8
VMEM_LIMIT = 56 * 1024 * 1024

_HI = lax.Precision.HIGHEST


def _cparams(sem):
    return pltpu.CompilerParams(dimension_semantics=sem, vmem_limit_bytes=VMEM_LIMIT)


def _dot(a, b):
    return jnp.dot(a, b, preferred_element_type=f32)


def _dot_nt(a, b):
    return lax.dot_general(a, b, (((1,), (1,)), ((), ())), preferred_element_type=f32)


def _dot_tn(a, b):
    return lax.dot_general(a, b, (((0,), (0,)), ((), ())), preferred_element_type=f32)


def _silu(x):
    return x * (1.0 / (1.0 + jnp.exp(-x)))


def _layer_norm(x, g, b):
    mu = jnp.mean(x, axis=-1, keepdims=True)
    xc = x - mu
    var = jnp.mean(xc * xc, axis=-1, keepdims=True)
    return xc * lax.rsqrt(var + LN_EPS) * g + b


def _mm_body(x_ref, w_ref, o_ref, *, act):
    y = _dot(x_ref[...], w_ref[...])
    if act == "sigmoid":
        y = 1.0 / (1.0 + jnp.exp(-y))
    o_ref[...] = y.astype(o_ref.dtype)


def matmul(x, w, out_dtype, act=None, tm=1024, tn=1024):
    M, K = x.shape
    N = w.shape[1]
    tn = next(t for t in (tn, 512, 256, LANES) if N % t == 0)
    return pl.pallas_call(
        functools.partial(_mm_body, act=act),
        out_shape=jax.ShapeDtypeStruct((M, N), out_dtype),
        grid=(M // tm, N // tn),
        in_specs=[pl.BlockSpec((tm, K), lambda i, j: (i, 0)), pl.BlockSpec((K, tn), lambda i, j: (0, j))],
        out_specs=pl.BlockSpec((tm, tn), lambda i, j: (i, j)),
        compiler_params=_cparams(("parallel", "parallel")),
        name="matmul",
    )(x, w)


def _mm_res_ln_body(x0_ref, x1_ref, w_ref, r0_ref, r1_ref, g_ref, b_ref, of_ref, ob_ref, *, n_head_blocks):
    i = pl.program_id(0)
    w = w_ref[...]
    head = i < n_head_blocks
    x = jnp.where(head, x0_ref[...], x1_ref[...].astype(bf16))
    y = DN_ALPHA * jnp.where(head, r0_ref[...], r1_ref[...]) + _dot(x, w)
    h = _layer_norm(y, g_ref[...], b_ref[...])
    of_ref[...] = h
    ob_ref[...] = h.astype(bf16)


def matmul_residual_ln(x_head, x_tail, w, resid_head, resid_tail, g, b, tm=1024):
    M0, K = x_head.shape
    M1 = x_tail.shape[0]
    N = w.shape[1]
    assert M0 % tm == 0 and M1 % tm == 0 and resid_head.shape[0] >= M0 and resid_tail.shape[0] == M1
    n_head = M0 // tm
    row_map = lambda i: (i, 0)
    head_map = lambda i: (jnp.minimum(i, n_head - 1), 0)
    tail_map = lambda i: (jnp.maximum(i - n_head, 0), 0)
    const = lambda i: (0, 0)
    once = pl.Buffered(1) if M1 == tm else None
    return pl.pallas_call(
        functools.partial(_mm_res_ln_body, n_head_blocks=n_head),
        out_shape=(jax.ShapeDtypeStruct((M0 + M1, N), f32), jax.ShapeDtypeStruct((M0 + M1, N), bf16)),
        grid=((M0 + M1) // tm,),
        in_specs=[
            pl.BlockSpec((tm, K), head_map),
            pl.BlockSpec((tm, K), tail_map, pipeline_mode=once),
            pl.BlockSpec((K, N), const, pipeline_mode=pl.Buffered(1)),
            pl.BlockSpec((tm, N), head_map),
            pl.BlockSpec((tm, N), tail_map, pipeline_mode=once),
            pl.BlockSpec((1, N), const),
            pl.BlockSpec((1, N), const),
        ],
        out_specs=(pl.BlockSpec((tm, N), row_map), pl.BlockSpec((tm, N), row_map)),
        compiler_params=_cparams(("arbitrary",)),
        name="matmul_residual_ln",
    )(x_head, x_tail, w, resid_head, resid_tail, g, b)


def _mlp_body(hb_ref, hf_ref, w1_ref, w2_ref, g_ref, b_ref, o0_ref, o1_ref, acc_ref, *, n_head_blocks):
    i = pl.program_id(0)
    j = pl.program_id(1)

    @pl.when(j == 0)
    def _():
        acc_ref[...] = jnp.zeros_like(acc_ref)

    u = jnp.maximum(_dot(hb_ref[...], w1_ref[...]), 0.0)
    acc_ref[...] += _dot((u * u).astype(bf16), w2_ref[...])
    last = j == pl.num_programs(1) - 1

    def result():
        return _layer_norm(DN_ALPHA * hf_ref[...] + acc_ref[...], g_ref[...], b_ref[...])

    if n_head_blocks is None:
        @pl.when(last)
        def _():
            h = result()
            o0_ref[...] = h
            o1_ref[...] = h.astype(bf16)
    else:
        @pl.when(last & (i < n_head_blocks))
        def _():
            o0_ref[...] = result()

        @pl.when(last & (i >= n_head_blocks))
        def _():
            o1_ref[...] = result()


def mlp_residual_ln(hb, hf, w1, w2, g, b, split_rows=None, tm=1024, tf=1024):
    M, D = hb.shape
    F = w1.shape[1]
    row_map = lambda i, j: (i, 0)
    if split_rows is None:
        n_head = None
        out_shape = (jax.ShapeDtypeStruct((M, D), f32), jax.ShapeDtypeStruct((M, D), bf16))
        out_specs = (pl.BlockSpec((tm, D), row_map), pl.BlockSpec((tm, D), row_map))
    else:
        assert split_rows % tm == 0 and (M - split_rows) % tm == 0
        n_head = split_rows // tm
        out_shape = (jax.ShapeDtypeStruct((split_rows, D), f32), jax.ShapeDtypeStruct((M - split_rows, D), f32))
        out_specs = (pl.BlockSpec((tm, D), lambda i, j: (jnp.minimum(i, n_head - 1), 0)),
                     pl.BlockSpec((tm, D), lambda i, j: (jnp.maximum(i - n_head, 0), 0)))
    return pl.pallas_call(
        functools.partial(_mlp_body, n_head_blocks=n_head),
        out_shape=out_shape,
        grid=(M // tm, F // tf),
        in_specs=[
            pl.BlockSpec((tm, D), row_map),
            pl.BlockSpec((tm, D), row_map),
            pl.BlockSpec((D, tf), lambda i, j: (0, j)),
            pl.BlockSpec((tf, D), lambda i, j: (j, 0)),
            pl.BlockSpec((1, D), lambda i, j: (0, 0)),
            pl.BlockSpec((1, D), lambda i, j: (0, 0)),
        ],
        out_specs=out_specs,
        scratch_shapes=[pltpu.VMEM((tm, D), f32)],
        compiler_params=_cparams(("arbitrary", "arbitrary")),
        name="mlp_residual_ln",
    )(hb, hf, w1, w2, g, b)


def _kv_project_body(wt_ref, h_ref, cmp_ref, sel_ref, win_ref, selp_ref, winp_ref):
    res = _dot_nt(wt_ref[...], h_ref[...])
    tm = h_ref.shape[0]
    cmp_ref[0] = res[0:KV_ROW]
    sel_ref[0] = res[KV_ROW:2 * KV_ROW]
    win_ref[0] = res[2 * KV_ROW:3 * KV_ROW]
    for k in range(tm // PAGE_SIZE):
        cols = slice(k * PAGE_SIZE, (k + 1) * PAGE_SIZE)
        selp_ref[0, k] = res[KV_ROW:2 * KV_ROW, cols].astype(bf16)
        winp_ref[0, k] = res[2 * KV_ROW:3 * KV_ROW, cols].astype(bf16)


def kv_project_feature_major(h_b, w_t, Bn, L, tm=512):
    nj = L // tm
    pages_per_step = tm // PAGE_SIZE
    fm = jax.ShapeDtypeStruct((Bn, KV_ROW, L), f32)
    pg = jax.ShapeDtypeStruct((Bn, L // PAGE_SIZE, KV_ROW, PAGE_SIZE), bf16)
    fm_spec = pl.BlockSpec((1, KV_ROW, tm), lambda b, j: (b, 0, j))
    pg_spec = pl.BlockSpec((1, pages_per_step, KV_ROW, PAGE_SIZE), lambda b, j: (b, j, 0, 0))
    return pl.pallas_call(
        _kv_project_body,
        out_shape=(fm, fm, fm, pg, pg),
        grid=(Bn, nj),
        in_specs=[
            pl.BlockSpec(w_t.shape, lambda b, j: (0, 0)),
            pl.BlockSpec((tm, h_b.shape[1]), lambda b, j: (b * nj + j, 0)),
        ],
        out_specs=(fm_spec, fm_spec, fm_spec, pg_spec, pg_spec),
        compiler_params=_cparams(("parallel", "parallel")),
        name="kv_project_feature_major",
    )(w_t, h_b)


SSD_SEQS_PER_STEP = 4
SSD_SEQS_PER_STEP_CHUNKED = 2


def _ssd_body(*refs, Q, BB, nc, has_init):
    n_in = 1 if nc == 1 else BB
    xbc_refs, z_refs, dt_refs = refs[0:n_in], refs[n_in:2 * n_in], refs[2 * n_in:3 * n_in]
    refs = refs[3 * n_in:]
    if has_init:
        conv0_ref, h0_ref = refs[0:2]
        refs = refs[2:]
    (cw_ref, cb_ref, dtb_ref, alog_ref, dskx_ref, nw_ref, e64_ref, eq_ref,
     y_ref, hout_ref, cout_ref, st_ref, xpad_ref, xc_ref, ybuf_ref, xdt_ref, xds_ref, ecx_ref, acx_ref) = refs

    def rows_of(row_refs, bb, cols=slice(None)):
        return row_refs[0][bb * Q:(bb + 1) * Q, cols] if nc == 1 else row_refs[bb][:, cols]

    def store_y(bb, cols, value):
        if nc == 1:
            y_ref[bb * Q:(bb + 1) * Q, cols] = value
        else:
            y_ref[bb, :, cols] = value

    c = pl.program_id(1)
    single_chunk = nc == 1
    P, N, R, G = M_HEAD_DIM, M_D_STATE, M_HPG, M_N_GROUPS
    PAD = SUBLANES

    per_head_state = single_chunk and has_init

    def first_chunk():
        for bb in range(BB):
            xpad_ref[bb, 0:PAD, :] = jnp.zeros((PAD, M_CONV_DIM), f32)
            if has_init:
                xpad_ref[bb, PAD - (M_CONV - 1):PAD, :] = conv0_ref[bb]
            if per_head_state:
                continue
            if has_init:
                for g in range(G):
                    for r in range(R):
                        st_ref[bb, g, :, r * P:(r + 1) * P] = h0_ref[bb, g * R + r].T
            else:
                st_ref[bb] = jnp.zeros(st_ref.shape[1:], f32)

    def last_chunk():
        if per_head_state:
            return
        for bb in range(BB):
            for g in range(G):
                for r in range(R):
                    hout_ref[bb, g * R + r] = st_ref[bb, g, :, r * P:(r + 1) * P].T

    if single_chunk:
        first_chunk()
    else:
        pl.when(c == 0)(first_chunk)

    ri = lax.broadcasted_iota(i32, (Q, Q), 0)
    ci = lax.broadcasted_iota(i32, (Q, Q), 1)
    tril = ri >= ci
    GW = M_D_INNER // G
    lane = lax.broadcasted_iota(i32, (Q, LANES), 1)
    third = M_N_HEADS
    assert 3 * third <= LANES

    def spread(v, e_ref):
        hi = v.astype(bf16).astype(f32)
        r1 = v - hi
        mid = r1.astype(bf16).astype(f32)
        lo = r1 - mid
        packed = jnp.where(lane < third, hi, jnp.where(lane < 2 * third, pltpu.roll(mid, third, 1),
                                                       jnp.where(lane < 3 * third, pltpu.roll(lo, 2 * third, 1), 0.0)))
        return _dot(packed.astype(bf16), e_ref[...])

    seqs = range(BB)
    a_cum, a_cum_t, e_last = [], [], []
    for bb in seqs:
        xpad_ref[bb, PAD:PAD + Q, :] = rows_of(xbc_refs, bb)
        acc = cb_ref[...] + xpad_ref[bb, pl.ds(PAD - 3, Q), :] * cw_ref[0:1, :]
        for k in range(1, M_CONV):
            acc = acc + xpad_ref[bb, pl.ds(PAD - 3 + k, Q), :] * cw_ref[k:k + 1, :]
        xc_ref[bb] = _silu(acc)
        cout_ref[bb] = xpad_ref[bb, pl.ds(Q + PAD - 3, 3), :]
        xpad_ref[bb, 0:PAD, :] = xpad_ref[bb, pl.ds(Q, PAD), :]

        xdt = rows_of(dt_refs, bb) + dtb_ref[...]
        dt = jnp.maximum(xdt, 0.0) + jnp.log1p(jnp.exp(-jnp.abs(xdt)))
        a = dt * (-jnp.exp(alog_ref[...]))
        a_cum.append(jnp.dot(tril.astype(f32), a, preferred_element_type=f32, precision=_HI))
        a_cum_t.append(a_cum[bb].T)
        a_last = a_cum[bb][Q - 1:Q, :]
        e_last.append(jnp.exp(a_last))
        xdt_all = xc_ref[bb, :, 0:M_D_INNER] * spread(dt, e64_ref)
        xdt_ref[bb] = xdt_all
        xds_ref[bb] = xdt_all * spread(jnp.exp(a_last - a_cum[bb]), e64_ref)
        ecx_ref[bb] = spread(jnp.exp(a_cum[bb]), e64_ref)
        if Q == LANES:
            acx_ref[bb] = spread(a_cum[bb], eq_ref)

    for g in range(G):
        gcols = slice(g * GW, (g + 1) * GW)
        bg = [xc_ref[bb, :, M_D_INNER + g * N:M_D_INNER + (g + 1) * N].astype(bf16) for bb in seqs]
        cg = [xc_ref[bb, :, M_D_INNER + G * N + g * N:M_D_INNER + G * N + (g + 1) * N].astype(bf16) for bb in seqs]
        gmat = [_dot_nt(cg[bb], bg[bb]) for bb in seqs]
        for r in range(R):
            h = g * R + r
            hcols = slice(h * P, (h + 1) * P)
            for bb in seqs:
                col = acx_ref[bb, :, h * Q:(h + 1) * Q] if Q == LANES else a_cum[bb][:, h:h + 1]
                row = a_cum_t[bb][h:h + 1, :]
                lmat = jnp.exp(jnp.where(tril, col - row, -jnp.inf))
                ydiag = _dot((gmat[bb] * lmat).astype(bf16), xdt_ref[bb, :, hcols].astype(bf16))
                if per_head_state:
                    h_in = h0_ref[bb, h]
                    hout_ref[bb, h] = (h_in * e_last[bb][:, h:h + 1]
                                       + _dot_tn(xds_ref[bb, :, hcols].astype(bf16), bg[bb]))
                    ydiag = ydiag + _dot_nt(cg[bb], h_in.astype(bf16)) * ecx_ref[bb, :, hcols]
                ybuf_ref[bb, :, hcols] = ydiag
        for bb in seqs:
            extra = xc_ref[bb, :, gcols] * dskx_ref[:, gcols]
            if not per_head_state:
                extra = extra + _dot(cg[bb], st_ref[bb, g].astype(bf16)) * ecx_ref[bb, :, gcols]
                new = _dot_tn(bg[bb], xds_ref[bb, :, gcols].astype(bf16))
                st_ref[bb, g] = st_ref[bb, g] * ecx_ref[bb, Q - 1:Q, gcols] + new
            ybuf_ref[bb, :, gcols] += extra

    for g in range(G):
        cols = slice(g * GW, (g + 1) * GW)
        for bb in seqs:
            yg = ybuf_ref[bb, :, cols] * _silu(rows_of(z_refs, bb, cols))
            ms = jnp.mean(yg * yg, axis=-1, keepdims=True)
            store_y(bb, cols, (yg * lax.rsqrt(ms + RMS_EPS) * nw_ref[:, cols]).astype(y_ref.dtype))

    if single_chunk:
        last_chunk()
    else:
        pl.when(c == pl.num_programs(1) - 1)(last_chunk)


def _spread_matrix(width):
    rows = np.arange(LANES)[:, None]
    cols = np.arange(M_N_HEADS * width)[None, :]
    return jnp.asarray((rows < 3 * M_N_HEADS) & (rows % M_N_HEADS == cols // width), bf16)


def ssd_mixer_core(xbc, z, dt, row0, Bn, L, conv0, h0, conv_w, conv_b, dt_bias, a_log, d_skip_x, norm_w):
    Q = M_CHUNK if L % M_CHUNK == 0 else L
    nc = L // Q
    has_init = h0 is not None
    BB = next(n for n in ((SSD_SEQS_PER_STEP if nc == 1 else SSD_SEQS_PER_STEP_CHUNKED), 1) if Bn % n == 0)
    const2 = lambda b, c: (0, 0)
    if nc == 1:
        blk0 = row0 // (BB * Q)
        row_maps = [lambda b, c: (blk0 + b, 0)]
        y_shape, y_block, y_map = (Bn * L, M_D_INNER), (BB * Q, M_D_INNER), (lambda b, c: (b, 0))
    else:
        blk0 = row0 // Q
        row_maps = [functools.partial(lambda b, c, bb: (blk0 + (b * BB + bb) * nc + c, 0), bb=bb) for bb in range(BB)]
        y_shape, y_block, y_map = (Bn, L, M_D_INNER), (BB, Q, M_D_INNER), (lambda b, c: (b, c, 0))
    rows_per_block = BB * Q if nc == 1 else Q
    in_specs, args = [], []
    for arr, width in ((xbc, M_CONV_DIM), (z, M_D_INNER), (dt, LANES)):
        in_specs += [pl.BlockSpec((rows_per_block, width), m) for m in row_maps]
        args += [arr] * len(row_maps)
    if has_init:
        in_specs += [
            pl.BlockSpec((BB, M_CONV - 1, M_CONV_DIM), lambda b, c: (b, 0, 0)),
            pl.BlockSpec((BB, M_N_HEADS, M_HEAD_DIM, M_D_STATE), lambda b, c: (b, 0, 0, 0)),
        ]
        args += [conv0, h0]
    e64 = _spread_matrix(M_HEAD_DIM)
    eq = _spread_matrix(Q if Q == LANES else SUBLANES)
    in_specs += [
        pl.BlockSpec((M_CONV, M_CONV_DIM), const2),
        pl.BlockSpec((1, M_CONV_DIM), const2),
        pl.BlockSpec((1, LANES), const2),
        pl.BlockSpec((1, LANES), const2),
        pl.BlockSpec((1, M_D_INNER), const2),
        pl.BlockSpec((1, M_D_INNER), const2),
        pl.BlockSpec(e64.shape, const2),
        pl.BlockSpec(eq.shape, const2),
    ]
    args += [conv_w, conv_b, dt_bias, a_log, d_skip_x, norm_w, e64, eq]
    y_dtype = bf16 if y_block[-2] % 16 == 0 else f32
    y, new_ssm, new_conv = pl.pallas_call(
        functools.partial(_ssd_body, Q=Q, BB=BB, nc=nc, has_init=has_init),
        out_shape=(
            jax.ShapeDtypeStruct(y_shape, y_dtype),
            jax.ShapeDtypeStruct((Bn, M_N_HEADS, M_HEAD_DIM, M_D_STATE), f32),
            jax.ShapeDtypeStruct((Bn, M_CONV - 1, M_CONV_DIM), f32),
        ),
        grid=(Bn // BB, nc),
        in_specs=in_specs,
        out_specs=(
            pl.BlockSpec(y_block, y_map),
            pl.BlockSpec((BB, M_N_HEADS, M_HEAD_DIM, M_D_STATE), lambda b, c: (b, 0, 0, 0)),
            pl.BlockSpec((BB, M_CONV - 1, M_CONV_DIM), lambda b, c: (b, 0, 0)),
        ),
        scratch_shapes=[
            pltpu.VMEM((1, 1, SUBLANES, LANES) if nc == 1 and has_init
                       else (BB, M_N_GROUPS, M_D_STATE, M_HPG * M_HEAD_DIM), f32),
            pltpu.VMEM((BB, Q + SUBLANES, M_CONV_DIM), f32),
            pltpu.VMEM((BB, Q, M_CONV_DIM), f32),
            pltpu.VMEM((BB, Q, M_D_INNER), f32),
            pltpu.VMEM((BB, Q, M_D_INNER), f32),
            pltpu.VMEM((BB, Q, M_D_INNER), f32),
            pltpu.VMEM((BB, Q, M_D_INNER), f32),
            pltpu.VMEM((BB, Q, M_N_HEADS * Q) if Q == LANES else (1, SUBLANES, LANES), f32),
        ],
        compiler_params=_cparams(("parallel", "arbitrary")),
        name="ssd_mixer_core",
    )(*args)
    return y.reshape(Bn * L, M_D_INNER), new_ssm, new_conv


N_PAGES = 16
N_CHUNKS = N_PAGES * PAGE_SIZE // CMP_STRIDE
N_SLABS = KV_ROW // LANES


def _compress_body(*refs, paged):
    if paged:
        pt_ref, cache_ref = refs[0:2]
        wbd_ref, pe_ref, w1f_ref, w2_ref, w2t_ref, kct_ref, vc_ref, xs_ref, buf_ref, sem_ref = refs[2:]
        b = pl.program_id(0)
        slot = b % 2

        def page_copies(seq, to_slot):
            return [pltpu.make_async_copy(cache_ref.at[pt_ref[seq, j]], buf_ref.at[to_slot, j], sem_ref.at[to_slot])
                    for j in range(N_PAGES)]

        @pl.when(b == 0)
        def _():
            for cp in page_copies(0, 0):
                cp.start()

        @pl.when(b + 1 < pl.num_programs(0))
        def _():
            for cp in page_copies(b + 1, 1 - slot):
                cp.start()

        for cp in page_copies(b, slot):
            cp.wait()
        page = lambda p, rows: buf_ref[slot, p, rows, :]
    else:
        pages = refs[:N_PAGES]
        wbd_ref, pe_ref, w1f_ref, w2_ref, w2t_ref, kct_ref, vc_ref, xs_ref = refs[N_PAGES:]
        page = lambda p, rows: pages[p][0, rows, :]
    H = CMP_HIDDEN
    n_steps = CMP_STRIDE // 2
    pages_per_step = N_PAGES // n_steps

    def to_token_major(sl, page_range):
        for p in page_range:
            xs_ref[sl, p * PAGE_SIZE:(p + 1) * PAGE_SIZE, :] = page(p, slice(sl * LANES, (sl + 1) * LANES)).T

    to_token_major(0, range(N_PAGES))
    row = lax.broadcasted_iota(i32, (N_CHUNKS, H), 0)
    for kv in range(2):
        pe_term = _dot(pe_ref[kv], w1f_ref[kv])[0:1, :]
        for gp in range(A_N_KV // 2):
            sl = kv * (A_N_KV // 2) + gp
            acc = jnp.zeros((N_CHUNKS, 2 * CMP_RATIO * H), f32)
            for jp in range(n_steps):
                x = jnp.concatenate(
                    [xs_ref[sl, pl.ds(2 * jp + k, N_CHUNKS, stride=CMP_STRIDE), :] for k in range(2)], axis=1).astype(bf16)
                acc = acc + _dot(x, wbd_ref[kv, jp])
                if sl + 1 < N_SLABS:
                    to_token_major(sl + 1, range(jp * pages_per_step, (jp + 1) * pages_per_step))
            for gi in range(2):
                g = gp * 2 + gi
                p0 = acc[:, gi * CMP_RATIO * H:gi * CMP_RATIO * H + H]
                p1 = acc[:, gi * CMP_RATIO * H + H:(gi + 1) * CMP_RATIO * H]
                p1_next = jnp.where(row == N_CHUNKS - 1, 0.0, pltpu.roll(p1, N_CHUNKS - 1, 0))
                hid = _silu(p0 + p1_next + pe_term).astype(bf16)
                if kv == 0:
                    kct_ref[0, g * A_HEAD_DIM:(g + 1) * A_HEAD_DIM, :] = _dot_nt(w2t_ref[...], hid).astype(kct_ref.dtype)
                else:
                    vc_ref[0, :, g * A_HEAD_DIM:(g + 1) * A_HEAD_DIM] = _dot(hid, w2_ref[...]).astype(vc_ref.dtype)


def compress_kv(pages, page_table, wbd, pe_rows, w1_flat, w2_v, w2t_k):
    paged = page_table is not None
    Bn = page_table.shape[0] if paged else pages.shape[0]
    if paged:
        page_specs = [pl.BlockSpec(memory_space=pl.ANY)]
        const = lambda nd: (lambda b, pt: (0,) * nd)
        out_map = lambda b, pt: (b, 0, 0)
    else:
        page_specs = [
            pl.BlockSpec((1, KV_ROW, PAGE_SIZE), functools.partial(lambda b, j: (b, 0, j), j=j)) for j in range(N_PAGES)
        ]
        const = lambda nd: (lambda b: (0,) * nd)
        out_map = lambda b: (b, 0, 0)
    in_specs = page_specs + [
        pl.BlockSpec(wbd.shape, const(4)),
        pl.BlockSpec(pe_rows.shape, const(3)),
        pl.BlockSpec(w1_flat.shape, const(3)),
        pl.BlockSpec(w2_v.shape, const(2)),
        pl.BlockSpec(w2t_k.shape, const(2)),
    ]
    out_shape = (jax.ShapeDtypeStruct((Bn, KV_HALF, N_CHUNKS), bf16), jax.ShapeDtypeStruct((Bn, N_CHUNKS, KV_HALF), bf16))
    out_specs = (pl.BlockSpec((1, KV_HALF, N_CHUNKS), out_map), pl.BlockSpec((1, N_CHUNKS, KV_HALF), out_map))
    scratch = [pltpu.VMEM((N_SLABS, N_PAGES * PAGE_SIZE, LANES), f32)]
    body = functools.partial(_compress_body, paged=paged)
    args = ([pages] * N_PAGES) + [wbd, pe_rows, w1_flat, w2_v, w2t_k]
    if paged:
        scratch = scratch + [pltpu.VMEM((2, N_PAGES, KV_ROW, PAGE_SIZE), f32), pltpu.SemaphoreType.DMA((2,))]
        return pl.pallas_call(
            body, out_shape=out_shape,
            grid_spec=pltpu.PrefetchScalarGridSpec(
                num_scalar_prefetch=1, grid=(Bn,), in_specs=in_specs, out_specs=out_specs, scratch_shapes=scratch),
            compiler_params=_cparams(("arbitrary",)),
            name="compress_kv_paged",
        )(page_table, pages, wbd, pe_rows, w1_flat, w2_v, w2t_k)
    return pl.pallas_call(
        body, out_shape=out_shape, grid=(Bn,), in_specs=in_specs, out_specs=out_specs, scratch_shapes=scratch,
        compiler_params=_cparams(("parallel",)), name="compress_kv",
    )(*args)


def _bias_table_body(rb_ref, tbl_ref, tz0_ref, tz1_ref):
    n = lax.broadcasted_iota(i32, (A_N_HEADS, LANES), 1)
    max_exact = N_BUCKETS // 2
    large = max_exact + jnp.floor(jnp.log(jnp.maximum(n, max_exact).astype(f32) / max_exact)
                                  / math.log(MAX_DISTANCE / max_exact) * (N_BUCKETS - max_exact))
    bucket = jnp.where(n < max_exact, n.astype(f32), jnp.minimum(large, N_BUCKETS - 1.0))
    tbl = jnp.zeros((A_N_HEADS, LANES), f32)
    for b in range(N_BUCKETS):
        tbl = jnp.where(bucket == float(b), rb_ref[:, b:b + 1], tbl)
    tbl_ref[...] = tbl
    dist = lax.broadcasted_iota(i32, (LANES, LANES), 0) - lax.broadcasted_iota(i32, (LANES, LANES), 1)
    for h in range(A_N_HEADS):
        row = jnp.broadcast_to(tbl[h:h + 1, :], (LANES, LANES))
        own = jnp.take_along_axis(row, jnp.clip(dist, 0, MAX_DISTANCE - 1), axis=1)
        tz0_ref[h] = jnp.where(dist >= 0, own, MASK_VALUE)
        tz1_ref[h] = jnp.take_along_axis(row, jnp.minimum(dist + LANES, MAX_DISTANCE - 1), axis=1)


def bias_table(rel_bias_t):
    assert MAX_DISTANCE <= LANES
    tile = jax.ShapeDtypeStruct((A_N_HEADS, LANES, LANES), f32)
    return pl.pallas_call(
        _bias_table_body, out_shape=(jax.ShapeDtypeStruct((A_N_HEADS, LANES), f32), tile, tile), name="bias_table",
    )(rel_bias_t)


TILE = 128
WIN_TILES = WINDOW // TILE


WIDE = 2 * TILE
N_SLOTS = N_PAGES * TILE // WIDE
BIG = -MASK_VALUE


SEL, WIN = 0, 1
FAR, NEAR = 0, 1
WIN_SLOT0 = N_SLOTS


def _attn_prompt_body(q_ref, gate_ref, kct_ref, vc_ref, tbl_ref, tz0_ref, tz1_ref, ovlt_ref, selp_ref, winp_ref, o_ref,
                      s_scr, m_scr, l_scr, acc_scr, lhs_scr, oc_scr, fb_scr, *, qb, n_sel):
    i = pl.program_id(1)
    R = A_GROUP * qb
    dh = A_HEAD_DIM
    pos0 = i * qb
    odd = (i % 2) == 1
    td = i // 2
    scale = dh ** -0.5
    n_rank = SUBLANES * (-(-n_sel // SUBLANES))

    lane = lax.broadcasted_iota(i32, (R, TILE), 1)
    q_in_blk = jnp.concatenate([lax.broadcasted_iota(i32, (qb, TILE), 0)] * A_GROUP, axis=0)
    qpos = pos0 + q_in_blk
    neg_tile = jnp.full((R, TILE), MASK_VALUE, f32)

    s_idx = lax.broadcasted_iota(i32, (n_rank, qb), 0)
    s_qpos = pos0 + lax.broadcasted_iota(i32, (n_rank, qb), 1)
    blk = s_qpos // SEL_BLOCK
    sel_valid = s_idx * SEL_BLOCK <= s_qpos
    sel_forced = (s_idx == 0) | (s_idx == blk) | (s_idx == blk - 1)

    f_row = lax.broadcasted_iota(i32, (TILE - dh, WIDE), 0)
    flag_rows = [jnp.where(f_row == k, BIG, 0.0).astype(bf16) for k in range(2)]
    zero_flag = jnp.zeros((TILE - dh, WIDE), bf16)
    zero_drop = jnp.zeros((TILE, WIDE), bf16)
    b_row = lax.broadcasted_iota(i32, (TILE, WIDE), 0)
    b_col = lax.broadcasted_iota(i32, (TILE, WIDE), 1)
    f_lane = lax.broadcasted_iota(i32, (R, TILE - dh), 1)
    flags = jnp.where(f_lane == 0, jnp.where(td < 1, -1.0, 0.0),
                      jnp.where(f_lane == 1, jnp.where(td < 2, -1.0, 0.0), 0.0)).astype(bf16)
    win_thr = q_in_blk + jnp.where(odd, TILE, 0)

    def pair(ref, T, g, half):
        rows = slice(half * KV_HALF + g * dh, half * KV_HALF + (g + 1) * dh)
        return jnp.concatenate([ref[0, 2 * T, rows, :], ref[0, 2 * T + 1, rows, :]], axis=1)

    def drop_rows(T):
        return jnp.where(b_row == T * (WIDE // SEL_BLOCK) + b_col // SEL_BLOCK, BIG, 0.0).astype(bf16)

    def store_scores(g, br, slot, s, left, right, kind, first=False):
        s_l = s[:, 0:TILE] if left is None else s[:, 0:TILE] + left
        s_r = s[:, TILE:WIDE] if right is None else s[:, TILE:WIDE] + right
        s_scr[g, slot, :, 0:TILE] = s_l
        s_scr[g, slot, :, TILE:WIDE] = s_r
        mx = jnp.maximum(s_l, s_r)
        m_scr[g, br, kind] = mx if first else jnp.maximum(m_scr[g, br, kind], mx)

    def accumulate(g, br, slot, kind, v_t, first=False):
        shift = m_scr[g, br, kind]
        p_l = jnp.exp(s_scr[g, slot, :, 0:TILE] - shift)
        p_r = jnp.exp(s_scr[g, slot, :, TILE:WIDE] - shift)
        pv = _dot_nt(jnp.concatenate([p_l, p_r], axis=1).astype(bf16), v_t)
        if first:
            l_scr[g, br] = p_l + p_r
            acc_scr[g, br] = pv
        else:
            l_scr[g, br] += p_l + p_r
            acc_scr[g, br] += pv

    def row_max(g, br):
        fb = fb_scr[g]
        m = jnp.max(jnp.maximum(m_scr[g, br, FAR] + fb, m_scr[g, br, NEAR]), axis=1, keepdims=True)
        m_b = jnp.broadcast_to(m, (R, TILE))
        m_scr[g, br, FAR] = m_b - fb
        m_scr[g, br, NEAR] = m_b

    def result(g, br):
        return acc_scr[g, br] / jnp.sum(l_scr[g, br], axis=1, keepdims=True)

    def near_tiles(g):
        tz0 = tz0_ref[g * A_GROUP:(g + 1) * A_GROUP].reshape(R, TILE)
        tz1 = tz1_ref[g * A_GROUP:(g + 1) * A_GROUP].reshape(R, TILE)
        return jnp.where(odd, fb_scr[g], tz1), jnp.where(odd, tz1, tz0), jnp.where(odd, tz0, neg_tile)

    t_prev = jnp.maximum(td - 1, 0)
    t_first = jnp.maximum(td - 2, 0)
    n_far = jnp.maximum(td - 1, 0)

    for g in range(A_N_KV):
        heads = [g * A_GROUP + r for r in range(A_GROUP)]
        qg = jnp.concatenate([q_ref[:, h * dh:(h + 1) * dh] for h in heads], axis=0)
        qg = (qg.astype(f32) * scale).astype(bf16)
        tbl_g = jnp.concatenate([jnp.broadcast_to(tbl_ref[h:h + 1, :], (qb, TILE)) for h in heads], axis=0)
        fb_scr[g] = jnp.concatenate(
            [jnp.broadcast_to(tbl_ref[h:h + 1, MAX_DISTANCE - 1:MAX_DISTANCE], (qb, TILE)) for h in heads], axis=0)

        dist_c = qpos - (lane * CMP_STRIDE + (CMP_LEN - 1))
        s_c = _dot(qg, kct_ref[0, g * dh:(g + 1) * dh, :])
        s_c = s_c + jnp.take_along_axis(tbl_g, jnp.clip(dist_c, 0, MAX_DISTANCE - 1), axis=1)
        s_c = jnp.where(dist_c >= 0, s_c, MASK_VALUE)
        m_c = jnp.max(s_c, axis=1, keepdims=True)
        p_c = jnp.where(dist_c >= 0, jnp.exp(s_c - m_c), 0.0)
        l_c = jnp.sum(p_c, axis=1, keepdims=True)
        l_c = jnp.where(l_c == 0.0, 1.0, l_c)
        oc_scr[g] = _dot(p_c.astype(bf16), vc_ref[0, :, g * dh:(g + 1) * dh]) / l_c
        p_c = p_c / l_c

        p_sum = p_c[0:qb]
        for r in range(1, A_GROUP):
            p_sum = p_sum + p_c[r * qb:(r + 1) * qb]
        imp_t = lax.dot_general(ovlt_ref[0:n_rank, :], p_sum, (((1,), (1,)), ((), ())),
                                preferred_element_type=f32, precision=_HI)
        score = jnp.where(sel_valid, imp_t + jnp.where(sel_forced, FORCE_SCORE, 0.0), -1.0)
        score = jnp.where(s_idx < n_sel, score, -3.0)
        rank = jnp.zeros((n_rank, qb), f32)
        for s2 in range(n_sel):
            other = score[s2:s2 + 1, :]
            rank = rank + jnp.where((other > score) | ((other == score) & (s_idx > s2)), 1.0, 0.0)
        dropped_t = jnp.where(rank < float(min(SEL_TOPN, n_sel)), 0.0, -1.0)
        dropped = jnp.concatenate([dropped_t, jnp.zeros((TILE - n_rank, qb), f32)], axis=0).T
        lhs_scr[g, SEL] = jnp.concatenate([qg, flags, jnp.concatenate([dropped] * A_GROUP, axis=0).astype(bf16)], axis=1)
        lhs_scr[g, WIN] = jnp.concatenate([qg, flags, jnp.zeros((R, TILE), bf16)], axis=1)
        m_scr[g, SEL, FAR] = neg_tile

    too_old_l = jnp.where(lane > win_thr, 0.0, MASK_VALUE)
    too_old_r = jnp.where(lane + TILE > win_thr, 0.0, MASK_VALUE)
    for g in range(A_N_KV):
        prev_right, diag_left, diag_right = near_tiles(g)
        lhs_win = lhs_scr[g, WIN]
        s = _dot(lhs_win, jnp.concatenate([pair(winp_ref, t_first, g, 0), flag_rows[1], zero_drop], axis=0))
        store_scores(g, WIN, WIN_SLOT0, s, too_old_l, too_old_r, FAR, first=True)
        s = _dot(lhs_win, jnp.concatenate([pair(winp_ref, t_prev, g, 0), flag_rows[0], zero_drop], axis=0))
        store_scores(g, WIN, WIN_SLOT0 + 1, s, fb_scr[g], prev_right, NEAR, first=True)
        s = _dot(lhs_win, jnp.concatenate([pair(winp_ref, td, g, 0), zero_flag, zero_drop], axis=0))
        store_scores(g, WIN, WIN_SLOT0 + 2, s, diag_left, diag_right, NEAR)
        lhs_sel = lhs_scr[g, SEL]
        s = _dot(lhs_sel, jnp.concatenate([pair(selp_ref, t_prev, g, 0), flag_rows[0], drop_rows(t_prev)], axis=0))
        store_scores(g, SEL, N_SLOTS - 2, s, fb_scr[g], prev_right, NEAR, first=True)
        s = _dot(lhs_sel, jnp.concatenate([pair(selp_ref, td, g, 0), zero_flag, drop_rows(td)], axis=0))
        store_scores(g, SEL, N_SLOTS - 1, s, diag_left, diag_right, NEAR)

    def far_scores(T, carry):
        drop = drop_rows(T)
        for g in range(A_N_KV):
            s = _dot(lhs_scr[g, SEL], jnp.concatenate([pair(selp_ref, T, g, 0), zero_flag, drop], axis=0))
            store_scores(g, SEL, T, s, None, None, FAR)
        return carry

    lax.fori_loop(0, n_far, far_scores, 0)

    for g in range(A_N_KV):
        row_max(g, WIN)
        row_max(g, SEL)
    for g in range(A_N_KV):
        accumulate(g, WIN, WIN_SLOT0, FAR, pair(winp_ref, t_first, g, 1), first=True)
        accumulate(g, WIN, WIN_SLOT0 + 1, NEAR, pair(winp_ref, t_prev, g, 1))
        accumulate(g, WIN, WIN_SLOT0 + 2, NEAR, pair(winp_ref, td, g, 1))
        accumulate(g, SEL, N_SLOTS - 2, NEAR, pair(selp_ref, t_prev, g, 1), first=True)
        accumulate(g, SEL, N_SLOTS - 1, NEAR, pair(selp_ref, td, g, 1))

    def far_accumulate(T, carry):
        for g in range(A_N_KV):
            accumulate(g, SEL, T, FAR, pair(selp_ref, T, g, 1))
        return carry

    lax.fori_loop(0, n_far, far_accumulate, 0)

    for g in range(A_N_KV):
        o_c, o_s, o_w = oc_scr[g], result(g, SEL), result(g, WIN)
        for r in range(A_GROUP):
            h = g * A_GROUP + r
            rows = slice(r * qb, (r + 1) * qb)
            gt = gate_ref[:, 3 * h:3 * h + 3]
            o_h = gt[:, 0:1] * o_c[rows] + gt[:, 1:2] * o_s[rows] + gt[:, 2:3] * o_w[rows]
            o_ref[:, h * dh:(h + 1) * dh] = o_h.astype(o_ref.dtype)


def nsa_attention_prompt(q, gates, Bn, L, kct, vc, tbl, tz0, tz1, overlap_t, sel_pages, win_pages):
    qb = Q_BLOCK
    nqb = L // qb
    n_sel = -(-L // SEL_BLOCK)
    n_tiles = L // TILE
    assert n_tiles == N_PAGES and qb == TILE
    R = A_GROUP * qb
    row_map = lambda b, i: (b * nqb + i, 0)
    seq_map3 = lambda b, i: (b, 0, 0)
    seq_map4 = lambda b, i: (b, 0, 0, 0)
    const2 = lambda b, i: (0, 0)
    const3 = lambda b, i: (0, 0, 0)
    return pl.pallas_call(
        functools.partial(_attn_prompt_body, qb=qb, n_sel=n_sel),
        out_shape=jax.ShapeDtypeStruct((Bn * L, A_Q_DIM), bf16),
        grid=(Bn, nqb),
        in_specs=[
            pl.BlockSpec((qb, A_Q_DIM), row_map),
            pl.BlockSpec((qb, LANES), row_map),
            pl.BlockSpec((1, KV_HALF, N_CHUNKS), seq_map3),
            pl.BlockSpec((1, N_CHUNKS, KV_HALF), seq_map3),
            pl.BlockSpec((A_N_HEADS, LANES), const2),
            pl.BlockSpec((A_N_HEADS, TILE, TILE), const3),
            pl.BlockSpec((A_N_HEADS, TILE, TILE), const3),
            pl.BlockSpec((TILE, TILE), const2),
            pl.BlockSpec((1, n_tiles, KV_ROW, TILE), seq_map4),
            pl.BlockSpec((1, n_tiles, KV_ROW, TILE), seq_map4),
        ],
        out_specs=pl.BlockSpec((qb, A_Q_DIM), row_map),
        scratch_shapes=[
            pltpu.VMEM((A_N_KV, N_SLOTS + 3, R, WIDE), f32),
            pltpu.VMEM((A_N_KV, 2, 2, R, TILE), f32),
            pltpu.VMEM((A_N_KV, 2, R, TILE), f32),
            pltpu.VMEM((A_N_KV, 2, R, A_HEAD_DIM), f32),
            pltpu.VMEM((A_N_KV, 2, R, WIDE), bf16),
            pltpu.VMEM((A_N_KV, R, A_HEAD_DIM), f32),
            pltpu.VMEM((A_N_KV, R, TILE), f32),
        ],
        compiler_params=_cparams(("parallel", "arbitrary")),
        name="nsa_attention_prompt",
    )(q, gates, kct, vc, tbl, tz0, tz1, overlap_t, sel_pages, win_pages)


def _attn_sample_body(pt_ref, q_ref, gate_ref, kct_ref, vc_ref, tbl_ref, ovl_ref, cache_ref, selnew_ref, winpast_ref,
                      winnew_ref, o_ref, s_scr, buf_ref, sem_ref, *, qb, start, n_sel):
    b = pl.program_id(0)
    slot = b % 2

    def page_copies(seq, to_slot):
        return [pltpu.make_async_copy(cache_ref.at[pt_ref[seq, j]], buf_ref.at[to_slot, j], sem_ref.at[to_slot])
                for j in range(N_PAGES)]

    @pl.when(b == 0)
    def _():
        for cp in page_copies(0, 0):
            cp.start()

    @pl.when(b + 1 < pl.num_programs(0))
    def _():
        for cp in page_copies(b + 1, 1 - slot):
            cp.start()
    dh = A_HEAD_DIM
    RG = A_GROUP * qb
    R = A_N_KV * RG
    scale = dh ** -0.5
    n_win_past = WINDOW // TILE
    assert R == TILE and n_sel <= SEL_BLOCK and start == N_PAGES * TILE

    qs = q_ref[...] * scale
    blocks = []
    for g in range(A_N_KV):
        qg = jnp.concatenate([qs[:, (g * A_GROUP + r) * dh:(g * A_GROUP + r + 1) * dh] for r in range(A_GROUP)], axis=0)
        parts = [qg if gg == g else jnp.zeros((RG, dh), f32) for gg in range(A_N_KV)]
        blocks.append(jnp.concatenate(parts, axis=1))
    qbd = jnp.concatenate(blocks, axis=0).astype(bf16)

    tbl_rows = jnp.concatenate([jnp.broadcast_to(tbl_ref[h:h + 1, :], (qb, LANES)) for h in range(A_N_HEADS)], axis=0)
    far_bias = tbl_rows[:, MAX_DISTANCE - 1:MAX_DISTANCE]
    lane = lax.broadcasted_iota(i32, (R, TILE), 1)
    row = lax.broadcasted_iota(i32, (R, TILE), 0)
    qpos = start + row % qb

    def near_bias(dist):
        return jnp.take_along_axis(tbl_rows, jnp.clip(dist, 0, MAX_DISTANCE - 1), axis=1)

    def softmax_rows(s):
        m = jnp.max(s, axis=1, keepdims=True)
        p = jnp.where(s > 0.5 * MASK_VALUE, jnp.exp(s - m), 0.0)
        l = jnp.sum(p, axis=1, keepdims=True)
        return p, jnp.where(l == 0.0, 1.0, l)

    def pad_rows(x):
        return jnp.concatenate([x, jnp.zeros((TILE - x.shape[0], x.shape[1]), x.dtype)], axis=0).astype(bf16)

    def branch_scores(tiles, new_ref, extra, col0):
        n = len(tiles)
        for j, (kv_t, kind) in enumerate(tiles):
            s = _dot(qbd, kv_t(0).astype(bf16)) + extra(j)
            dist = qpos - (start - (n - j) * TILE + lane)
            s = s + (far_bias if kind == "far" else near_bias(dist))
            if kind == "edge":
                s = jnp.where(dist < WINDOW, s, MASK_VALUE)
            s_scr[:, col0 + j * TILE:col0 + (j + 1) * TILE] = s
        dist = qpos - (start + lane)
        s = _dot_nt(qbd, pad_rows(new_ref[:, 0:KV_HALF])) + extra(n) + near_bias(dist)
        s_scr[:, col0 + n * TILE:col0 + (n + 1) * TILE] = jnp.where(dist >= 0, s, MASK_VALUE)

    def branch_output(tiles, new_ref, col0):
        n = len(tiles)
        p, l = softmax_rows(s_scr[:, col0:col0 + (n + 1) * TILE])
        p = p.astype(bf16)
        o = _dot(p[:, n * TILE:(n + 1) * TILE], pad_rows(new_ref[:, KV_HALF:KV_ROW]))
        for j, (kv_t, _) in enumerate(tiles):
            o = o + _dot_nt(p[:, j * TILE:(j + 1) * TILE], kv_t(1).astype(bf16))
        return o / l

    def page_getter(ref, cols=slice(None)):
        return lambda half: ref[0, half * KV_HALF:(half + 1) * KV_HALF, cols]

    sel_tiles = [(functools.partial(lambda half, t: buf_ref[slot, t, half * KV_HALF:(half + 1) * KV_HALF, :], t=t),
                  "near" if t == N_PAGES - 1 else "far") for t in range(N_PAGES)]
    win_tiles = [(page_getter(winpast_ref, slice(j * TILE, (j + 1) * TILE)),
                  "edge" if j == 0 else ("near" if j == n_win_past - 1 else "far")) for j in range(n_win_past)]
    win_col0 = (N_PAGES + 1) * TILE

    def sel_scores(extra):
        branch_scores(sel_tiles, selnew_ref, extra, 0)

    branch_scores(win_tiles, winnew_ref, lambda j: 0.0, win_col0)

    dist_c = qpos - (lane * CMP_STRIDE + (CMP_LEN - 1))
    s_c = _dot(qbd, kct_ref[0]) + near_bias(dist_c)
    p_c, l_c = softmax_rows(jnp.where(dist_c >= 0, s_c, MASK_VALUE))
    o_c = _dot(p_c.astype(bf16), vc_ref[0]) / l_c
    p_c = p_c / l_c

    p_sum = []
    for g in range(A_N_KV):
        acc = p_c[g * RG:g * RG + qb]
        for r in range(1, A_GROUP):
            acc = acc + p_c[g * RG + r * qb:g * RG + (r + 1) * qb]
        p_sum.append(acc)
    p_sum = jnp.concatenate(p_sum, axis=0)
    imp = jnp.dot(p_sum, ovl_ref[...], preferred_element_type=f32, precision=_HI)
    s_lane = lax.broadcasted_iota(i32, (A_N_KV * qb, TILE), 1)
    s_qpos = start + lax.broadcasted_iota(i32, (A_N_KV * qb, TILE), 0) % qb
    blk = s_qpos // SEL_BLOCK
    valid = s_lane * SEL_BLOCK <= s_qpos
    forced = (s_lane == 0) | (s_lane == blk) | (s_lane == blk - 1)
    score = jnp.where(valid, imp + jnp.where(forced, FORCE_SCORE, 0.0), -1.0)
    score = jnp.where(s_lane < n_sel, score, -3.0)
    rank = jnp.zeros(score.shape, f32)
    for s2 in range(n_sel):
        col = score[:, s2:s2 + 1]
        rank = rank + jnp.where((col > score) | ((col == score) & (s_lane > s2)), 1.0, 0.0)
    not_chosen = jnp.where(rank < float(min(SEL_TOPN, n_sel)), 0.0, -1.0)
    drop = jnp.concatenate(
        [not_chosen[g * qb:(g + 1) * qb] for g in range(A_N_KV) for _ in range(A_GROUP)], axis=0)
    drop = drop[:, 0:SEL_BLOCK].astype(bf16)
    b_row = lax.broadcasted_iota(i32, (SEL_BLOCK, TILE), 0)
    b_col = lax.broadcasted_iota(i32, (SEL_BLOCK, TILE), 1)

    def drop_unselected(t):
        expand = jnp.where(b_row == t * (TILE // SEL_BLOCK) + b_col // SEL_BLOCK, -MASK_VALUE, 0.0).astype(bf16)
        return _dot(drop, expand)

    for cp in page_copies(b, slot):
        cp.wait()
    sel_scores(lambda j: drop_unselected(j))
    o_w = branch_output(win_tiles, winnew_ref, win_col0)
    o_s = branch_output(sel_tiles, selnew_ref, 0)

    for g in range(A_N_KV):
        for r in range(A_GROUP):
            h = g * A_GROUP + r
            rows = slice(g * RG + r * qb, g * RG + (r + 1) * qb)
            cols = slice(g * dh, (g + 1) * dh)
            gt = gate_ref[:, 3 * h:3 * h + 3]
            o_h = gt[:, 0:1] * o_c[rows, cols] + gt[:, 1:2] * o_s[rows, cols] + gt[:, 2:3] * o_w[rows, cols]
            o_ref[:, h * dh:(h + 1) * dh] = o_h.astype(o_ref.dtype)


def nsa_attention_sample(q, gates, row0, Bn, L, start, kct, vc, tbl, overlap, sel_pages, page_table, sel_new,
                         win_past, win_new):
    qb = L
    n_sel = -(-(start + L) // SEL_BLOCK)
    blk0 = row0 // qb
    row_map = lambda b, pt: (blk0 + b, 0)
    seq_map = lambda b, pt: (b, 0, 0)
    page_specs = [pl.BlockSpec(memory_space=pl.ANY)]
    return pl.pallas_call(
        functools.partial(_attn_sample_body, qb=qb, start=start, n_sel=n_sel),
        out_shape=jax.ShapeDtypeStruct((Bn * L, A_Q_DIM), f32),
        grid_spec=pltpu.PrefetchScalarGridSpec(
            num_scalar_prefetch=1,
            grid=(Bn,),
            in_specs=[
                pl.BlockSpec((qb, A_Q_DIM), row_map),
                pl.BlockSpec((qb, LANES), row_map),
                pl.BlockSpec((1, KV_HALF, N_CHUNKS), seq_map),
                pl.BlockSpec((1, N_CHUNKS, KV_HALF), seq_map),
                pl.BlockSpec((A_N_HEADS, LANES), lambda b, pt: (0, 0)),
                pl.BlockSpec((TILE, TILE), lambda b, pt: (0, 0)),
            ] + page_specs + [
                pl.BlockSpec((qb, KV_ROW), row_map),
                pl.BlockSpec((1, KV_ROW, WINDOW), seq_map),
                pl.BlockSpec((qb, KV_ROW), row_map),
            ],
            out_specs=pl.BlockSpec((qb, A_Q_DIM), lambda b, pt: (b, 0)),
            scratch_shapes=[pltpu.VMEM((TILE, (N_PAGES + 1 + WINDOW // TILE + 1) * TILE), f32),
                            pltpu.VMEM((2, N_PAGES, KV_ROW, PAGE_SIZE), f32), pltpu.SemaphoreType.DMA((2,))],
        ),
        compiler_params=_cparams(("arbitrary",)),
        name="nsa_attention_sample",
    )(page_table, q, gates, kct, vc, tbl, overlap, sel_pages, sel_new, win_past, win_new)


WINDOW_UPDATE_SEQS_PER_STEP = 4


def _window_update_body(old_ref, new_ref, place_ref, o_ref, *, n_new):
    lane = lax.broadcasted_iota(i32, (KV_ROW, TILE), 1)
    for bb in range(old_ref.shape[0]):
        x = old_ref[bb]
        shifted = pltpu.roll(x, WINDOW - n_new, 1)
        new_t = lax.dot_general(new_ref[bb * n_new:(bb + 1) * n_new, :], place_ref[...], (((0,), (0,)), ((), ())),
                                preferred_element_type=f32, precision=_HI)
        o_ref[bb, :, 0:WINDOW - TILE] = shifted[:, 0:WINDOW - TILE]
        o_ref[bb, :, WINDOW - TILE:WINDOW] = jnp.where(lane >= TILE - n_new, new_t, shifted[:, WINDOW - TILE:WINDOW])


def window_update(old_t, new_rows):
    Bn = old_t.shape[0]
    n_new = new_rows.shape[0] // Bn
    BB = next(n for n in (WINDOW_UPDATE_SEQS_PER_STEP, 1) if Bn % n == 0)
    place = jnp.asarray(np.eye(n_new, TILE, k=TILE - n_new, dtype=np.float32))
    return pl.pallas_call(
        functools.partial(_window_update_body, n_new=n_new),
        out_shape=jax.ShapeDtypeStruct(old_t.shape, f32),
        grid=(Bn // BB,),
        in_specs=[
            pl.BlockSpec((BB, KV_ROW, WINDOW), lambda b: (b, 0, 0)),
            pl.BlockSpec((BB * n_new, KV_ROW), lambda b: (b, 0)),
            pl.BlockSpec((n_new, TILE), lambda b: (0, 0)),
        ],
        out_specs=pl.BlockSpec((BB, KV_ROW, WINDOW), lambda b: (b, 0, 0)),
        compiler_params=_cparams(("parallel",)),
        name="window_update",
    )(old_t, new_rows, place)


def _pad_lanes(v):
    return jnp.pad(v, (0, LANES - v.shape[0])).reshape(1, LANES)


def _overlap_matrix():
    n_cmp = N_CHUNKS - CMP_RATIO + 1
    c = np.arange(TILE)[:, None] * CMP_STRIDE
    s = np.arange(TILE)[None, :] * SEL_BLOCK
    ov = (c < s + SEL_BLOCK) & (c + CMP_LEN > s) & (np.arange(TILE)[:, None] < n_cmp)
    return jnp.asarray(ov.astype(np.float32))


def _feature_major(x):
    lead = x.shape[:-4]
    n = len(lead)
    return jnp.transpose(x, tuple(range(n)) + (n + 1, n + 2, n + 3, n)).reshape(lead + (KV_ROW, x.shape[-4]))


def _token_major(x_t):
    B, _, T = x_t.shape
    return jnp.transpose(x_t.reshape(B, 2, A_N_KV, A_HEAD_DIM, T), (0, 4, 1, 2, 3))


def kernel(x_prompt, x_sample, state_ssm, state_conv, cache_cmp_kv, cache_sel_kv, cache_win_kv, page_table, ln_g, ln_b, m_in_w, m_conv_w, m_conv_b, m_dt_bias, m_a_log, m_d, m_norm_w, m_out_w, kv_w, cmp_w1, cmp_pe, cmp_w2, q_w, o_w, rel_bias, mlp_w1, mlp_w2):
    Bp, Lp, D = x_prompt.shape
    Bs, Ls, _ = x_sample.shape
    NP, NS = Bp * Lp, Bs * Ls
    past_len = page_table.shape[1] * PAGE_SIZE
    assert past_len == N_PAGES * PAGE_SIZE and Lp == N_PAGES * PAGE_SIZE and cache_win_kv.shape[1] == WINDOW

    in_w = m_in_w[0].astype(bf16)
    z_w = in_w[:, :M_D_INNER]
    xbc_w = in_w[:, M_D_INNER:M_D_INNER + M_CONV_DIM]
    dt_w = jnp.pad(in_w[:, M_D_INNER + M_CONV_DIM:], ((0, 0), (0, LANES - M_N_HEADS)))
    kvw = kv_w.astype(bf16)
    qw = q_w[0].astype(bf16)
    gate_w = jnp.pad(qw[:, A_Q_DIM:], ((0, 0), (0, LANES - 3 * A_N_HEADS)))
    w1b = cmp_w1.astype(bf16)
    w_j = jnp.transpose(w1b, (0, 2, 3, 1, 4)).reshape(2, CMP_STRIDE, A_HEAD_DIM, CMP_RATIO * CMP_HIDDEN)
    zeros = jnp.zeros_like(w_j)
    wbd = jnp.concatenate([jnp.concatenate([w_j, zeros], axis=3), jnp.concatenate([zeros, w_j], axis=3)], axis=2)
    wbd = wbd.reshape(2, CMP_STRIDE // 2, 2 * LANES, 2 * CMP_RATIO * CMP_HIDDEN)
    pe_rows = jnp.broadcast_to(cmp_pe.astype(bf16).reshape(2, 1, CMP_LEN * A_HEAD_DIM), (2, SUBLANES, CMP_LEN * A_HEAD_DIM))
    w1_flat = w1b.reshape(2, CMP_LEN * A_HEAD_DIM, CMP_HIDDEN)
    cmp_w = (wbd, pe_rows, w1_flat, cmp_w2[1].astype(bf16), cmp_w2[0].T.astype(bf16))

    x_p, x_s = x_prompt.reshape(NP, D), x_sample.reshape(NS, D)
    xb = jnp.concatenate([x_p.astype(bf16), x_s.astype(bf16)], axis=0)
    z = matmul(xb, z_w, f32)
    xbc = matmul(xb, xbc_w, f32)
    dt = matmul(xb, dt_w, f32)
    ssd_w = (m_conv_w[0], m_conv_b[0].reshape(1, -1), _pad_lanes(m_dt_bias[0]), _pad_lanes(m_a_log[0]),
             jnp.repeat(m_d[0], M_HEAD_DIM).reshape(1, -1), m_norm_w[0].reshape(1, -1))
    y_p, p_ssm, p_conv = ssd_mixer_core(xbc, z, dt, 0, Bp, Lp, None, None, *ssd_w)
    y_s, s_ssm, s_conv = ssd_mixer_core(xbc, z, dt, NP, Bs, Ls, state_conv[0], state_ssm[0], *ssd_w)
    h_f, h_b = matmul_residual_ln(y_p, y_s, m_out_w[0].astype(bf16), x_p, x_s,
                                  ln_g[0, 0].reshape(1, D), ln_b[0, 0].reshape(1, D))
    h_f, h_b = mlp_residual_ln(h_b, h_f, mlp_w1[0].astype(bf16), mlp_w2[0].astype(bf16),
                               ln_g[0, 1].reshape(1, D), ln_b[0, 1].reshape(1, D))

    cmp_t, sel_t, win_t, sel_pg, win_pg = kv_project_feature_major(h_b, kvw.T, Bp, Lp)
    kv_s = matmul(h_b[NP:], kvw, f32)
    cmp_s, sel_s, win_s = kv_s[:, 0:KV_ROW], kv_s[:, KV_ROW:2 * KV_ROW], kv_s[:, 2 * KV_ROW:3 * KV_ROW]
    kct_p, vc_p = compress_kv(cmp_t, None, *cmp_w)
    kct_s, vc_s = compress_kv(_feature_major(cache_cmp_kv), page_table, *cmp_w)

    q = matmul(h_b, qw[:, :A_Q_DIM], f32)
    gates = matmul(h_b, gate_w, f32, act="sigmoid")
    tbl, tz0, tz1 = bias_table(rel_bias.T)
    overlap = _overlap_matrix()
    o_p = nsa_attention_prompt(q, gates, Bp, Lp, kct_p, vc_p, tbl, tz0, tz1, overlap.T, sel_pg, win_pg)
    win_cache_t = _feature_major(cache_win_kv)
    o_s = nsa_attention_sample(q[NP:], gates[NP:], 0, Bs, Ls, past_len, kct_s, vc_s, tbl, overlap,
                               _feature_major(cache_sel_kv), page_table, sel_s, win_cache_t, win_s)
    h_f, h_b = matmul_residual_ln(o_p, o_s, o_w[0].astype(bf16), h_f, h_f[NP:],
                                  ln_g[1, 0].reshape(1, D), ln_b[1, 0].reshape(1, D))
    out_p, out_s = mlp_residual_ln(h_b, h_f, mlp_w1[1].astype(bf16), mlp_w2[1].astype(bf16),
                                   ln_g[1, 1].reshape(1, D), ln_b[1, 1].reshape(1, D), split_rows=NP)

    kv_shape = (2, A_N_KV, A_HEAD_DIM)
    n_keep = min(WINDOW, Lp)
    s_win = _token_major(window_update(win_cache_t, win_s))
    return (
        out_p.reshape(Bp, Lp, D), out_s.reshape(Bs, Ls, D),
        p_ssm[None], p_conv[None],
        _token_major(cmp_t), _token_major(sel_t), _token_major(win_t[:, :, Lp - n_keep:]),
        s_ssm[None], s_conv[None],
        cmp_s.reshape((Bs, Ls) + kv_shape), sel_s.reshape((Bs, Ls) + kv_shape), s_win,
    )
```
